```python
import math, functools
import jax, jax.numpy as jnp
from jax import lax
import numpy as np

D_MODEL = 2048
BATCH = 16
SEQ = 2048
DEPTH = 1
DEC_BATCH = 32
DEC_SEQ = 8
PAST_LEN = 16384
PAGE_SIZE = 128

D_PLE = 256
NORM_EPS = 1e-6
M_WIDTH = D_MODEL // 2
M_HEADS = 4
M_DH = M_WIDTH // M_HEADS
M_CONV = 4
M_CHUNK = 64
A_WIDTH = D_MODEL - M_WIDTH
A_HEADS = 16
A_DH = A_WIDTH // A_HEADS
A_KV = 4
A_GROUP = A_HEADS // A_KV
CMP_STRIDE = 16
CMP_LEN = 2 * CMP_STRIDE
CMP_HID = 64
SLC_BLK = 64
SLC_TOPN = 16
WINDOW = 512
Q_BLOCK = 128
KV_WIDTH = 6 * A_KV * A_DH
IN_SIZES = (M_WIDTH, M_WIDTH, M_WIDTH, M_HEADS, M_HEADS, A_WIDTH, KV_WIDTH, 3 * A_HEADS, A_WIDTH)
D_IN = sum(IN_SIZES)
SPLIT_AT = tuple(int(v) for v in np.cumsum(IN_SIZES)[:-1])

kernel_name = "hymba_mlstm_nsa_decoder_step"


def rms_norm(x, w):
    xf = x.astype(jnp.float32)
    y = xf * lax.rsqrt(jnp.mean(xf * xf, axis=-1, keepdims=True) + NORM_EPS)
    return (y * w.astype(jnp.float32)).astype(x.dtype)


def masked_softmax(s, mask):
    s = jnp.where(mask, s.astype(jnp.float32), -jnp.inf)
    m = jnp.max(s, axis=-1, keepdims=True)
    m = jnp.where(jnp.isfinite(m), m, 0.0)
    e = jnp.where(mask, jnp.exp(s - m), 0.0)
    return e / jnp.maximum(jnp.sum(e, axis=-1, keepdims=True), 1e-30)


def causal_conv(u, buf, w, b):
    T = u.shape[1]
    full = jnp.concatenate([buf.astype(u.dtype), u], axis=1)
    out = b + sum(full[:, j:j + T] * w[j] for j in range(M_CONV))
    return out, full[:, T:]


def mlstm_chunk(carry, inp):
    C, n, m = carry
    q, k, v, li, lf = inp
    L = q.shape[2]
    b = jnp.cumsum(lf, axis=-1)
    causal = jnp.tril(jnp.ones((L, L), dtype=bool))
    logd = jnp.where(causal, b[..., :, None] - b[..., None, :] + li[..., None, :], -jnp.inf)
    inter = b + m[..., None]
    m_t = jnp.maximum(inter, jnp.max(logd, axis=-1))
    s = jnp.einsum('bhtd,bhsd->bhts', q, k) * jnp.exp(logd - m_t[..., None])
    a = jnp.exp(inter - m_t)
    num = jnp.einsum('bhts,bhsv->bhtv', s, v) + a[..., None] * jnp.einsum('bhtd,bhdv->bhtv', q, C)
    den = jnp.sum(s, axis=-1) + a * jnp.einsum('bhtd,bhd->bht', q, n)
    h = num / jnp.maximum(jnp.abs(den), jnp.exp(-m_t))[..., None]
    b_end = b[..., -1]
    logw = b_end[..., None] - b + li
    m_new = jnp.maximum(b_end + m, jnp.max(logw, axis=-1))
    w = jnp.exp(logw - m_new[..., None])
    decay = jnp.exp(b_end + m - m_new)
    C_new = decay[..., None, None] * C + jnp.einsum('bhs,bhsd,bhsv->bhdv', w, k, v)
    n_new = decay[..., None] * n + jnp.einsum('bhs,bhsd->bhd', w, k)
    return (C_new, n_new, m_new), h


def mlstm_mixer(u, o_pre, z, i_pre, f_pre, conv_buf, C0, n0, m0,
                conv_w, conv_b, wq, wk, wv, b_i, b_f, norm_w):
    B, T, _ = u.shape
    c, new_buf = causal_conv(u, conv_buf, conv_w, conv_b)
    ch = jax.nn.silu(c).reshape(B, T, M_HEADS, M_DH)
    uh = u.reshape(B, T, M_HEADS, M_DH)
    q = jnp.einsum('bthc,hcd->bhtd', ch, wq).astype(jnp.float32)
    k = jnp.einsum('bthc,hcd->bhtd', ch, wk).astype(jnp.float32) * (M_DH ** -0.5)
    v = jnp.einsum('bthc,hcd->bhtd', uh, wv).astype(jnp.float32)
    li = (i_pre + b_i).astype(jnp.float32).transpose(0, 2, 1)
    lf = jax.nn.log_sigmoid((f_pre + b_f).astype(jnp.float32)).transpose(0, 2, 1)
    L = M_CHUNK if T % M_CHUNK == 0 else T
    nc = T // L

    def to_chunks(a):
        return jnp.moveaxis(a.reshape(a.shape[:2] + (nc, L) + a.shape[3:]), 2, 0)

    (C, n, m), h = lax.scan(mlstm_chunk,
                            (C0.astype(jnp.float32), n0.astype(jnp.float32), m0.astype(jnp.float32)),
                            (to_chunks(q), to_chunks(k), to_chunks(v), to_chunks(li), to_chunks(lf)))
    h = jnp.moveaxis(h, 0, 2).reshape(B, M_HEADS, T, M_DH).transpose(0, 2, 1, 3)
    h = h * jax.nn.sigmoid(o_pre.astype(jnp.float32)).reshape(B, T, M_HEADS, M_DH)
    h = rms_norm(h, norm_w).reshape(B, T, M_WIDTH)
    out = (h * jax.nn.silu(z.astype(jnp.float32))).astype(u.dtype)
    return out, new_buf, C, n, m


def compress(rows, pe, w1, w2, k_norm):
    B, L = rows.shape[:2]
    n_seg = L // CMP_STRIDE
    seg = rows[:, :n_seg * CMP_STRIDE].reshape(B, n_seg, CMP_STRIDE, 2, A_KV, A_DH)
    w1r = w1.reshape(2, 2, CMP_STRIDE, A_DH, CMP_HID)
    per = jnp.einsum('bnjcgd,cxjdh->bnxcgh', seg, w1r)
    bias = jnp.einsum('cxjd,cxjdh->ch', pe.reshape(2, 2, CMP_STRIDE, A_DH), w1r)
    hid = per[:, :-1, 0] + per[:, 1:, 1] + bias[:, None, :]
    out = jnp.einsum('bncgh,chd->bncgd', jax.nn.silu(hid), w2)
    return rms_norm(out[:, :, 0], k_norm), out[:, :, 1]


def cmp_end_positions(n_cmp):
    return jnp.arange(n_cmp, dtype=jnp.int32) * CMP_STRIDE + (CMP_LEN - 1)


def nsa_compressed(q, tq, kc, vc, c_end, n_slc):
    Tq = q.shape[0]
    qg = q.reshape(Tq, A_KV, A_GROUP, A_DH)
    s = jnp.einsum('tgad,ngd->tgan', qg, kc) * (A_DH ** -0.5)
    p = masked_softmax(s, (c_end[None, :] <= tq[:, None])[:, None, None, :])
    o = jnp.einsum('tgan,ngd->tgad', p, vc)
    pg = jnp.sum(p, axis=2)
    n_cmp = kc.shape[0]
    r = SLC_BLK // CMP_STRIDE
    pad_back = r * (n_slc + 1) - 1 - n_cmp
    pp = jnp.pad(pg, ((0, 0), (0, 0), (1, pad_back))).reshape(Tq, A_KV, n_slc + 1, r)
    ps = jnp.sum(pp[..., :-1, :], axis=-1) + pp[..., 1:, 0]
    blk = jnp.arange(n_slc, dtype=jnp.int32)[None, :]
    valid = blk * SLC_BLK <= tq[:, None]
    cur = (tq // SLC_BLK)[:, None]
    forced = (blk == 0) | (blk == cur) | (blk == cur - 1)
    score = jnp.where(valid[:, None], jnp.where(forced[:, None], jnp.inf, ps), -jnp.inf)
    _, idx = lax.top_k(score, min(SLC_TOPN, n_slc))
    return o, idx


def nsa_combine(q, tq, o_cmp, idx, k_sel, v_sel, kw, vw, pw, g):
    Tq = q.shape[0]
    qg = q.reshape(Tq, A_KV, A_GROUP, A_DH)
    scale = A_DH ** -0.5
    pos = idx[..., None] * SLC_BLK + jnp.arange(SLC_BLK, dtype=jnp.int32)
    s = jnp.einsum('tgad,tgmd->tgam', qg, k_sel.reshape(Tq, A_KV, -1, A_DH)) * scale
    p = masked_softmax(s, (pos <= tq[:, None, None, None]).reshape(Tq, A_KV, 1, -1))
    o_sel = jnp.einsum('tgam,tgmd->tgad', p, v_sel.reshape(Tq, A_KV, -1, A_DH))
    diff = tq[:, None] - pw[None, :]
    wmask = (pw[None, :] >= 0) & (diff >= 0) & (diff < WINDOW)
    s = jnp.einsum('tgad,sgd->tgas', qg, kw) * scale
    p = masked_softmax(s, wmask[:, None, None, :])
    o_win = jnp.einsum('tgas,sgd->tgad', p, vw)
    gg = g.astype(jnp.float32).reshape(Tq, 3, A_KV, A_GROUP)[..., None]
    o = gg[:, 0] * o_cmp + gg[:, 1] * o_sel + gg[:, 2] * o_win
    return o.reshape(Tq, A_WIDTH)


def nsa_prompt(q, g, kv, cmp_pe, cmp_w1, cmp_w2, k_norm_cmp):
    B, T = q.shape[:2]
    kc, vc = compress(kv[:, :, 0:2], cmp_pe, cmp_w1, cmp_w2, k_norm_cmp)
    c_end = cmp_end_positions(kc.shape[1])
    n_slc = T // SLC_BLK
    k_blk = kv[:, :, 2].reshape(B, n_slc, SLC_BLK, A_KV, A_DH).transpose(0, 3, 1, 2, 4)
    v_blk = kv[:, :, 3].reshape(B, n_slc, SLC_BLK, A_KV, A_DH).transpose(0, 3, 1, 2, 4)
    win = jnp.pad(kv[:, :, 4:6], ((0, 0), (WINDOW, 0), (0, 0), (0, 0), (0, 0)))
    nqb = T // Q_BLOCK
    qr = q.astype(jnp.float32).reshape(B, nqb, Q_BLOCK, A_HEADS, A_DH)
    gr = g.reshape(B, nqb, Q_BLOCK, 3, A_HEADS)
    g_ax = jnp.arange(A_KV)[None, :, None]

    def one(ids):
        b, qb = ids[0], ids[1]
        t0 = qb * Q_BLOCK
        tq = t0 + jnp.arange(Q_BLOCK, dtype=jnp.int32)
        qq = qr[b, qb]
        o_cmp, idx = nsa_compressed(qq, tq, kc[b], vc[b], c_end, n_slc)
        k_sel = k_blk[b][g_ax, idx]
        v_sel = v_blk[b][g_ax, idx]
        w = lax.dynamic_slice_in_dim(win[b], t0, Q_BLOCK + WINDOW, axis=0)
        pw = t0 - WINDOW + jnp.arange(Q_BLOCK + WINDOW, dtype=jnp.int32)
        return nsa_combine(qq, tq, o_cmp, idx, k_sel, v_sel, w[:, 0], w[:, 1], pw, gr[b, qb])

    ids = jnp.stack([jnp.repeat(jnp.arange(B, dtype=jnp.int32), nqb),
                     jnp.tile(jnp.arange(nqb, dtype=jnp.int32), B)], axis=-1)
    return lax.map(one, ids).reshape(B, T, A_WIDTH)


def nsa_sample(q, g, kv, cache_kv, win_buf, page_table, layer, cmp_pe, cmp_w1, cmp_w2, k_norm_cmp):
    Bd, Ts = q.shape[:2]
    n_pages = page_table.shape[1]
    past_len = n_pages * PAGE_SIZE
    past = cache_kv[layer, page_table, :, 0:2].reshape(Bd, past_len, 2, A_KV, A_DH).astype(kv.dtype)
    kc, vc = compress(jnp.concatenate([past, kv[:, :, 0:2]], axis=1), cmp_pe, cmp_w1, cmp_w2, k_norm_cmp)
    c_end = cmp_end_positions(kc.shape[1])
    n_past_blk = past_len // SLC_BLK
    n_new_blk = -(-Ts // SLC_BLK)
    n_slc = n_past_blk + n_new_blk
    blk_per_page = PAGE_SIZE // SLC_BLK
    new_blk = jnp.pad(kv[:, :, 2:4], ((0, 0), (0, n_new_blk * SLC_BLK - Ts), (0, 0), (0, 0), (0, 0)))
    new_blk = new_blk.reshape(Bd, n_new_blk, SLC_BLK, 2, A_KV, A_DH)
    wb = win_buf.shape[1]
    tq = past_len + jnp.arange(Ts, dtype=jnp.int32)
    pw = past_len - wb + jnp.arange(wb + Ts, dtype=jnp.int32)
    g_ax = jnp.arange(A_KV)[None, :, None, None]
    r_blk = jnp.arange(SLC_BLK, dtype=jnp.int32)

    def one(qq, gg, kc_b, vc_b, new_b, kvw_b, win_b, pt_row):
        o_cmp, idx = nsa_compressed(qq, tq, kc_b, vc_b, c_end, n_slc)
        pidx = jnp.minimum(idx, n_past_blk - 1)
        page = pt_row[pidx // blk_per_page][..., None]
        rows = ((pidx % blk_per_page) * SLC_BLK)[..., None] + r_blk
        nidx = jnp.clip(idx - n_past_blk, 0, n_new_blk - 1)[..., None]
        is_new = (idx >= n_past_blk)[..., None, None]
        k_sel = jnp.where(is_new, new_b[nidx, r_blk, 0, g_ax], cache_kv[layer, page, rows, 2, g_ax])
        v_sel = jnp.where(is_new, new_b[nidx, r_blk, 1, g_ax], cache_kv[layer, page, rows, 3, g_ax])
        kw = jnp.concatenate([win_b[:, 0].astype(kvw_b.dtype), kvw_b[:, 0]], axis=0)
        vw = jnp.concatenate([win_b[:, 1].astype(kvw_b.dtype), kvw_b[:, 1]], axis=0)
        return nsa_combine(qq, tq, o_cmp, idx, k_sel, v_sel, kw, vw, pw, gg)

    return jax.vmap(one)(q.astype(jnp.float32), g, kc, vc, new_blk, kv[:, :, 4:6], win_buf, page_table)


def mixer_inputs(x, norm_w, w_in, a_q_norm, a_k_norm):
    B, T, _ = x.shape
    xn = rms_norm(x, norm_w)
    u, o, mz, ig, fg, q, kv, g, az = jnp.split(xn @ w_in, SPLIT_AT, axis=-1)
    q = rms_norm(q.reshape(B, T, A_HEADS, A_DH), a_q_norm)
    kv = kv.reshape(B, T, 6, A_KV, A_DH)
    kv = jnp.stack([kv[:, :, 0], kv[:, :, 1], rms_norm(kv[:, :, 2], a_k_norm[1]), kv[:, :, 3],
                    rms_norm(kv[:, :, 4], a_k_norm[2]), kv[:, :, 5]], axis=2)
    g = jax.nn.sigmoid(g).reshape(B, T, 3, A_HEADS)
    return (u, o, mz, ig, fg), (q, kv, g, az)


def mixer_output(x, m_out, a_out, p, w_out, ple_proj, ple_norm, ple_gate):
    h = x + jnp.concatenate([m_out, a_out.astype(x.dtype)], axis=-1) @ w_out
    gate = jax.nn.sigmoid(rms_norm(h, ple_norm) @ ple_gate)
    return h + gate * (p @ ple_proj)


def setup_inputs(seed: int = 0) -> dict:
    key = jax.random.key(seed)
    ks = iter(jax.random.split(key, 40))
    f32 = jnp.float32

    def nrm(shape, scale):
        return jax.random.normal(next(ks), shape, f32) * scale

    n_pages = PAST_LEN // PAGE_SIZE
    n_used = DEC_BATCH * n_pages
    n_pool = n_used + (n_used + 3) // 4
    wb = min(WINDOW, PAST_LEN)
    page_table = jax.random.permutation(next(ks), n_pool)[:n_used].reshape(DEC_BATCH, n_pages).astype(jnp.int32)
    return {
        "x_prompt": nrm((BATCH, SEQ, D_MODEL), 1.0),
        "x_sample": nrm((DEC_BATCH, DEC_SEQ, D_MODEL), 1.0),
        "cache_nsa_kv": nrm((DEPTH, n_pool, PAGE_SIZE, 4, A_KV, A_DH), 1.0),
        "cache_win_kv": nrm((DEPTH, DEC_BATCH, wb, 2, A_KV, A_DH), 1.0),
        "state_mlstm_C": nrm((DEPTH, DEC_BATCH, M_HEADS, M_DH, M_DH), 1.0),
        "state_mlstm_n": nrm((DEPTH, DEC_BATCH, M_HEADS, M_DH), 1.0),
        "state_mlstm_m": nrm((DEPTH, DEC_BATCH, M_HEADS), 1.0),
        "state_mlstm_conv": nrm((DEPTH, DEC_BATCH, M_CONV - 1, M_WIDTH), 1.0),
        "page_table": page_table,
        "p_prompt": nrm((DEPTH, BATCH, SEQ, D_PLE), 1.0),
        "p_sample": nrm((DEPTH, DEC_BATCH, DEC_SEQ, D_PLE), 1.0),
        "norm_w": 1.0 + nrm((DEPTH, D_MODEL), 0.02),
        "w_in": nrm((DEPTH, D_MODEL, D_IN), D_MODEL ** -0.5),
        "m_conv_w": nrm((DEPTH, M_CONV, M_WIDTH), M_CONV ** -0.5),
        "m_conv_b": nrm((DEPTH, M_WIDTH), 0.02),
        "m_wq": nrm((DEPTH, M_HEADS, M_DH, M_DH), M_DH ** -0.5),
        "m_wk": nrm((DEPTH, M_HEADS, M_DH, M_DH), M_DH ** -0.5),
        "m_wv": nrm((DEPTH, M_HEADS, M_DH, M_DH), M_DH ** -0.5),
        "m_b_i": nrm((DEPTH, M_HEADS), 0.1),
        "m_b_f": jnp.linspace(3.0, 6.0, M_HEADS, dtype=f32)[None, :] + nrm((DEPTH, M_HEADS), 0.1),
        "m_norm_w": 1.0 + nrm((DEPTH, M_HEADS, M_DH), 0.02),
        "a_q_norm": 1.0 + nrm((DEPTH, A_DH), 0.02),
        "a_k_norm": 1.0 + nrm((DEPTH, 3, A_DH), 0.02),
        "cmp_pe": nrm((DEPTH, 2, CMP_LEN, A_DH), 0.02),
        "cmp_w1": nrm((DEPTH, 2, CMP_LEN, A_DH, CMP_HID), (CMP_LEN * A_DH) ** -0.5),
        "cmp_w2": nrm((DEPTH, 2, CMP_HID, A_DH), CMP_HID ** -0.5),
        "w_out": nrm((DEPTH, M_WIDTH + A_WIDTH, D_MODEL), (M_WIDTH + A_WIDTH) ** -0.5),
        "ple_proj": nrm((DEPTH, D_PLE, D_MODEL), D_PLE ** -0.5),
        "ple_norm": 1.0 + nrm((DEPTH, D_MODEL), 0.02),
        "ple_gate": nrm((DEPTH, D_MODEL, D_MODEL), D_MODEL ** -0.5),
    }


def reference(x_prompt, x_sample, cache_nsa_kv, cache_win_kv, state_mlstm_C, state_mlstm_n,
              state_mlstm_m, state_mlstm_conv, page_table, p_prompt, p_sample, norm_w, w_in,
              m_conv_w, m_conv_b, m_wq, m_wk, m_wv, m_b_i, m_b_f, m_norm_w, a_q_norm, a_k_norm,
              cmp_pe, cmp_w1, cmp_w2, w_out, ple_proj, ple_norm, ple_gate):
    xp, xs = x_prompt, x_sample
    B, T = xp.shape[:2]
    per_layer = []
    for i in range(DEPTH):
        mw = (m_conv_w[i], m_conv_b[i], m_wq[i], m_wk[i], m_wv[i], m_b_i[i], m_b_f[i], m_norm_w[i])
        cw = (cmp_pe[i], cmp_w1[i], cmp_w2[i], a_k_norm[i, 0])
        ow = (w_out[i], ple_proj[i], ple_norm[i], ple_gate[i])
        (u, o, mz, ig, fg), (q, kv, g, az) = mixer_inputs(xp, norm_w[i], w_in[i], a_q_norm[i], a_k_norm[i])
        m_out, conv_p, C_p, n_p, m_p = mlstm_mixer(
            u, o, mz, ig, fg, jnp.zeros((B, M_CONV - 1, M_WIDTH), u.dtype),
            jnp.zeros((B, M_HEADS, M_DH, M_DH), jnp.float32), jnp.zeros((B, M_HEADS, M_DH), jnp.float32),
            jnp.full((B, M_HEADS), -jnp.inf, jnp.float32), *mw)
        a_out = nsa_prompt(q, g, kv, *cw) * jax.nn.silu(az.astype(jnp.float32))
        xp_new = mixer_output(xp, m_out, a_out, p_prompt[i], *ow)
        kv_p = kv[:, :, 0:4]
        win_p = kv[:, -min(WINDOW, T):, 4:6]
        (u, o, mz, ig, fg), (q, kv, g, az) = mixer_inputs(xs, norm_w[i], w_in[i], a_q_norm[i], a_k_norm[i])
        m_out, conv_s, C_s, n_s, m_s = mlstm_mixer(
            u, o, mz, ig, fg, state_mlstm_conv[i], state_mlstm_C[i], state_mlstm_n[i], state_mlstm_m[i], *mw)
        a_out = nsa_sample(q, g, kv, cache_nsa_kv, cache_win_kv[i], page_table, i, *cw)
        a_out = a_out * jax.nn.silu(az.astype(jnp.float32))
        xs_new = mixer_output(xs, m_out, a_out, p_sample[i], *ow)
        kv_s = kv[:, :, 0:4]
        wb = cache_win_kv.shape[2]
        win_s = jnp.concatenate([cache_win_kv[i].astype(kv.dtype), kv[:, :, 4:6]], axis=1)[:, -wb:]
        per_layer.append((kv_p, kv_s, win_p, win_s, C_p, n_p, m_p, conv_p, C_s, n_s, m_s, conv_s))
        xp, xs = xp_new, xs_new
    (kv_p, kv_s, win_p, win_s, C_p, n_p, m_p, conv_p,
     C_s, n_s, m_s, conv_s) = [jnp.stack(a, axis=0) for a in zip(*per_layer)]
    return (xp, xs, kv_p, kv_s, win_p, win_s, C_p, n_p, m_p, conv_p, C_s, n_s, m_s, conv_s)
```

```python
import functools
import math

import jax
import jax.numpy as jnp
from jax import lax
from jax.experimental import pallas as pl
from jax.experimental.pallas import tpu as pltpu

NORM_EPS = 1e-6
M_HEADS = 4
M_CONV = 4
M_CHUNK = 64
A_HEADS = 16
A_KV = 4
A_GROUP = A_HEADS // A_KV
CMP_STRIDE = 16
CMP_LEN = 2 * CMP_STRIDE
SLC_BLK = 64
SLC_TOPN = 16
WINDOW = 512
PAGE_SIZE = 128

_BF16 = jnp.bfloat16
_F32 = jnp.float32
_VMEM_LIMIT = 56 * 1024 * 1024


def _dot(a, b):
    return jnp.dot(a.astype(_BF16), b.astype(_BF16), preferred_element_type=_F32)


_PT = 512


def _proj_in_kernel(groups, norm_tiles, x_ref, nw_ref, w_ref, wsm_ref, seg_ref, hw_ref, hm_ref,
                    a_ref, q_ref, az_ref, kv4_ref, kvw_ref, sm_ref, xn_ref):
    j = pl.program_id(1)

    @pl.when(j == 0)
    def _():
        x = x_ref[...]
        ms = jnp.mean(x * x, axis=-1, keepdims=True)
        xn = x * lax.rsqrt(ms + NORM_EPS) * nw_ref[...]
        xn_ref[...] = xn.astype(_BF16)
        sm = jnp.dot(xn_ref[...], wsm_ref[...], preferred_element_type=_F32)
        lane = lax.broadcasted_iota(jnp.int32, sm.shape, 1)
        is_gate = (lane >= 2 * M_HEADS) & (lane < 2 * M_HEADS + 3 * A_HEADS)
        sm_ref[...] = jnp.where(is_gate, jax.nn.sigmoid(sm), sm)

    y = jnp.dot(xn_ref[...], w_ref[...], preferred_element_type=_F32)
    outs = (a_ref, q_ref, az_ref, kv4_ref, kvw_ref)
    for (start, n), o_ref in zip(groups, outs):
        plain = [t for t in range(start, start + n) if t not in norm_tiles]
        normed = [t for t in range(start, start + n) if t in norm_tiles]
        if plain:
            cond = functools.reduce(jnp.logical_or, [j == t for t in plain])

            @pl.when(cond)
            def _(o_ref=o_ref):
                o_ref[...] = y

        if normed:
            cond = functools.reduce(jnp.logical_or, [j == t for t in normed])

            @pl.when(cond)
            def _(o_ref=o_ref):
                ms = jnp.dot((y * y).astype(_BF16), seg_ref[...], preferred_element_type=_F32)
                yn = y * lax.rsqrt(ms + NORM_EPS) * hw_ref[...]
                o_ref[...] = jnp.where(hm_ref[...] > 0, yn, y)


def _proj_in(x2d, norm_w, w_main, w_small, seg, hw, hm, groups, norm_tiles, tm):
    M, D = x2d.shape
    n_tiles = w_main.shape[1] // _PT
    widths = [n * _PT for _, n in groups]

    def out_map(start, n):
        return lambda i, j: (i, jnp.clip(j - start, 0, n - 1))

    out_specs = [pl.BlockSpec((tm, _PT), out_map(s, n)) for s, n in groups]
    out_specs.append(pl.BlockSpec((tm, 128), lambda i, j: (i, 0)))
    out_shape = [jax.ShapeDtypeStruct((M, w), _F32) for w in widths]
    out_shape.append(jax.ShapeDtypeStruct((M, 128), _F32))
    return pl.pallas_call(
        functools.partial(_proj_in_kernel, groups, norm_tiles),
        grid=(M // tm, n_tiles),
        in_specs=[
            pl.BlockSpec((tm, D), lambda i, j: (i, 0)),
            pl.BlockSpec((1, D), lambda i, j: (0, 0)),
            pl.BlockSpec((D, _PT), lambda i, j: (0, j)),
            pl.BlockSpec((D, 128), lambda i, j: (0, 0)),
            pl.BlockSpec((_PT, _PT), lambda i, j: (0, 0)),
            pl.BlockSpec((None, 1, _PT), lambda i, j: (j, 0, 0)),
            pl.BlockSpec((None, 1, _PT), lambda i, j: (j, 0, 0)),
        ],
        out_specs=out_specs,
        out_shape=out_shape,
        scratch_shapes=[pltpu.VMEM((tm, D), _BF16)],
        compiler_params=pltpu.CompilerParams(
            dimension_semantics=("arbitrary", "arbitrary"), vmem_limit_bytes=_VMEM_LIMIT),
        name="proj_in",
    )(x2d, norm_w.reshape(1, D), w_main, w_small, seg, hw, hm)


def _prep_proj_weights(w_in, a_q_norm, a_k_norm, m_width, a_width, a_dh):
    kvw = A_KV * a_dh
    sizes = (m_width, m_width, m_width, M_HEADS, M_HEADS, a_width, 6 * kvw, 3 * A_HEADS, a_width)
    offs = [0]
    for s in sizes:
        offs.append(offs[-1] + s)
    u0, o0, z0, i0, f0, q0, kv0, g0, az0, end = offs
    assert end == w_in.shape[1]
    w_main = jnp.concatenate([
        w_in[:, u0:i0], w_in[:, q0:kv0], w_in[:, az0:end], w_in[:, kv0:kv0 + 4 * kvw],
        w_in[:, kv0 + 4 * kvw:g0]], axis=1).astype(_BF16)
    n_small = 2 * M_HEADS + 3 * A_HEADS
    w_small = jnp.concatenate([
        w_in[:, i0:q0], w_in[:, g0:az0],
        jnp.zeros((w_in.shape[0], 128 - n_small), w_in.dtype)], axis=1).astype(_BF16)
    widths = (3 * m_width, a_width, a_width, 4 * kvw, 2 * kvw)
    groups, start = [], 0
    for w in widths:
        assert w % _PT == 0
        groups.append((start, w // _PT))
        start += w // _PT
    n_tiles = start
    hw = jnp.zeros((n_tiles * _PT,), _F32)
    hm = jnp.zeros((n_tiles * _PT,), _F32)
    qs = groups[1][0] * _PT
    hw = hw.at[qs:qs + a_width].set(jnp.tile(a_q_norm, a_width // a_dh))
    hm = hm.at[qs:qs + a_width].set(1.0)
    k4 = groups[3][0] * _PT
    hw = hw.at[k4 + 2 * kvw:k4 + 3 * kvw].set(jnp.tile(a_k_norm[1], A_KV))
    hm = hm.at[k4 + 2 * kvw:k4 + 3 * kvw].set(1.0)
    kw = groups[4][0] * _PT
    hw = hw.at[kw:kw + kvw].set(jnp.tile(a_k_norm[2], A_KV))
    hm = hm.at[kw:kw + kvw].set(1.0)
    assert kvw * 2 == _PT and _PT % a_dh == 0
    norm_tiles = tuple(range(groups[1][0], groups[1][0] + groups[1][1])) + (
        (k4 + 2 * kvw) // _PT, kw // _PT)
    r = jnp.arange(_PT) // a_dh
    seg = ((r[:, None] == r[None, :]).astype(_F32) / a_dh).astype(_BF16)
    return (w_main, w_small, seg, hw.reshape(n_tiles, 1, _PT), hm.reshape(n_tiles, 1, _PT),
            tuple(groups), norm_tiles)


def _out_proj_kernel(mw, x_ref, m_ref, a_ref, p_ref, wo_ref, pp_ref, pn_ref, pg_ref, o_ref):
    h = x_ref[...] + _dot(m_ref[...], wo_ref[0:mw, :]) + _dot(a_ref[...], wo_ref[mw:, :])
    ms = jnp.mean(h * h, axis=-1, keepdims=True)
    hn = h * lax.rsqrt(ms + NORM_EPS) * pn_ref[...]
    gate = jax.nn.sigmoid(_dot(hn, pg_ref[...]))
    o_ref[...] = h + gate * _dot(p_ref[...], pp_ref[...])


def _out_proj(x2d, m_out, a_out, p2d, w_out, ple_proj, ple_norm, ple_gate, tm):
    M, D = x2d.shape
    mw, aw, dp = m_out.shape[1], a_out.shape[1], p2d.shape[1]

    def const(shape):
        return pl.BlockSpec(shape, lambda i: (0, 0), pipeline_mode=pl.Buffered(1))

    return pl.pallas_call(
        functools.partial(_out_proj_kernel, mw),
        grid=(M // tm,),
        in_specs=[
            pl.BlockSpec((tm, D), lambda i: (i, 0)),
            pl.BlockSpec((tm, mw), lambda i: (i, 0)),
            pl.BlockSpec((tm, aw), lambda i: (i, 0)),
            pl.BlockSpec((tm, dp), lambda i: (i, 0)),
            const((mw + aw, D)),
            const((dp, D)),
            const((1, D)),
            const((D, D)),
        ],
        out_specs=pl.BlockSpec((tm, D), lambda i: (i, 0)),
        out_shape=jax.ShapeDtypeStruct((M, D), _F32),
        compiler_params=pltpu.CompilerParams(
            dimension_semantics=("arbitrary",), vmem_limit_bytes=_VMEM_LIMIT),
        name="out_proj",
    )(x2d, m_out, a_out, p2d, w_out.astype(_BF16), ple_proj.astype(_BF16),
      ple_norm.reshape(1, D), ple_gate.astype(_BF16))


def _rms_norm(x, w):
    xf = x.astype(_F32)
    y = xf * lax.rsqrt(jnp.mean(xf * xf, axis=-1, keepdims=True) + NORM_EPS)
    return (y * w.astype(_F32)).astype(x.dtype)


def _masked_softmax(s, mask):
    s = jnp.where(mask, s.astype(_F32), -jnp.inf)
    m = jnp.max(s, axis=-1, keepdims=True)
    m = jnp.where(jnp.isfinite(m), m, 0.0)
    e = jnp.where(mask, jnp.exp(s - m), 0.0)
    return e / jnp.maximum(jnp.sum(e, axis=-1, keepdims=True), 1e-30)


def _mlstm_chunk(carry, inp):
    C, n, m = carry
    q, k, v, li, lf = inp
    L = q.shape[2]
    b = jnp.cumsum(lf, axis=-1)
    causal = jnp.tril(jnp.ones((L, L), dtype=bool))
    logd = jnp.where(causal, b[..., :, None] - b[..., None, :] + li[..., None, :], -jnp.inf)
    inter = b + m[..., None]
    m_t = jnp.maximum(inter, jnp.max(logd, axis=-1))
    s = jnp.einsum('bhtd,bhsd->bhts', q, k) * jnp.exp(logd - m_t[..., None])
    a = jnp.exp(inter - m_t)
    num = jnp.einsum('bhts,bhsv->bhtv', s, v) + a[..., None] * jnp.einsum('bhtd,bhdv->bhtv', q, C)
    den = jnp.sum(s, axis=-1) + a * jnp.einsum('bhtd,bhd->bht', q, n)
    h = num / jnp.maximum(jnp.abs(den), jnp.exp(-m_t))[..., None]
    b_end = b[..., -1]
    logw = b_end[..., None] - b + li
    m_new = jnp.maximum(b_end + m, jnp.max(logw, axis=-1))
    w = jnp.exp(logw - m_new[..., None])
    decay = jnp.exp(b_end + m - m_new)
    C_new = decay[..., None, None] * C + jnp.einsum('bhs,bhsd,bhsv->bhdv', w, k, v)
    n_new = decay[..., None] * n + jnp.einsum('bhs,bhsd->bhd', w, k)
    return (C_new, n_new, m_new), h


def _mlstm_mixer(u, o_pre, z, i_pre, f_pre, conv_buf, C0, n0, m0,
                 conv_w, conv_b, wq, wk, wv, b_i, b_f, norm_w):
    B, T, mw = u.shape
    dh = mw // M_HEADS
    full = jnp.concatenate([conv_buf.astype(u.dtype), u], axis=1)
    c = conv_b + sum(full[:, j:j + T] * conv_w[j] for j in range(M_CONV))
    new_buf = full[:, T:]
    ch = jax.nn.silu(c).reshape(B, T, M_HEADS, dh)
    uh = u.reshape(B, T, M_HEADS, dh)
    q = jnp.einsum('bthc,hcd->bhtd', ch, wq).astype(_F32)
    k = jnp.einsum('bthc,hcd->bhtd', ch, wk).astype(_F32) * (dh ** -0.5)
    v = jnp.einsum('bthc,hcd->bhtd', uh, wv).astype(_F32)
    li = (i_pre + b_i).astype(_F32).transpose(0, 2, 1)
    lf = jax.nn.log_sigmoid((f_pre + b_f).astype(_F32)).transpose(0, 2, 1)
    L = M_CHUNK if T % M_CHUNK == 0 else T
    nc = T // L

    def to_chunks(a):
        return jnp.moveaxis(a.reshape(a.shape[:2] + (nc, L) + a.shape[3:]), 2, 0)

    (C, n, m), h = lax.scan(_mlstm_chunk, (C0.astype(_F32), n0.astype(_F32), m0.astype(_F32)),
                            (to_chunks(q), to_chunks(k), to_chunks(v), to_chunks(li), to_chunks(lf)))
    h = jnp.moveaxis(h, 0, 2).reshape(B, M_HEADS, T, dh).transpose(0, 2, 1, 3)
    h = h * jax.nn.sigmoid(o_pre.astype(_F32)).reshape(B, T, M_HEADS, dh)
    h = _rms_norm(h, norm_w).reshape(B, T, mw)
    out = (h * jax.nn.silu(z.astype(_F32))).astype(u.dtype)
    return out, new_buf, C, n, m


def _compress(rows, pe, w1, w2, k_norm):
    B, L = rows.shape[:2]
    a_dh = rows.shape[-1]
    hid_n = w1.shape[-1]
    n_seg = L // CMP_STRIDE
    seg = rows[:, :n_seg * CMP_STRIDE].reshape(B, n_seg, CMP_STRIDE, 2, A_KV, a_dh)
    w1r = w1.reshape(2, 2, CMP_STRIDE, a_dh, hid_n)
    per = jnp.einsum('bnjcgd,cxjdh->bnxcgh', seg, w1r)
    bias = jnp.einsum('cxjd,cxjdh->ch', pe.reshape(2, 2, CMP_STRIDE, a_dh), w1r)
    hid = per[:, :-1, 0] + per[:, 1:, 1] + bias[:, None, :]
    out = jnp.einsum('bncgh,chd->bncgd', jax.nn.silu(hid), w2)
    return _rms_norm(out[:, :, 0], k_norm), out[:, :, 1]


def _select_blocks(pg, tq, n_slc):
    Tq, _, n_cmp = pg.shape
    r = SLC_BLK // CMP_STRIDE
    pad_back = r * (n_slc + 1) - 1 - n_cmp
    pp = jnp.pad(pg, ((0, 0), (0, 0), (1, pad_back))).reshape(Tq, A_KV, n_slc + 1, r)
    ps = jnp.sum(pp[..., :-1, :], axis=-1) + pp[..., 1:, 0]
    blk = jnp.arange(n_slc, dtype=jnp.int32)[None, :]
    valid = blk * SLC_BLK <= tq[:, None]
    cur = (tq // SLC_BLK)[:, None]
    forced = (blk == 0) | (blk == cur) | (blk == cur - 1)
    score = jnp.where(valid[:, None], jnp.where(forced[:, None], jnp.inf, ps), -jnp.inf)
    si = score[..., None, :]
    sj = score[..., :, None]
    idx = jnp.arange(n_slc, dtype=jnp.int32)
    ahead = (si > sj) | ((si == sj) & (idx[None, :] < idx[:, None]))
    rank = jnp.sum(ahead, axis=-1)
    return rank < min(SLC_TOPN, n_slc)


def _nsa_prompt_one(q, g, kv4, kvw, kc, vc):
    T, _, dh = q.shape
    scale = dh ** -0.5
    tq = jnp.arange(T, dtype=jnp.int32)
    qg = q.reshape(T, A_KV, A_GROUP, dh)
    n_cmp = kc.shape[0]
    c_end = jnp.arange(n_cmp, dtype=jnp.int32) * CMP_STRIDE + (CMP_LEN - 1)
    s = jnp.einsum('tgad,ngd->tgan', qg, kc) * scale
    p = _masked_softmax(s, (c_end[None, :] <= tq[:, None])[:, None, None, :])
    o_cmp = jnp.einsum('tgan,ngd->tgad', p, vc)
    n_slc = T // SLC_BLK
    sel = _select_blocks(jnp.sum(p, axis=2), tq, n_slc)
    kpos = jnp.arange(T, dtype=jnp.int32)
    kmask = sel[:, :, kpos // SLC_BLK] & (kpos[None, None, :] <= tq[:, None, None])
    s = jnp.einsum('tgad,sgd->tgas', qg, kv4[:, 2]) * scale
    p = _masked_softmax(s, kmask[:, :, None, :])
    o_sel = jnp.einsum('tgas,sgd->tgad', p, kv4[:, 3])
    diff = tq[:, None] - kpos[None, :]
    wmask = (diff >= 0) & (diff < WINDOW)
    s = jnp.einsum('tgad,sgd->tgas', qg, kvw[:, 0]) * scale
    p = _masked_softmax(s, wmask[:, None, None, :])
    o_win = jnp.einsum('tgas,sgd->tgad', p, kvw[:, 1])
    gg = g.reshape(T, 3, A_KV, A_GROUP)[..., None]
    o = gg[:, 0] * o_cmp + gg[:, 1] * o_sel + gg[:, 2] * o_win
    return o.reshape(T, -1)


def _nsa_combine(q, tq, o_cmp, idx, k_sel, v_sel, kw, vw, pw, g):
    Tq, _, dh = q.shape
    qg = q.reshape(Tq, A_KV, A_GROUP, dh)
    scale = dh ** -0.5
    pos = idx[..., None] * SLC_BLK + jnp.arange(SLC_BLK, dtype=jnp.int32)
    s = jnp.einsum('tgad,tgmd->tgam', qg, k_sel.reshape(Tq, A_KV, -1, dh)) * scale
    p = _masked_softmax(s, (pos <= tq[:, None, None, None]).reshape(Tq, A_KV, 1, -1))
    o_sel = jnp.einsum('tgam,tgmd->tgad', p, v_sel.reshape(Tq, A_KV, -1, dh))
    diff = tq[:, None] - pw[None, :]
    wmask = (pw[None, :] >= 0) & (diff >= 0) & (diff < WINDOW)
    s = jnp.einsum('tgad,sgd->tgas', qg, kw) * scale
    p = _masked_softmax(s, wmask[:, None, None, :])
    o_win = jnp.einsum('tgas,sgd->tgad', p, vw)
    gg = g.astype(_F32).reshape(Tq, 3, A_KV, A_GROUP)[..., None]
    o = gg[:, 0] * o_cmp + gg[:, 1] * o_sel + gg[:, 2] * o_win
    return o.reshape(Tq, -1)


def _nsa_compressed(q, tq, kc, vc, c_end, n_slc):
    Tq, _, dh = q.shape
    qg = q.reshape(Tq, A_KV, A_GROUP, dh)
    s = jnp.einsum('tgad,ngd->tgan', qg, kc) * (dh ** -0.5)
    p = _masked_softmax(s, (c_end[None, :] <= tq[:, None])[:, None, None, :])
    o = jnp.einsum('tgan,ngd->tgad', p, vc)
    sel = _select_blocks(jnp.sum(p, axis=2), tq, n_slc)
    score = jnp.where(sel, 1.0, 0.0)
    _, idx = lax.top_k(score, min(SLC_TOPN, n_slc))
    return o, idx


def _nsa_sample(q, g, kv4, kvw, cache_kv, win_buf, page_table, layer, cmp_pe, cmp_w1, cmp_w2, k_norm_cmp):
    Bd, Ts, _, dh = q.shape
    n_pages = page_table.shape[1]
    past_len = n_pages * PAGE_SIZE
    past = cache_kv[layer, page_table, :, 0:2].reshape(Bd, past_len, 2, A_KV, dh)
    kc, vc = _compress(jnp.concatenate([past, kv4[:, :, 0:2]], axis=1), cmp_pe, cmp_w1, cmp_w2, k_norm_cmp)
    n_cmp = kc.shape[1]
    c_end = jnp.arange(n_cmp, dtype=jnp.int32) * CMP_STRIDE + (CMP_LEN - 1)
    n_past_blk = past_len // SLC_BLK
    n_new_blk = -(-Ts // SLC_BLK)
    n_slc = n_past_blk + n_new_blk
    blk_per_page = PAGE_SIZE // SLC_BLK
    new_blk = jnp.pad(kv4[:, :, 2:4], ((0, 0), (0, n_new_blk * SLC_BLK - Ts), (0, 0), (0, 0), (0, 0)))
    new_blk = new_blk.reshape(Bd, n_new_blk, SLC_BLK, 2, A_KV, dh)
    wb = win_buf.shape[1]
    tq = past_len + jnp.arange(Ts, dtype=jnp.int32)
    pw = past_len - wb + jnp.arange(wb + Ts, dtype=jnp.int32)
    g_ax = jnp.arange(A_KV)[None, :, None, None]
    r_blk = jnp.arange(SLC_BLK, dtype=jnp.int32)

    def one(qq, gg, kc_b, vc_b, new_b, kvw_b, win_b, pt_row):
        o_cmp, idx = _nsa_compressed(qq, tq, kc_b, vc_b, c_end, n_slc)
        pidx = jnp.minimum(idx, n_past_blk - 1)
        page = pt_row[pidx // blk_per_page][..., None]
        rows = ((pidx % blk_per_page) * SLC_BLK)[..., None] + r_blk
        nidx = jnp.clip(idx - n_past_blk, 0, n_new_blk - 1)[..., None]
        is_new = (idx >= n_past_blk)[..., None, None]
        k_sel = jnp.where(is_new, new_b[nidx, r_blk, 0, g_ax], cache_kv[layer, page, rows, 2, g_ax])
        v_sel = jnp.where(is_new, new_b[nidx, r_blk, 1, g_ax], cache_kv[layer, page, rows, 3, g_ax])
        kw = jnp.concatenate([win_b[:, 0], kvw_b[:, 0]], axis=0)
        vw = jnp.concatenate([win_b[:, 1], kvw_b[:, 1]], axis=0)
        return _nsa_combine(qq, tq, o_cmp, idx, k_sel, v_sel, kw, vw, pw, gg)

    return jax.vmap(one)(q, g, kc, vc, new_blk, kvw, win_buf, page_table)


def _mixer_inputs(x, norm_w, pw, m_width, a_width, a_dh):
    B, T, D = x.shape
    w_main, w_small, seg, hw, hm, groups, norm_tiles = pw
    M = B * T
    tm = 512 if M % 512 == 0 else M
    a, q, az, kv4, kvw, sm = _proj_in(x.reshape(M, D), norm_w, w_main, w_small, seg, hw, hm,
                                      groups, norm_tiles, tm)
    a = a.reshape(B, T, 3 * m_width)
    u, o, z = a[..., :m_width], a[..., m_width:2 * m_width], a[..., 2 * m_width:]
    sm = sm.reshape(B, T, 128)
    ig, fg = sm[..., :M_HEADS], sm[..., M_HEADS:2 * M_HEADS]
    g = sm[..., 2 * M_HEADS:2 * M_HEADS + 3 * A_HEADS].reshape(B, T, 3, A_HEADS)
    q = q.reshape(B, T, A_HEADS, a_dh)
    kv4 = kv4.reshape(B, T, 4, A_KV, a_dh)
    kvw = kvw.reshape(B, T, 2, A_KV, a_dh)
    return (u, o, z, ig, fg), (q, kv4, kvw, g, az.reshape(B, T, a_width))


def _mixer_output(x, m_out, a_out, p, ow):
    B, T, D = x.shape
    M = B * T
    tm = 256 if M % 256 == 0 else M
    y = _out_proj(x.reshape(M, D), m_out.reshape(M, -1), a_out.reshape(M, -1), p.reshape(M, -1), *ow, tm)
    return y.reshape(B, T, D)


def kernel(x_prompt, x_sample, cache_nsa_kv, cache_win_kv, state_mlstm_C, state_mlstm_n,
           state_mlstm_m, state_mlstm_conv, page_table, p_prompt, p_sample, norm_w, w_in,
           m_conv_w, m_conv_b, m_wq, m_wk, m_wv, m_b_i, m_b_f, m_norm_w, a_q_norm, a_k_norm,
           cmp_pe, cmp_w1, cmp_w2, w_out, ple_proj, ple_norm, ple_gate):
    xp, xs = x_prompt, x_sample
    B, T, D = xp.shape
    depth = w_in.shape[0]
    m_width = m_conv_w.shape[-1]
    a_width = D - m_width
    a_dh = a_q_norm.shape[-1]
    dh = m_width // M_HEADS
    per_layer = []
    for i in range(depth):
        mw = (m_conv_w[i], m_conv_b[i], m_wq[i], m_wk[i], m_wv[i], m_b_i[i], m_b_f[i], m_norm_w[i])
        cw = (cmp_pe[i], cmp_w1[i], cmp_w2[i], a_k_norm[i, 0])
        ow = (w_out[i], ple_proj[i], ple_norm[i], ple_gate[i])
        pw = _prep_proj_weights(w_in[i], a_q_norm[i], a_k_norm[i], m_width, a_width, a_dh)
        (u, o, z, ig, fg), (q, kv4, kvw, g, az) = _mixer_inputs(xp, norm_w[i], pw, m_width, a_width, a_dh)
        m_out, conv_p, C_p, n_p, m_p = _mlstm_mixer(
            u, o, z, ig, fg, jnp.zeros((B, M_CONV - 1, m_width), _F32),
            jnp.zeros((B, M_HEADS, dh, dh), _F32), jnp.zeros((B, M_HEADS, dh), _F32),
            jnp.full((B, M_HEADS), -jnp.inf, _F32), *mw)
        kc, vc = _compress(kv4[:, :, 0:2], *cw)
        a_att = lax.map(lambda t: _nsa_prompt_one(*t), (q, g, kv4, kvw, kc, vc))
        a_out = a_att * jax.nn.silu(az)
        xp_new = _mixer_output(xp, m_out, a_out, p_prompt[i], ow)
        kv_p = kv4
        win_p = kvw[:, -min(WINDOW, T):]
        (u, o, z, ig, fg), (q, kv4, kvw, g, az) = _mixer_inputs(xs, norm_w[i], pw, m_width, a_width, a_dh)
        m_out, conv_s, C_s, n_s, m_s = _mlstm_mixer(
            u, o, z, ig, fg, state_mlstm_conv[i], state_mlstm_C[i], state_mlstm_n[i], state_mlstm_m[i], *mw)
        a_att = _nsa_sample(q, g, kv4, kvw, cache_nsa_kv, cache_win_kv[i], page_table, i, *cw)
        a_out = a_att * jax.nn.silu(az)
        xs_new = _mixer_output(xs, m_out, a_out, p_sample[i], ow)
        kv_s = kv4
        wb = cache_win_kv.shape[2]
        win_s = jnp.concatenate([cache_win_kv[i], kvw], axis=1)[:, -wb:]
        per_layer.append((kv_p, kv_s, win_p, win_s, C_p, n_p, m_p, conv_p, C_s, n_s, m_s, conv_s))
        xp, xs = xp_new, xs_new
    (kv_p, kv_s, win_p, win_s, C_p, n_p, m_p, conv_p,
     C_s, n_s, m_s, conv_s) = [jnp.stack(a, axis=0) for a in zip(*per_layer)]
    return (xp, xs, kv_p, kv_s, win_p, win_s, C_p, n_p, m_p, conv_p, C_s, n_s, m_s, conv_s)
```

```python
import functools
import math

import jax
import jax.numpy as jnp
from jax import lax
from jax.experimental import pallas as pl
from jax.experimental.pallas import tpu as pltpu

NORM_EPS = 1e-6
M_HEADS = 4
M_CONV = 4
M_CHUNK = 64
A_HEADS = 16
A_KV = 4
A_GROUP = A_HEADS // A_KV
CMP_STRIDE = 16
CMP_LEN = 2 * CMP_STRIDE
SLC_BLK = 64
SLC_TOPN = 16
WINDOW = 512
PAGE_SIZE = 128

_BF16 = jnp.bfloat16
_F32 = jnp.float32
_VMEM_LIMIT = 56 * 1024 * 1024


def _dot(a, b):
    return jnp.dot(a.astype(_BF16), b.astype(_BF16), preferred_element_type=_F32)


_PT = 512


def _proj_in_kernel(groups, norm_tiles, x_ref, nw_ref, w_ref, wsm_ref, seg_ref, hw_ref, hm_ref,
                    a_ref, q_ref, az_ref, kv4_ref, kvw_ref, sm_ref, xn_ref):
    j = pl.program_id(1)

    @pl.when(j == 0)
    def _():
        x = x_ref[...]
        ms = jnp.mean(x * x, axis=-1, keepdims=True)
        xn = x * lax.rsqrt(ms + NORM_EPS) * nw_ref[...]
        xn_ref[...] = xn.astype(_BF16)
        sm = jnp.dot(xn_ref[...], wsm_ref[...], preferred_element_type=_F32)
        lane = lax.broadcasted_iota(jnp.int32, sm.shape, 1)
        is_gate = (lane >= 2 * M_HEADS) & (lane < 2 * M_HEADS + 3 * A_HEADS)
        sm_ref[...] = jnp.where(is_gate, jax.nn.sigmoid(sm), sm)

    y = jnp.dot(xn_ref[...], w_ref[...], preferred_element_type=_F32)
    outs = (a_ref, q_ref, az_ref, kv4_ref, kvw_ref)
    for (start, n), o_ref in zip(groups, outs):
        plain = [t for t in range(start, start + n) if t not in norm_tiles]
        normed = [t for t in range(start, start + n) if t in norm_tiles]
        if plain:
            cond = functools.reduce(jnp.logical_or, [j == t for t in plain])

            @pl.when(cond)
            def _(o_ref=o_ref):
                o_ref[...] = y

        if normed:
            cond = functools.reduce(jnp.logical_or, [j == t for t in normed])

            @pl.when(cond)
            def _(o_ref=o_ref):
                ms = jnp.dot((y * y).astype(_BF16), seg_ref[...], preferred_element_type=_F32)
                yn = y * lax.rsqrt(ms + NORM_EPS) * hw_ref[...]
                o_ref[...] = jnp.where(hm_ref[...] > 0, yn, y)


def _proj_in(x2d, norm_w, w_main, w_small, seg, hw, hm, groups, norm_tiles, tm):
    M, D = x2d.shape
    n_tiles = w_main.shape[1] // _PT
    widths = [n * _PT for _, n in groups]

    def out_map(start, n):
        return lambda i, j: (i, jnp.clip(j - start, 0, n - 1))

    out_specs = [pl.BlockSpec((tm, _PT), out_map(s, n)) for s, n in groups]
    out_specs.append(pl.BlockSpec((tm, 128), lambda i, j: (i, 0)))
    out_shape = [jax.ShapeDtypeStruct((M, w), _F32) for w in widths]
    out_shape.append(jax.ShapeDtypeStruct((M, 128), _F32))
    return pl.pallas_call(
        functools.partial(_proj_in_kernel, groups, norm_tiles),
        grid=(M // tm, n_tiles),
        in_specs=[
            pl.BlockSpec((tm, D), lambda i, j: (i, 0)),
            pl.BlockSpec((1, D), lambda i, j: (0, 0)),
            pl.BlockSpec((D, _PT), lambda i, j: (0, j)),
            pl.BlockSpec((D, 128), lambda i, j: (0, 0)),
            pl.BlockSpec((_PT, _PT), lambda i, j: (0, 0)),
            pl.BlockSpec((None, 1, _PT), lambda i, j: (j, 0, 0)),
            pl.BlockSpec((None, 1, _PT), lambda i, j: (j, 0, 0)),
        ],
        out_specs=out_specs,
        out_shape=out_shape,
        scratch_shapes=[pltpu.VMEM((tm, D), _BF16)],
        compiler_params=pltpu.CompilerParams(
            dimension_semantics=("arbitrary", "arbitrary"), vmem_limit_bytes=_VMEM_LIMIT),
        name="proj_in",
    )(x2d, norm_w.reshape(1, D), w_main, w_small, seg, hw, hm)


def _heads_to_agd(w, a_dh):
    lead = w.shape[:-1]
    return jnp.swapaxes(w.reshape(lead + (A_KV, A_GROUP, a_dh)), -3, -2).reshape(w.shape)


def _heads_to_gad(w, a_dh):
    lead = w.shape[:-1]
    return jnp.swapaxes(w.reshape(lead + (A_GROUP, A_KV, a_dh)), -3, -2).reshape(w.shape)


def _prep_proj_weights(w_in, a_q_norm, a_k_norm, m_width, a_width, a_dh):
    kvw = A_KV * a_dh
    sizes = (m_width, m_width, m_width, M_HEADS, M_HEADS, a_width, 6 * kvw, 3 * A_HEADS, a_width)
    offs = [0]
    for s in sizes:
        offs.append(offs[-1] + s)
    u0, o0, z0, i0, f0, q0, kv0, g0, az0, end = offs
    assert end == w_in.shape[1]
    w_main = jnp.concatenate([
        w_in[:, u0:i0], _heads_to_agd(w_in[:, q0:kv0], a_dh), _heads_to_agd(w_in[:, az0:end], a_dh),
        w_in[:, kv0:kv0 + 4 * kvw], w_in[:, kv0 + 4 * kvw:g0]], axis=1).astype(_BF16)
    n_small = 2 * M_HEADS + 3 * A_HEADS
    w_small = jnp.concatenate([
        w_in[:, i0:q0], w_in[:, g0:az0],
        jnp.zeros((w_in.shape[0], 128 - n_small), w_in.dtype)], axis=1).astype(_BF16)
    widths = (3 * m_width, a_width, a_width, 4 * kvw, 2 * kvw)
    groups, start = [], 0
    for w in widths:
        assert w % _PT == 0
        groups.append((start, w // _PT))
        start += w // _PT
    n_tiles = start
    hw = jnp.zeros((n_tiles * _PT,), _F32)
    hm = jnp.zeros((n_tiles * _PT,), _F32)
    qs = groups[1][0] * _PT
    hw = hw.at[qs:qs + a_width].set(jnp.tile(a_q_norm, a_width // a_dh))
    hm = hm.at[qs:qs + a_width].set(1.0)
    k4 = groups[3][0] * _PT
    hw = hw.at[k4 + 2 * kvw:k4 + 3 * kvw].set(jnp.tile(a_k_norm[1], A_KV))
    hm = hm.at[k4 + 2 * kvw:k4 + 3 * kvw].set(1.0)
    kw = groups[4][0] * _PT
    hw = hw.at[kw:kw + kvw].set(jnp.tile(a_k_norm[2], A_KV))
    hm = hm.at[kw:kw + kvw].set(1.0)
    assert kvw * 2 == _PT and _PT % a_dh == 0
    norm_tiles = tuple(range(groups[1][0], groups[1][0] + groups[1][1])) + (
        (k4 + 2 * kvw) // _PT, kw // _PT)
    r = jnp.arange(_PT) // a_dh
    seg = ((r[:, None] == r[None, :]).astype(_F32) / a_dh).astype(_BF16)
    return (w_main, w_small, seg, hw.reshape(n_tiles, 1, _PT), hm.reshape(n_tiles, 1, _PT),
            tuple(groups), norm_tiles)


def _out_proj_kernel(mw, x_ref, m_ref, a_ref, p_ref, wo_ref, pp_ref, pn_ref, pg_ref, o_ref):
    h = x_ref[...] + _dot(m_ref[...], wo_ref[0:mw, :]) + _dot(a_ref[...], wo_ref[mw:, :])
    ms = jnp.mean(h * h, axis=-1, keepdims=True)
    hn = h * lax.rsqrt(ms + NORM_EPS) * pn_ref[...]
    gate = jax.nn.sigmoid(_dot(hn, pg_ref[...]))
    o_ref[...] = h + gate * _dot(p_ref[...], pp_ref[...])


def _out_proj(x2d, m_out, a_out, p2d, w_out, ple_proj, ple_norm, ple_gate, tm):
    M, D = x2d.shape
    mw, aw, dp = m_out.shape[1], a_out.shape[1], p2d.shape[1]

    def const(shape):
        return pl.BlockSpec(shape, lambda i: (0, 0), pipeline_mode=pl.Buffered(1))

    return pl.pallas_call(
        functools.partial(_out_proj_kernel, mw),
        grid=(M // tm,),
        in_specs=[
            pl.BlockSpec((tm, D), lambda i: (i, 0)),
            pl.BlockSpec((tm, mw), lambda i: (i, 0)),
            pl.BlockSpec((tm, aw), lambda i: (i, 0)),
            pl.BlockSpec((tm, dp), lambda i: (i, 0)),
            const((mw + aw, D)),
            const((dp, D)),
            const((1, D)),
            const((D, D)),
        ],
        out_specs=pl.BlockSpec((tm, D), lambda i: (i, 0)),
        out_shape=jax.ShapeDtypeStruct((M, D), _F32),
        compiler_params=pltpu.CompilerParams(
            dimension_semantics=("arbitrary",), vmem_limit_bytes=_VMEM_LIMIT),
        name="out_proj",
    )(x2d, m_out, a_out, p2d, w_out.astype(_BF16), ple_proj.astype(_BF16),
      ple_norm.reshape(1, D), ple_gate.astype(_BF16))


_MLSTM_L = 256
_CONV_PAD = 8


def _mlstm_kernel(L, dh, n_steps,
                  u_ref, o_ref, z_ref, sm_ref, cbuf_ref, c0_ref, n0_ref, m0_ref,
                  wq_ref, wk_ref, wv_ref, cw_ref, cb_ref, bi_ref, bf_ref, nw_ref,
                  out_ref, cout_ref, nout_ref, mout_ref, convout_ref,
                  xbuf, c_s, n_s, m_s):
    h = pl.program_id(1)
    s = pl.program_id(2)
    tail = M_CONV - 1

    @pl.when(s == 0)
    def _():
        c_s[...] = c0_ref[...]
        n_s[...] = n0_ref[...]
        m_s[...] = m0_ref[...]
        xbuf[_CONV_PAD - tail:_CONV_PAD, :] = cbuf_ref[...]

    u = u_ref[...]
    xbuf[_CONV_PAD:_CONV_PAD + L, :] = u
    c = cb_ref[...]
    for j in range(M_CONV):
        c = c + xbuf[_CONV_PAD - tail + j:_CONV_PAD - tail + j + L, :] * cw_ref[j:j + 1, :]
    xbuf[_CONV_PAD - tail:_CONV_PAD, :] = u[L - tail:L, :]
    ch = c * jax.nn.sigmoid(c)
    q = _dot(ch, wq_ref[...])
    k = _dot(ch, wk_ref[...]) * (dh ** -0.5)
    v = _dot(u, wv_ref[...])

    sm = sm_ref[...]
    lane = lax.broadcasted_iota(jnp.int32, sm.shape, 1)
    i_pre = jnp.sum(jnp.where(lane == h, sm, 0.0), axis=1, keepdims=True)
    f_pre = jnp.sum(jnp.where(lane == h + M_HEADS, sm, 0.0), axis=1, keepdims=True)
    li_col = i_pre + bi_ref[:, 0:1]
    f_in = f_pre + bf_ref[:, 0:1]
    lf_col = jnp.minimum(f_in, 0.0) - jnp.log(1.0 + jnp.exp(-jnp.abs(f_in)))

    ii = lax.broadcasted_iota(jnp.int32, (L, L), 0)
    jj = lax.broadcasted_iota(jnp.int32, (L, L), 1)
    eye = ii == jj
    causal = jj <= ii
    lf_row = jnp.sum(jnp.where(eye, lf_col, 0.0), axis=0, keepdims=True)
    li_row = jnp.sum(jnp.where(eye, li_col, 0.0), axis=0, keepdims=True)
    b_col = jnp.sum(jnp.where(causal, lf_row, 0.0), axis=1, keepdims=True)
    b_row = jnp.sum(jnp.where(ii <= jj, lf_col, 0.0), axis=0, keepdims=True)

    m_prev = m_s[0:1, 0:1]
    c_prev = c_s[...]
    n_prev = n_s[...]
    logd = jnp.where(causal, b_col - b_row + li_row, -jnp.inf)
    inter = b_col + m_prev
    m_t = jnp.maximum(inter, jnp.max(logd, axis=1, keepdims=True))
    qb = q.astype(_BF16)
    sc = lax.dot_general(qb, k.astype(_BF16), (((1,), (1,)), ((), ())),
                         preferred_element_type=_F32) * jnp.exp(logd - m_t)
    a_col = jnp.exp(inter - m_t)
    num = _dot(sc, v) + a_col * _dot(qb, c_prev)
    den = jnp.sum(sc, axis=1, keepdims=True) + a_col * jnp.sum(q * n_prev, axis=1, keepdims=True)
    hh = num / jnp.maximum(jnp.abs(den), jnp.exp(-m_t))

    b_end = b_col[L - 1:L, :]
    logw = b_end - b_col + li_col
    m_new = jnp.maximum(b_end + m_prev, jnp.max(logw, axis=0, keepdims=True))
    w_col = jnp.exp(logw - m_new)
    decay = jnp.exp(b_end + m_prev - m_new)
    c_s[...] = decay * c_prev + lax.dot_general(
        k.astype(_BF16), (w_col * v).astype(_BF16), (((0,), (0,)), ((), ())),
        preferred_element_type=_F32)
    n_s[...] = decay * n_prev + jnp.sum(w_col * k, axis=0, keepdims=True)
    m_s[...] = jnp.broadcast_to(m_new, m_s.shape)

    hg = hh * jax.nn.sigmoid(o_ref[...])
    hn = hg * lax.rsqrt(jnp.mean(hg * hg, axis=-1, keepdims=True) + NORM_EPS) * nw_ref[...]
    zz = z_ref[...]
    out_ref[...] = hn * (zz * jax.nn.sigmoid(zz))

    @pl.when(s == n_steps - 1)
    def _():
        cout_ref[...] = c_s[...]
        nout_ref[...] = n_s[...]
        mout_ref[...] = m_s[...]
        convout_ref[...] = u[L - tail:L, :]


def _mlstm(a3, sm, conv_buf, C0, n0, m0, conv_w, conv_b, wq, wk, wv, b_i, b_f, norm_w, L):
    B, T, mw3 = a3.shape
    mw = mw3 // 3
    dh = mw // M_HEADS
    tail = M_CONV - 1
    assert T % L == 0 and L >= tail and (L % 8 == 0)
    n_steps = T // L
    H = M_HEADS

    def lane_b(vec):
        return jnp.broadcast_to(vec.astype(_F32)[:, None, None], (H, 1, 128))

    m0b = jnp.broadcast_to(m0.astype(_F32)[:, :, None, None], (B, H, 1, 128))
    blk = lambda off: pl.BlockSpec((None, L, dh), lambda b, h, s: (b, s, off * H + h))
    per_h = lambda r, c: pl.BlockSpec((None, r, c), lambda b, h, s: (h, 0, 0))
    st = lambda r, c: pl.BlockSpec((None, None, r, c), lambda b, h, s: (b, h, 0, 0))
    out, C, n, m, conv = pl.pallas_call(
        functools.partial(_mlstm_kernel, L, dh, n_steps),
        grid=(B, H, n_steps),
        in_specs=[
            blk(0), blk(1), blk(2),
            pl.BlockSpec((None, L, 128), lambda b, h, s: (b, s, 0)),
            pl.BlockSpec((None, tail, dh), lambda b, h, s: (b, 0, h)),
            st(dh, dh), st(1, dh), st(1, 128),
            per_h(dh, dh), per_h(dh, dh), per_h(dh, dh),
            pl.BlockSpec((M_CONV, dh), lambda b, h, s: (0, h)),
            pl.BlockSpec((1, dh), lambda b, h, s: (0, h)),
            per_h(1, 128), per_h(1, 128), per_h(1, dh),
        ],
        out_specs=[
            pl.BlockSpec((None, L, dh), lambda b, h, s: (b, s, h)),
            st(dh, dh), st(1, dh), st(1, 128),
            pl.BlockSpec((None, tail, dh), lambda b, h, s: (b, 0, h)),
        ],
        out_shape=[
            jax.ShapeDtypeStruct((B, T, mw), _F32),
            jax.ShapeDtypeStruct((B, H, dh, dh), _F32),
            jax.ShapeDtypeStruct((B, H, 1, dh), _F32),
            jax.ShapeDtypeStruct((B, H, 1, 128), _F32),
            jax.ShapeDtypeStruct((B, tail, mw), _F32),
        ],
        scratch_shapes=[
            pltpu.VMEM((_CONV_PAD + L, dh), _F32),
            pltpu.VMEM((dh, dh), _F32),
            pltpu.VMEM((1, dh), _F32),
            pltpu.VMEM((1, 128), _F32),
        ],
        compiler_params=pltpu.CompilerParams(
            dimension_semantics=("arbitrary", "arbitrary", "arbitrary"), vmem_limit_bytes=_VMEM_LIMIT),
        name="mlstm",
    )(a3, a3, a3, sm, conv_buf, C0, n0.reshape(B, H, 1, dh), m0b,
      wq.astype(_BF16), wk.astype(_BF16), wv.astype(_BF16), conv_w, conv_b.reshape(1, mw),
      lane_b(b_i), lane_b(b_f), norm_w.reshape(H, 1, dh))
    return out, conv, C, n.reshape(B, H, dh), m[:, :, 0, 0]


def _prep_cmp_weights(pe, w1, w2, k_norm):
    a_dh, hid = w1.shape[-2], w1.shape[-1]
    w1r = w1.reshape(2, 2, CMP_STRIDE, a_dh, hid)
    eye2 = jnp.eye(2, dtype=w1.dtype)
    w1p = jnp.einsum('cxjdh,ab->cjadxbh', w1r, eye2).reshape(2, CMP_STRIDE * 2 * a_dh, 2 * 2 * hid)
    bias = jnp.einsum('cxjd,cxjdh->ch', pe.reshape(2, 2, CMP_STRIDE, a_dh), w1r)
    bias2 = jnp.tile(bias, (1, 2)).reshape(2, 1, 2 * hid)
    w2p = jnp.einsum('chd,ab->cahbd', w2, eye2).reshape(2, 2 * hid, 2 * a_dh)
    r = jnp.arange(2 * a_dh) // a_dh
    seg = ((r[:, None] == r[None, :]).astype(_F32) / a_dh).astype(_BF16)
    kn = jnp.tile(k_norm, 2).reshape(1, 2 * a_dh)
    return w1p.astype(_BF16), bias2.astype(_F32), w2p.astype(_BF16), seg, kn.astype(_F32)


def _cmp_first_layer(rows_ref, n_seg, w1_ref, c):
    x = jnp.concatenate(
        [rows_ref[pl.ds(j, n_seg, stride=CMP_STRIDE), :].astype(_BF16)
         for j in range(CMP_STRIDE)], axis=1)
    return jnp.dot(x, w1_ref[c], preferred_element_type=_F32)


def _cmp_second_layer(p, n_seg, c, b_ref, w2_ref, seg_ref, kn_ref):
    hid = p[:, 0:128] + pltpu.roll(p[:, 128:256], n_seg - 1, axis=0) + b_ref[c]
    act = hid * jax.nn.sigmoid(hid)
    out = jnp.dot(act.astype(_BF16), w2_ref[c], preferred_element_type=_F32)
    if c == 0:
        ms = jnp.dot((out * out).astype(_BF16), seg_ref[...], preferred_element_type=_F32)
        out = out * lax.rsqrt(ms + NORM_EPS) * kn_ref[...]
    return out


def _compress_prompt_kernel(n_seg, r00, r01, r10, r11, w1_ref, b_ref, w2_ref, seg_ref, kn_ref,
                            kc_ref, vc_ref):
    for c, o_ref, refs in ((0, kc_ref, (r00, r01)), (1, vc_ref, (r10, r11))):
        for gp in range(2):
            p = _cmp_first_layer(refs[gp], n_seg, w1_ref, c)
            o_ref[:, gp * 128:(gp + 1) * 128] = _cmp_second_layer(p, n_seg, c, b_ref, w2_ref, seg_ref, kn_ref)


def _compress_prompt(kv4, cwp):
    B, T, W = kv4.shape
    assert W == 1024 and T % (CMP_STRIDE * 8) == 0
    n_seg = T // CMP_STRIDE
    w1p, bias2, w2p, seg, kn = cwp
    full = lambda a: pl.BlockSpec(a.shape, lambda b: (0,) * a.ndim)
    return pl.pallas_call(
        functools.partial(_compress_prompt_kernel, n_seg),
        grid=(B,),
        in_specs=[pl.BlockSpec((None, T, 128), functools.partial(lambda k, b: (b, 0, k), k))
                  for k in range(4)] + [full(w1p), full(bias2), full(w2p), full(seg), full(kn)],
        out_specs=[pl.BlockSpec((None, n_seg, 256), lambda b: (b, 0, 0))] * 2,
        out_shape=[jax.ShapeDtypeStruct((B, n_seg, 256), _F32)] * 2,
        compiler_params=pltpu.CompilerParams(
            dimension_semantics=("arbitrary",), vmem_limit_bytes=_VMEM_LIMIT),
        name="compress_prompt",
    )(kv4, kv4, kv4, kv4, w1p, bias2, w2p, seg, kn)


_QB = 128
_SEL_TK = 512
_GW = 32


def _softmax_rows(s, mask):
    s = jnp.where(mask, s, -jnp.inf)
    m = jnp.max(s, axis=-1, keepdims=True)
    m = jnp.where(m == -jnp.inf, 0.0, m)
    e = jnp.where(mask, jnp.exp(s - m), 0.0)
    return e / jnp.maximum(jnp.sum(e, axis=-1, keepdims=True), 1e-30)


def _nsa_prompt_kernel(T, n_cmp, n_slc,
                       q_ref, ks_ref, kw_ref, kc_ref, vc_ref, sm_ref, az_ref, out_ref,
                       qpl_ref, m_ref, l_ref, acc_ref):
    QB, TK, G, A = _QB, _SEL_TK, A_KV, A_GROUP
    R = A * QB
    t0 = pl.program_id(1) * QB
    lane256 = lax.broadcasted_iota(jnp.int32, (QB, 256), 1)
    q = q_ref[...]
    for g in range(G):
        gm = (lane256 // 64) == g
        qpl_ref[g] = jnp.concatenate(
            [jnp.where(gm, q[:, a * 256:(a + 1) * 256] * 0.125, 0.0) for a in range(A)],
            axis=0).astype(_BF16)
    tq_r = t0 + lax.broadcasted_iota(jnp.int32, (R, 1), 0) % QB
    tq = t0 + lax.broadcasted_iota(jnp.int32, (QB, 1), 0)

    n_seg = kc_ref.shape[0]
    kc = kc_ref[...].astype(_BF16)
    vc = vc_ref[...].astype(_BF16)
    ci = lax.broadcasted_iota(jnp.int32, (R, n_seg), 1)
    cmask = (ci < n_cmp) & (ci * CMP_STRIDE + (CMP_LEN - 1) <= tq_r)
    mi = lax.broadcasted_iota(jnp.int32, (n_seg, G * _GW), 0)
    mj = lax.broadcasted_iota(jnp.int32, (n_seg, G * _GW), 1)
    ratio = SLC_BLK // CMP_STRIDE
    o_cmp = []
    ps = jnp.zeros((QB, G * _GW), _F32)
    for g in range(G):
        s = lax.dot_general(qpl_ref[g], kc, (((1,), (1,)), ((), ())), preferred_element_type=_F32)
        p = _softmax_rows(s, cmask)
        o_cmp.append(jnp.dot(p.astype(_BF16), vc, preferred_element_type=_F32))
        pg = p[0:QB]
        for a in range(1, A):
            pg = pg + p[a * QB:(a + 1) * QB]
        blk = mj - g * _GW
        msel = ((mj // _GW == g) & (mi >= ratio * blk - 1) & (mi <= ratio * blk + ratio - 1)
                ).astype(_BF16)
        pg_hi = pg.astype(_BF16)
        pg_lo = (pg - pg_hi.astype(_F32)).astype(_BF16)
        ps = ps + jnp.dot(pg_hi, msel, preferred_element_type=_F32) \
                + jnp.dot(pg_lo, msel, preferred_element_type=_F32)

    lane = lax.broadcasted_iota(jnp.int32, (QB, G * _GW), 1)
    blk = lane % _GW
    cur = tq // SLC_BLK
    valid = (blk < n_slc) & (blk * SLC_BLK <= tq)
    forced = (blk == 0) | (blk == cur) | (blk == cur - 1)
    score = jnp.where(valid, jnp.where(forced, jnp.inf, ps), -jnp.inf)
    rank = jnp.zeros((QB, G * _GW), _F32)
    for r in range(1, _GW):
        other = jnp.where(blk >= r, pltpu.roll(score, r, axis=1),
                          pltpu.roll(score, (r - _GW) % (G * _GW), axis=1))
        ahead = (other > score) | ((other == score) & (blk >= r))
        rank = rank + jnp.where(ahead, 1.0, 0.0)
    sel = jnp.where(rank < min(SLC_TOPN, n_slc), 1.0, 0.0).astype(_BF16)

    m_ref[...] = jnp.full(m_ref.shape, -jnp.inf, _F32)
    l_ref[...] = jnp.zeros(l_ref.shape, _F32)
    acc_ref[...] = jnp.zeros(acc_ref.shape, _F32)
    er = lax.broadcasted_iota(jnp.int32, (G * _GW, TK), 0)
    ec = lax.broadcasted_iota(jnp.int32, (G * _GW, TK), 1)
    kcol = lax.broadcasted_iota(jnp.int32, (R, TK), 1)

    def sel_tile(kt, carry):
        k0 = pl.multiple_of(kt * TK, TK)
        kt_b = ks_ref[pl.ds(k0, TK), 0:256].astype(_BF16)
        vt_b = ks_ref[pl.ds(k0, TK), 256:512].astype(_BF16)
        causal = (k0 + kcol) <= tq_r
        for g in range(G):
            expand = (er == g * _GW + (k0 + ec) // SLC_BLK).astype(_BF16)
            hit = jnp.dot(sel, expand, preferred_element_type=_F32)
            mask = causal & (jnp.concatenate([hit] * A, axis=0) > 0.5)
            s = lax.dot_general(qpl_ref[g], kt_b, (((1,), (1,)), ((), ())), preferred_element_type=_F32)
            s = jnp.where(mask, s, -jnp.inf)
            m_old = m_ref[g]
            m_new = jnp.maximum(m_old, jnp.max(s, axis=-1, keepdims=True))
            m_safe = jnp.where(m_new == -jnp.inf, 0.0, m_new)
            p = jnp.where(mask, jnp.exp(s - m_safe), 0.0)
            alpha = jnp.exp(m_old - m_safe)
            l_ref[g] = alpha * l_ref[g] + jnp.sum(p, axis=-1, keepdims=True)
            acc_ref[g] = alpha * acc_ref[g] + jnp.dot(p.astype(_BF16), vt_b, preferred_element_type=_F32)
            m_ref[g] = m_new
        return carry

    lax.fori_loop(0, (t0 + QB + TK - 1) // TK, sel_tile, 0)

    WK = WINDOW + QB
    ws = pl.multiple_of(jnp.clip(t0 - WINDOW, 0, T - WK), QB)
    kw_b = kw_ref[pl.ds(ws, WK), 0:256].astype(_BF16)
    vw_b = kw_ref[pl.ds(ws, WK), 256:512].astype(_BF16)
    diff = tq_r - (ws + lax.broadcasted_iota(jnp.int32, (R, WK), 1))
    wmask = (diff >= 0) & (diff < WINDOW)

    sm = sm_ref[...]
    g0 = 2 * M_HEADS
    for g in range(G):
        gm = (lane256 // 64) == g
        s = lax.dot_general(qpl_ref[g], kw_b, (((1,), (1,)), ((), ())), preferred_element_type=_F32)
        p = _softmax_rows(s, wmask)
        o_win = jnp.dot(p.astype(_BF16), vw_b, preferred_element_type=_F32)
        o_sel = acc_ref[g] / jnp.maximum(l_ref[g], 1e-30)
        for a in range(A):
            head = g0 + g * A + a
            rows = slice(a * QB, (a + 1) * QB)
            mix = (sm[:, head:head + 1] * o_cmp[g][rows]
                   + sm[:, head + A_HEADS:head + A_HEADS + 1] * o_sel[rows]
                   + sm[:, head + 2 * A_HEADS:head + 2 * A_HEADS + 1] * o_win[rows])
            mix = jnp.where(gm, mix, 0.0)
            cols = slice(a * 256, (a + 1) * 256)
            if g == 0:
                out_ref[:, cols] = mix
            else:
                out_ref[:, cols] += mix
    az = az_ref[...]
    out_ref[...] = out_ref[...] * (az * jax.nn.sigmoid(az))


def _nsa_prompt(q, kv4, kvw, kc, vc, sm, az):
    B, T, _ = q.shape
    n_seg = kc.shape[1]
    n_cmp = n_seg - 1
    n_slc = T // SLC_BLK
    assert T % _SEL_TK == 0 and T >= WINDOW + _QB and n_slc <= _GW and A_KV * _GW == 128
    assert A_KV * 64 == 256 and n_seg % 8 == 0
    R = A_GROUP * _QB
    blk = lambda w: pl.BlockSpec((None, _QB, w), lambda b, i: (b, i, 0))
    return pl.pallas_call(
        functools.partial(_nsa_prompt_kernel, T, n_cmp, n_slc),
        grid=(B, T // _QB),
        in_specs=[
            blk(1024),
            pl.BlockSpec((None, T, 512), lambda b, i: (b, 0, 1)),
            pl.BlockSpec((None, T, 512), lambda b, i: (b, 0, 0)),
            pl.BlockSpec((None, n_seg, 256), lambda b, i: (b, 0, 0)),
            pl.BlockSpec((None, n_seg, 256), lambda b, i: (b, 0, 0)),
            blk(128), blk(1024),
        ],
        out_specs=blk(1024),
        out_shape=jax.ShapeDtypeStruct((B, T, 1024), _F32),
        scratch_shapes=[
            pltpu.VMEM((A_KV, R, 256), _BF16),
            pltpu.VMEM((A_KV, R, 1), _F32),
            pltpu.VMEM((A_KV, R, 1), _F32),
            pltpu.VMEM((A_KV, R, 256), _F32),
        ],
        compiler_params=pltpu.CompilerParams(
            dimension_semantics=("arbitrary", "arbitrary"), vmem_limit_bytes=_VMEM_LIMIT),
        name="nsa_prompt",
    )(q, kv4, kvw, kc, vc, sm, az)


def _rms_norm(x, w):
    xf = x.astype(_F32)
    y = xf * lax.rsqrt(jnp.mean(xf * xf, axis=-1, keepdims=True) + NORM_EPS)
    return (y * w.astype(_F32)).astype(x.dtype)


def _masked_softmax(s, mask):
    s = jnp.where(mask, s.astype(_F32), -jnp.inf)
    m = jnp.max(s, axis=-1, keepdims=True)
    m = jnp.where(jnp.isfinite(m), m, 0.0)
    e = jnp.where(mask, jnp.exp(s - m), 0.0)
    return e / jnp.maximum(jnp.sum(e, axis=-1, keepdims=True), 1e-30)


def _compress(rows, pe, w1, w2, k_norm):
    B, L = rows.shape[:2]
    a_dh = rows.shape[-1]
    hid_n = w1.shape[-1]
    n_seg = L // CMP_STRIDE
    seg = rows[:, :n_seg * CMP_STRIDE].reshape(B, n_seg, CMP_STRIDE, 2, A_KV, a_dh)
    w1r = w1.reshape(2, 2, CMP_STRIDE, a_dh, hid_n)
    per = jnp.einsum('bnjcgd,cxjdh->bnxcgh', seg, w1r)
    bias = jnp.einsum('cxjd,cxjdh->ch', pe.reshape(2, 2, CMP_STRIDE, a_dh), w1r)
    hid = per[:, :-1, 0] + per[:, 1:, 1] + bias[:, None, :]
    out = jnp.einsum('bncgh,chd->bncgd', jax.nn.silu(hid), w2)
    return _rms_norm(out[:, :, 0], k_norm), out[:, :, 1]


def _select_blocks(pg, tq, n_slc):
    Tq, _, n_cmp = pg.shape
    r = SLC_BLK // CMP_STRIDE
    pad_back = r * (n_slc + 1) - 1 - n_cmp
    pp = jnp.pad(pg, ((0, 0), (0, 0), (1, pad_back))).reshape(Tq, A_KV, n_slc + 1, r)
    ps = jnp.sum(pp[..., :-1, :], axis=-1) + pp[..., 1:, 0]
    blk = jnp.arange(n_slc, dtype=jnp.int32)[None, :]
    valid = blk * SLC_BLK <= tq[:, None]
    cur = (tq // SLC_BLK)[:, None]
    forced = (blk == 0) | (blk == cur) | (blk == cur - 1)
    score = jnp.where(valid[:, None], jnp.where(forced[:, None], jnp.inf, ps), -jnp.inf)
    si = score[..., None, :]
    sj = score[..., :, None]
    idx = jnp.arange(n_slc, dtype=jnp.int32)
    ahead = (si > sj) | ((si == sj) & (idx[None, :] < idx[:, None]))
    rank = jnp.sum(ahead, axis=-1)
    return rank < min(SLC_TOPN, n_slc)


def _nsa_prompt_one(q, g, kv4, kvw, kc, vc):
    T, _, dh = q.shape
    scale = dh ** -0.5
    tq = jnp.arange(T, dtype=jnp.int32)
    qg = q.reshape(T, A_KV, A_GROUP, dh)
    n_cmp = kc.shape[0]
    c_end = jnp.arange(n_cmp, dtype=jnp.int32) * CMP_STRIDE + (CMP_LEN - 1)
    s = jnp.einsum('tgad,ngd->tgan', qg, kc) * scale
    p = _masked_softmax(s, (c_end[None, :] <= tq[:, None])[:, None, None, :])
    o_cmp = jnp.einsum('tgan,ngd->tgad', p, vc)
    n_slc = T // SLC_BLK
    sel = _select_blocks(jnp.sum(p, axis=2), tq, n_slc)
    kpos = jnp.arange(T, dtype=jnp.int32)
    kmask = sel[:, :, kpos // SLC_BLK] & (kpos[None, None, :] <= tq[:, None, None])
    s = jnp.einsum('tgad,sgd->tgas', qg, kv4[:, 2]) * scale
    p = _masked_softmax(s, kmask[:, :, None, :])
    o_sel = jnp.einsum('tgas,sgd->tgad', p, kv4[:, 3])
    diff = tq[:, None] - kpos[None, :]
    wmask = (diff >= 0) & (diff < WINDOW)
    s = jnp.einsum('tgad,sgd->tgas', qg, kvw[:, 0]) * scale
    p = _masked_softmax(s, wmask[:, None, None, :])
    o_win = jnp.einsum('tgas,sgd->tgad', p, kvw[:, 1])
    gg = g.reshape(T, 3, A_KV, A_GROUP)[..., None]
    o = gg[:, 0] * o_cmp + gg[:, 1] * o_sel + gg[:, 2] * o_win
    return o.reshape(T, -1)


def _nsa_combine(q, tq, o_cmp, idx, k_sel, v_sel, kw, vw, pw, g):
    Tq, _, dh = q.shape
    qg = q.reshape(Tq, A_KV, A_GROUP, dh)
    scale = dh ** -0.5
    pos = idx[..., None] * SLC_BLK + jnp.arange(SLC_BLK, dtype=jnp.int32)
    s = jnp.einsum('tgad,tgmd->tgam', qg, k_sel.reshape(Tq, A_KV, -1, dh)) * scale
    p = _masked_softmax(s, (pos <= tq[:, None, None, None]).reshape(Tq, A_KV, 1, -1))
    o_sel = jnp.einsum('tgam,tgmd->tgad', p, v_sel.reshape(Tq, A_KV, -1, dh))
    diff = tq[:, None] - pw[None, :]
    wmask = (pw[None, :] >= 0) & (diff >= 0) & (diff < WINDOW)
    s = jnp.einsum('tgad,sgd->tgas', qg, kw) * scale
    p = _masked_softmax(s, wmask[:, None, None, :])
    o_win = jnp.einsum('tgas,sgd->tgad', p, vw)
    gg = g.astype(_F32).reshape(Tq, 3, A_KV, A_GROUP)[..., None]
    o = gg[:, 0] * o_cmp + gg[:, 1] * o_sel + gg[:, 2] * o_win
    return o.reshape(Tq, -1)


def _nsa_compressed(q, tq, kc, vc, c_end, n_slc):
    Tq, _, dh = q.shape
    qg = q.reshape(Tq, A_KV, A_GROUP, dh)
    s = jnp.einsum('tgad,ngd->tgan', qg, kc) * (dh ** -0.5)
    p = _masked_softmax(s, (c_end[None, :] <= tq[:, None])[:, None, None, :])
    o = jnp.einsum('tgan,ngd->tgad', p, vc)
    sel = _select_blocks(jnp.sum(p, axis=2), tq, n_slc)
    score = jnp.where(sel, 1.0, 0.0)
    _, idx = lax.top_k(score, min(SLC_TOPN, n_slc))
    return o, idx


def _nsa_sample(q, g, kv4, kvw, cache_kv, win_buf, page_table, layer, cmp_pe, cmp_w1, cmp_w2, k_norm_cmp):
    Bd, Ts, _, dh = q.shape
    n_pages = page_table.shape[1]
    past_len = n_pages * PAGE_SIZE
    past = cache_kv[layer, page_table, :, 0:2].reshape(Bd, past_len, 2, A_KV, dh)
    kc, vc = _compress(jnp.concatenate([past, kv4[:, :, 0:2]], axis=1), cmp_pe, cmp_w1, cmp_w2, k_norm_cmp)
    n_cmp = kc.shape[1]
    c_end = jnp.arange(n_cmp, dtype=jnp.int32) * CMP_STRIDE + (CMP_LEN - 1)
    n_past_blk = past_len // SLC_BLK
    n_new_blk = -(-Ts // SLC_BLK)
    n_slc = n_past_blk + n_new_blk
    blk_per_page = PAGE_SIZE // SLC_BLK
    new_blk = jnp.pad(kv4[:, :, 2:4], ((0, 0), (0, n_new_blk * SLC_BLK - Ts), (0, 0), (0, 0), (0, 0)))
    new_blk = new_blk.reshape(Bd, n_new_blk, SLC_BLK, 2, A_KV, dh)
    wb = win_buf.shape[1]
    tq = past_len + jnp.arange(Ts, dtype=jnp.int32)
    pw = past_len - wb + jnp.arange(wb + Ts, dtype=jnp.int32)
    g_ax = jnp.arange(A_KV)[None, :, None, None]
    r_blk = jnp.arange(SLC_BLK, dtype=jnp.int32)

    def one(qq, gg, kc_b, vc_b, new_b, kvw_b, win_b, pt_row):
        o_cmp, idx = _nsa_compressed(qq, tq, kc_b, vc_b, c_end, n_slc)
        pidx = jnp.minimum(idx, n_past_blk - 1)
        page = pt_row[pidx // blk_per_page][..., None]
        rows = ((pidx % blk_per_page) * SLC_BLK)[..., None] + r_blk
        nidx = jnp.clip(idx - n_past_blk, 0, n_new_blk - 1)[..., None]
        is_new = (idx >= n_past_blk)[..., None, None]
        k_sel = jnp.where(is_new, new_b[nidx, r_blk, 0, g_ax], cache_kv[layer, page, rows, 2, g_ax])
        v_sel = jnp.where(is_new, new_b[nidx, r_blk, 1, g_ax], cache_kv[layer, page, rows, 3, g_ax])
        kw = jnp.concatenate([win_b[:, 0], kvw_b[:, 0]], axis=0)
        vw = jnp.concatenate([win_b[:, 1], kvw_b[:, 1]], axis=0)
        return _nsa_combine(qq, tq, o_cmp, idx, k_sel, v_sel, kw, vw, pw, gg)

    return jax.vmap(one)(q, g, kc, vc, new_blk, kvw, win_buf, page_table)


def _mixer_inputs(x, norm_w, pw, m_width, a_width, a_dh):
    B, T, D = x.shape
    w_main, w_small, seg, hw, hm, groups, norm_tiles = pw
    M = B * T
    tm = 512 if M % 512 == 0 else M
    a, q, az, kv4, kvw, sm = _proj_in(x.reshape(M, D), norm_w, w_main, w_small, seg, hw, hm,
                                      groups, norm_tiles, tm)
    r3 = lambda v: v.reshape(B, T, v.shape[-1])
    return r3(a), r3(sm), r3(q), r3(kv4), r3(kvw), r3(az)


def _mixer_output(x, m_out, a_out, p, ow):
    B, T, D = x.shape
    M = B * T
    tm = 256 if M % 256 == 0 else M
    y = _out_proj(x.reshape(M, D), m_out.reshape(M, -1), a_out.reshape(M, -1), p.reshape(M, -1), *ow, tm)
    return y.reshape(B, T, D)


def kernel(x_prompt, x_sample, cache_nsa_kv, cache_win_kv, state_mlstm_C, state_mlstm_n,
           state_mlstm_m, state_mlstm_conv, page_table, p_prompt, p_sample, norm_w, w_in,
           m_conv_w, m_conv_b, m_wq, m_wk, m_wv, m_b_i, m_b_f, m_norm_w, a_q_norm, a_k_norm,
           cmp_pe, cmp_w1, cmp_w2, w_out, ple_proj, ple_norm, ple_gate):
    xp, xs = x_prompt, x_sample
    B, T, D = xp.shape
    depth = w_in.shape[0]
    m_width = m_conv_w.shape[-1]
    a_width = D - m_width
    a_dh = a_q_norm.shape[-1]
    dh = m_width // M_HEADS
    per_layer = []
    for i in range(depth):
        mw = (m_conv_w[i], m_conv_b[i], m_wq[i], m_wk[i], m_wv[i], m_b_i[i], m_b_f[i], m_norm_w[i])
        cw = (cmp_pe[i], cmp_w1[i], cmp_w2[i], a_k_norm[i, 0])
        w_out_i = jnp.concatenate(
            [w_out[i][:m_width], _heads_to_agd(w_out[i][m_width:].T, a_dh).T], axis=0)
        ow = (w_out_i, ple_proj[i], ple_norm[i], ple_gate[i])
        pw = _prep_proj_weights(w_in[i], a_q_norm[i], a_k_norm[i], m_width, a_width, a_dh)
        cwp = _prep_cmp_weights(*cw)
        a3, sm, q, kv4, kvw, az = _mixer_inputs(xp, norm_w[i], pw, m_width, a_width, a_dh)
        L = _MLSTM_L if T % _MLSTM_L == 0 else (M_CHUNK if T % M_CHUNK == 0 else T)
        m_out, conv_p, C_p, n_p, m_p = _mlstm(
            a3, sm, jnp.zeros((B, M_CONV - 1, m_width), _F32),
            jnp.zeros((B, M_HEADS, dh, dh), _F32), jnp.zeros((B, M_HEADS, dh), _F32),
            jnp.full((B, M_HEADS), -jnp.inf, _F32), *mw, L)
        kc, vc = _compress_prompt(kv4, cwp)
        a_out = _nsa_prompt(q, kv4, kvw, kc, vc, sm, az)
        xp_new = _mixer_output(xp, m_out, a_out, p_prompt[i], ow)
        kv_p = kv4.reshape(B, T, 4, A_KV, a_dh)
        win_p = kvw[:, -min(WINDOW, T):].reshape(B, -1, 2, A_KV, a_dh)
        a3, sm, q, kv4, kvw, az = _mixer_inputs(xs, norm_w[i], pw, m_width, a_width, a_dh)
        Bd, Ts = xs.shape[:2]
        g = sm[..., 2 * M_HEADS:2 * M_HEADS + 3 * A_HEADS].reshape(Bd, Ts, 3, A_HEADS)
        q = _heads_to_gad(q, a_dh).reshape(Bd, Ts, A_HEADS, a_dh)
        kv4 = kv4.reshape(Bd, Ts, 4, A_KV, a_dh)
        kvw = kvw.reshape(Bd, Ts, 2, A_KV, a_dh)
        Ls = _MLSTM_L if Ts % _MLSTM_L == 0 else (M_CHUNK if Ts % M_CHUNK == 0 else Ts)
        m_out, conv_s, C_s, n_s, m_s = _mlstm(
            a3, sm, state_mlstm_conv[i], state_mlstm_C[i], state_mlstm_n[i], state_mlstm_m[i], *mw, Ls)
        a_att = _nsa_sample(q, g, kv4, kvw, cache_nsa_kv, cache_win_kv[i], page_table, i, *cw)
        a_out = _heads_to_agd(a_att, a_dh) * jax.nn.silu(az)
        xs_new = _mixer_output(xs, m_out, a_out, p_sample[i], ow)
        kv_s = kv4
        wb = cache_win_kv.shape[2]
        win_s = jnp.concatenate([cache_win_kv[i], kvw], axis=1)[:, -wb:]
        per_layer.append((kv_p, kv_s, win_p, win_s, C_p, n_p, m_p, conv_p, C_s, n_s, m_s, conv_s))
        xp, xs = xp_new, xs_new
    (kv_p, kv_s, win_p, win_s, C_p, n_p, m_p, conv_p,
     C_s, n_s, m_s, conv_s) = [jnp.stack(a, axis=0) for a in zip(*per_layer)]
    return (xp, xs, kv_p, kv_s, win_p, win_s, C_p, n_p, m_p, conv_p, C_s, n_s, m_s, conv_s)
```

```python
import functools
import math

import jax
import jax.numpy as jnp
from jax import lax
from jax.experimental import pallas as pl
from jax.experimental.pallas import tpu as pltpu

NORM_EPS = 1e-6
M_HEADS = 4
M_CONV = 4
M_CHUNK = 64
A_HEADS = 16
A_KV = 4
A_GROUP = A_HEADS // A_KV
CMP_STRIDE = 16
CMP_LEN = 2 * CMP_STRIDE
SLC_BLK = 64
SLC_TOPN = 16
WINDOW = 512
PAGE_SIZE = 128

_BF16 = jnp.bfloat16
_F32 = jnp.float32
_VMEM_LIMIT = 56 * 1024 * 1024


def _dot(a, b):
    return jnp.dot(a.astype(_BF16), b.astype(_BF16), preferred_element_type=_F32)


_PT = 512


def _proj_in_kernel(groups, norm_tiles, x_ref, nw_ref, w_ref, wsm_ref, seg_ref, hw_ref, hm_ref,
                    a_ref, q_ref, az_ref, kv4_ref, kvw_ref, sm_ref, xn_ref):
    j = pl.program_id(1)

    @pl.when(j == 0)
    def _():
        x = x_ref[...]
        ms = jnp.mean(x * x, axis=-1, keepdims=True)
        xn = x * lax.rsqrt(ms + NORM_EPS) * nw_ref[...]
        xn_ref[...] = xn.astype(_BF16)
        sm = jnp.dot(xn_ref[...], wsm_ref[...], preferred_element_type=_F32)
        lane = lax.broadcasted_iota(jnp.int32, sm.shape, 1)
        is_gate = (lane >= 2 * M_HEADS) & (lane < 2 * M_HEADS + 3 * A_HEADS)
        sm_ref[...] = jnp.where(is_gate, jax.nn.sigmoid(sm), sm)

    y = jnp.dot(xn_ref[...], w_ref[...], preferred_element_type=_F32)
    outs = (a_ref, q_ref, az_ref, kv4_ref, kvw_ref)
    for (start, n), o_ref in zip(groups, outs):
        plain = [t for t in range(start, start + n) if t not in norm_tiles]
        normed = [t for t in range(start, start + n) if t in norm_tiles]
        if plain:
            cond = functools.reduce(jnp.logical_or, [j == t for t in plain])

            @pl.when(cond)
            def _(o_ref=o_ref):
                o_ref[...] = y

        if normed:
            cond = functools.reduce(jnp.logical_or, [j == t for t in normed])

            @pl.when(cond)
            def _(o_ref=o_ref):
                ms = jnp.dot((y * y).astype(_BF16), seg_ref[...], preferred_element_type=_F32)
                yn = y * lax.rsqrt(ms + NORM_EPS) * hw_ref[...]
                o_ref[...] = jnp.where(hm_ref[...] > 0, yn, y)


def _proj_in(x2d, norm_w, w_main, w_small, seg, hw, hm, groups, norm_tiles, tm):
    M, D = x2d.shape
    n_tiles = w_main.shape[1] // _PT
    widths = [n * _PT for _, n in groups]

    def out_map(start, n):
        return lambda i, j: (i, jnp.clip(j - start, 0, n - 1))

    out_specs = [pl.BlockSpec((tm, _PT), out_map(s, n)) for s, n in groups]
    out_specs.append(pl.BlockSpec((tm, 128), lambda i, j: (i, 0)))
    out_shape = [jax.ShapeDtypeStruct((M, w), _F32) for w in widths]
    out_shape.append(jax.ShapeDtypeStruct((M, 128), _F32))
    return pl.pallas_call(
        functools.partial(_proj_in_kernel, groups, norm_tiles),
        grid=(M // tm, n_tiles),
        in_specs=[
            pl.BlockSpec((tm, D), lambda i, j: (i, 0)),
            pl.BlockSpec((1, D), lambda i, j: (0, 0)),
            pl.BlockSpec((D, _PT), lambda i, j: (0, j)),
            pl.BlockSpec((D, 128), lambda i, j: (0, 0)),
            pl.BlockSpec((_PT, _PT), lambda i, j: (0, 0)),
            pl.BlockSpec((None, 1, _PT), lambda i, j: (j, 0, 0)),
            pl.BlockSpec((None, 1, _PT), lambda i, j: (j, 0, 0)),
        ],
        out_specs=out_specs,
        out_shape=out_shape,
        scratch_shapes=[pltpu.VMEM((tm, D), _BF16)],
        compiler_params=pltpu.CompilerParams(
            dimension_semantics=("arbitrary", "arbitrary"), vmem_limit_bytes=_VMEM_LIMIT),
        name="proj_in",
    )(x2d, norm_w.reshape(1, D), w_main, w_small, seg, hw, hm)


def _heads_to_agd(w, a_dh):
    lead = w.shape[:-1]
    return jnp.swapaxes(w.reshape(lead + (A_KV, A_GROUP, a_dh)), -3, -2).reshape(w.shape)


def _heads_to_gad(w, a_dh):
    lead = w.shape[:-1]
    return jnp.swapaxes(w.reshape(lead + (A_GROUP, A_KV, a_dh)), -3, -2).reshape(w.shape)


def _prep_proj_weights(w_in, a_q_norm, a_k_norm, m_width, a_width, a_dh):
    kvw = A_KV * a_dh
    sizes = (m_width, m_width, m_width, M_HEADS, M_HEADS, a_width, 6 * kvw, 3 * A_HEADS, a_width)
    offs = [0]
    for s in sizes:
        offs.append(offs[-1] + s)
    u0, o0, z0, i0, f0, q0, kv0, g0, az0, end = offs
    assert end == w_in.shape[1]
    w_main = jnp.concatenate([
        w_in[:, u0:i0], _heads_to_agd(w_in[:, q0:kv0], a_dh), _heads_to_agd(w_in[:, az0:end], a_dh),
        w_in[:, kv0:kv0 + 4 * kvw], w_in[:, kv0 + 4 * kvw:g0]], axis=1).astype(_BF16)
    n_small = 2 * M_HEADS + 3 * A_HEADS
    w_small = jnp.concatenate([
        w_in[:, i0:q0], w_in[:, g0:az0],
        jnp.zeros((w_in.shape[0], 128 - n_small), w_in.dtype)], axis=1).astype(_BF16)
    widths = (3 * m_width, a_width, a_width, 4 * kvw, 2 * kvw)
    groups, start = [], 0
    for w in widths:
        assert w % _PT == 0
        groups.append((start, w // _PT))
        start += w // _PT
    n_tiles = start
    hw = jnp.zeros((n_tiles * _PT,), _F32)
    hm = jnp.zeros((n_tiles * _PT,), _F32)
    qs = groups[1][0] * _PT
    hw = hw.at[qs:qs + a_width].set(jnp.tile(a_q_norm, a_width // a_dh))
    hm = hm.at[qs:qs + a_width].set(1.0)
    k4 = groups[3][0] * _PT
    hw = hw.at[k4 + 2 * kvw:k4 + 3 * kvw].set(jnp.tile(a_k_norm[1], A_KV))
    hm = hm.at[k4 + 2 * kvw:k4 + 3 * kvw].set(1.0)
    kw = groups[4][0] * _PT
    hw = hw.at[kw:kw + kvw].set(jnp.tile(a_k_norm[2], A_KV))
    hm = hm.at[kw:kw + kvw].set(1.0)
    assert kvw * 2 == _PT and _PT % a_dh == 0
    norm_tiles = tuple(range(groups[1][0], groups[1][0] + groups[1][1])) + (
        (k4 + 2 * kvw) // _PT, kw // _PT)
    r = jnp.arange(_PT) // a_dh
    seg = ((r[:, None] == r[None, :]).astype(_F32) / a_dh).astype(_BF16)
    return (w_main, w_small, seg, hw.reshape(n_tiles, 1, _PT), hm.reshape(n_tiles, 1, _PT),
            tuple(groups), norm_tiles)


def _out_proj_kernel(mw, x_ref, m_ref, a_ref, p_ref, wo_ref, pp_ref, pn_ref, pg_ref, o_ref):
    h = x_ref[...] + _dot(m_ref[...], wo_ref[0:mw, :]) + _dot(a_ref[...], wo_ref[mw:, :])
    ms = jnp.mean(h * h, axis=-1, keepdims=True)
    hn = h * lax.rsqrt(ms + NORM_EPS) * pn_ref[...]
    gate = jax.nn.sigmoid(_dot(hn, pg_ref[...]))
    o_ref[...] = h + gate * _dot(p_ref[...], pp_ref[...])


def _out_proj(x2d, m_out, a_out, p2d, w_out, ple_proj, ple_norm, ple_gate, tm):
    M, D = x2d.shape
    mw, aw, dp = m_out.shape[1], a_out.shape[1], p2d.shape[1]

    def const(shape):
        return pl.BlockSpec(shape, lambda i: (0, 0), pipeline_mode=pl.Buffered(1))

    return pl.pallas_call(
        functools.partial(_out_proj_kernel, mw),
        grid=(M // tm,),
        in_specs=[
            pl.BlockSpec((tm, D), lambda i: (i, 0)),
            pl.BlockSpec((tm, mw), lambda i: (i, 0)),
            pl.BlockSpec((tm, aw), lambda i: (i, 0)),
            pl.BlockSpec((tm, dp), lambda i: (i, 0)),
            const((mw + aw, D)),
            const((dp, D)),
            const((1, D)),
            const((D, D)),
        ],
        out_specs=pl.BlockSpec((tm, D), lambda i: (i, 0)),
        out_shape=jax.ShapeDtypeStruct((M, D), _F32),
        compiler_params=pltpu.CompilerParams(
            dimension_semantics=("arbitrary",), vmem_limit_bytes=_VMEM_LIMIT),
        name="out_proj",
    )(x2d, m_out, a_out, p2d, w_out.astype(_BF16), ple_proj.astype(_BF16),
      ple_norm.reshape(1, D), ple_gate.astype(_BF16))


_MLSTM_L = 256
_CONV_PAD = 8


def _mlstm_kernel(L, dh, n_steps,
                  u_ref, o_ref, z_ref, sm_ref, cbuf_ref, c0_ref, n0_ref, m0_ref,
                  wq_ref, wk_ref, wv_ref, cw_ref, cb_ref, bi_ref, bf_ref, nw_ref,
                  out_ref, cout_ref, nout_ref, mout_ref, convout_ref,
                  xbuf, c_s, n_s, m_s):
    h = pl.program_id(1)
    s = pl.program_id(2)
    tail = M_CONV - 1

    @pl.when(s == 0)
    def _():
        c_s[...] = c0_ref[...]
        n_s[...] = n0_ref[...]
        m_s[...] = m0_ref[...]
        xbuf[_CONV_PAD - tail:_CONV_PAD, :] = cbuf_ref[...]

    u = u_ref[...]
    xbuf[_CONV_PAD:_CONV_PAD + L, :] = u
    c = cb_ref[...]
    for j in range(M_CONV):
        c = c + xbuf[_CONV_PAD - tail + j:_CONV_PAD - tail + j + L, :] * cw_ref[j:j + 1, :]
    xbuf[_CONV_PAD - tail:_CONV_PAD, :] = u[L - tail:L, :]
    ch = c * jax.nn.sigmoid(c)
    q = _dot(ch, wq_ref[...])
    k = _dot(ch, wk_ref[...]) * (dh ** -0.5)
    v = _dot(u, wv_ref[...])

    sm = sm_ref[...]
    lane = lax.broadcasted_iota(jnp.int32, sm.shape, 1)
    i_pre = jnp.sum(jnp.where(lane == h, sm, 0.0), axis=1, keepdims=True)
    f_pre = jnp.sum(jnp.where(lane == h + M_HEADS, sm, 0.0), axis=1, keepdims=True)
    li_col = i_pre + bi_ref[:, 0:1]
    f_in = f_pre + bf_ref[:, 0:1]
    lf_col = jnp.minimum(f_in, 0.0) - jnp.log(1.0 + jnp.exp(-jnp.abs(f_in)))

    ii = lax.broadcasted_iota(jnp.int32, (L, L), 0)
    jj = lax.broadcasted_iota(jnp.int32, (L, L), 1)
    eye = ii == jj
    causal = jj <= ii
    lf_row = jnp.sum(jnp.where(eye, lf_col, 0.0), axis=0, keepdims=True)
    li_row = jnp.sum(jnp.where(eye, li_col, 0.0), axis=0, keepdims=True)
    b_col = jnp.sum(jnp.where(causal, lf_row, 0.0), axis=1, keepdims=True)
    b_row = jnp.sum(jnp.where(ii <= jj, lf_col, 0.0), axis=0, keepdims=True)

    m_prev = m_s[0:1, 0:1]
    c_prev = c_s[...]
    n_prev = n_s[...]
    logd = jnp.where(causal, b_col - b_row + li_row, -jnp.inf)
    inter = b_col + m_prev
    m_t = jnp.maximum(inter, jnp.max(logd, axis=1, keepdims=True))
    qb = q.astype(_BF16)
    sc = lax.dot_general(qb, k.astype(_BF16), (((1,), (1,)), ((), ())),
                         preferred_element_type=_F32) * jnp.exp(logd - m_t)
    a_col = jnp.exp(inter - m_t)
    num = _dot(sc, v) + a_col * _dot(qb, c_prev)
    den = jnp.sum(sc, axis=1, keepdims=True) + a_col * jnp.sum(q * n_prev, axis=1, keepdims=True)
    hh = num / jnp.maximum(jnp.abs(den), jnp.exp(-m_t))

    b_end = b_col[L - 1:L, :]
    logw = b_end - b_col + li_col
    m_new = jnp.maximum(b_end + m_prev, jnp.max(logw, axis=0, keepdims=True))
    w_col = jnp.exp(logw - m_new)
    decay = jnp.exp(b_end + m_prev - m_new)
    c_s[...] = decay * c_prev + lax.dot_general(
        k.astype(_BF16), (w_col * v).astype(_BF16), (((0,), (0,)), ((), ())),
        preferred_element_type=_F32)
    n_s[...] = decay * n_prev + jnp.sum(w_col * k, axis=0, keepdims=True)
    m_s[...] = jnp.broadcast_to(m_new, m_s.shape)

    hg = hh * jax.nn.sigmoid(o_ref[...])
    hn = hg * lax.rsqrt(jnp.mean(hg * hg, axis=-1, keepdims=True) + NORM_EPS) * nw_ref[...]
    zz = z_ref[...]
    out_ref[...] = hn * (zz * jax.nn.sigmoid(zz))

    @pl.when(s == n_steps - 1)
    def _():
        cout_ref[...] = c_s[...]
        nout_ref[...] = n_s[...]
        mout_ref[...] = m_s[...]
        convout_ref[...] = u[L - tail:L, :]


def _mlstm(a3, sm, conv_buf, C0, n0, m0, conv_w, conv_b, wq, wk, wv, b_i, b_f, norm_w, L):
    B, T, mw3 = a3.shape
    mw = mw3 // 3
    dh = mw // M_HEADS
    tail = M_CONV - 1
    assert T % L == 0 and L >= tail and (L % 8 == 0)
    n_steps = T // L
    H = M_HEADS

    def lane_b(vec):
        return jnp.broadcast_to(vec.astype(_F32)[:, None, None], (H, 1, 128))

    m0b = jnp.broadcast_to(m0.astype(_F32)[:, :, None, None], (B, H, 1, 128))
    blk = lambda off: pl.BlockSpec((None, L, dh), lambda b, h, s: (b, s, off * H + h))
    per_h = lambda r, c: pl.BlockSpec((None, r, c), lambda b, h, s: (h, 0, 0))
    st = lambda r, c: pl.BlockSpec((None, None, r, c), lambda b, h, s: (b, h, 0, 0))
    out, C, n, m, conv = pl.pallas_call(
        functools.partial(_mlstm_kernel, L, dh, n_steps),
        grid=(B, H, n_steps),
        in_specs=[
            blk(0), blk(1), blk(2),
            pl.BlockSpec((None, L, 128), lambda b, h, s: (b, s, 0)),
            pl.BlockSpec((None, tail, dh), lambda b, h, s: (b, 0, h)),
            st(dh, dh), st(1, dh), st(1, 128),
            per_h(dh, dh), per_h(dh, dh), per_h(dh, dh),
            pl.BlockSpec((M_CONV, dh), lambda b, h, s: (0, h)),
            pl.BlockSpec((1, dh), lambda b, h, s: (0, h)),
            per_h(1, 128), per_h(1, 128), per_h(1, dh),
        ],
        out_specs=[
            pl.BlockSpec((None, L, dh), lambda b, h, s: (b, s, h)),
            st(dh, dh), st(1, dh), st(1, 128),
            pl.BlockSpec((None, tail, dh), lambda b, h, s: (b, 0, h)),
        ],
        out_shape=[
            jax.ShapeDtypeStruct((B, T, mw), _F32),
            jax.ShapeDtypeStruct((B, H, dh, dh), _F32),
            jax.ShapeDtypeStruct((B, H, 1, dh), _F32),
            jax.ShapeDtypeStruct((B, H, 1, 128), _F32),
            jax.ShapeDtypeStruct((B, tail, mw), _F32),
        ],
        scratch_shapes=[
            pltpu.VMEM((_CONV_PAD + L, dh), _F32),
            pltpu.VMEM((dh, dh), _F32),
            pltpu.VMEM((1, dh), _F32),
            pltpu.VMEM((1, 128), _F32),
        ],
        compiler_params=pltpu.CompilerParams(
            dimension_semantics=("arbitrary", "arbitrary", "arbitrary"), vmem_limit_bytes=_VMEM_LIMIT),
        name="mlstm",
    )(a3, a3, a3, sm, conv_buf, C0, n0.reshape(B, H, 1, dh), m0b,
      wq.astype(_BF16), wk.astype(_BF16), wv.astype(_BF16), conv_w, conv_b.reshape(1, mw),
      lane_b(b_i), lane_b(b_f), norm_w.reshape(H, 1, dh))
    return out, conv, C, n.reshape(B, H, dh), m[:, :, 0, 0]


def _prep_cmp_weights(pe, w1, w2, k_norm):
    a_dh, hid = w1.shape[-2], w1.shape[-1]
    w1r = w1.reshape(2, 2, CMP_STRIDE, a_dh, hid)
    eye2 = jnp.eye(2, dtype=w1.dtype)
    w1p = jnp.einsum('cxjdh,ab->cjadxbh', w1r, eye2).reshape(2, CMP_STRIDE * 2 * a_dh, 2 * 2 * hid)
    bias = jnp.einsum('cxjd,cxjdh->ch', pe.reshape(2, 2, CMP_STRIDE, a_dh), w1r)
    bias2 = jnp.tile(bias, (1, 2)).reshape(2, 1, 2 * hid)
    w2p = jnp.einsum('chd,ab->cahbd', w2, eye2).reshape(2, 2 * hid, 2 * a_dh)
    r = jnp.arange(2 * a_dh) // a_dh
    seg = ((r[:, None] == r[None, :]).astype(_F32) / a_dh).astype(_BF16)
    kn = jnp.tile(k_norm, 2).reshape(1, 2 * a_dh)
    return w1p.astype(_BF16), bias2.astype(_F32), w2p.astype(_BF16), seg, kn.astype(_F32)


def _cmp_first_layer(rows_ref, n_seg, w1_ref, c):
    x = jnp.concatenate(
        [rows_ref[pl.ds(j, n_seg, stride=CMP_STRIDE), :].astype(_BF16)
         for j in range(CMP_STRIDE)], axis=1)
    return jnp.dot(x, w1_ref[c], preferred_element_type=_F32)


def _cmp_second_layer(p, n_seg, c, b_ref, w2_ref, seg_ref, kn_ref):
    hid = p[:, 0:128] + pltpu.roll(p[:, 128:256], n_seg - 1, axis=0) + b_ref[c]
    act = hid * jax.nn.sigmoid(hid)
    out = jnp.dot(act.astype(_BF16), w2_ref[c], preferred_element_type=_F32)
    if c == 0:
        ms = jnp.dot((out * out).astype(_BF16), seg_ref[...], preferred_element_type=_F32)
        out = out * lax.rsqrt(ms + NORM_EPS) * kn_ref[...]
    return out


def _compress_prompt_kernel(n_seg, r00, r01, r10, r11, w1_ref, b_ref, w2_ref, seg_ref, kn_ref,
                            kc_ref, vc_ref):
    for c, o_ref, refs in ((0, kc_ref, (r00, r01)), (1, vc_ref, (r10, r11))):
        for gp in range(2):
            p = _cmp_first_layer(refs[gp], n_seg, w1_ref, c)
            o_ref[:, gp * 128:(gp + 1) * 128] = _cmp_second_layer(p, n_seg, c, b_ref, w2_ref, seg_ref, kn_ref)


def _compress_prompt(kv4, cwp):
    B, T, W = kv4.shape
    assert W == 1024 and T % (CMP_STRIDE * 8) == 0
    n_seg = T // CMP_STRIDE
    w1p, bias2, w2p, seg, kn = cwp
    full = lambda a: pl.BlockSpec(a.shape, lambda b: (0,) * a.ndim)
    return pl.pallas_call(
        functools.partial(_compress_prompt_kernel, n_seg),
        grid=(B,),
        in_specs=[pl.BlockSpec((None, T, 128), functools.partial(lambda k, b: (b, 0, k), k))
                  for k in range(4)] + [full(w1p), full(bias2), full(w2p), full(seg), full(kn)],
        out_specs=[pl.BlockSpec((None, n_seg, 256), lambda b: (b, 0, 0))] * 2,
        out_shape=[jax.ShapeDtypeStruct((B, n_seg, 256), _F32)] * 2,
        compiler_params=pltpu.CompilerParams(
            dimension_semantics=("arbitrary",), vmem_limit_bytes=_VMEM_LIMIT),
        name="compress_prompt",
    )(kv4, kv4, kv4, kv4, w1p, bias2, w2p, seg, kn)


_QB = 128
_SEL_TK = 512
_GW = 32


def _softmax_rows(s, mask):
    s = jnp.where(mask, s, -jnp.inf)
    m = jnp.max(s, axis=-1, keepdims=True)
    m = jnp.where(m == -jnp.inf, 0.0, m)
    e = jnp.where(mask, jnp.exp(s - m), 0.0)
    return e / jnp.maximum(jnp.sum(e, axis=-1, keepdims=True), 1e-30)


def _nsa_prompt_kernel(T, n_cmp, n_slc,
                       q_ref, ks_ref, kw_ref, kc_ref, vc_ref, sm_ref, az_ref, out_ref,
                       qpl_ref, m_ref, l_ref, acc_ref):
    QB, TK, G, A = _QB, _SEL_TK, A_KV, A_GROUP
    R = A * QB
    t0 = pl.program_id(1) * QB
    lane256 = lax.broadcasted_iota(jnp.int32, (QB, 256), 1)
    q = q_ref[...]
    for g in range(G):
        gm = (lane256 // 64) == g
        qpl_ref[g] = jnp.concatenate(
            [jnp.where(gm, q[:, a * 256:(a + 1) * 256] * 0.125, 0.0) for a in range(A)],
            axis=0).astype(_BF16)
    tq_r = t0 + lax.broadcasted_iota(jnp.int32, (R, 1), 0) % QB
    tq = t0 + lax.broadcasted_iota(jnp.int32, (QB, 1), 0)

    n_seg = kc_ref.shape[0]
    kc = kc_ref[...].astype(_BF16)
    vc = vc_ref[...].astype(_BF16)
    ci = lax.broadcasted_iota(jnp.int32, (R, n_seg), 1)
    cmask = (ci < n_cmp) & (ci * CMP_STRIDE + (CMP_LEN - 1) <= tq_r)
    mi = lax.broadcasted_iota(jnp.int32, (n_seg, G * _GW), 0)
    mj = lax.broadcasted_iota(jnp.int32, (n_seg, G * _GW), 1)
    ratio = SLC_BLK // CMP_STRIDE
    o_cmp = []
    ps = jnp.zeros((QB, G * _GW), _F32)
    for g in range(G):
        s = lax.dot_general(qpl_ref[g], kc, (((1,), (1,)), ((), ())), preferred_element_type=_F32)
        p = _softmax_rows(s, cmask)
        o_cmp.append(jnp.dot(p.astype(_BF16), vc, preferred_element_type=_F32))
        pg = p[0:QB]
        for a in range(1, A):
            pg = pg + p[a * QB:(a + 1) * QB]
        blk = mj - g * _GW
        msel = ((mj // _GW == g) & (mi >= ratio * blk - 1) & (mi <= ratio * blk + ratio - 1)
                ).astype(_BF16)
        pg_hi = pg.astype(_BF16)
        pg_lo = (pg - pg_hi.astype(_F32)).astype(_BF16)
        ps = ps + jnp.dot(pg_hi, msel, preferred_element_type=_F32) \
                + jnp.dot(pg_lo, msel, preferred_element_type=_F32)

    lane = lax.broadcasted_iota(jnp.int32, (QB, G * _GW), 1)
    blk = lane % _GW
    cur = tq // SLC_BLK
    valid = (blk < n_slc) & (blk * SLC_BLK <= tq)
    forced = (blk == 0) | (blk == cur) | (blk == cur - 1)
    score = jnp.where(valid, jnp.where(forced, jnp.inf, ps), -jnp.inf)
    rank = jnp.zeros((QB, G * _GW), _F32)
    for r in range(1, _GW):
        other = jnp.where(blk >= r, pltpu.roll(score, r, axis=1),
                          pltpu.roll(score, (r - _GW) % (G * _GW), axis=1))
        ahead = (other > score) | ((other == score) & (blk >= r))
        rank = rank + jnp.where(ahead, 1.0, 0.0)
    sel = jnp.where(rank < min(SLC_TOPN, n_slc), 1.0, 0.0).astype(_BF16)

    m_ref[...] = jnp.full(m_ref.shape, -jnp.inf, _F32)
    l_ref[...] = jnp.zeros(l_ref.shape, _F32)
    acc_ref[...] = jnp.zeros(acc_ref.shape, _F32)
    er = lax.broadcasted_iota(jnp.int32, (G * _GW, TK), 0)
    ec = lax.broadcasted_iota(jnp.int32, (G * _GW, TK), 1)
    kcol = lax.broadcasted_iota(jnp.int32, (R, TK), 1)

    def sel_tile(kt, carry):
        k0 = pl.multiple_of(kt * TK, TK)
        kt_b = ks_ref[pl.ds(k0, TK), 0:256].astype(_BF16)
        vt_b = ks_ref[pl.ds(k0, TK), 256:512].astype(_BF16)
        causal = (k0 + kcol) <= tq_r
        for g in range(G):
            expand = (er == g * _GW + (k0 + ec) // SLC_BLK).astype(_BF16)
            hit = jnp.dot(sel, expand, preferred_element_type=_F32)
            mask = causal & (jnp.concatenate([hit] * A, axis=0) > 0.5)
            s = lax.dot_general(qpl_ref[g], kt_b, (((1,), (1,)), ((), ())), preferred_element_type=_F32)
            s = jnp.where(mask, s, -jnp.inf)
            m_old = m_ref[g]
            m_new = jnp.maximum(m_old, jnp.max(s, axis=-1, keepdims=True))
            m_safe = jnp.where(m_new == -jnp.inf, 0.0, m_new)
            p = jnp.where(mask, jnp.exp(s - m_safe), 0.0)
            alpha = jnp.exp(m_old - m_safe)
            l_ref[g] = alpha * l_ref[g] + jnp.sum(p, axis=-1, keepdims=True)
            acc_ref[g] = alpha * acc_ref[g] + jnp.dot(p.astype(_BF16), vt_b, preferred_element_type=_F32)
            m_ref[g] = m_new
        return carry

    lax.fori_loop(0, (t0 + QB + TK - 1) // TK, sel_tile, 0)

    WK = WINDOW + QB
    ws = pl.multiple_of(jnp.clip(t0 - WINDOW, 0, T - WK), QB)
    kw_b = kw_ref[pl.ds(ws, WK), 0:256].astype(_BF16)
    vw_b = kw_ref[pl.ds(ws, WK), 256:512].astype(_BF16)
    diff = tq_r - (ws + lax.broadcasted_iota(jnp.int32, (R, WK), 1))
    wmask = (diff >= 0) & (diff < WINDOW)

    sm = sm_ref[...]
    g0 = 2 * M_HEADS
    for g in range(G):
        gm = (lane256 // 64) == g
        s = lax.dot_general(qpl_ref[g], kw_b, (((1,), (1,)), ((), ())), preferred_element_type=_F32)
        p = _softmax_rows(s, wmask)
        o_win = jnp.dot(p.astype(_BF16), vw_b, preferred_element_type=_F32)
        o_sel = acc_ref[g] / jnp.maximum(l_ref[g], 1e-30)
        for a in range(A):
            head = g0 + g * A + a
            rows = slice(a * QB, (a + 1) * QB)
            mix = (sm[:, head:head + 1] * o_cmp[g][rows]
                   + sm[:, head + A_HEADS:head + A_HEADS + 1] * o_sel[rows]
                   + sm[:, head + 2 * A_HEADS:head + 2 * A_HEADS + 1] * o_win[rows])
            mix = jnp.where(gm, mix, 0.0)
            cols = slice(a * 256, (a + 1) * 256)
            if g == 0:
                out_ref[:, cols] = mix
            else:
                out_ref[:, cols] += mix
    az = az_ref[...]
    out_ref[...] = out_ref[...] * (az * jax.nn.sigmoid(az))


def _nsa_prompt(q, kv4, kvw, kc, vc, sm, az):
    B, T, _ = q.shape
    n_seg = kc.shape[1]
    n_cmp = n_seg - 1
    n_slc = T // SLC_BLK
    assert T % _SEL_TK == 0 and T >= WINDOW + _QB and n_slc <= _GW and A_KV * _GW == 128
    assert A_KV * 64 == 256 and n_seg % 8 == 0
    R = A_GROUP * _QB
    blk = lambda w: pl.BlockSpec((None, _QB, w), lambda b, i: (b, i, 0))
    return pl.pallas_call(
        functools.partial(_nsa_prompt_kernel, T, n_cmp, n_slc),
        grid=(B, T // _QB),
        in_specs=[
            blk(1024),
            pl.BlockSpec((None, T, 512), lambda b, i: (b, 0, 1)),
            pl.BlockSpec((None, T, 512), lambda b, i: (b, 0, 0)),
            pl.BlockSpec((None, n_seg, 256), lambda b, i: (b, 0, 0)),
            pl.BlockSpec((None, n_seg, 256), lambda b, i: (b, 0, 0)),
            blk(128), blk(1024),
        ],
        out_specs=blk(1024),
        out_shape=jax.ShapeDtypeStruct((B, T, 1024), _F32),
        scratch_shapes=[
            pltpu.VMEM((A_KV, R, 256), _BF16),
            pltpu.VMEM((A_KV, R, 1), _F32),
            pltpu.VMEM((A_KV, R, 1), _F32),
            pltpu.VMEM((A_KV, R, 256), _F32),
        ],
        compiler_params=pltpu.CompilerParams(
            dimension_semantics=("arbitrary", "arbitrary"), vmem_limit_bytes=_VMEM_LIMIT),
        name="nsa_prompt",
    )(q, kv4, kvw, kc, vc, sm, az)


_CMP_PAGES = 32


def _compress_sample_kernel(P, n_chunks, n_seg, pt_ref, cache_ref, w1_ref, b_ref, w2_ref, seg_ref,
                            kn_ref, kc_ref, vc_ref, stage, lanes, per, sem):
    b = pl.program_id(0)
    ch = pl.program_id(1)
    step = b * n_chunks + ch
    n_steps = pl.num_programs(0) * n_chunks
    slot = step % 2
    rows = P * PAGE_SIZE

    def page_copy(bb, cc, sl, p):
        page = pt_ref[bb, cc * P + p]
        return pltpu.make_async_copy(cache_ref.at[page, :, pl.ds(0, 512)],
                                     stage.at[sl, pl.ds(p * PAGE_SIZE, PAGE_SIZE), :], sem.at[sl])

    @pl.when(step == 0)
    def _():
        for p in range(P):
            page_copy(b, ch, slot, p).start()

    @pl.when(step + 1 < n_steps)
    def _():
        nxt = step + 1
        for p in range(P):
            page_copy(nxt // n_chunks, nxt % n_chunks, 1 - slot, p).start()

    for p in range(P):
        page_copy(b, ch, slot, p).wait()

    segs = rows // CMP_STRIDE
    for k in range(4):
        lanes[k] = stage[slot, :, k * 128:(k + 1) * 128]
    for k in range(4):
        per[pl.ds(pl.multiple_of(ch * segs, segs), segs), k * 256:(k + 1) * 256] = _cmp_first_layer(
            lanes.at[k], segs, w1_ref, k // 2)

    @pl.when(ch == n_chunks - 1)
    def _():
        for k in range(4):
            c, gp = k // 2, k % 2
            o_ref = kc_ref if c == 0 else vc_ref
            o_ref[:, gp * 128:(gp + 1) * 128] = _cmp_second_layer(
                per[:, k * 256:(k + 1) * 256], n_seg, c, b_ref, w2_ref, seg_ref, kn_ref)


def _compress_sample(cache3, page_table, cwp):
    B, n_pages = page_table.shape
    P = _CMP_PAGES if n_pages % _CMP_PAGES == 0 else n_pages
    n_chunks = n_pages // P
    n_seg = n_pages * PAGE_SIZE // CMP_STRIDE
    w1p, bias2, w2p, seg, kn = cwp
    full = lambda a: pl.BlockSpec(a.shape, lambda b, c, pt: (0,) * a.ndim)
    grid_spec = pltpu.PrefetchScalarGridSpec(
        num_scalar_prefetch=1,
        grid=(B, n_chunks),
        in_specs=[pl.BlockSpec(memory_space=pl.ANY),
                  full(w1p), full(bias2), full(w2p), full(seg), full(kn)],
        out_specs=[pl.BlockSpec((None, n_seg, 256), lambda b, c, pt: (b, 0, 0))] * 2,
        scratch_shapes=[
            pltpu.VMEM((2, P * PAGE_SIZE, 512), _F32),
            pltpu.VMEM((4, P * PAGE_SIZE, 128), _F32),
            pltpu.VMEM((n_seg, 1024), _F32),
            pltpu.SemaphoreType.DMA((2,)),
        ])
    return pl.pallas_call(
        functools.partial(_compress_sample_kernel, P, n_chunks, n_seg),
        grid_spec=grid_spec,
        out_shape=[jax.ShapeDtypeStruct((B, n_seg, 256), _F32)] * 2,
        compiler_params=pltpu.CompilerParams(
            dimension_semantics=("arbitrary", "arbitrary"), vmem_limit_bytes=_VMEM_LIMIT),
        name="compress_sample",
    )(page_table, cache3, w1p, bias2, w2p, seg, kn)


def _merge_heads(o_ref, o_g, g, Ts):
    gm = (lax.broadcasted_iota(jnp.int32, (Ts, 256), 1) // 64) == g
    for a in range(A_GROUP):
        part = jnp.where(gm, o_g[a * Ts:(a + 1) * Ts], 0.0)
        cols = slice(a * 256, (a + 1) * 256)
        if g == 0:
            o_ref[:, cols] = part
        else:
            o_ref[:, cols] += part


def _nsa_sample_front_kernel(Ts, past_len, n_cmp, n_slc, NL, topn,
                             q_ref, kc_ref, vc_ref, wold_ref, wnew_ref,
                             ocmp_ref, owin_ref, idx_ref, qpl_ref):
    G, A = A_KV, A_GROUP
    R = A * Ts
    lane256 = lax.broadcasted_iota(jnp.int32, (Ts, 256), 1)
    q = q_ref[...]
    for g in range(G):
        gm = (lane256 // 64) == g
        qpl_ref[g] = jnp.concatenate(
            [jnp.where(gm, q[:, a * 256:(a + 1) * 256] * 0.125, 0.0) for a in range(A)],
            axis=0).astype(_BF16)
    tq_r = past_len + lax.broadcasted_iota(jnp.int32, (R, 1), 0) % Ts
    tq = past_len + lax.broadcasted_iota(jnp.int32, (Ts, 1), 0)

    n_seg = kc_ref.shape[0]
    kc = kc_ref[...].astype(_BF16)
    vc = vc_ref[...].astype(_BF16)
    ci = lax.broadcasted_iota(jnp.int32, (R, n_seg), 1)
    cmask = (ci < n_cmp) & (ci * CMP_STRIDE + (CMP_LEN - 1) <= tq_r)
    ratio = SLC_BLK // CMP_STRIDE
    mi = lax.broadcasted_iota(jnp.int32, (n_seg, NL), 0)
    mj = lax.broadcasted_iota(jnp.int32, (n_seg, NL), 1)
    msel = ((mi >= ratio * mj - 1) & (mi <= ratio * mj + ratio - 1)).astype(_BF16)
    lane = lax.broadcasted_iota(jnp.int32, (Ts, NL), 1)
    cur = tq // SLC_BLK
    valid = (lane < n_slc) & (lane * SLC_BLK <= tq)
    forced = (lane == 0) | (lane == cur) | (lane == cur - 1)
    scores = []
    for g in range(G):
        s = lax.dot_general(qpl_ref[g], kc, (((1,), (1,)), ((), ())), preferred_element_type=_F32)
        p = _softmax_rows(s, cmask)
        _merge_heads(ocmp_ref, jnp.dot(p.astype(_BF16), vc, preferred_element_type=_F32), g, Ts)
        pg = p[0:Ts]
        for a in range(1, A):
            pg = pg + p[a * Ts:(a + 1) * Ts]
        pg_hi = pg.astype(_BF16)
        pg_lo = (pg - pg_hi.astype(_F32)).astype(_BF16)
        ps = jnp.dot(pg_hi, msel, preferred_element_type=_F32) \
            + jnp.dot(pg_lo, msel, preferred_element_type=_F32)
        scores.append(jnp.where(valid, jnp.where(forced, jnp.inf, ps), -jnp.inf))
    score = jnp.concatenate(scores, axis=0)

    lane_f = lax.broadcasted_iota(jnp.int32, (G * Ts, NL), 1).astype(_F32)
    out_lane = lax.broadcasted_iota(jnp.int32, (G * Ts, 128), 1)
    avail = lane_f < n_slc
    picks = jnp.zeros((G * Ts, 128), _F32)
    for k in range(topn):
        mx = jnp.max(jnp.where(avail, score, -jnp.inf), axis=1, keepdims=True)
        pick = jnp.min(jnp.where(avail & (score == mx), lane_f, float(NL)), axis=1, keepdims=True)
        avail = avail & (lane_f != pick)
        picks = jnp.where(out_lane == k, pick, picks)
    idx_ref[...] = picks.astype(jnp.int32)

    wb = wold_ref.shape[0]
    k_old = wold_ref[:, 0:256].astype(_BF16)
    v_old = wold_ref[:, 256:512].astype(_BF16)
    k_new = wnew_ref[:, 0:256]
    v_new = wnew_ref[:, 256:512]
    pw_old = past_len - wb + lax.broadcasted_iota(jnp.int32, (R, wb), 1)
    d_old = tq_r - pw_old
    m_old = (pw_old >= 0) & (d_old >= 0) & (d_old < WINDOW)
    d_new = tq_r - (past_len + lax.broadcasted_iota(jnp.int32, (R, Ts), 1))
    m_new = (d_new >= 0) & (d_new < WINDOW)
    for g in range(G):
        qg = qpl_ref[g]
        s1 = lax.dot_general(qg, k_old, (((1,), (1,)), ((), ())), preferred_element_type=_F32)
        s2 = lax.dot_general(qg.astype(_F32), k_new, (((1,), (1,)), ((), ())),
                             preferred_element_type=_F32)
        s1 = jnp.where(m_old, s1, -jnp.inf)
        s2 = jnp.where(m_new, s2, -jnp.inf)
        mx = jnp.maximum(jnp.max(s1, axis=1, keepdims=True), jnp.max(s2, axis=1, keepdims=True))
        mx = jnp.where(mx == -jnp.inf, 0.0, mx)
        e1 = jnp.where(m_old, jnp.exp(s1 - mx), 0.0)
        e2 = jnp.where(m_new, jnp.exp(s2 - mx), 0.0)
        den = jnp.maximum(jnp.sum(e1, axis=1, keepdims=True) + jnp.sum(e2, axis=1, keepdims=True), 1e-30)
        o = (jnp.dot(e1.astype(_BF16), v_old, preferred_element_type=_F32)
             + jnp.dot(e2, v_new, preferred_element_type=_F32)) / den
        _merge_heads(owin_ref, o, g, Ts)


def _nsa_sample_front(q, kc, vc, win_old, win_new, past_len):
    B, Ts, _ = q.shape
    n_seg = kc.shape[1]
    n_cmp = (past_len + Ts) // CMP_STRIDE - 1
    assert n_cmp == n_seg - 1 and Ts % 8 == 0 and Ts <= SLC_BLK
    n_slc = past_len // SLC_BLK + 1
    NL = -(-n_slc // 128) * 128
    topn = min(SLC_TOPN, n_slc)
    wb = win_old.shape[1]
    R = A_GROUP * Ts
    b3 = lambda r, w: pl.BlockSpec((None, r, w), lambda b: (b, 0, 0))
    return pl.pallas_call(
        functools.partial(_nsa_sample_front_kernel, Ts, past_len, n_cmp, n_slc, NL, topn),
        grid=(B,),
        in_specs=[b3(Ts, 1024), b3(n_seg, 256), b3(n_seg, 256), b3(wb, 512), b3(Ts, 512)],
        out_specs=[b3(Ts, 1024), b3(Ts, 1024), b3(A_KV * Ts, 128)],
        out_shape=[jax.ShapeDtypeStruct((B, Ts, 1024), _F32), jax.ShapeDtypeStruct((B, Ts, 1024), _F32),
                   jax.ShapeDtypeStruct((B, A_KV * Ts, 128), jnp.int32)],
        scratch_shapes=[pltpu.VMEM((A_KV, R, 256), _BF16)],
        compiler_params=pltpu.CompilerParams(
            dimension_semantics=("arbitrary",), vmem_limit_bytes=_VMEM_LIMIT),
        name="nsa_sample_front",
    )(q, kc, vc, win_old, win_new)


def _nsa_sample_gather_kernel(Ts, n_past_blk, past_len, topn,
                              idx_ref, pt_ref, q_ref, kvn_ref, cache_ref, o_ref, kbuf, vbuf, sem):
    b = pl.program_id(0)
    n_it = A_KV * Ts
    bpp = PAGE_SIZE // SLC_BLK
    nk = topn * SLC_BLK

    def block_copies(it, sl, k):
        g = it // Ts
        blk = jnp.minimum(idx_ref[b, it * topn + k], n_past_blk - 1)
        page = pt_ref[b, blk // bpp]
        r0 = (blk % bpp) * SLC_BLK
        dst = pl.ds(k * SLC_BLK, SLC_BLK)
        return (pltpu.make_async_copy(cache_ref.at[page, pl.ds(r0, SLC_BLK), 2 * A_KV + g, :],
                                      kbuf.at[sl, dst, :], sem.at[sl]),
                pltpu.make_async_copy(cache_ref.at[page, pl.ds(r0, SLC_BLK), 3 * A_KV + g, :],
                                      vbuf.at[sl, dst, :], sem.at[sl]))

    def issue(it, sl):
        for k in range(topn):
            ck, cv = block_copies(it, sl, k)
            ck.start()
            cv.start()

    issue(0, 0)
    lane = lax.broadcasted_iota(jnp.int32, (1, nk), 1)
    new_r = lax.broadcasted_iota(jnp.int32, (1, Ts), 1)

    def body(it, carry):
        sl = it % 2

        @pl.when(it + 1 < n_it)
        def _():
            issue(it + 1, 1 - sl)

        for k in range(topn):
            ck, cv = block_copies(it, sl, k)
            ck.wait()
            cv.wait()

        t = it % Ts
        g = it // Ts
        tq = past_len + t
        pos = jnp.zeros((1, nk), jnp.int32)
        past = jnp.zeros((1, nk), jnp.int32)
        new_sel = jnp.int32(0)
        for k in range(topn):
            blk = idx_ref[b, it * topn + k]
            in_slot = (lane // SLC_BLK) == k
            pos = jnp.where(in_slot, blk * SLC_BLK + lane % SLC_BLK, pos)
            past = jnp.where(in_slot, (blk < n_past_blk).astype(jnp.int32), past)
            new_sel = jnp.maximum(new_sel, (blk == n_past_blk).astype(jnp.int32))
        m1 = (past > 0) & (pos <= tq)
        m2 = (new_sel > 0) & (n_past_blk * SLC_BLK + new_r <= tq)
        q8 = q_ref[it] * 0.125
        s1 = lax.dot_general(q8.astype(_BF16), kbuf[sl].astype(_BF16), (((1,), (1,)), ((), ())),
                             preferred_element_type=_F32)
        s2 = lax.dot_general(q8, kvn_ref[2 * A_KV + g], (((1,), (1,)), ((), ())),
                             preferred_element_type=_F32)
        s1 = jnp.where(m1, s1, -jnp.inf)
        s2 = jnp.where(m2, s2, -jnp.inf)
        mx = jnp.maximum(jnp.max(s1, axis=1, keepdims=True), jnp.max(s2, axis=1, keepdims=True))
        mx = jnp.where(mx == -jnp.inf, 0.0, mx)
        e1 = jnp.where(m1, jnp.exp(s1 - mx), 0.0)
        e2 = jnp.where(m2, jnp.exp(s2 - mx), 0.0)
        den = jnp.maximum(jnp.sum(e1, axis=1, keepdims=True) + jnp.sum(e2, axis=1, keepdims=True), 1e-30)
        o_ref[it] = (jnp.dot(e1.astype(_BF16), vbuf[sl].astype(_BF16), preferred_element_type=_F32)
                     + jnp.dot(e2, kvn_ref[3 * A_KV + g], preferred_element_type=_F32)) / den
        return carry

    lax.fori_loop(0, n_it, body, 0)


def _nsa_sample_gather(idx, page_table, q_rows, kv_new, cache4, past_len):
    B, n_it, _, dh = q_rows.shape
    Ts = n_it // A_KV
    n_past_blk = past_len // SLC_BLK
    topn = idx.shape[1] // n_it
    grid_spec = pltpu.PrefetchScalarGridSpec(
        num_scalar_prefetch=2,
        grid=(B,),
        in_specs=[pl.BlockSpec((None, n_it, 8, dh), lambda b, i, p: (b, 0, 0, 0)),
                  pl.BlockSpec((None, 4 * A_KV, Ts, dh), lambda b, i, p: (b, 0, 0, 0)),
                  pl.BlockSpec(memory_space=pl.ANY)],
        out_specs=pl.BlockSpec((None, n_it, 8, dh), lambda b, i, p: (b, 0, 0, 0)),
        scratch_shapes=[
            pltpu.VMEM((2, topn * SLC_BLK, dh), _F32),
            pltpu.VMEM((2, topn * SLC_BLK, dh), _F32),
            pltpu.SemaphoreType.DMA((2,)),
        ])
    return pl.pallas_call(
        functools.partial(_nsa_sample_gather_kernel, Ts, n_past_blk, past_len, topn),
        grid_spec=grid_spec,
        out_shape=jax.ShapeDtypeStruct((B, n_it, 8, dh), _F32),
        compiler_params=pltpu.CompilerParams(
            dimension_semantics=("arbitrary",), vmem_limit_bytes=_VMEM_LIMIT),
        name="nsa_sample_gather",
    )(idx, page_table, q_rows, kv_new, cache4)


def _nsa_combine_kernel(ocmp_ref, osel_ref, owin_ref, sm_ref, az_ref, o_ref):
    sm = sm_ref[...]
    shape = ocmp_ref.shape
    head_lane = lax.broadcasted_iota(jnp.int32, shape, 1) // 64
    g0 = 2 * M_HEADS
    acc = jnp.zeros(shape, _F32)
    for br, ref in enumerate((ocmp_ref, osel_ref, owin_ref)):
        gate = jnp.zeros(shape, _F32)
        for a in range(A_GROUP):
            for g in range(A_KV):
                c = g0 + br * A_HEADS + g * A_GROUP + a
                gate = jnp.where(head_lane == a * A_KV + g, sm[:, c:c + 1], gate)
        acc = acc + gate * ref[...]
    az = az_ref[...]
    o_ref[...] = acc * (az * jax.nn.sigmoid(az))


def _nsa_combine(ocmp, osel, owin, sm, az):
    M, W = ocmp.shape
    full = lambda w: pl.BlockSpec((M, w), lambda i: (0, 0))
    return pl.pallas_call(
        _nsa_combine_kernel,
        grid=(1,),
        in_specs=[full(W), full(W), full(W), full(128), full(W)],
        out_specs=full(W),
        out_shape=jax.ShapeDtypeStruct((M, W), _F32),
        name="nsa_combine",
    )(ocmp, osel, owin, sm, az)


def _rms_norm(x, w):
    xf = x.astype(_F32)
    y = xf * lax.rsqrt(jnp.mean(xf * xf, axis=-1, keepdims=True) + NORM_EPS)
    return (y * w.astype(_F32)).astype(x.dtype)


def _masked_softmax(s, mask):
    s = jnp.where(mask, s.astype(_F32), -jnp.inf)
    m = jnp.max(s, axis=-1, keepdims=True)
    m = jnp.where(jnp.isfinite(m), m, 0.0)
    e = jnp.where(mask, jnp.exp(s - m), 0.0)
    return e / jnp.maximum(jnp.sum(e, axis=-1, keepdims=True), 1e-30)


def _compress(rows, pe, w1, w2, k_norm):
    B, L = rows.shape[:2]
    a_dh = rows.shape[-1]
    hid_n = w1.shape[-1]
    n_seg = L // CMP_STRIDE
    seg = rows[:, :n_seg * CMP_STRIDE].reshape(B, n_seg, CMP_STRIDE, 2, A_KV, a_dh)
    w1r = w1.reshape(2, 2, CMP_STRIDE, a_dh, hid_n)
    per = jnp.einsum('bnjcgd,cxjdh->bnxcgh', seg, w1r)
    bias = jnp.einsum('cxjd,cxjdh->ch', pe.reshape(2, 2, CMP_STRIDE, a_dh), w1r)
    hid = per[:, :-1, 0] + per[:, 1:, 1] + bias[:, None, :]
    out = jnp.einsum('bncgh,chd->bncgd', jax.nn.silu(hid), w2)
    return _rms_norm(out[:, :, 0], k_norm), out[:, :, 1]


def _select_blocks(pg, tq, n_slc):
    Tq, _, n_cmp = pg.shape
    r = SLC_BLK // CMP_STRIDE
    pad_back = r * (n_slc + 1) - 1 - n_cmp
    pp = jnp.pad(pg, ((0, 0), (0, 0), (1, pad_back))).reshape(Tq, A_KV, n_slc + 1, r)
    ps = jnp.sum(pp[..., :-1, :], axis=-1) + pp[..., 1:, 0]
    blk = jnp.arange(n_slc, dtype=jnp.int32)[None, :]
    valid = blk * SLC_BLK <= tq[:, None]
    cur = (tq // SLC_BLK)[:, None]
    forced = (blk == 0) | (blk == cur) | (blk == cur - 1)
    score = jnp.where(valid[:, None], jnp.where(forced[:, None], jnp.inf, ps), -jnp.inf)
    si = score[..., None, :]
    sj = score[..., :, None]
    idx = jnp.arange(n_slc, dtype=jnp.int32)
    ahead = (si > sj) | ((si == sj) & (idx[None, :] < idx[:, None]))
    rank = jnp.sum(ahead, axis=-1)
    return rank < min(SLC_TOPN, n_slc)


def _nsa_prompt_one(q, g, kv4, kvw, kc, vc):
    T, _, dh = q.shape
    scale = dh ** -0.5
    tq = jnp.arange(T, dtype=jnp.int32)
    qg = q.reshape(T, A_KV, A_GROUP, dh)
    n_cmp = kc.shape[0]
    c_end = jnp.arange(n_cmp, dtype=jnp.int32) * CMP_STRIDE + (CMP_LEN - 1)
    s = jnp.einsum('tgad,ngd->tgan', qg, kc) * scale
    p = _masked_softmax(s, (c_end[None, :] <= tq[:, None])[:, None, None, :])
    o_cmp = jnp.einsum('tgan,ngd->tgad', p, vc)
    n_slc = T // SLC_BLK
    sel = _select_blocks(jnp.sum(p, axis=2), tq, n_slc)
    kpos = jnp.arange(T, dtype=jnp.int32)
    kmask = sel[:, :, kpos // SLC_BLK] & (kpos[None, None, :] <= tq[:, None, None])
    s = jnp.einsum('tgad,sgd->tgas', qg, kv4[:, 2]) * scale
    p = _masked_softmax(s, kmask[:, :, None, :])
    o_sel = jnp.einsum('tgas,sgd->tgad', p, kv4[:, 3])
    diff = tq[:, None] - kpos[None, :]
    wmask = (diff >= 0) & (diff < WINDOW)
    s = jnp.einsum('tgad,sgd->tgas', qg, kvw[:, 0]) * scale
    p = _masked_softmax(s, wmask[:, None, None, :])
    o_win = jnp.einsum('tgas,sgd->tgad', p, kvw[:, 1])
    gg = g.reshape(T, 3, A_KV, A_GROUP)[..., None]
    o = gg[:, 0] * o_cmp + gg[:, 1] * o_sel + gg[:, 2] * o_win
    return o.reshape(T, -1)


def _nsa_combine_jnp(q, tq, o_cmp, idx, k_sel, v_sel, kw, vw, pw, g):
    Tq, _, dh = q.shape
    qg = q.reshape(Tq, A_KV, A_GROUP, dh)
    scale = dh ** -0.5
    pos = idx[..., None] * SLC_BLK + jnp.arange(SLC_BLK, dtype=jnp.int32)
    s = jnp.einsum('tgad,tgmd->tgam', qg, k_sel.reshape(Tq, A_KV, -1, dh)) * scale
    p = _masked_softmax(s, (pos <= tq[:, None, None, None]).reshape(Tq, A_KV, 1, -1))
    o_sel = jnp.einsum('tgam,tgmd->tgad', p, v_sel.reshape(Tq, A_KV, -1, dh))
    diff = tq[:, None] - pw[None, :]
    wmask = (pw[None, :] >= 0) & (diff >= 0) & (diff < WINDOW)
    s = jnp.einsum('tgad,sgd->tgas', qg, kw) * scale
    p = _masked_softmax(s, wmask[:, None, None, :])
    o_win = jnp.einsum('tgas,sgd->tgad', p, vw)
    gg = g.astype(_F32).reshape(Tq, 3, A_KV, A_GROUP)[..., None]
    o = gg[:, 0] * o_cmp + gg[:, 1] * o_sel + gg[:, 2] * o_win
    return o.reshape(Tq, -1)


def _nsa_compressed(q, tq, kc, vc, c_end, n_slc):
    Tq, _, dh = q.shape
    qg = q.reshape(Tq, A_KV, A_GROUP, dh)
    s = jnp.einsum('tgad,ngd->tgan', qg, kc) * (dh ** -0.5)
    p = _masked_softmax(s, (c_end[None, :] <= tq[:, None])[:, None, None, :])
    o = jnp.einsum('tgan,ngd->tgad', p, vc)
    sel = _select_blocks(jnp.sum(p, axis=2), tq, n_slc)
    score = jnp.where(sel, 1.0, 0.0)
    _, idx = lax.top_k(score, min(SLC_TOPN, n_slc))
    return o, idx


def _nsa_sample(q, g, kv4, kvw, cache_kv, win_buf, page_table, layer, cmp_pe, cmp_w1, cmp_w2, k_norm_cmp):
    Bd, Ts, _, dh = q.shape
    n_pages = page_table.shape[1]
    past_len = n_pages * PAGE_SIZE
    past = cache_kv[layer, page_table, :, 0:2].reshape(Bd, past_len, 2, A_KV, dh)
    kc, vc = _compress(jnp.concatenate([past, kv4[:, :, 0:2]], axis=1), cmp_pe, cmp_w1, cmp_w2, k_norm_cmp)
    n_cmp = kc.shape[1]
    c_end = jnp.arange(n_cmp, dtype=jnp.int32) * CMP_STRIDE + (CMP_LEN - 1)
    n_past_blk = past_len // SLC_BLK
    n_new_blk = -(-Ts // SLC_BLK)
    n_slc = n_past_blk + n_new_blk
    blk_per_page = PAGE_SIZE // SLC_BLK
    new_blk = jnp.pad(kv4[:, :, 2:4], ((0, 0), (0, n_new_blk * SLC_BLK - Ts), (0, 0), (0, 0), (0, 0)))
    new_blk = new_blk.reshape(Bd, n_new_blk, SLC_BLK, 2, A_KV, dh)
    wb = win_buf.shape[1]
    tq = past_len + jnp.arange(Ts, dtype=jnp.int32)
    pw = past_len - wb + jnp.arange(wb + Ts, dtype=jnp.int32)
    g_ax = jnp.arange(A_KV)[None, :, None, None]
    r_blk = jnp.arange(SLC_BLK, dtype=jnp.int32)

    def one(qq, gg, kc_b, vc_b, new_b, kvw_b, win_b, pt_row):
        o_cmp, idx = _nsa_compressed(qq, tq, kc_b, vc_b, c_end, n_slc)
        pidx = jnp.minimum(idx, n_past_blk - 1)
        page = pt_row[pidx // blk_per_page][..., None]
        rows = ((pidx % blk_per_page) * SLC_BLK)[..., None] + r_blk
        nidx = jnp.clip(idx - n_past_blk, 0, n_new_blk - 1)[..., None]
        is_new = (idx >= n_past_blk)[..., None, None]
        k_sel = jnp.where(is_new, new_b[nidx, r_blk, 0, g_ax], cache_kv[layer, page, rows, 2, g_ax])
        v_sel = jnp.where(is_new, new_b[nidx, r_blk, 1, g_ax], cache_kv[layer, page, rows, 3, g_ax])
        kw = jnp.concatenate([win_b[:, 0], kvw_b[:, 0]], axis=0)
        vw = jnp.concatenate([win_b[:, 1], kvw_b[:, 1]], axis=0)
        return _nsa_combine(qq, tq, o_cmp, idx, k_sel, v_sel, kw, vw, pw, gg)

    return jax.vmap(one)(q, g, kc, vc, new_blk, kvw, win_buf, page_table)


def _mixer_inputs(x, norm_w, pw, m_width, a_width, a_dh):
    B, T, D = x.shape
    w_main, w_small, seg, hw, hm, groups, norm_tiles = pw
    M = B * T
    tm = 512 if M % 512 == 0 else M
    a, q, az, kv4, kvw, sm = _proj_in(x.reshape(M, D), norm_w, w_main, w_small, seg, hw, hm,
                                      groups, norm_tiles, tm)
    r3 = lambda v: v.reshape(B, T, v.shape[-1])
    return r3(a), r3(sm), r3(q), r3(kv4), r3(kvw), r3(az)


def _mixer_output(x, m_out, a_out, p, ow):
    B, T, D = x.shape
    M = B * T
    tm = 256 if M % 256 == 0 else M
    y = _out_proj(x.reshape(M, D), m_out.reshape(M, -1), a_out.reshape(M, -1), p.reshape(M, -1), *ow, tm)
    return y.reshape(B, T, D)


def kernel(x_prompt, x_sample, cache_nsa_kv, cache_win_kv, state_mlstm_C, state_mlstm_n,
           state_mlstm_m, state_mlstm_conv, page_table, p_prompt, p_sample, norm_w, w_in,
           m_conv_w, m_conv_b, m_wq, m_wk, m_wv, m_b_i, m_b_f, m_norm_w, a_q_norm, a_k_norm,
           cmp_pe, cmp_w1, cmp_w2, w_out, ple_proj, ple_norm, ple_gate):
    xp, xs = x_prompt, x_sample
    B, T, D = xp.shape
    depth = w_in.shape[0]
    m_width = m_conv_w.shape[-1]
    a_width = D - m_width
    a_dh = a_q_norm.shape[-1]
    dh = m_width // M_HEADS
    per_layer = []
    for i in range(depth):
        mw = (m_conv_w[i], m_conv_b[i], m_wq[i], m_wk[i], m_wv[i], m_b_i[i], m_b_f[i], m_norm_w[i])
        cw = (cmp_pe[i], cmp_w1[i], cmp_w2[i], a_k_norm[i, 0])
        w_out_i = jnp.concatenate(
            [w_out[i][:m_width], _heads_to_agd(w_out[i][m_width:].T, a_dh).T], axis=0)
        ow = (w_out_i, ple_proj[i], ple_norm[i], ple_gate[i])
        pw = _prep_proj_weights(w_in[i], a_q_norm[i], a_k_norm[i], m_width, a_width, a_dh)
        cwp = _prep_cmp_weights(*cw)
        a3, sm, q, kv4, kvw, az = _mixer_inputs(xp, norm_w[i], pw, m_width, a_width, a_dh)
        L = _MLSTM_L if T % _MLSTM_L == 0 else (M_CHUNK if T % M_CHUNK == 0 else T)
        m_out, conv_p, C_p, n_p, m_p = _mlstm(
            a3, sm, jnp.zeros((B, M_CONV - 1, m_width), _F32),
            jnp.zeros((B, M_HEADS, dh, dh), _F32), jnp.zeros((B, M_HEADS, dh), _F32),
            jnp.full((B, M_HEADS), -jnp.inf, _F32), *mw, L)
        kc, vc = _compress_prompt(kv4, cwp)
        a_out = _nsa_prompt(q, kv4, kvw, kc, vc, sm, az)
        xp_new = _mixer_output(xp, m_out, a_out, p_prompt[i], ow)
        kv_p = kv4.reshape(B, T, 4, A_KV, a_dh)
        win_p = kvw[:, -min(WINDOW, T):].reshape(B, -1, 2, A_KV, a_dh)
        a3, sm, q, kv4, kvw, az = _mixer_inputs(xs, norm_w[i], pw, m_width, a_width, a_dh)
        Bd, Ts = xs.shape[:2]
        Ls = _MLSTM_L if Ts % _MLSTM_L == 0 else (M_CHUNK if Ts % M_CHUNK == 0 else Ts)
        m_out, conv_s, C_s, n_s, m_s = _mlstm(
            a3, sm, state_mlstm_conv[i], state_mlstm_C[i], state_mlstm_n[i], state_mlstm_m[i], *mw, Ls)
        past_len = page_table.shape[1] * PAGE_SIZE
        cache_l = cache_nsa_kv[i]
        n_pool = cache_l.shape[0]
        wb = cache_win_kv.shape[2]
        kc, vc = _compress_sample(cache_l.reshape(n_pool, PAGE_SIZE, 4 * A_KV * a_dh), page_table, cwp)
        ocmp, owin, idx = _nsa_sample_front(
            q, kc, vc, cache_win_kv[i].reshape(Bd, wb, 2 * A_KV * a_dh), kvw, past_len)
        topn = min(SLC_TOPN, past_len // SLC_BLK + 1)
        q_rows = q.reshape(Bd, Ts, A_GROUP, A_KV, a_dh).transpose(0, 3, 1, 2, 4)
        q_rows = jnp.pad(q_rows, ((0, 0), (0, 0), (0, 0), (0, 8 - A_GROUP), (0, 0)))
        kv_new = kv4.reshape(Bd, Ts, 4 * A_KV, a_dh).transpose(0, 2, 1, 3)
        osel = _nsa_sample_gather(
            idx[:, :, :topn].reshape(Bd, -1), page_table, q_rows.reshape(Bd, A_KV * Ts, 8, a_dh), kv_new,
            cache_l.reshape(n_pool, PAGE_SIZE, 4 * A_KV, a_dh), past_len)
        osel = osel.reshape(Bd, A_KV, Ts, 8, a_dh)[:, :, :, :A_GROUP].transpose(0, 2, 3, 1, 4)
        Ms = Bd * Ts
        a_out = _nsa_combine(ocmp.reshape(Ms, -1), osel.reshape(Ms, -1), owin.reshape(Ms, -1),
                             sm.reshape(Ms, -1), az.reshape(Ms, -1)).reshape(Bd, Ts, -1)
        xs_new = _mixer_output(xs, m_out, a_out, p_sample[i], ow)
        kv_s = kv4.reshape(Bd, Ts, 4, A_KV, a_dh)
        win_s = jnp.concatenate(
            [cache_win_kv[i], kvw.reshape(Bd, Ts, 2, A_KV, a_dh)], axis=1)[:, -wb:]
        per_layer.append((kv_p, kv_s, win_p, win_s, C_p, n_p, m_p, conv_p, C_s, n_s, m_s, conv_s))
        xp, xs = xp_new, xs_new
    (kv_p, kv_s, win_p, win_s, C_p, n_p, m_p, conv_p,
     C_s, n_s, m_s, conv_s) = [jnp.stack(a, axis=0) for a in zip(*per_layer)]
    return (xp, xs, kv_p, kv_s, win_p, win_s, C_p, n_p, m_p, conv_p, C_s, n_s, m_s, conv_s)
```

```python
import functools
import math

import jax
import jax.numpy as jnp
from jax import lax
from jax.experimental import pallas as pl
from jax.experimental.pallas import tpu as pltpu

NORM_EPS = 1e-6
M_HEADS = 4
M_CONV = 4
M_CHUNK = 64
A_HEADS = 16
A_KV = 4
A_GROUP = A_HEADS // A_KV
CMP_STRIDE = 16
CMP_LEN = 2 * CMP_STRIDE
SLC_BLK = 64
SLC_TOPN = 16
WINDOW = 512
PAGE_SIZE = 128

_BF16 = jnp.bfloat16
_F32 = jnp.float32
_VMEM_LIMIT = 56 * 1024 * 1024


def _dot(a, b):
    return jnp.dot(a.astype(_BF16), b.astype(_BF16), preferred_element_type=_F32)


_PT = 512


def _proj_in_kernel(groups, norm_tiles, x_ref, nw_ref, w_ref, wsm_ref, seg_ref, hw_ref, hm_ref,
                    a_ref, q_ref, az_ref, kv4_ref, kvw_ref, sm_ref, xn_ref):
    j = pl.program_id(1)

    @pl.when(j == 0)
    def _():
        x = x_ref[...]
        ms = jnp.mean(x * x, axis=-1, keepdims=True)
        xn = x * lax.rsqrt(ms + NORM_EPS) * nw_ref[...]
        xn_ref[...] = xn.astype(_BF16)
        sm = jnp.dot(xn_ref[...], wsm_ref[...], preferred_element_type=_F32)
        lane = lax.broadcasted_iota(jnp.int32, sm.shape, 1)
        is_gate = (lane >= 2 * M_HEADS) & (lane < 2 * M_HEADS + 3 * A_HEADS)
        sm_ref[...] = jnp.where(is_gate, jax.nn.sigmoid(sm), sm)

    y = jnp.dot(xn_ref[...], w_ref[...], preferred_element_type=_F32)
    outs = (a_ref, q_ref, az_ref, kv4_ref, kvw_ref)
    for (start, n), o_ref in zip(groups, outs):
        plain = [t for t in range(start, start + n) if t not in norm_tiles]
        normed = [t for t in range(start, start + n) if t in norm_tiles]
        if plain:
            cond = functools.reduce(jnp.logical_or, [j == t for t in plain])

            @pl.when(cond)
            def _(o_ref=o_ref):
                o_ref[...] = y

        if normed:
            cond = functools.reduce(jnp.logical_or, [j == t for t in normed])

            @pl.when(cond)
            def _(o_ref=o_ref):
                ms = jnp.dot((y * y).astype(_BF16), seg_ref[...], preferred_element_type=_F32)
                yn = y * lax.rsqrt(ms + NORM_EPS) * hw_ref[...]
                o_ref[...] = jnp.where(hm_ref[...] > 0, yn, y)


def _proj_in(x2d, norm_w, w_main, w_small, seg, hw, hm, groups, norm_tiles, tm):
    M, D = x2d.shape
    n_tiles = w_main.shape[1] // _PT
    widths = [n * _PT for _, n in groups]

    def out_map(start, n):
        return lambda i, j: (i, jnp.clip(j - start, 0, n - 1))

    out_specs = [pl.BlockSpec((tm, _PT), out_map(s, n)) for s, n in groups]
    out_specs.append(pl.BlockSpec((tm, 128), lambda i, j: (i, 0)))
    out_shape = [jax.ShapeDtypeStruct((M, w), _F32) for w in widths]
    out_shape.append(jax.ShapeDtypeStruct((M, 128), _F32))
    return pl.pallas_call(
        functools.partial(_proj_in_kernel, groups, norm_tiles),
        grid=(M // tm, n_tiles),
        in_specs=[
            pl.BlockSpec((tm, D), lambda i, j: (i, 0)),
            pl.BlockSpec((1, D), lambda i, j: (0, 0)),
            pl.BlockSpec((D, _PT), lambda i, j: (0, j)),
            pl.BlockSpec((D, 128), lambda i, j: (0, 0)),
            pl.BlockSpec((_PT, _PT), lambda i, j: (0, 0)),
            pl.BlockSpec((None, 1, _PT), lambda i, j: (j, 0, 0)),
            pl.BlockSpec((None, 1, _PT), lambda i, j: (j, 0, 0)),
        ],
        out_specs=out_specs,
        out_shape=out_shape,
        scratch_shapes=[pltpu.VMEM((tm, D), _BF16)],
        compiler_params=pltpu.CompilerParams(
            dimension_semantics=("arbitrary", "arbitrary"), vmem_limit_bytes=_VMEM_LIMIT),
        name="proj_in",
    )(x2d, norm_w.reshape(1, D), w_main, w_small, seg, hw, hm)


def _heads_to_agd(w, a_dh):
    lead = w.shape[:-1]
    return jnp.swapaxes(w.reshape(lead + (A_KV, A_GROUP, a_dh)), -3, -2).reshape(w.shape)


def _heads_to_gad(w, a_dh):
    lead = w.shape[:-1]
    return jnp.swapaxes(w.reshape(lead + (A_GROUP, A_KV, a_dh)), -3, -2).reshape(w.shape)


def _prep_proj_weights(w_in, a_q_norm, a_k_norm, m_width, a_width, a_dh):
    kvw = A_KV * a_dh
    sizes = (m_width, m_width, m_width, M_HEADS, M_HEADS, a_width, 6 * kvw, 3 * A_HEADS, a_width)
    offs = [0]
    for s in sizes:
        offs.append(offs[-1] + s)
    u0, o0, z0, i0, f0, q0, kv0, g0, az0, end = offs
    assert end == w_in.shape[1]
    w_main = jnp.concatenate([
        w_in[:, u0:i0], _heads_to_agd(w_in[:, q0:kv0], a_dh), _heads_to_agd(w_in[:, az0:end], a_dh),
        w_in[:, kv0:kv0 + 4 * kvw], w_in[:, kv0 + 4 * kvw:g0]], axis=1).astype(_BF16)
    n_small = 2 * M_HEADS + 3 * A_HEADS
    w_small = jnp.concatenate([
        w_in[:, i0:q0], w_in[:, g0:az0],
        jnp.zeros((w_in.shape[0], 128 - n_small), w_in.dtype)], axis=1).astype(_BF16)
    widths = (3 * m_width, a_width, a_width, 4 * kvw, 2 * kvw)
    groups, start = [], 0
    for w in widths:
        assert w % _PT == 0
        groups.append((start, w // _PT))
        start += w // _PT
    n_tiles = start
    hw = jnp.zeros((n_tiles * _PT,), _F32)
    hm = jnp.zeros((n_tiles * _PT,), _F32)
    qs = groups[1][0] * _PT
    hw = hw.at[qs:qs + a_width].set(jnp.tile(a_q_norm, a_width // a_dh))
    hm = hm.at[qs:qs + a_width].set(1.0)
    k4 = groups[3][0] * _PT
    hw = hw.at[k4 + 2 * kvw:k4 + 3 * kvw].set(jnp.tile(a_k_norm[1], A_KV))
    hm = hm.at[k4 + 2 * kvw:k4 + 3 * kvw].set(1.0)
    kw = groups[4][0] * _PT
    hw = hw.at[kw:kw + kvw].set(jnp.tile(a_k_norm[2], A_KV))
    hm = hm.at[kw:kw + kvw].set(1.0)
    assert kvw * 2 == _PT and _PT % a_dh == 0
    norm_tiles = tuple(range(groups[1][0], groups[1][0] + groups[1][1])) + (
        (k4 + 2 * kvw) // _PT, kw // _PT)
    r = jnp.arange(_PT) // a_dh
    seg = ((r[:, None] == r[None, :]).astype(_F32) / a_dh).astype(_BF16)
    return (w_main, w_small, seg, hw.reshape(n_tiles, 1, _PT), hm.reshape(n_tiles, 1, _PT),
            tuple(groups), norm_tiles)


def _out_proj_kernel(mw, x_ref, m_ref, a_ref, p_ref, wo_ref, pp_ref, pn_ref, pg_ref, o_ref):
    h = x_ref[...] + _dot(m_ref[...], wo_ref[0:mw, :]) + _dot(a_ref[...], wo_ref[mw:, :])
    ms = jnp.mean(h * h, axis=-1, keepdims=True)
    hn = h * lax.rsqrt(ms + NORM_EPS) * pn_ref[...]
    gate = jax.nn.sigmoid(_dot(hn, pg_ref[...]))
    o_ref[...] = h + gate * _dot(p_ref[...], pp_ref[...])


def _out_proj(x2d, m_out, a_out, p2d, w_out, ple_proj, ple_norm, ple_gate, tm):
    M, D = x2d.shape
    mw, aw, dp = m_out.shape[1], a_out.shape[1], p2d.shape[1]

    def const(shape):
        return pl.BlockSpec(shape, lambda i: (0, 0), pipeline_mode=pl.Buffered(1))

    return pl.pallas_call(
        functools.partial(_out_proj_kernel, mw),
        grid=(M // tm,),
        in_specs=[
            pl.BlockSpec((tm, D), lambda i: (i, 0)),
            pl.BlockSpec((tm, mw), lambda i: (i, 0)),
            pl.BlockSpec((tm, aw), lambda i: (i, 0)),
            pl.BlockSpec((tm, dp), lambda i: (i, 0)),
            const((mw + aw, D)),
            const((dp, D)),
            const((1, D)),
            const((D, D)),
        ],
        out_specs=pl.BlockSpec((tm, D), lambda i: (i, 0)),
        out_shape=jax.ShapeDtypeStruct((M, D), _F32),
        compiler_params=pltpu.CompilerParams(
            dimension_semantics=("arbitrary",), vmem_limit_bytes=_VMEM_LIMIT),
        name="out_proj",
    )(x2d, m_out, a_out, p2d, w_out.astype(_BF16), ple_proj.astype(_BF16),
      ple_norm.reshape(1, D), ple_gate.astype(_BF16))


_MLSTM_L = 256
_CONV_PAD = 8


def _mlstm_kernel(L, dh, n_steps,
                  u_ref, o_ref, z_ref, sm_ref, cbuf_ref, c0_ref, n0_ref, m0_ref,
                  wq_ref, wk_ref, wv_ref, cw_ref, cb_ref, bi_ref, bf_ref, nw_ref,
                  out_ref, cout_ref, nout_ref, mout_ref, convout_ref,
                  xbuf, c_s, n_s, m_s):
    h = pl.program_id(1)
    s = pl.program_id(2)
    tail = M_CONV - 1

    @pl.when(s == 0)
    def _():
        c_s[...] = c0_ref[...]
        n_s[...] = n0_ref[...]
        m_s[...] = m0_ref[...]
        xbuf[_CONV_PAD - tail:_CONV_PAD, :] = cbuf_ref[...]

    u = u_ref[...]
    xbuf[_CONV_PAD:_CONV_PAD + L, :] = u
    c = cb_ref[...]
    for j in range(M_CONV):
        c = c + xbuf[_CONV_PAD - tail + j:_CONV_PAD - tail + j + L, :] * cw_ref[j:j + 1, :]
    xbuf[_CONV_PAD - tail:_CONV_PAD, :] = u[L - tail:L, :]
    ch = c * jax.nn.sigmoid(c)
    q = _dot(ch, wq_ref[...])
    k = _dot(ch, wk_ref[...]) * (dh ** -0.5)
    v = _dot(u, wv_ref[...])

    sm = sm_ref[...]
    lane = lax.broadcasted_iota(jnp.int32, sm.shape, 1)
    i_pre = jnp.sum(jnp.where(lane == h, sm, 0.0), axis=1, keepdims=True)
    f_pre = jnp.sum(jnp.where(lane == h + M_HEADS, sm, 0.0), axis=1, keepdims=True)
    li_col = i_pre + bi_ref[:, 0:1]
    f_in = f_pre + bf_ref[:, 0:1]
    lf_col = jnp.minimum(f_in, 0.0) - jnp.log(1.0 + jnp.exp(-jnp.abs(f_in)))

    ii = lax.broadcasted_iota(jnp.int32, (L, L), 0)
    jj = lax.broadcasted_iota(jnp.int32, (L, L), 1)
    eye = ii == jj
    causal = jj <= ii
    lf_row = jnp.sum(jnp.where(eye, lf_col, 0.0), axis=0, keepdims=True)
    li_row = jnp.sum(jnp.where(eye, li_col, 0.0), axis=0, keepdims=True)
    b_col = jnp.sum(jnp.where(causal, lf_row, 0.0), axis=1, keepdims=True)
    b_row = jnp.sum(jnp.where(ii <= jj, lf_col, 0.0), axis=0, keepdims=True)

    m_prev = m_s[0:1, 0:1]
    c_prev = c_s[...]
    n_prev = n_s[...]
    logd = jnp.where(causal, b_col - b_row + li_row, -jnp.inf)
    inter = b_col + m_prev
    m_t = jnp.maximum(inter, jnp.max(logd, axis=1, keepdims=True))
    qb = q.astype(_BF16)
    sc = lax.dot_general(qb, k.astype(_BF16), (((1,), (1,)), ((), ())),
                         preferred_element_type=_F32) * jnp.exp(logd - m_t)
    a_col = jnp.exp(inter - m_t)
    num = _dot(sc, v) + a_col * _dot(qb, c_prev)
    den = jnp.sum(sc, axis=1, keepdims=True) + a_col * jnp.sum(q * n_prev, axis=1, keepdims=True)
    hh = num / jnp.maximum(jnp.abs(den), jnp.exp(-m_t))

    b_end = b_col[L - 1:L, :]
    logw = b_end - b_col + li_col
    m_new = jnp.maximum(b_end + m_prev, jnp.max(logw, axis=0, keepdims=True))
    w_col = jnp.exp(logw - m_new)
    decay = jnp.exp(b_end + m_prev - m_new)
    c_s[...] = decay * c_prev + lax.dot_general(
        k.astype(_BF16), (w_col * v).astype(_BF16), (((0,), (0,)), ((), ())),
        preferred_element_type=_F32)
    n_s[...] = decay * n_prev + jnp.sum(w_col * k, axis=0, keepdims=True)
    m_s[...] = jnp.broadcast_to(m_new, m_s.shape)

    hg = hh * jax.nn.sigmoid(o_ref[...])
    hn = hg * lax.rsqrt(jnp.mean(hg * hg, axis=-1, keepdims=True) + NORM_EPS) * nw_ref[...]
    zz = z_ref[...]
    out_ref[...] = hn * (zz * jax.nn.sigmoid(zz))

    @pl.when(s == n_steps - 1)
    def _():
        cout_ref[...] = c_s[...]
        nout_ref[...] = n_s[...]
        mout_ref[...] = m_s[...]
        convout_ref[...] = u[L - tail:L, :]


def _mlstm(a3, sm, conv_buf, C0, n0, m0, conv_w, conv_b, wq, wk, wv, b_i, b_f, norm_w, L):
    B, T, mw3 = a3.shape
    mw = mw3 // 3
    dh = mw // M_HEADS
    tail = M_CONV - 1
    assert T % L == 0 and L >= tail and (L % 8 == 0)
    n_steps = T // L
    H = M_HEADS

    def lane_b(vec):
        return jnp.broadcast_to(vec.astype(_F32)[:, None, None], (H, 1, 128))

    m0b = jnp.broadcast_to(m0.astype(_F32)[:, :, None, None], (B, H, 1, 128))
    blk = lambda off: pl.BlockSpec((None, L, dh), lambda b, h, s: (b, s, off * H + h))
    per_h = lambda r, c: pl.BlockSpec((None, r, c), lambda b, h, s: (h, 0, 0))
    st = lambda r, c: pl.BlockSpec((None, None, r, c), lambda b, h, s: (b, h, 0, 0))
    out, C, n, m, conv = pl.pallas_call(
        functools.partial(_mlstm_kernel, L, dh, n_steps),
        grid=(B, H, n_steps),
        in_specs=[
            blk(0), blk(1), blk(2),
            pl.BlockSpec((None, L, 128), lambda b, h, s: (b, s, 0)),
            pl.BlockSpec((None, tail, dh), lambda b, h, s: (b, 0, h)),
            st(dh, dh), st(1, dh), st(1, 128),
            per_h(dh, dh), per_h(dh, dh), per_h(dh, dh),
            pl.BlockSpec((M_CONV, dh), lambda b, h, s: (0, h)),
            pl.BlockSpec((1, dh), lambda b, h, s: (0, h)),
            per_h(1, 128), per_h(1, 128), per_h(1, dh),
        ],
        out_specs=[
            pl.BlockSpec((None, L, dh), lambda b, h, s: (b, s, h)),
            st(dh, dh), st(1, dh), st(1, 128),
            pl.BlockSpec((None, tail, dh), lambda b, h, s: (b, 0, h)),
        ],
        out_shape=[
            jax.ShapeDtypeStruct((B, T, mw), _F32),
            jax.ShapeDtypeStruct((B, H, dh, dh), _F32),
            jax.ShapeDtypeStruct((B, H, 1, dh), _F32),
            jax.ShapeDtypeStruct((B, H, 1, 128), _F32),
            jax.ShapeDtypeStruct((B, tail, mw), _F32),
        ],
        scratch_shapes=[
            pltpu.VMEM((_CONV_PAD + L, dh), _F32),
            pltpu.VMEM((dh, dh), _F32),
            pltpu.VMEM((1, dh), _F32),
            pltpu.VMEM((1, 128), _F32),
        ],
        compiler_params=pltpu.CompilerParams(
            dimension_semantics=("arbitrary", "arbitrary", "arbitrary"), vmem_limit_bytes=_VMEM_LIMIT),
        name="mlstm",
    )(a3, a3, a3, sm, conv_buf, C0, n0.reshape(B, H, 1, dh), m0b,
      wq.astype(_BF16), wk.astype(_BF16), wv.astype(_BF16), conv_w, conv_b.reshape(1, mw),
      lane_b(b_i), lane_b(b_f), norm_w.reshape(H, 1, dh))
    return out, conv, C, n.reshape(B, H, dh), m[:, :, 0, 0]


def _prep_cmp_weights(pe, w1, w2, k_norm):
    a_dh, hid = w1.shape[-2], w1.shape[-1]
    w1r = w1.reshape(2, 2, CMP_STRIDE, a_dh, hid)
    eye2 = jnp.eye(2, dtype=w1.dtype)
    w1p = jnp.einsum('cxjdh,ab->cjadxbh', w1r, eye2).reshape(2, CMP_STRIDE * 2 * a_dh, 2 * 2 * hid)
    bias = jnp.einsum('cxjd,cxjdh->ch', pe.reshape(2, 2, CMP_STRIDE, a_dh), w1r)
    bias2 = jnp.tile(bias, (1, 2)).reshape(2, 1, 2 * hid)
    w2p = jnp.einsum('chd,ab->cahbd', w2, eye2).reshape(2, 2 * hid, 2 * a_dh)
    r = jnp.arange(2 * a_dh) // a_dh
    seg = ((r[:, None] == r[None, :]).astype(_F32) / a_dh).astype(_BF16)
    kn = jnp.tile(k_norm, 2).reshape(1, 2 * a_dh)
    return w1p.astype(_BF16), bias2.astype(_F32), w2p.astype(_BF16), seg, kn.astype(_F32)


def _cmp_first_layer(rows_ref, n_seg, w1_ref, c):
    x = jnp.concatenate(
        [rows_ref[pl.ds(j, n_seg, stride=CMP_STRIDE), :].astype(_BF16)
         for j in range(CMP_STRIDE)], axis=1)
    return jnp.dot(x, w1_ref[c], preferred_element_type=_F32)


def _cmp_second_layer(p, n_seg, c, b_ref, w2_ref, seg_ref, kn_ref):
    hid = p[:, 0:128] + pltpu.roll(p[:, 128:256], n_seg - 1, axis=0) + b_ref[c]
    act = hid * jax.nn.sigmoid(hid)
    out = jnp.dot(act.astype(_BF16), w2_ref[c], preferred_element_type=_F32)
    if c == 0:
        ms = jnp.dot((out * out).astype(_BF16), seg_ref[...], preferred_element_type=_F32)
        out = out * lax.rsqrt(ms + NORM_EPS) * kn_ref[...]
    return out


def _compress_prompt_kernel(n_seg, r00, r01, r10, r11, w1_ref, b_ref, w2_ref, seg_ref, kn_ref,
                            kc_ref, vc_ref):
    for c, o_ref, refs in ((0, kc_ref, (r00, r01)), (1, vc_ref, (r10, r11))):
        for gp in range(2):
            p = _cmp_first_layer(refs[gp], n_seg, w1_ref, c)
            o_ref[:, gp * 128:(gp + 1) * 128] = _cmp_second_layer(p, n_seg, c, b_ref, w2_ref, seg_ref, kn_ref)


def _compress_prompt(kv4, cwp):
    B, T, W = kv4.shape
    assert W == 1024 and T % (CMP_STRIDE * 8) == 0
    n_seg = T // CMP_STRIDE
    w1p, bias2, w2p, seg, kn = cwp
    full = lambda a: pl.BlockSpec(a.shape, lambda b: (0,) * a.ndim)
    return pl.pallas_call(
        functools.partial(_compress_prompt_kernel, n_seg),
        grid=(B,),
        in_specs=[pl.BlockSpec((None, T, 128), functools.partial(lambda k, b: (b, 0, k), k))
                  for k in range(4)] + [full(w1p), full(bias2), full(w2p), full(seg), full(kn)],
        out_specs=[pl.BlockSpec((None, n_seg, 256), lambda b: (b, 0, 0))] * 2,
        out_shape=[jax.ShapeDtypeStruct((B, n_seg, 256), _F32)] * 2,
        compiler_params=pltpu.CompilerParams(
            dimension_semantics=("arbitrary",), vmem_limit_bytes=_VMEM_LIMIT),
        name="compress_prompt",
    )(kv4, kv4, kv4, kv4, w1p, bias2, w2p, seg, kn)


_QB = 128
_SEL_TK = 512
_GW = 32


def _softmax_parts(s):
    m = jnp.max(s, axis=-1, keepdims=True)
    m = jnp.where(m == -jnp.inf, 0.0, m)
    e = jnp.exp(s - m)
    return e, 1.0 / jnp.maximum(jnp.sum(e, axis=-1, keepdims=True), 1e-30)


def _softmax_rows(s, mask):
    e, inv = _softmax_parts(jnp.where(mask, s, -jnp.inf))
    return e * inv


def _tile_rows(x, n):
    return jnp.concatenate([x] * n, axis=0)


def _nsa_prompt_kernel(T, n_cmp, n_slc,
                       q_ref, ks_ref, kw_ref, kc_ref, vc_ref, sm_ref, az_ref, out_ref,
                       qpl_ref, m_ref, l_ref, acc_ref):
    QB, TK, G, A = _QB, _SEL_TK, A_KV, A_GROUP
    R = A * QB
    t0 = pl.program_id(1) * QB
    lane256 = lax.broadcasted_iota(jnp.int32, (QB, 256), 1)
    q = q_ref[...]
    for g in range(G):
        gm = (lane256 // 64) == g
        qpl_ref[g] = jnp.concatenate(
            [jnp.where(gm, q[:, a * 256:(a + 1) * 256] * 0.125, 0.0) for a in range(A)],
            axis=0).astype(_BF16)
    tq = t0 + lax.broadcasted_iota(jnp.int32, (QB, 1), 0)

    n_seg = kc_ref.shape[0]
    kc = kc_ref[...].astype(_BF16)
    vc = vc_ref[...].astype(_BF16)
    ci = lax.broadcasted_iota(jnp.int32, (QB, n_seg), 1)
    cbias = _tile_rows(jnp.where((ci < n_cmp) & (ci * CMP_STRIDE + (CMP_LEN - 1) <= tq), 0.0, -jnp.inf), A)
    mi = lax.broadcasted_iota(jnp.int32, (n_seg, G * _GW), 0)
    mj = lax.broadcasted_iota(jnp.int32, (n_seg, G * _GW), 1)
    ratio = SLC_BLK // CMP_STRIDE
    o_cmp = []
    ps = jnp.zeros((QB, G * _GW), _F32)
    for g in range(G):
        s = lax.dot_general(qpl_ref[g], kc, (((1,), (1,)), ((), ())), preferred_element_type=_F32)
        e, inv = _softmax_parts(s + cbias)
        o_cmp.append(jnp.dot(e.astype(_BF16), vc, preferred_element_type=_F32) * inv)
        p = e * inv
        pg = p[0:QB]
        for a in range(1, A):
            pg = pg + p[a * QB:(a + 1) * QB]
        blk = mj - g * _GW
        msel = ((mj // _GW == g) & (mi >= ratio * blk - 1) & (mi <= ratio * blk + ratio - 1)
                ).astype(_BF16)
        pg_hi = pg.astype(_BF16)
        pg_lo = (pg - pg_hi.astype(_F32)).astype(_BF16)
        ps = ps + jnp.dot(pg_hi, msel, preferred_element_type=_F32) \
                + jnp.dot(pg_lo, msel, preferred_element_type=_F32)

    lane = lax.broadcasted_iota(jnp.int32, (QB, G * _GW), 1)
    blk = lane % _GW
    cur = tq // SLC_BLK
    valid = (blk < n_slc) & (blk * SLC_BLK <= tq)
    forced = (blk == 0) | (blk == cur) | (blk == cur - 1)
    score = jnp.where(valid, jnp.where(forced, jnp.inf, ps), -jnp.inf)
    rank = jnp.zeros((QB, G * _GW), _F32)
    for r in range(1, _GW):
        other = jnp.where(blk >= r, pltpu.roll(score, r, axis=1),
                          pltpu.roll(score, (r - _GW) % (G * _GW), axis=1))
        ahead = (other > score) | ((other == score) & (blk >= r))
        rank = rank + jnp.where(ahead, 1.0, 0.0)
    sel = jnp.where(rank < min(SLC_TOPN, n_slc), 1.0, 0.0).astype(_BF16)

    m_ref[...] = jnp.full(m_ref.shape, -jnp.inf, _F32)
    l_ref[...] = jnp.zeros(l_ref.shape, _F32)
    acc_ref[...] = jnp.zeros(acc_ref.shape, _F32)
    er = lax.broadcasted_iota(jnp.int32, (G * _GW, TK), 0)
    ec = lax.broadcasted_iota(jnp.int32, (G * _GW, TK), 1)
    kcol = lax.broadcasted_iota(jnp.int32, (QB, TK), 1)

    def sel_tile(kt, carry):
        k0 = pl.multiple_of(kt * TK, TK)
        kt_b = ks_ref[pl.ds(k0, TK), 0:256].astype(_BF16)
        vt_b = ks_ref[pl.ds(k0, TK), 256:512].astype(_BF16)
        causal = (k0 + kcol) <= tq
        for g in range(G):
            expand = (er == g * _GW + (k0 + ec) // SLC_BLK).astype(_BF16)
            hit = jnp.dot(sel, expand, preferred_element_type=_F32)
            bias = _tile_rows(jnp.where(causal & (hit > 0.5), 0.0, -jnp.inf), A)
            s = lax.dot_general(qpl_ref[g], kt_b, (((1,), (1,)), ((), ())),
                                preferred_element_type=_F32) + bias
            m_old = m_ref[g]
            m_new = jnp.maximum(m_old, jnp.max(s, axis=-1, keepdims=True))
            m_safe = jnp.where(m_new == -jnp.inf, 0.0, m_new)
            p = jnp.exp(s - m_safe)
            alpha = jnp.exp(m_old - m_safe)
            l_ref[g] = alpha * l_ref[g] + jnp.sum(p, axis=-1, keepdims=True)
            acc_ref[g] = alpha * acc_ref[g] + jnp.dot(p.astype(_BF16), vt_b, preferred_element_type=_F32)
            m_ref[g] = m_new
        return carry

    lax.fori_loop(0, (t0 + QB + TK - 1) // TK, sel_tile, 0)

    WK = WINDOW + QB
    ws = pl.multiple_of(jnp.clip(t0 - WINDOW, 0, T - WK), QB)
    kw_b = kw_ref[pl.ds(ws, WK), 0:256].astype(_BF16)
    vw_b = kw_ref[pl.ds(ws, WK), 256:512].astype(_BF16)
    diff = tq - (ws + lax.broadcasted_iota(jnp.int32, (QB, WK), 1))
    wbias = _tile_rows(jnp.where((diff >= 0) & (diff < WINDOW), 0.0, -jnp.inf), A)

    sm = sm_ref[...]
    g0 = 2 * M_HEADS
    for g in range(G):
        gm = (lane256 // 64) == g
        s = lax.dot_general(qpl_ref[g], kw_b, (((1,), (1,)), ((), ())), preferred_element_type=_F32)
        e, inv = _softmax_parts(s + wbias)
        o_win = jnp.dot(e.astype(_BF16), vw_b, preferred_element_type=_F32) * inv
        o_sel = acc_ref[g] * (1.0 / jnp.maximum(l_ref[g], 1e-30))
        for a in range(A):
            head = g0 + g * A + a
            rows = slice(a * QB, (a + 1) * QB)
            mix = (sm[:, head:head + 1] * o_cmp[g][rows]
                   + sm[:, head + A_HEADS:head + A_HEADS + 1] * o_sel[rows]
                   + sm[:, head + 2 * A_HEADS:head + 2 * A_HEADS + 1] * o_win[rows])
            mix = jnp.where(gm, mix, 0.0)
            cols = slice(a * 256, (a + 1) * 256)
            if g == 0:
                out_ref[:, cols] = mix
            else:
                out_ref[:, cols] += mix
    az = az_ref[...]
    out_ref[...] = out_ref[...] * (az * jax.nn.sigmoid(az))


def _nsa_prompt(q, kv4, kvw, kc, vc, sm, az):
    B, T, _ = q.shape
    n_seg = kc.shape[1]
    n_cmp = n_seg - 1
    n_slc = T // SLC_BLK
    assert T % _SEL_TK == 0 and T >= WINDOW + _QB and n_slc <= _GW and A_KV * _GW == 128
    assert A_KV * 64 == 256 and n_seg % 8 == 0
    R = A_GROUP * _QB
    blk = lambda w: pl.BlockSpec((None, _QB, w), lambda b, i: (b, i, 0))
    return pl.pallas_call(
        functools.partial(_nsa_prompt_kernel, T, n_cmp, n_slc),
        grid=(B, T // _QB),
        in_specs=[
            blk(1024),
            pl.BlockSpec((None, T, 512), lambda b, i: (b, 0, 1)),
            pl.BlockSpec((None, T, 512), lambda b, i: (b, 0, 0)),
            pl.BlockSpec((None, n_seg, 256), lambda b, i: (b, 0, 0)),
            pl.BlockSpec((None, n_seg, 256), lambda b, i: (b, 0, 0)),
            blk(128), blk(1024),
        ],
        out_specs=blk(1024),
        out_shape=jax.ShapeDtypeStruct((B, T, 1024), _F32),
        scratch_shapes=[
            pltpu.VMEM((A_KV, R, 256), _BF16),
            pltpu.VMEM((A_KV, R, 1), _F32),
            pltpu.VMEM((A_KV, R, 1), _F32),
            pltpu.VMEM((A_KV, R, 256), _F32),
        ],
        compiler_params=pltpu.CompilerParams(
            dimension_semantics=("arbitrary", "arbitrary"), vmem_limit_bytes=_VMEM_LIMIT),
        name="nsa_prompt",
    )(q, kv4, kvw, kc, vc, sm, az)


_CMP_PAGES = 32


def _compress_sample_kernel(P, n_chunks, n_seg, pt_ref, cache_ref, w1_ref, b_ref, w2_ref, seg_ref,
                            kn_ref, kc_ref, vc_ref, stage, lanes, per, sem):
    b = pl.program_id(0)
    ch = pl.program_id(1)
    step = b * n_chunks + ch
    n_steps = pl.num_programs(0) * n_chunks
    slot = step % 2
    rows = P * PAGE_SIZE

    def page_copy(bb, cc, sl, p):
        page = pt_ref[bb, cc * P + p]
        return pltpu.make_async_copy(cache_ref.at[page, pl.ds(0, 4)], stage.at[sl, p], sem.at[sl])

    @pl.when(step == 0)
    def _():
        for p in range(P):
            page_copy(b, ch, slot, p).start()

    @pl.when(step + 1 < n_steps)
    def _():
        nxt = step + 1
        for p in range(P):
            page_copy(nxt // n_chunks, nxt % n_chunks, 1 - slot, p).start()

    for p in range(P):
        page_copy(b, ch, slot, p).wait()

    segs = rows // CMP_STRIDE

    def to_rows(p, carry):
        r0 = pl.multiple_of(p * PAGE_SIZE, PAGE_SIZE)
        for k in range(4):
            lanes[k, pl.ds(r0, PAGE_SIZE), :] = stage[slot, p, k].T
        return carry

    lax.fori_loop(0, P, to_rows, 0)
    for k in range(4):
        per[pl.ds(pl.multiple_of(ch * segs, segs), segs), k * 256:(k + 1) * 256] = _cmp_first_layer(
            lanes.at[k], segs, w1_ref, k // 2)

    @pl.when(ch == n_chunks - 1)
    def _():
        for k in range(4):
            c, gp = k // 2, k % 2
            o_ref = kc_ref if c == 0 else vc_ref
            o_ref[:, gp * 128:(gp + 1) * 128] = _cmp_second_layer(
                per[:, k * 256:(k + 1) * 256], n_seg, c, b_ref, w2_ref, seg_ref, kn_ref)


def _compress_sample(cache_t, page_table, cwp):
    B, n_pages = page_table.shape
    P = _CMP_PAGES if n_pages % _CMP_PAGES == 0 else n_pages
    n_chunks = n_pages // P
    n_seg = n_pages * PAGE_SIZE // CMP_STRIDE
    w1p, bias2, w2p, seg, kn = cwp
    full = lambda a: pl.BlockSpec(a.shape, lambda b, c, pt: (0,) * a.ndim)
    grid_spec = pltpu.PrefetchScalarGridSpec(
        num_scalar_prefetch=1,
        grid=(B, n_chunks),
        in_specs=[pl.BlockSpec(memory_space=pl.ANY),
                  full(w1p), full(bias2), full(w2p), full(seg), full(kn)],
        out_specs=[pl.BlockSpec((None, n_seg, 256), lambda b, c, pt: (b, 0, 0))] * 2,
        scratch_shapes=[
            pltpu.VMEM((2, P, 4, 128, PAGE_SIZE), _F32),
            pltpu.VMEM((4, P * PAGE_SIZE, 128), _F32),
            pltpu.VMEM((n_seg, 1024), _F32),
            pltpu.SemaphoreType.DMA((2,)),
        ])
    return pl.pallas_call(
        functools.partial(_compress_sample_kernel, P, n_chunks, n_seg),
        grid_spec=grid_spec,
        out_shape=[jax.ShapeDtypeStruct((B, n_seg, 256), _F32)] * 2,
        compiler_params=pltpu.CompilerParams(
            dimension_semantics=("arbitrary", "arbitrary"), vmem_limit_bytes=_VMEM_LIMIT),
        name="compress_sample",
    )(page_table, cache_t, w1p, bias2, w2p, seg, kn)


def _merge_heads(o_ref, o_g, g, Ts):
    gm = (lax.broadcasted_iota(jnp.int32, (Ts, 256), 1) // 64) == g
    for a in range(A_GROUP):
        part = jnp.where(gm, o_g[a * Ts:(a + 1) * Ts], 0.0)
        cols = slice(a * 256, (a + 1) * 256)
        if g == 0:
            o_ref[:, cols] = part
        else:
            o_ref[:, cols] += part


def _nsa_sample_front_kernel(Ts, past_len, n_cmp, n_slc, NL, topn,
                             q_ref, kc_ref, vc_ref, wold_ref, wnew_ref,
                             ocmp_ref, owin_ref, idx_ref, qpl_ref):
    G, A = A_KV, A_GROUP
    R = A * Ts
    lane256 = lax.broadcasted_iota(jnp.int32, (Ts, 256), 1)
    q = q_ref[...]
    for g in range(G):
        gm = (lane256 // 64) == g
        qpl_ref[g] = jnp.concatenate(
            [jnp.where(gm, q[:, a * 256:(a + 1) * 256] * 0.125, 0.0) for a in range(A)],
            axis=0).astype(_BF16)
    tq_r = past_len + lax.broadcasted_iota(jnp.int32, (R, 1), 0) % Ts
    tq = past_len + lax.broadcasted_iota(jnp.int32, (Ts, 1), 0)

    n_seg = kc_ref.shape[0]
    kc = kc_ref[...].astype(_BF16)
    vc = vc_ref[...].astype(_BF16)
    ci = lax.broadcasted_iota(jnp.int32, (R, n_seg), 1)
    cmask = (ci < n_cmp) & (ci * CMP_STRIDE + (CMP_LEN - 1) <= tq_r)
    ratio = SLC_BLK // CMP_STRIDE
    mi = lax.broadcasted_iota(jnp.int32, (n_seg, NL), 0)
    mj = lax.broadcasted_iota(jnp.int32, (n_seg, NL), 1)
    msel = ((mi >= ratio * mj - 1) & (mi <= ratio * mj + ratio - 1)).astype(_BF16)
    lane = lax.broadcasted_iota(jnp.int32, (Ts, NL), 1)
    cur = tq // SLC_BLK
    valid = (lane < n_slc) & (lane * SLC_BLK <= tq)
    forced = (lane == 0) | (lane == cur) | (lane == cur - 1)
    scores = []
    for g in range(G):
        s = lax.dot_general(qpl_ref[g], kc, (((1,), (1,)), ((), ())), preferred_element_type=_F32)
        p = _softmax_rows(s, cmask)
        _merge_heads(ocmp_ref, jnp.dot(p.astype(_BF16), vc, preferred_element_type=_F32), g, Ts)
        pg = p[0:Ts]
        for a in range(1, A):
            pg = pg + p[a * Ts:(a + 1) * Ts]
        pg_hi = pg.astype(_BF16)
        pg_lo = (pg - pg_hi.astype(_F32)).astype(_BF16)
        ps = jnp.dot(pg_hi, msel, preferred_element_type=_F32) \
            + jnp.dot(pg_lo, msel, preferred_element_type=_F32)
        scores.append(jnp.where(valid, jnp.where(forced, jnp.inf, ps), -jnp.inf))
    score = jnp.concatenate(scores, axis=0)

    lane_f = lax.broadcasted_iota(jnp.int32, (G * Ts, NL), 1).astype(_F32)
    out_lane = lax.broadcasted_iota(jnp.int32, (G * Ts, 128), 1)
    avail = lane_f < n_slc
    picks = jnp.zeros((G * Ts, 128), _F32)
    for k in range(topn):
        mx = jnp.max(jnp.where(avail, score, -jnp.inf), axis=1, keepdims=True)
        pick = jnp.min(jnp.where(avail & (score == mx), lane_f, float(NL)), axis=1, keepdims=True)
        avail = avail & (lane_f != pick)
        picks = jnp.where(out_lane == k, pick, picks)
    idx_ref[...] = picks.astype(jnp.int32)

    wb = wold_ref.shape[0]
    k_old = wold_ref[:, 0:256].astype(_BF16)
    v_old = wold_ref[:, 256:512].astype(_BF16)
    k_new = wnew_ref[:, 0:256]
    v_new = wnew_ref[:, 256:512]
    pw_old = past_len - wb + lax.broadcasted_iota(jnp.int32, (R, wb), 1)
    d_old = tq_r - pw_old
    m_old = (pw_old >= 0) & (d_old >= 0) & (d_old < WINDOW)
    d_new = tq_r - (past_len + lax.broadcasted_iota(jnp.int32, (R, Ts), 1))
    m_new = (d_new >= 0) & (d_new < WINDOW)
    for g in range(G):
        qg = qpl_ref[g]
        s1 = lax.dot_general(qg, k_old, (((1,), (1,)), ((), ())), preferred_element_type=_F32)
        s2 = lax.dot_general(qg.astype(_F32), k_new, (((1,), (1,)), ((), ())),
                             preferred_element_type=_F32)
        s1 = jnp.where(m_old, s1, -jnp.inf)
        s2 = jnp.where(m_new, s2, -jnp.inf)
        mx = jnp.maximum(jnp.max(s1, axis=1, keepdims=True), jnp.max(s2, axis=1, keepdims=True))
        mx = jnp.where(mx == -jnp.inf, 0.0, mx)
        e1 = jnp.where(m_old, jnp.exp(s1 - mx), 0.0)
        e2 = jnp.where(m_new, jnp.exp(s2 - mx), 0.0)
        den = jnp.maximum(jnp.sum(e1, axis=1, keepdims=True) + jnp.sum(e2, axis=1, keepdims=True), 1e-30)
        o = (jnp.dot(e1.astype(_BF16), v_old, preferred_element_type=_F32)
             + jnp.dot(e2, v_new, preferred_element_type=_F32)) / den
        _merge_heads(owin_ref, o, g, Ts)


def _nsa_sample_front(q, kc, vc, win_old, win_new, past_len):
    B, Ts, _ = q.shape
    n_seg = kc.shape[1]
    n_cmp = (past_len + Ts) // CMP_STRIDE - 1
    assert n_cmp == n_seg - 1 and Ts % 8 == 0 and Ts <= SLC_BLK
    n_slc = past_len // SLC_BLK + 1
    NL = -(-n_slc // 128) * 128
    topn = min(SLC_TOPN, n_slc)
    wb = win_old.shape[1]
    R = A_GROUP * Ts
    b3 = lambda r, w: pl.BlockSpec((None, r, w), lambda b: (b, 0, 0))
    return pl.pallas_call(
        functools.partial(_nsa_sample_front_kernel, Ts, past_len, n_cmp, n_slc, NL, topn),
        grid=(B,),
        in_specs=[b3(Ts, 1024), b3(n_seg, 256), b3(n_seg, 256), b3(wb, 512), b3(Ts, 512)],
        out_specs=[b3(Ts, 1024), b3(Ts, 1024), b3(A_KV * Ts, 128)],
        out_shape=[jax.ShapeDtypeStruct((B, Ts, 1024), _F32), jax.ShapeDtypeStruct((B, Ts, 1024), _F32),
                   jax.ShapeDtypeStruct((B, A_KV * Ts, 128), jnp.int32)],
        scratch_shapes=[pltpu.VMEM((A_KV, R, 256), _BF16)],
        compiler_params=pltpu.CompilerParams(
            dimension_semantics=("arbitrary",), vmem_limit_bytes=_VMEM_LIMIT),
        name="nsa_sample_front",
    )(q, kc, vc, win_old, win_new)


def _nsa_sample_gather_kernel(Ts, n_past_blk, past_len, topn,
                              idx_ref, pt_ref, q_ref, kvn_ref, cache_ref, o_ref, kbuf, vbuf, sem):
    b = pl.program_id(0)
    n_it = A_KV * Ts
    bpp = PAGE_SIZE // SLC_BLK
    nk = topn * PAGE_SIZE

    def block_copies(it, sl, k):
        g = it // Ts
        blk = jnp.minimum(idx_ref[b, it * topn + k], n_past_blk - 1)
        page = pt_ref[b, blk // bpp]
        return (pltpu.make_async_copy(cache_ref.at[page, 2 * A_KV + g], kbuf.at[sl, k], sem.at[sl]),
                pltpu.make_async_copy(cache_ref.at[page, 3 * A_KV + g], vbuf.at[sl, k], sem.at[sl]))

    def issue(it, sl):
        for k in range(topn):
            ck, cv = block_copies(it, sl, k)
            ck.start()
            cv.start()

    issue(0, 0)
    lane = lax.broadcasted_iota(jnp.int32, (1, nk), 1)
    new_r = lax.broadcasted_iota(jnp.int32, (1, Ts), 1)

    def body(it, carry):
        sl = it % 2

        @pl.when(it + 1 < n_it)
        def _():
            issue(it + 1, 1 - sl)

        for k in range(topn):
            ck, cv = block_copies(it, sl, k)
            ck.wait()
            cv.wait()

        t = it % Ts
        g = it // Ts
        tq = past_len + t
        pos = jnp.zeros((1, nk), jnp.int32)
        past = jnp.zeros((1, nk), jnp.int32)
        new_sel = jnp.zeros((1, Ts), jnp.int32)
        for k in range(topn):
            blk = idx_ref[b, it * topn + k]
            in_slot = (lane // PAGE_SIZE) == k
            row = lane % PAGE_SIZE
            hit = in_slot & ((row // SLC_BLK) == (blk % bpp))
            pos = jnp.where(in_slot, (blk // bpp) * PAGE_SIZE + row, pos)
            past = jnp.where(hit, jnp.where(blk < n_past_blk, 1, 0), past)
            new_sel = jnp.maximum(new_sel, jnp.where(blk == n_past_blk, 1, 0))
        m1 = (past > 0) & (pos <= tq)
        m2 = (new_sel > 0) & (n_past_blk * SLC_BLK + new_r <= tq)
        q8 = q_ref[it] * 0.125
        kt = jnp.concatenate([kbuf[sl, k] for k in range(topn)], axis=1).astype(_BF16)
        vt = jnp.concatenate([vbuf[sl, k] for k in range(topn)], axis=1).astype(_BF16)
        s1 = jnp.dot(q8.astype(_BF16), kt, preferred_element_type=_F32)
        s2 = lax.dot_general(q8, kvn_ref[2 * A_KV + g], (((1,), (1,)), ((), ())),
                             preferred_element_type=_F32)
        s1 = jnp.where(m1, s1, -jnp.inf)
        s2 = jnp.where(m2, s2, -jnp.inf)
        mx = jnp.maximum(jnp.max(s1, axis=1, keepdims=True), jnp.max(s2, axis=1, keepdims=True))
        mx = jnp.where(mx == -jnp.inf, 0.0, mx)
        e1 = jnp.where(m1, jnp.exp(s1 - mx), 0.0)
        e2 = jnp.where(m2, jnp.exp(s2 - mx), 0.0)
        den = jnp.maximum(jnp.sum(e1, axis=1, keepdims=True) + jnp.sum(e2, axis=1, keepdims=True), 1e-30)
        o_ref[it] = (lax.dot_general(e1.astype(_BF16), vt, (((1,), (1,)), ((), ())),
                                     preferred_element_type=_F32)
                     + jnp.dot(e2, kvn_ref[3 * A_KV + g], preferred_element_type=_F32)) / den
        return carry

    lax.fori_loop(0, n_it, body, 0)


def _nsa_sample_gather(idx, page_table, q_rows, kv_new, cache4, past_len):
    B, n_it, _, dh = q_rows.shape
    Ts = n_it // A_KV
    n_past_blk = past_len // SLC_BLK
    topn = idx.shape[1] // n_it
    grid_spec = pltpu.PrefetchScalarGridSpec(
        num_scalar_prefetch=2,
        grid=(B,),
        in_specs=[pl.BlockSpec((None, n_it, 8, dh), lambda b, i, p: (b, 0, 0, 0)),
                  pl.BlockSpec((None, 4 * A_KV, Ts, dh), lambda b, i, p: (b, 0, 0, 0)),
                  pl.BlockSpec(memory_space=pl.ANY)],
        out_specs=pl.BlockSpec((None, n_it, 8, dh), lambda b, i, p: (b, 0, 0, 0)),
        scratch_shapes=[
            pltpu.VMEM((2, topn, dh, PAGE_SIZE), _F32),
            pltpu.VMEM((2, topn, dh, PAGE_SIZE), _F32),
            pltpu.SemaphoreType.DMA((2,)),
        ])
    return pl.pallas_call(
        functools.partial(_nsa_sample_gather_kernel, Ts, n_past_blk, past_len, topn),
        grid_spec=grid_spec,
        out_shape=jax.ShapeDtypeStruct((B, n_it, 8, dh), _F32),
        compiler_params=pltpu.CompilerParams(
            dimension_semantics=("arbitrary",), vmem_limit_bytes=_VMEM_LIMIT),
        name="nsa_sample_gather",
    )(idx, page_table, q_rows, kv_new, cache4)


def _nsa_combine_kernel(ocmp_ref, osel_ref, owin_ref, sm_ref, az_ref, o_ref):
    sm = sm_ref[...]
    shape = ocmp_ref.shape
    head_lane = lax.broadcasted_iota(jnp.int32, shape, 1) // 64
    g0 = 2 * M_HEADS
    acc = jnp.zeros(shape, _F32)
    for br, ref in enumerate((ocmp_ref, osel_ref, owin_ref)):
        gate = jnp.zeros(shape, _F32)
        for a in range(A_GROUP):
            for g in range(A_KV):
                c = g0 + br * A_HEADS + g * A_GROUP + a
                gate = jnp.where(head_lane == a * A_KV + g, sm[:, c:c + 1], gate)
        acc = acc + gate * ref[...]
    az = az_ref[...]
    o_ref[...] = acc * (az * jax.nn.sigmoid(az))


def _nsa_combine(ocmp, osel, owin, sm, az):
    M, W = ocmp.shape
    full = lambda w: pl.BlockSpec((M, w), lambda i: (0, 0))
    return pl.pallas_call(
        _nsa_combine_kernel,
        grid=(1,),
        in_specs=[full(W), full(W), full(W), full(128), full(W)],
        out_specs=full(W),
        out_shape=jax.ShapeDtypeStruct((M, W), _F32),
        name="nsa_combine",
    )(ocmp, osel, owin, sm, az)


def _rms_norm(x, w):
    xf = x.astype(_F32)
    y = xf * lax.rsqrt(jnp.mean(xf * xf, axis=-1, keepdims=True) + NORM_EPS)
    return (y * w.astype(_F32)).astype(x.dtype)


def _masked_softmax(s, mask):
    s = jnp.where(mask, s.astype(_F32), -jnp.inf)
    m = jnp.max(s, axis=-1, keepdims=True)
    m = jnp.where(jnp.isfinite(m), m, 0.0)
    e = jnp.where(mask, jnp.exp(s - m), 0.0)
    return e / jnp.maximum(jnp.sum(e, axis=-1, keepdims=True), 1e-30)


def _compress(rows, pe, w1, w2, k_norm):
    B, L = rows.shape[:2]
    a_dh = rows.shape[-1]
    hid_n = w1.shape[-1]
    n_seg = L // CMP_STRIDE
    seg = rows[:, :n_seg * CMP_STRIDE].reshape(B, n_seg, CMP_STRIDE, 2, A_KV, a_dh)
    w1r = w1.reshape(2, 2, CMP_STRIDE, a_dh, hid_n)
    per = jnp.einsum('bnjcgd,cxjdh->bnxcgh', seg, w1r)
    bias = jnp.einsum('cxjd,cxjdh->ch', pe.reshape(2, 2, CMP_STRIDE, a_dh), w1r)
    hid = per[:, :-1, 0] + per[:, 1:, 1] + bias[:, None, :]
    out = jnp.einsum('bncgh,chd->bncgd', jax.nn.silu(hid), w2)
    return _rms_norm(out[:, :, 0], k_norm), out[:, :, 1]


def _select_blocks(pg, tq, n_slc):
    Tq, _, n_cmp = pg.shape
    r = SLC_BLK // CMP_STRIDE
    pad_back = r * (n_slc + 1) - 1 - n_cmp
    pp = jnp.pad(pg, ((0, 0), (0, 0), (1, pad_back))).reshape(Tq, A_KV, n_slc + 1, r)
    ps = jnp.sum(pp[..., :-1, :], axis=-1) + pp[..., 1:, 0]
    blk = jnp.arange(n_slc, dtype=jnp.int32)[None, :]
    valid = blk * SLC_BLK <= tq[:, None]
    cur = (tq // SLC_BLK)[:, None]
    forced = (blk == 0) | (blk == cur) | (blk == cur - 1)
    score = jnp.where(valid[:, None], jnp.where(forced[:, None], jnp.inf, ps), -jnp.inf)
    si = score[..., None, :]
    sj = score[..., :, None]
    idx = jnp.arange(n_slc, dtype=jnp.int32)
    ahead = (si > sj) | ((si == sj) & (idx[None, :] < idx[:, None]))
    rank = jnp.sum(ahead, axis=-1)
    return rank < min(SLC_TOPN, n_slc)


def _nsa_prompt_one(q, g, kv4, kvw, kc, vc):
    T, _, dh = q.shape
    scale = dh ** -0.5
    tq = jnp.arange(T, dtype=jnp.int32)
    qg = q.reshape(T, A_KV, A_GROUP, dh)
    n_cmp = kc.shape[0]
    c_end = jnp.arange(n_cmp, dtype=jnp.int32) * CMP_STRIDE + (CMP_LEN - 1)
    s = jnp.einsum('tgad,ngd->tgan', qg, kc) * scale
    p = _masked_softmax(s, (c_end[None, :] <= tq[:, None])[:, None, None, :])
    o_cmp = jnp.einsum('tgan,ngd->tgad', p, vc)
    n_slc = T // SLC_BLK
    sel = _select_blocks(jnp.sum(p, axis=2), tq, n_slc)
    kpos = jnp.arange(T, dtype=jnp.int32)
    kmask = sel[:, :, kpos // SLC_BLK] & (kpos[None, None, :] <= tq[:, None, None])
    s = jnp.einsum('tgad,sgd->tgas', qg, kv4[:, 2]) * scale
    p = _masked_softmax(s, kmask[:, :, None, :])
    o_sel = jnp.einsum('tgas,sgd->tgad', p, kv4[:, 3])
    diff = tq[:, None] - kpos[None, :]
    wmask = (diff >= 0) & (diff < WINDOW)
    s = jnp.einsum('tgad,sgd->tgas', qg, kvw[:, 0]) * scale
    p = _masked_softmax(s, wmask[:, None, None, :])
    o_win = jnp.einsum('tgas,sgd->tgad', p, kvw[:, 1])
    gg = g.reshape(T, 3, A_KV, A_GROUP)[..., None]
    o = gg[:, 0] * o_cmp + gg[:, 1] * o_sel + gg[:, 2] * o_win
    return o.reshape(T, -1)


def _nsa_combine_jnp(q, tq, o_cmp, idx, k_sel, v_sel, kw, vw, pw, g):
    Tq, _, dh = q.shape
    qg = q.reshape(Tq, A_KV, A_GROUP, dh)
    scale = dh ** -0.5
    pos = idx[..., None] * SLC_BLK + jnp.arange(SLC_BLK, dtype=jnp.int32)
    s = jnp.einsum('tgad,tgmd->tgam', qg, k_sel.reshape(Tq, A_KV, -1, dh)) * scale
    p = _masked_softmax(s, (pos <= tq[:, None, None, None]).reshape(Tq, A_KV, 1, -1))
    o_sel = jnp.einsum('tgam,tgmd->tgad', p, v_sel.reshape(Tq, A_KV, -1, dh))
    diff = tq[:, None] - pw[None, :]
    wmask = (pw[None, :] >= 0) & (diff >= 0) & (diff < WINDOW)
    s = jnp.einsum('tgad,sgd->tgas', qg, kw) * scale
    p = _masked_softmax(s, wmask[:, None, None, :])
    o_win = jnp.einsum('tgas,sgd->tgad', p, vw)
    gg = g.astype(_F32).reshape(Tq, 3, A_KV, A_GROUP)[..., None]
    o = gg[:, 0] * o_cmp + gg[:, 1] * o_sel + gg[:, 2] * o_win
    return o.reshape(Tq, -1)


def _nsa_compressed(q, tq, kc, vc, c_end, n_slc):
    Tq, _, dh = q.shape
    qg = q.reshape(Tq, A_KV, A_GROUP, dh)
    s = jnp.einsum('tgad,ngd->tgan', qg, kc) * (dh ** -0.5)
    p = _masked_softmax(s, (c_end[None, :] <= tq[:, None])[:, None, None, :])
    o = jnp.einsum('tgan,ngd->tgad', p, vc)
    sel = _select_blocks(jnp.sum(p, axis=2), tq, n_slc)
    score = jnp.where(sel, 1.0, 0.0)
    _, idx = lax.top_k(score, min(SLC_TOPN, n_slc))
    return o, idx


def _nsa_sample(q, g, kv4, kvw, cache_kv, win_buf, page_table, layer, cmp_pe, cmp_w1, cmp_w2, k_norm_cmp):
    Bd, Ts, _, dh = q.shape
    n_pages = page_table.shape[1]
    past_len = n_pages * PAGE_SIZE
    past = cache_kv[layer, page_table, :, 0:2].reshape(Bd, past_len, 2, A_KV, dh)
    kc, vc = _compress(jnp.concatenate([past, kv4[:, :, 0:2]], axis=1), cmp_pe, cmp_w1, cmp_w2, k_norm_cmp)
    n_cmp = kc.shape[1]
    c_end = jnp.arange(n_cmp, dtype=jnp.int32) * CMP_STRIDE + (CMP_LEN - 1)
    n_past_blk = past_len // SLC_BLK
    n_new_blk = -(-Ts // SLC_BLK)
    n_slc = n_past_blk + n_new_blk
    blk_per_page = PAGE_SIZE // SLC_BLK
    new_blk = jnp.pad(kv4[:, :, 2:4], ((0, 0), (0, n_new_blk * SLC_BLK - Ts), (0, 0), (0, 0), (0, 0)))
    new_blk = new_blk.reshape(Bd, n_new_blk, SLC_BLK, 2, A_KV, dh)
    wb = win_buf.shape[1]
    tq = past_len + jnp.arange(Ts, dtype=jnp.int32)
    pw = past_len - wb + jnp.arange(wb + Ts, dtype=jnp.int32)
    g_ax = jnp.arange(A_KV)[None, :, None, None]
    r_blk = jnp.arange(SLC_BLK, dtype=jnp.int32)

    def one(qq, gg, kc_b, vc_b, new_b, kvw_b, win_b, pt_row):
        o_cmp, idx = _nsa_compressed(qq, tq, kc_b, vc_b, c_end, n_slc)
        pidx = jnp.minimum(idx, n_past_blk - 1)
        page = pt_row[pidx // blk_per_page][..., None]
        rows = ((pidx % blk_per_page) * SLC_BLK)[..., None] + r_blk
        nidx = jnp.clip(idx - n_past_blk, 0, n_new_blk - 1)[..., None]
        is_new = (idx >= n_past_blk)[..., None, None]
        k_sel = jnp.where(is_new, new_b[nidx, r_blk, 0, g_ax], cache_kv[layer, page, rows, 2, g_ax])
        v_sel = jnp.where(is_new, new_b[nidx, r_blk, 1, g_ax], cache_kv[layer, page, rows, 3, g_ax])
        kw = jnp.concatenate([win_b[:, 0], kvw_b[:, 0]], axis=0)
        vw = jnp.concatenate([win_b[:, 1], kvw_b[:, 1]], axis=0)
        return _nsa_combine(qq, tq, o_cmp, idx, k_sel, v_sel, kw, vw, pw, gg)

    return jax.vmap(one)(q, g, kc, vc, new_blk, kvw, win_buf, page_table)


def _mixer_inputs(x, norm_w, pw, m_width, a_width, a_dh):
    B, T, D = x.shape
    w_main, w_small, seg, hw, hm, groups, norm_tiles = pw
    M = B * T
    tm = 512 if M % 512 == 0 else M
    a, q, az, kv4, kvw, sm = _proj_in(x.reshape(M, D), norm_w, w_main, w_small, seg, hw, hm,
                                      groups, norm_tiles, tm)
    r3 = lambda v: v.reshape(B, T, v.shape[-1])
    return r3(a), r3(sm), r3(q), r3(kv4), r3(kvw), r3(az)


def _mixer_output(x, m_out, a_out, p, ow):
    B, T, D = x.shape
    M = B * T
    tm = 256 if M % 256 == 0 else M
    y = _out_proj(x.reshape(M, D), m_out.reshape(M, -1), a_out.reshape(M, -1), p.reshape(M, -1), *ow, tm)
    return y.reshape(B, T, D)


def kernel(x_prompt, x_sample, cache_nsa_kv, cache_win_kv, state_mlstm_C, state_mlstm_n,
           state_mlstm_m, state_mlstm_conv, page_table, p_prompt, p_sample, norm_w, w_in,
           m_conv_w, m_conv_b, m_wq, m_wk, m_wv, m_b_i, m_b_f, m_norm_w, a_q_norm, a_k_norm,
           cmp_pe, cmp_w1, cmp_w2, w_out, ple_proj, ple_norm, ple_gate):
    xp, xs = x_prompt, x_sample
    B, T, D = xp.shape
    depth = w_in.shape[0]
    m_width = m_conv_w.shape[-1]
    a_width = D - m_width
    a_dh = a_q_norm.shape[-1]
    dh = m_width // M_HEADS
    per_layer = []
    for i in range(depth):
        mw = (m_conv_w[i], m_conv_b[i], m_wq[i], m_wk[i], m_wv[i], m_b_i[i], m_b_f[i], m_norm_w[i])
        cw = (cmp_pe[i], cmp_w1[i], cmp_w2[i], a_k_norm[i, 0])
        w_out_i = jnp.concatenate(
            [w_out[i][:m_width], _heads_to_agd(w_out[i][m_width:].T, a_dh).T], axis=0)
        ow = (w_out_i, ple_proj[i], ple_norm[i], ple_gate[i])
        pw = _prep_proj_weights(w_in[i], a_q_norm[i], a_k_norm[i], m_width, a_width, a_dh)
        cwp = _prep_cmp_weights(*cw)
        a3, sm, q, kv4, kvw, az = _mixer_inputs(xp, norm_w[i], pw, m_width, a_width, a_dh)
        L = _MLSTM_L if T % _MLSTM_L == 0 else (M_CHUNK if T % M_CHUNK == 0 else T)
        m_out, conv_p, C_p, n_p, m_p = _mlstm(
            a3, sm, jnp.zeros((B, M_CONV - 1, m_width), _F32),
            jnp.zeros((B, M_HEADS, dh, dh), _F32), jnp.zeros((B, M_HEADS, dh), _F32),
            jnp.full((B, M_HEADS), -jnp.inf, _F32), *mw, L)
        kc, vc = _compress_prompt(kv4, cwp)
        a_out = _nsa_prompt(q, kv4, kvw, kc, vc, sm, az)
        xp_new = _mixer_output(xp, m_out, a_out, p_prompt[i], ow)
        kv_p = kv4.reshape(B, T, 4, A_KV, a_dh)
        win_p = kvw[:, -min(WINDOW, T):].reshape(B, -1, 2, A_KV, a_dh)
        a3, sm, q, kv4, kvw, az = _mixer_inputs(xs, norm_w[i], pw, m_width, a_width, a_dh)
        Bd, Ts = xs.shape[:2]
        Ls = _MLSTM_L if Ts % _MLSTM_L == 0 else (M_CHUNK if Ts % M_CHUNK == 0 else Ts)
        m_out, conv_s, C_s, n_s, m_s = _mlstm(
            a3, sm, state_mlstm_conv[i], state_mlstm_C[i], state_mlstm_n[i], state_mlstm_m[i], *mw, Ls)
        past_len = page_table.shape[1] * PAGE_SIZE
        cache_l = cache_nsa_kv[i]
        n_pool = cache_l.shape[0]
        wb = cache_win_kv.shape[2]
        cache_t = jnp.transpose(cache_l, (0, 2, 3, 4, 1))
        kc, vc = _compress_sample(cache_t.reshape(n_pool, 8, 2 * a_dh, PAGE_SIZE), page_table, cwp)
        ocmp, owin, idx = _nsa_sample_front(
            q, kc, vc, cache_win_kv[i].reshape(Bd, wb, 2 * A_KV * a_dh), kvw, past_len)
        topn = min(SLC_TOPN, past_len // SLC_BLK + 1)
        q_rows = q.reshape(Bd, Ts, A_GROUP, A_KV, a_dh).transpose(0, 3, 1, 2, 4)
        q_rows = jnp.pad(q_rows, ((0, 0), (0, 0), (0, 0), (0, 8 - A_GROUP), (0, 0)))
        kv_new = kv4.reshape(Bd, Ts, 4 * A_KV, a_dh).transpose(0, 2, 1, 3)
        osel = _nsa_sample_gather(
            idx[:, :, :topn].reshape(Bd, -1), page_table, q_rows.reshape(Bd, A_KV * Ts, 8, a_dh), kv_new,
            cache_t.reshape(n_pool, 4 * A_KV, a_dh, PAGE_SIZE), past_len)
        osel = osel.reshape(Bd, A_KV, Ts, 8, a_dh)[:, :, :, :A_GROUP].transpose(0, 2, 3, 1, 4)
        Ms = Bd * Ts
        a_out = _nsa_combine(ocmp.reshape(Ms, -1), osel.reshape(Ms, -1), owin.reshape(Ms, -1),
                             sm.reshape(Ms, -1), az.reshape(Ms, -1)).reshape(Bd, Ts, -1)
        xs_new = _mixer_output(xs, m_out, a_out, p_sample[i], ow)
        kv_s = kv4.reshape(Bd, Ts, 4, A_KV, a_dh)
        win_s = jnp.concatenate(
            [cache_win_kv[i], kvw.reshape(Bd, Ts, 2, A_KV, a_dh)], axis=1)[:, -wb:]
        per_layer.append((kv_p, kv_s, win_p, win_s, C_p, n_p, m_p, conv_p, C_s, n_s, m_s, conv_s))
        xp, xs = xp_new, xs_new
    (kv_p, kv_s, win_p, win_s, C_p, n_p, m_p, conv_p,
     C_s, n_s, m_s, conv_s) = [jnp.stack(a, axis=0) for a in zip(*per_layer)]
    return (xp, xs, kv_p, kv_s, win_p, win_s, C_p, n_p, m_p, conv_p, C_s, n_s, m_s, conv_s)
```

```python
import functools
import math

import jax
import jax.numpy as jnp
from jax import lax
from jax.experimental import pallas as pl
from jax.experimental.pallas import tpu as pltpu

NORM_EPS = 1e-6
M_HEADS = 4
M_CONV = 4
M_CHUNK = 64
A_HEADS = 16
A_KV = 4
A_GROUP = A_HEADS // A_KV
CMP_STRIDE = 16
CMP_LEN = 2 * CMP_STRIDE
SLC_BLK = 64
SLC_TOPN = 16
WINDOW = 512
PAGE_SIZE = 128

_BF16 = jnp.bfloat16
_F32 = jnp.float32
_VMEM_LIMIT = 56 * 1024 * 1024


def _dot(a, b):
    return jnp.dot(a.astype(_BF16), b.astype(_BF16), preferred_element_type=_F32)


_PT = 512


def _proj_in_kernel(groups, norm_tiles, x_ref, nw_ref, w_ref, wsm_ref, seg_ref, hw_ref, hm_ref,
                    a_ref, q_ref, az_ref, kv4_ref, kvw_ref, sm_ref, xn_ref):
    j = pl.program_id(1)

    @pl.when(j == 0)
    def _():
        x = x_ref[...]
        ms = jnp.mean(x * x, axis=-1, keepdims=True)
        xn = x * lax.rsqrt(ms + NORM_EPS) * nw_ref[...]
        xn_ref[...] = xn.astype(_BF16)
        sm = jnp.dot(xn_ref[...], wsm_ref[...], preferred_element_type=_F32)
        lane = lax.broadcasted_iota(jnp.int32, sm.shape, 1)
        is_gate = (lane >= 2 * M_HEADS) & (lane < 2 * M_HEADS + 3 * A_HEADS)
        sm_ref[...] = jnp.where(is_gate, jax.nn.sigmoid(sm), sm)

    y = jnp.dot(xn_ref[...], w_ref[...], preferred_element_type=_F32)
    outs = (a_ref, q_ref, az_ref, kv4_ref, kvw_ref)
    for (start, n), o_ref in zip(groups, outs):
        plain = [t for t in range(start, start + n) if t not in norm_tiles]
        normed = [t for t in range(start, start + n) if t in norm_tiles]
        if plain:
            cond = functools.reduce(jnp.logical_or, [j == t for t in plain])

            @pl.when(cond)
            def _(o_ref=o_ref):
                o_ref[...] = y

        if normed:
            cond = functools.reduce(jnp.logical_or, [j == t for t in normed])

            @pl.when(cond)
            def _(o_ref=o_ref):
                ms = jnp.dot((y * y).astype(_BF16), seg_ref[...], preferred_element_type=_F32)
                yn = y * lax.rsqrt(ms + NORM_EPS) * hw_ref[...]
                o_ref[...] = jnp.where(hm_ref[...] > 0, yn, y)


def _proj_in(x2d, norm_w, w_main, w_small, seg, hw, hm, groups, norm_tiles, tm):
    M, D = x2d.shape
    n_tiles = w_main.shape[1] // _PT
    widths = [n * _PT for _, n in groups]

    def out_map(start, n):
        return lambda i, j: (i, jnp.clip(j - start, 0, n - 1))

    out_specs = [pl.BlockSpec((tm, _PT), out_map(s, n)) for s, n in groups]
    out_specs.append(pl.BlockSpec((tm, 128), lambda i, j: (i, 0)))
    out_shape = [jax.ShapeDtypeStruct((M, w), _F32) for w in widths]
    out_shape.append(jax.ShapeDtypeStruct((M, 128), _F32))
    return pl.pallas_call(
        functools.partial(_proj_in_kernel, groups, norm_tiles),
        grid=(M // tm, n_tiles),
        in_specs=[
            pl.BlockSpec((tm, D), lambda i, j: (i, 0)),
            pl.BlockSpec((1, D), lambda i, j: (0, 0)),
            pl.BlockSpec((D, _PT), lambda i, j: (0, j)),
            pl.BlockSpec((D, 128), lambda i, j: (0, 0)),
            pl.BlockSpec((_PT, _PT), lambda i, j: (0, 0)),
            pl.BlockSpec((None, 1, _PT), lambda i, j: (j, 0, 0)),
            pl.BlockSpec((None, 1, _PT), lambda i, j: (j, 0, 0)),
        ],
        out_specs=out_specs,
        out_shape=out_shape,
        scratch_shapes=[pltpu.VMEM((tm, D), _BF16)],
        compiler_params=pltpu.CompilerParams(
            dimension_semantics=("arbitrary", "arbitrary"), vmem_limit_bytes=_VMEM_LIMIT),
        name="proj_in",
    )(x2d, norm_w.reshape(1, D), w_main, w_small, seg, hw, hm)


def _heads_to_agd(w, a_dh):
    lead = w.shape[:-1]
    return jnp.swapaxes(w.reshape(lead + (A_KV, A_GROUP, a_dh)), -3, -2).reshape(w.shape)


def _heads_to_gad(w, a_dh):
    lead = w.shape[:-1]
    return jnp.swapaxes(w.reshape(lead + (A_GROUP, A_KV, a_dh)), -3, -2).reshape(w.shape)


def _prep_proj_weights(w_in, a_q_norm, a_k_norm, m_width, a_width, a_dh):
    kvw = A_KV * a_dh
    sizes = (m_width, m_width, m_width, M_HEADS, M_HEADS, a_width, 6 * kvw, 3 * A_HEADS, a_width)
    offs = [0]
    for s in sizes:
        offs.append(offs[-1] + s)
    u0, o0, z0, i0, f0, q0, kv0, g0, az0, end = offs
    assert end == w_in.shape[1]
    w_main = jnp.concatenate([
        w_in[:, u0:i0], _heads_to_agd(w_in[:, q0:kv0], a_dh), _heads_to_agd(w_in[:, az0:end], a_dh),
        w_in[:, kv0:kv0 + 4 * kvw], w_in[:, kv0 + 4 * kvw:g0]], axis=1).astype(_BF16)
    n_small = 2 * M_HEADS + 3 * A_HEADS
    w_small = jnp.concatenate([
        w_in[:, i0:q0], w_in[:, g0:az0],
        jnp.zeros((w_in.shape[0], 128 - n_small), w_in.dtype)], axis=1).astype(_BF16)
    widths = (3 * m_width, a_width, a_width, 4 * kvw, 2 * kvw)
    groups, start = [], 0
    for w in widths:
        assert w % _PT == 0
        groups.append((start, w // _PT))
        start += w // _PT
    n_tiles = start
    hw = jnp.zeros((n_tiles * _PT,), _F32)
    hm = jnp.zeros((n_tiles * _PT,), _F32)
    qs = groups[1][0] * _PT
    hw = hw.at[qs:qs + a_width].set(jnp.tile(a_q_norm, a_width // a_dh))
    hm = hm.at[qs:qs + a_width].set(1.0)
    k4 = groups[3][0] * _PT
    hw = hw.at[k4 + 2 * kvw:k4 + 3 * kvw].set(jnp.tile(a_k_norm[1], A_KV))
    hm = hm.at[k4 + 2 * kvw:k4 + 3 * kvw].set(1.0)
    kw = groups[4][0] * _PT
    hw = hw.at[kw:kw + kvw].set(jnp.tile(a_k_norm[2], A_KV))
    hm = hm.at[kw:kw + kvw].set(1.0)
    assert kvw * 2 == _PT and _PT % a_dh == 0
    norm_tiles = tuple(range(groups[1][0], groups[1][0] + groups[1][1])) + (
        (k4 + 2 * kvw) // _PT, kw // _PT)
    r = jnp.arange(_PT) // a_dh
    seg = ((r[:, None] == r[None, :]).astype(_F32) / a_dh).astype(_BF16)
    return (w_main, w_small, seg, hw.reshape(n_tiles, 1, _PT), hm.reshape(n_tiles, 1, _PT),
            tuple(groups), norm_tiles)


def _out_proj_kernel(mw, x_ref, m_ref, a_ref, p_ref, wo_ref, pp_ref, pn_ref, pg_ref, o_ref):
    h = x_ref[...] + _dot(m_ref[...], wo_ref[0:mw, :]) + _dot(a_ref[...], wo_ref[mw:, :])
    ms = jnp.mean(h * h, axis=-1, keepdims=True)
    hn = h * lax.rsqrt(ms + NORM_EPS) * pn_ref[...]
    gate = jax.nn.sigmoid(_dot(hn, pg_ref[...]))
    o_ref[...] = h + gate * _dot(p_ref[...], pp_ref[...])


def _out_proj(x2d, m_out, a_out, p2d, w_out, ple_proj, ple_norm, ple_gate, tm):
    M, D = x2d.shape
    mw, aw, dp = m_out.shape[1], a_out.shape[1], p2d.shape[1]

    def const(shape):
        return pl.BlockSpec(shape, lambda i: (0, 0), pipeline_mode=pl.Buffered(1))

    return pl.pallas_call(
        functools.partial(_out_proj_kernel, mw),
        grid=(M // tm,),
        in_specs=[
            pl.BlockSpec((tm, D), lambda i: (i, 0)),
            pl.BlockSpec((tm, mw), lambda i: (i, 0)),
            pl.BlockSpec((tm, aw), lambda i: (i, 0)),
            pl.BlockSpec((tm, dp), lambda i: (i, 0)),
            const((mw + aw, D)),
            const((dp, D)),
            const((1, D)),
            const((D, D)),
        ],
        out_specs=pl.BlockSpec((tm, D), lambda i: (i, 0)),
        out_shape=jax.ShapeDtypeStruct((M, D), _F32),
        compiler_params=pltpu.CompilerParams(
            dimension_semantics=("arbitrary",), vmem_limit_bytes=_VMEM_LIMIT),
        name="out_proj",
    )(x2d, m_out, a_out, p2d, w_out.astype(_BF16), ple_proj.astype(_BF16),
      ple_norm.reshape(1, D), ple_gate.astype(_BF16))


_MLSTM_L = 256
_CONV_PAD = 8


def _mlstm_kernel(L, dh, n_steps,
                  u_ref, o_ref, z_ref, sm_ref, cbuf_ref, c0_ref, n0_ref, m0_ref,
                  wq_ref, wk_ref, wv_ref, cw_ref, cb_ref, bi_ref, bf_ref, nw_ref,
                  out_ref, cout_ref, nout_ref, mout_ref, convout_ref,
                  xbuf, c_s, n_s, m_s):
    h = pl.program_id(1)
    s = pl.program_id(2)
    tail = M_CONV - 1

    @pl.when(s == 0)
    def _():
        c_s[...] = c0_ref[...]
        n_s[...] = n0_ref[...]
        m_s[...] = m0_ref[...]
        xbuf[_CONV_PAD - tail:_CONV_PAD, :] = cbuf_ref[...]

    u = u_ref[...]
    xbuf[_CONV_PAD:_CONV_PAD + L, :] = u
    c = cb_ref[...]
    for j in range(M_CONV):
        c = c + xbuf[_CONV_PAD - tail + j:_CONV_PAD - tail + j + L, :] * cw_ref[j:j + 1, :]
    xbuf[_CONV_PAD - tail:_CONV_PAD, :] = u[L - tail:L, :]
    ch = c * jax.nn.sigmoid(c)
    q = _dot(ch, wq_ref[...])
    k = _dot(ch, wk_ref[...]) * (dh ** -0.5)
    v = _dot(u, wv_ref[...])

    sm = sm_ref[...]
    lane = lax.broadcasted_iota(jnp.int32, sm.shape, 1)
    i_pre = jnp.sum(jnp.where(lane == h, sm, 0.0), axis=1, keepdims=True)
    f_pre = jnp.sum(jnp.where(lane == h + M_HEADS, sm, 0.0), axis=1, keepdims=True)
    li_col = i_pre + bi_ref[:, 0:1]
    f_in = f_pre + bf_ref[:, 0:1]
    lf_col = jnp.minimum(f_in, 0.0) - jnp.log(1.0 + jnp.exp(-jnp.abs(f_in)))

    ii = lax.broadcasted_iota(jnp.int32, (L, L), 0)
    jj = lax.broadcasted_iota(jnp.int32, (L, L), 1)
    eye = ii == jj
    causal = jj <= ii
    lf_row = jnp.sum(jnp.where(eye, lf_col, 0.0), axis=0, keepdims=True)
    li_row = jnp.sum(jnp.where(eye, li_col, 0.0), axis=0, keepdims=True)
    b_col = jnp.sum(jnp.where(causal, lf_row, 0.0), axis=1, keepdims=True)
    b_row = jnp.sum(jnp.where(ii <= jj, lf_col, 0.0), axis=0, keepdims=True)

    m_prev = m_s[0:1, 0:1]
    c_prev = c_s[...]
    n_prev = n_s[...]
    logd = jnp.where(causal, b_col - b_row + li_row, -jnp.inf)
    inter = b_col + m_prev
    m_t = jnp.maximum(inter, jnp.max(logd, axis=1, keepdims=True))
    qb = q.astype(_BF16)
    sc = lax.dot_general(qb, k.astype(_BF16), (((1,), (1,)), ((), ())),
                         preferred_element_type=_F32) * jnp.exp(logd - m_t)
    a_col = jnp.exp(inter - m_t)
    num = _dot(sc, v) + a_col * _dot(qb, c_prev)
    den = jnp.sum(sc, axis=1, keepdims=True) + a_col * jnp.sum(q * n_prev, axis=1, keepdims=True)
    hh = num / jnp.maximum(jnp.abs(den), jnp.exp(-m_t))

    b_end = b_col[L - 1:L, :]
    logw = b_end - b_col + li_col
    m_new = jnp.maximum(b_end + m_prev, jnp.max(logw, axis=0, keepdims=True))
    w_col = jnp.exp(logw - m_new)
    decay = jnp.exp(b_end + m_prev - m_new)
    c_s[...] = decay * c_prev + lax.dot_general(
        k.astype(_BF16), (w_col * v).astype(_BF16), (((0,), (0,)), ((), ())),
        preferred_element_type=_F32)
    n_s[...] = decay * n_prev + jnp.sum(w_col * k, axis=0, keepdims=True)
    m_s[...] = jnp.broadcast_to(m_new, m_s.shape)

    hg = hh * jax.nn.sigmoid(o_ref[...])
    hn = hg * lax.rsqrt(jnp.mean(hg * hg, axis=-1, keepdims=True) + NORM_EPS) * nw_ref[...]
    zz = z_ref[...]
    out_ref[...] = hn * (zz * jax.nn.sigmoid(zz))

    @pl.when(s == n_steps - 1)
    def _():
        cout_ref[...] = c_s[...]
        nout_ref[...] = n_s[...]
        mout_ref[...] = m_s[...]
        convout_ref[...] = u[L - tail:L, :]


def _mlstm(a3, sm, conv_buf, C0, n0, m0, conv_w, conv_b, wq, wk, wv, b_i, b_f, norm_w, L):
    B, T, mw3 = a3.shape
    mw = mw3 // 3
    dh = mw // M_HEADS
    tail = M_CONV - 1
    assert T % L == 0 and L >= tail and (L % 8 == 0)
    n_steps = T // L
    H = M_HEADS

    def lane_b(vec):
        return jnp.broadcast_to(vec.astype(_F32)[:, None, None], (H, 1, 128))

    m0b = jnp.broadcast_to(m0.astype(_F32)[:, :, None, None], (B, H, 1, 128))
    blk = lambda off: pl.BlockSpec((None, L, dh), lambda b, h, s: (b, s, off * H + h))
    per_h = lambda r, c: pl.BlockSpec((None, r, c), lambda b, h, s: (h, 0, 0))
    st = lambda r, c: pl.BlockSpec((None, None, r, c), lambda b, h, s: (b, h, 0, 0))
    out, C, n, m, conv = pl.pallas_call(
        functools.partial(_mlstm_kernel, L, dh, n_steps),
        grid=(B, H, n_steps),
        in_specs=[
            blk(0), blk(1), blk(2),
            pl.BlockSpec((None, L, 128), lambda b, h, s: (b, s, 0)),
            pl.BlockSpec((None, tail, dh), lambda b, h, s: (b, 0, h)),
            st(dh, dh), st(1, dh), st(1, 128),
            per_h(dh, dh), per_h(dh, dh), per_h(dh, dh),
            pl.BlockSpec((M_CONV, dh), lambda b, h, s: (0, h)),
            pl.BlockSpec((1, dh), lambda b, h, s: (0, h)),
            per_h(1, 128), per_h(1, 128), per_h(1, dh),
        ],
        out_specs=[
            pl.BlockSpec((None, L, dh), lambda b, h, s: (b, s, h)),
            st(dh, dh), st(1, dh), st(1, 128),
            pl.BlockSpec((None, tail, dh), lambda b, h, s: (b, 0, h)),
        ],
        out_shape=[
            jax.ShapeDtypeStruct((B, T, mw), _F32),
            jax.ShapeDtypeStruct((B, H, dh, dh), _F32),
            jax.ShapeDtypeStruct((B, H, 1, dh), _F32),
            jax.ShapeDtypeStruct((B, H, 1, 128), _F32),
            jax.ShapeDtypeStruct((B, tail, mw), _F32),
        ],
        scratch_shapes=[
            pltpu.VMEM((_CONV_PAD + L, dh), _F32),
            pltpu.VMEM((dh, dh), _F32),
            pltpu.VMEM((1, dh), _F32),
            pltpu.VMEM((1, 128), _F32),
        ],
        compiler_params=pltpu.CompilerParams(
            dimension_semantics=("arbitrary", "arbitrary", "arbitrary"), vmem_limit_bytes=_VMEM_LIMIT),
        name="mlstm",
    )(a3, a3, a3, sm, conv_buf, C0, n0.reshape(B, H, 1, dh), m0b,
      wq.astype(_BF16), wk.astype(_BF16), wv.astype(_BF16), conv_w, conv_b.reshape(1, mw),
      lane_b(b_i), lane_b(b_f), norm_w.reshape(H, 1, dh))
    return out, conv, C, n.reshape(B, H, dh), m[:, :, 0, 0]


def _prep_cmp_weights(pe, w1, w2, k_norm):
    a_dh, hid = w1.shape[-2], w1.shape[-1]
    w1r = w1.reshape(2, 2, CMP_STRIDE, a_dh, hid)
    eye2 = jnp.eye(2, dtype=w1.dtype)
    w1p = jnp.einsum('cxjdh,ab->cjadxbh', w1r, eye2).reshape(2, CMP_STRIDE * 2 * a_dh, 2 * 2 * hid)
    bias = jnp.einsum('cxjd,cxjdh->ch', pe.reshape(2, 2, CMP_STRIDE, a_dh), w1r)
    bias2 = jnp.tile(bias, (1, 2)).reshape(2, 1, 2 * hid)
    w2p = jnp.einsum('chd,ab->cahbd', w2, eye2).reshape(2, 2 * hid, 2 * a_dh)
    r = jnp.arange(2 * a_dh) // a_dh
    seg = ((r[:, None] == r[None, :]).astype(_F32) / a_dh).astype(_BF16)
    kn = jnp.tile(k_norm, 2).reshape(1, 2 * a_dh)
    return w1p.astype(_BF16), bias2.astype(_F32), w2p.astype(_BF16), seg, kn.astype(_F32)


def _cmp_first_layer(rows_ref, n_seg, w1_ref, c):
    x = jnp.concatenate(
        [rows_ref[pl.ds(j, n_seg, stride=CMP_STRIDE), :].astype(_BF16)
         for j in range(CMP_STRIDE)], axis=1)
    return jnp.dot(x, w1_ref[c], preferred_element_type=_F32)


def _cmp_second_layer(p, n_seg, c, b_ref, w2_ref, seg_ref, kn_ref):
    hid = p[:, 0:128] + pltpu.roll(p[:, 128:256], n_seg - 1, axis=0) + b_ref[c]
    act = hid * jax.nn.sigmoid(hid)
    out = jnp.dot(act.astype(_BF16), w2_ref[c], preferred_element_type=_F32)
    if c == 0:
        ms = jnp.dot((out * out).astype(_BF16), seg_ref[...], preferred_element_type=_F32)
        out = out * lax.rsqrt(ms + NORM_EPS) * kn_ref[...]
    return out


def _compress_prompt_kernel(n_seg, r00, r01, r10, r11, w1_ref, b_ref, w2_ref, seg_ref, kn_ref,
                            kc_ref, vc_ref):
    for c, o_ref, refs in ((0, kc_ref, (r00, r01)), (1, vc_ref, (r10, r11))):
        for gp in range(2):
            p = _cmp_first_layer(refs[gp], n_seg, w1_ref, c)
            o_ref[:, gp * 128:(gp + 1) * 128] = _cmp_second_layer(p, n_seg, c, b_ref, w2_ref, seg_ref, kn_ref)


def _compress_prompt(kv4, cwp):
    B, T, W = kv4.shape
    assert W == 1024 and T % (CMP_STRIDE * 8) == 0
    n_seg = T // CMP_STRIDE
    w1p, bias2, w2p, seg, kn = cwp
    full = lambda a: pl.BlockSpec(a.shape, lambda b: (0,) * a.ndim)
    return pl.pallas_call(
        functools.partial(_compress_prompt_kernel, n_seg),
        grid=(B,),
        in_specs=[pl.BlockSpec((None, T, 128), functools.partial(lambda k, b: (b, 0, k), k))
                  for k in range(4)] + [full(w1p), full(bias2), full(w2p), full(seg), full(kn)],
        out_specs=[pl.BlockSpec((None, n_seg, 256), lambda b: (b, 0, 0))] * 2,
        out_shape=[jax.ShapeDtypeStruct((B, n_seg, 256), _F32)] * 2,
        compiler_params=pltpu.CompilerParams(
            dimension_semantics=("arbitrary",), vmem_limit_bytes=_VMEM_LIMIT),
        name="compress_prompt",
    )(kv4, kv4, kv4, kv4, w1p, bias2, w2p, seg, kn)


_QB = 128
_SEL_TK = 512
_GW = 32


def _softmax_parts(s):
    m = jnp.max(s, axis=-1, keepdims=True)
    m = jnp.where(m == -jnp.inf, 0.0, m)
    e = jnp.exp(s - m)
    return e, 1.0 / jnp.maximum(jnp.sum(e, axis=-1, keepdims=True), 1e-30)


def _softmax_rows(s, mask):
    e, inv = _softmax_parts(jnp.where(mask, s, -jnp.inf))
    return e * inv


def _tile_rows(x, n):
    return jnp.concatenate([x] * n, axis=0)


def _nsa_prompt_kernel(T, n_cmp, n_slc,
                       q_ref, ks_ref, kw_ref, kc_ref, vc_ref, sm_ref, az_ref, out_ref, *scratch):
    QB, TK, G, A = _QB, _SEL_TK, A_KV, A_GROUP
    qpl_ref, m_ref, l_ref, acc_ref, cmp_ref, win_ref = (scratch[i * G:(i + 1) * G] for i in range(6))
    R = A * QB
    t0 = pl.program_id(1) * QB
    lane256 = lax.broadcasted_iota(jnp.int32, (QB, 256), 1)
    q = q_ref[...]
    for g in range(G):
        gm = (lane256 // 64) == g
        qpl_ref[g][...] = jnp.concatenate(
            [jnp.where(gm, q[:, a * 256:(a + 1) * 256] * 0.125, 0.0) for a in range(A)],
            axis=0).astype(_BF16)
    tq = t0 + lax.broadcasted_iota(jnp.int32, (QB, 1), 0)

    n_seg = kc_ref.shape[0]
    kc = kc_ref[...].astype(_BF16)
    vc = vc_ref[...].astype(_BF16)
    ci = lax.broadcasted_iota(jnp.int32, (QB, n_seg), 1)
    cbias = _tile_rows(jnp.where((ci < n_cmp) & (ci * CMP_STRIDE + (CMP_LEN - 1) <= tq), 0.0, -jnp.inf), A)
    mi = lax.broadcasted_iota(jnp.int32, (n_seg, G * _GW), 0)
    mj = lax.broadcasted_iota(jnp.int32, (n_seg, G * _GW), 1)
    ratio = SLC_BLK // CMP_STRIDE
    ps = jnp.zeros((QB, G * _GW), _F32)
    for g in range(G):
        s = lax.dot_general(qpl_ref[g][...], kc, (((1,), (1,)), ((), ())), preferred_element_type=_F32)
        e, inv = _softmax_parts(s + cbias)
        cmp_ref[g][...] = jnp.dot(e.astype(_BF16), vc, preferred_element_type=_F32) * inv
        p = e * inv
        pg = p[0:QB]
        for a in range(1, A):
            pg = pg + p[a * QB:(a + 1) * QB]
        blk = mj - g * _GW
        msel = ((mj // _GW == g) & (mi >= ratio * blk - 1) & (mi <= ratio * blk + ratio - 1)
                ).astype(_BF16)
        pg_hi = pg.astype(_BF16)
        pg_lo = (pg - pg_hi.astype(_F32)).astype(_BF16)
        ps = ps + jnp.dot(pg_hi, msel, preferred_element_type=_F32) \
                + jnp.dot(pg_lo, msel, preferred_element_type=_F32)

    lane = lax.broadcasted_iota(jnp.int32, (QB, G * _GW), 1)
    blk = lane % _GW
    cur = tq // SLC_BLK
    valid = (blk < n_slc) & (blk * SLC_BLK <= tq)
    forced = (blk == 0) | (blk == cur) | (blk == cur - 1)
    score = jnp.where(valid, jnp.where(forced, jnp.inf, ps), -jnp.inf)
    rank = jnp.zeros((QB, G * _GW), _F32)
    for r in range(1, _GW):
        other = jnp.where(blk >= r, pltpu.roll(score, r, axis=1),
                          pltpu.roll(score, (r - _GW) % (G * _GW), axis=1))
        ahead = (other > score) | ((other == score) & (blk >= r))
        rank = rank + jnp.where(ahead, 1.0, 0.0)
    sel = jnp.where(rank < min(SLC_TOPN, n_slc), 1.0, 0.0).astype(_BF16)

    for g in range(G):
        m_ref[g][...] = jnp.full((R, 1), -jnp.inf, _F32)
        l_ref[g][...] = jnp.zeros((R, 1), _F32)
        acc_ref[g][...] = jnp.zeros((R, 256), _F32)
    er = lax.broadcasted_iota(jnp.int32, (G * _GW, TK), 0)
    ec = lax.broadcasted_iota(jnp.int32, (G * _GW, TK), 1)
    kcol = lax.broadcasted_iota(jnp.int32, (QB, TK), 1)

    def sel_tile(kt, carry):
        k0 = pl.multiple_of(kt * TK, TK)
        kt_b = ks_ref[pl.ds(k0, TK), 0:256].astype(_BF16)
        vt_b = ks_ref[pl.ds(k0, TK), 256:512].astype(_BF16)
        causal = (k0 + kcol) <= tq
        for g in range(G):
            expand = (er == g * _GW + (k0 + ec) // SLC_BLK).astype(_BF16)
            hit = jnp.dot(sel, expand, preferred_element_type=_F32)
            bias = _tile_rows(jnp.where(causal & (hit > 0.5), 0.0, -jnp.inf), A)
            s = lax.dot_general(qpl_ref[g][...], kt_b, (((1,), (1,)), ((), ())),
                                preferred_element_type=_F32) + bias
            m_old = m_ref[g][...]
            m_new = jnp.maximum(m_old, jnp.max(s, axis=-1, keepdims=True))
            m_safe = jnp.where(m_new == -jnp.inf, 0.0, m_new)
            p = jnp.exp(s - m_safe)
            alpha = jnp.exp(m_old - m_safe)
            l_ref[g][...] = alpha * l_ref[g][...] + jnp.sum(p, axis=-1, keepdims=True)
            acc_ref[g][...] = alpha * acc_ref[g][...] + jnp.dot(
                p.astype(_BF16), vt_b, preferred_element_type=_F32)
            m_ref[g][...] = m_new
        return carry

    lax.fori_loop(0, (t0 + QB + TK - 1) // TK, sel_tile, 0)

    WK = WINDOW + QB
    ws = pl.multiple_of(jnp.clip(t0 - WINDOW, 0, T - WK), QB)
    kw_b = kw_ref[pl.ds(ws, WK), 0:256].astype(_BF16)
    vw_b = kw_ref[pl.ds(ws, WK), 256:512].astype(_BF16)
    diff = tq - (ws + lax.broadcasted_iota(jnp.int32, (QB, WK), 1))
    wbias = _tile_rows(jnp.where((diff >= 0) & (diff < WINDOW), 0.0, -jnp.inf), A)

    sm = sm_ref[...]
    g0 = 2 * M_HEADS
    for g in range(G):
        s = lax.dot_general(qpl_ref[g][...], kw_b, (((1,), (1,)), ((), ())), preferred_element_type=_F32)
        e, inv = _softmax_parts(s + wbias)
        win_ref[g][...] = jnp.dot(e.astype(_BF16), vw_b, preferred_element_type=_F32) * inv
        acc_ref[g][...] = acc_ref[g][...] * (1.0 / jnp.maximum(l_ref[g][...], 1e-30))
    for a in range(A):
        rows = slice(a * QB, (a + 1) * QB)
        cols = slice(a * 256, (a + 1) * 256)
        out_a = jnp.zeros((QB, 256), _F32)
        for g in range(G):
            head = g0 + g * A + a
            mix = (sm[:, head:head + 1] * cmp_ref[g][rows, :]
                   + sm[:, head + A_HEADS:head + A_HEADS + 1] * acc_ref[g][rows, :]
                   + sm[:, head + 2 * A_HEADS:head + 2 * A_HEADS + 1] * win_ref[g][rows, :])
            out_a = jnp.where((lane256 // 64) == g, mix, out_a)
        az = az_ref[:, cols]
        out_ref[:, cols] = out_a * (az * jax.nn.sigmoid(az))


def _nsa_prompt(q, kv4, kvw, kc, vc, sm, az):
    B, T, _ = q.shape
    n_seg = kc.shape[1]
    n_cmp = n_seg - 1
    n_slc = T // SLC_BLK
    assert T % _SEL_TK == 0 and T >= WINDOW + _QB and n_slc <= _GW and A_KV * _GW == 128
    assert A_KV * 64 == 256 and n_seg % 8 == 0
    R = A_GROUP * _QB
    blk = lambda w: pl.BlockSpec((None, _QB, w), lambda b, i: (b, i, 0))
    return pl.pallas_call(
        functools.partial(_nsa_prompt_kernel, T, n_cmp, n_slc),
        grid=(B, T // _QB),
        in_specs=[
            blk(1024),
            pl.BlockSpec((None, T, 512), lambda b, i: (b, 0, 1)),
            pl.BlockSpec((None, T, 512), lambda b, i: (b, 0, 0)),
            pl.BlockSpec((None, n_seg, 256), lambda b, i: (b, 0, 0)),
            pl.BlockSpec((None, n_seg, 256), lambda b, i: (b, 0, 0)),
            blk(128), blk(1024),
        ],
        out_specs=blk(1024),
        out_shape=jax.ShapeDtypeStruct((B, T, 1024), _F32),
        scratch_shapes=(
            [pltpu.VMEM((R, 256), _BF16)] * A_KV
            + [pltpu.VMEM((R, 1), _F32)] * (2 * A_KV)
            + [pltpu.VMEM((R, 256), _F32)] * (3 * A_KV)
        ),
        compiler_params=pltpu.CompilerParams(
            dimension_semantics=("arbitrary", "arbitrary"), vmem_limit_bytes=_VMEM_LIMIT),
        name="nsa_prompt",
    )(q, kv4, kvw, kc, vc, sm, az)


_CMP_PAGES = 32


def _compress_sample_kernel(P, n_chunks, n_seg, pt_ref, cache_ref, w1_ref, b_ref, w2_ref, seg_ref,
                            kn_ref, kc_ref, vc_ref, stage, lanes, per, sem):
    b = pl.program_id(0)
    ch = pl.program_id(1)
    step = b * n_chunks + ch
    n_steps = pl.num_programs(0) * n_chunks
    slot = step % 2
    rows = P * PAGE_SIZE

    def page_copy(bb, cc, sl, p):
        page = pt_ref[bb, cc * P + p]
        return pltpu.make_async_copy(cache_ref.at[page, pl.ds(0, 4)], stage.at[sl, p], sem.at[sl])

    @pl.when(step == 0)
    def _():
        for p in range(P):
            page_copy(b, ch, slot, p).start()

    @pl.when(step + 1 < n_steps)
    def _():
        nxt = step + 1
        for p in range(P):
            page_copy(nxt // n_chunks, nxt % n_chunks, 1 - slot, p).start()

    pltpu.make_async_copy(cache_ref.at[pl.ds(0, P), pl.ds(0, 4)], stage.at[slot], sem.at[slot]).wait()

    segs = rows // CMP_STRIDE

    for k in range(4):
        for p in range(P):
            lanes[k, p * PAGE_SIZE:(p + 1) * PAGE_SIZE, :] = stage[slot, p, k].T
        per[pl.ds(pl.multiple_of(ch * segs, segs), segs), k * 256:(k + 1) * 256] = _cmp_first_layer(
            lanes.at[k], segs, w1_ref, k // 2)

    @pl.when(ch == n_chunks - 1)
    def _():
        for k in range(4):
            c, gp = k // 2, k % 2
            o_ref = kc_ref if c == 0 else vc_ref
            o_ref[:, gp * 128:(gp + 1) * 128] = _cmp_second_layer(
                per[:, k * 256:(k + 1) * 256], n_seg, c, b_ref, w2_ref, seg_ref, kn_ref)


def _compress_sample(cache_t, page_table, cwp):
    B, n_pages = page_table.shape
    P = _CMP_PAGES if n_pages % _CMP_PAGES == 0 else n_pages
    n_chunks = n_pages // P
    n_seg = n_pages * PAGE_SIZE // CMP_STRIDE
    w1p, bias2, w2p, seg, kn = cwp
    full = lambda a: pl.BlockSpec(a.shape, lambda b, c, pt: (0,) * a.ndim)
    grid_spec = pltpu.PrefetchScalarGridSpec(
        num_scalar_prefetch=1,
        grid=(B, n_chunks),
        in_specs=[pl.BlockSpec(memory_space=pl.ANY),
                  full(w1p), full(bias2), full(w2p), full(seg), full(kn)],
        out_specs=[pl.BlockSpec((None, n_seg, 256), lambda b, c, pt: (b, 0, 0))] * 2,
        scratch_shapes=[
            pltpu.VMEM((2, P, 4, 128, PAGE_SIZE), _F32),
            pltpu.VMEM((4, P * PAGE_SIZE, 128), _F32),
            pltpu.VMEM((n_seg, 1024), _F32),
            pltpu.SemaphoreType.DMA((2,)),
        ])
    return pl.pallas_call(
        functools.partial(_compress_sample_kernel, P, n_chunks, n_seg),
        grid_spec=grid_spec,
        out_shape=[jax.ShapeDtypeStruct((B, n_seg, 256), _F32)] * 2,
        compiler_params=pltpu.CompilerParams(
            dimension_semantics=("arbitrary", "arbitrary"), vmem_limit_bytes=_VMEM_LIMIT),
        name="compress_sample",
    )(page_table, cache_t, w1p, bias2, w2p, seg, kn)


def _merge_heads(o_ref, o_g, g, Ts):
    gm = (lax.broadcasted_iota(jnp.int32, (Ts, 256), 1) // 64) == g
    for a in range(A_GROUP):
        part = jnp.where(gm, o_g[a * Ts:(a + 1) * Ts], 0.0)
        cols = slice(a * 256, (a + 1) * 256)
        if g == 0:
            o_ref[:, cols] = part
        else:
            o_ref[:, cols] += part


def _nsa_sample_front_kernel(Ts, past_len, n_cmp, n_slc, NL, topn,
                             q_ref, kc_ref, vc_ref, wold_ref, wnew_ref,
                             ocmp_ref, owin_ref, idx_ref, qpl_ref):
    G, A = A_KV, A_GROUP
    R = A * Ts
    lane256 = lax.broadcasted_iota(jnp.int32, (Ts, 256), 1)
    q = q_ref[...]
    for g in range(G):
        gm = (lane256 // 64) == g
        qpl_ref[g] = jnp.concatenate(
            [jnp.where(gm, q[:, a * 256:(a + 1) * 256] * 0.125, 0.0) for a in range(A)],
            axis=0).astype(_BF16)
    tq_r = past_len + lax.broadcasted_iota(jnp.int32, (R, 1), 0) % Ts
    tq = past_len + lax.broadcasted_iota(jnp.int32, (Ts, 1), 0)

    n_seg = kc_ref.shape[0]
    kc = kc_ref[...].astype(_BF16)
    vc = vc_ref[...].astype(_BF16)
    ci = lax.broadcasted_iota(jnp.int32, (R, n_seg), 1)
    cmask = (ci < n_cmp) & (ci * CMP_STRIDE + (CMP_LEN - 1) <= tq_r)
    ratio = SLC_BLK // CMP_STRIDE
    mi = lax.broadcasted_iota(jnp.int32, (n_seg, NL), 0)
    mj = lax.broadcasted_iota(jnp.int32, (n_seg, NL), 1)
    msel = ((mi >= ratio * mj - 1) & (mi <= ratio * mj + ratio - 1)).astype(_BF16)
    lane = lax.broadcasted_iota(jnp.int32, (Ts, NL), 1)
    cur = tq // SLC_BLK
    valid = (lane < n_slc) & (lane * SLC_BLK <= tq)
    forced = (lane == 0) | (lane == cur) | (lane == cur - 1)
    scores = []
    for g in range(G):
        s = lax.dot_general(qpl_ref[g], kc, (((1,), (1,)), ((), ())), preferred_element_type=_F32)
        p = _softmax_rows(s, cmask)
        _merge_heads(ocmp_ref, jnp.dot(p.astype(_BF16), vc, preferred_element_type=_F32), g, Ts)
        pg = p[0:Ts]
        for a in range(1, A):
            pg = pg + p[a * Ts:(a + 1) * Ts]
        pg_hi = pg.astype(_BF16)
        pg_lo = (pg - pg_hi.astype(_F32)).astype(_BF16)
        ps = jnp.dot(pg_hi, msel, preferred_element_type=_F32) \
            + jnp.dot(pg_lo, msel, preferred_element_type=_F32)
        scores.append(jnp.where(valid, jnp.where(forced, jnp.inf, ps), -jnp.inf))
    score = jnp.concatenate(scores, axis=0)

    lane_f = lax.broadcasted_iota(jnp.int32, (G * Ts, NL), 1).astype(_F32)
    out_lane = lax.broadcasted_iota(jnp.int32, (G * Ts, 128), 1)
    avail = lane_f < n_slc
    picks = jnp.zeros((G * Ts, 128), _F32)
    for k in range(topn):
        mx = jnp.max(jnp.where(avail, score, -jnp.inf), axis=1, keepdims=True)
        pick = jnp.min(jnp.where(avail & (score == mx), lane_f, float(NL)), axis=1, keepdims=True)
        avail = avail & (lane_f != pick)
        picks = jnp.where(out_lane == k, pick, picks)
    idx_ref[...] = picks.astype(jnp.int32)

    wb = wold_ref.shape[0]
    k_old = wold_ref[:, 0:256].astype(_BF16)
    v_old = wold_ref[:, 256:512].astype(_BF16)
    k_new = wnew_ref[:, 0:256]
    v_new = wnew_ref[:, 256:512]
    pw_old = past_len - wb + lax.broadcasted_iota(jnp.int32, (R, wb), 1)
    d_old = tq_r - pw_old
    m_old = (pw_old >= 0) & (d_old >= 0) & (d_old < WINDOW)
    d_new = tq_r - (past_len + lax.broadcasted_iota(jnp.int32, (R, Ts), 1))
    m_new = (d_new >= 0) & (d_new < WINDOW)
    for g in range(G):
        qg = qpl_ref[g]
        s1 = lax.dot_general(qg, k_old, (((1,), (1,)), ((), ())), preferred_element_type=_F32)
        s2 = lax.dot_general(qg.astype(_F32), k_new, (((1,), (1,)), ((), ())),
                             preferred_element_type=_F32)
        s1 = jnp.where(m_old, s1, -jnp.inf)
        s2 = jnp.where(m_new, s2, -jnp.inf)
        mx = jnp.maximum(jnp.max(s1, axis=1, keepdims=True), jnp.max(s2, axis=1, keepdims=True))
        mx = jnp.where(mx == -jnp.inf, 0.0, mx)
        e1 = jnp.where(m_old, jnp.exp(s1 - mx), 0.0)
        e2 = jnp.where(m_new, jnp.exp(s2 - mx), 0.0)
        den = jnp.maximum(jnp.sum(e1, axis=1, keepdims=True) + jnp.sum(e2, axis=1, keepdims=True), 1e-30)
        o = (jnp.dot(e1.astype(_BF16), v_old, preferred_element_type=_F32)
             + jnp.dot(e2, v_new, preferred_element_type=_F32)) / den
        _merge_heads(owin_ref, o, g, Ts)


def _nsa_sample_front(q, kc, vc, win_old, win_new, past_len):
    B, Ts, _ = q.shape
    n_seg = kc.shape[1]
    n_cmp = (past_len + Ts) // CMP_STRIDE - 1
    assert n_cmp == n_seg - 1 and Ts % 8 == 0 and Ts <= SLC_BLK
    n_slc = past_len // SLC_BLK + 1
    NL = -(-n_slc // 128) * 128
    topn = min(SLC_TOPN, n_slc)
    wb = win_old.shape[1]
    R = A_GROUP * Ts
    b3 = lambda r, w: pl.BlockSpec((None, r, w), lambda b: (b, 0, 0))
    return pl.pallas_call(
        functools.partial(_nsa_sample_front_kernel, Ts, past_len, n_cmp, n_slc, NL, topn),
        grid=(B,),
        in_specs=[b3(Ts, 1024), b3(n_seg, 256), b3(n_seg, 256), b3(wb, 512), b3(Ts, 512)],
        out_specs=[b3(Ts, 1024), b3(Ts, 1024), b3(A_KV * Ts, 128)],
        out_shape=[jax.ShapeDtypeStruct((B, Ts, 1024), _F32), jax.ShapeDtypeStruct((B, Ts, 1024), _F32),
                   jax.ShapeDtypeStruct((B, A_KV * Ts, 128), jnp.int32)],
        scratch_shapes=[pltpu.VMEM((A_KV, R, 256), _BF16)],
        compiler_params=pltpu.CompilerParams(
            dimension_semantics=("arbitrary",), vmem_limit_bytes=_VMEM_LIMIT),
        name="nsa_sample_front",
    )(q, kc, vc, win_old, win_new)


def _nsa_sample_gather_kernel(Ts, n_past_blk, past_len, topn,
                              idx_ref, pt_ref, q_ref, kvn_ref, cache_ref, o_ref, kbuf, vbuf, sem):
    b = pl.program_id(0)
    n_it = A_KV * Ts
    bpp = PAGE_SIZE // SLC_BLK
    nk = topn * PAGE_SIZE

    def block_copies(it, sl, k):
        g = it // Ts
        blk = jnp.minimum(idx_ref[b, it * topn + k], n_past_blk - 1)
        page = pt_ref[b, blk // bpp]
        return (pltpu.make_async_copy(cache_ref.at[page, 2 * A_KV + g], kbuf.at[sl, k], sem.at[sl]),
                pltpu.make_async_copy(cache_ref.at[page, 3 * A_KV + g], vbuf.at[sl, k], sem.at[sl]))

    def issue(it, sl):
        for k in range(topn):
            ck, cv = block_copies(it, sl, k)
            ck.start()
            cv.start()

    issue(0, 0)
    lane = lax.broadcasted_iota(jnp.int32, (1, nk), 1)
    new_r = lax.broadcasted_iota(jnp.int32, (1, Ts), 1)

    def body(it, carry):
        sl = it % 2

        @pl.when(it + 1 < n_it)
        def _():
            issue(it + 1, 1 - sl)

        pltpu.make_async_copy(cache_ref.at[0, pl.ds(0, topn)], kbuf.at[sl], sem.at[sl]).wait()
        pltpu.make_async_copy(cache_ref.at[0, pl.ds(0, topn)], vbuf.at[sl], sem.at[sl]).wait()

        t = it % Ts
        g = it // Ts
        tq = past_len + t
        pos = jnp.zeros((1, nk), jnp.int32)
        past = jnp.zeros((1, nk), jnp.int32)
        new_sel = jnp.zeros((1, Ts), jnp.int32)
        for k in range(topn):
            blk = idx_ref[b, it * topn + k]
            in_slot = (lane // PAGE_SIZE) == k
            row = lane % PAGE_SIZE
            hit = in_slot & ((row // SLC_BLK) == (blk % bpp))
            pos = jnp.where(in_slot, (blk // bpp) * PAGE_SIZE + row, pos)
            past = jnp.where(hit, jnp.where(blk < n_past_blk, 1, 0), past)
            new_sel = jnp.maximum(new_sel, jnp.where(blk == n_past_blk, 1, 0))
        m1 = (past > 0) & (pos <= tq)
        m2 = (new_sel > 0) & (n_past_blk * SLC_BLK + new_r <= tq)
        q8 = q_ref[it] * 0.125
        kt = jnp.concatenate([kbuf[sl, k] for k in range(topn)], axis=1).astype(_BF16)
        vt = jnp.concatenate([vbuf[sl, k] for k in range(topn)], axis=1).astype(_BF16)
        s1 = jnp.dot(q8.astype(_BF16), kt, preferred_element_type=_F32)
        s2 = lax.dot_general(q8, kvn_ref[2 * A_KV + g], (((1,), (1,)), ((), ())),
                             preferred_element_type=_F32)
        s1 = jnp.where(m1, s1, -jnp.inf)
        s2 = jnp.where(m2, s2, -jnp.inf)
        mx = jnp.maximum(jnp.max(s1, axis=1, keepdims=True), jnp.max(s2, axis=1, keepdims=True))
        mx = jnp.where(mx == -jnp.inf, 0.0, mx)
        e1 = jnp.where(m1, jnp.exp(s1 - mx), 0.0)
        e2 = jnp.where(m2, jnp.exp(s2 - mx), 0.0)
        den = jnp.maximum(jnp.sum(e1, axis=1, keepdims=True) + jnp.sum(e2, axis=1, keepdims=True), 1e-30)
        o_ref[it] = (lax.dot_general(e1.astype(_BF16), vt, (((1,), (1,)), ((), ())),
                                     preferred_element_type=_F32)
                     + jnp.dot(e2, kvn_ref[3 * A_KV + g], preferred_element_type=_F32)) / den
        return carry

    lax.fori_loop(0, n_it, body, 0)


def _nsa_sample_gather(idx, page_table, q_rows, kv_new, cache4, past_len):
    B, n_it, _, dh = q_rows.shape
    Ts = n_it // A_KV
    n_past_blk = past_len // SLC_BLK
    topn = idx.shape[1] // n_it
    grid_spec = pltpu.PrefetchScalarGridSpec(
        num_scalar_prefetch=2,
        grid=(B,),
        in_specs=[pl.BlockSpec((None, n_it, 8, dh), lambda b, i, p: (b, 0, 0, 0)),
                  pl.BlockSpec((None, 4 * A_KV, Ts, dh), lambda b, i, p: (b, 0, 0, 0)),
                  pl.BlockSpec(memory_space=pl.ANY)],
        out_specs=pl.BlockSpec((None, n_it, 8, dh), lambda b, i, p: (b, 0, 0, 0)),
        scratch_shapes=[
            pltpu.VMEM((2, topn, dh, PAGE_SIZE), _F32),
            pltpu.VMEM((2, topn, dh, PAGE_SIZE), _F32),
            pltpu.SemaphoreType.DMA((2,)),
        ])
    return pl.pallas_call(
        functools.partial(_nsa_sample_gather_kernel, Ts, n_past_blk, past_len, topn),
        grid_spec=grid_spec,
        out_shape=jax.ShapeDtypeStruct((B, n_it, 8, dh), _F32),
        compiler_params=pltpu.CompilerParams(
            dimension_semantics=("arbitrary",), vmem_limit_bytes=_VMEM_LIMIT),
        name="nsa_sample_gather",
    )(idx, page_table, q_rows, kv_new, cache4)


def _nsa_combine_kernel(ocmp_ref, osel_ref, owin_ref, sm_ref, az_ref, o_ref):
    sm = sm_ref[...]
    shape = ocmp_ref.shape
    head_lane = lax.broadcasted_iota(jnp.int32, shape, 1) // 64
    g0 = 2 * M_HEADS
    acc = jnp.zeros(shape, _F32)
    for br, ref in enumerate((ocmp_ref, osel_ref, owin_ref)):
        gate = jnp.zeros(shape, _F32)
        for a in range(A_GROUP):
            for g in range(A_KV):
                c = g0 + br * A_HEADS + g * A_GROUP + a
                gate = jnp.where(head_lane == a * A_KV + g, sm[:, c:c + 1], gate)
        acc = acc + gate * ref[...]
    az = az_ref[...]
    o_ref[...] = acc * (az * jax.nn.sigmoid(az))


def _nsa_combine(ocmp, osel, owin, sm, az):
    M, W = ocmp.shape
    full = lambda w: pl.BlockSpec((M, w), lambda i: (0, 0))
    return pl.pallas_call(
        _nsa_combine_kernel,
        grid=(1,),
        in_specs=[full(W), full(W), full(W), full(128), full(W)],
        out_specs=full(W),
        out_shape=jax.ShapeDtypeStruct((M, W), _F32),
        name="nsa_combine",
    )(ocmp, osel, owin, sm, az)


def _rms_norm(x, w):
    xf = x.astype(_F32)
    y = xf * lax.rsqrt(jnp.mean(xf * xf, axis=-1, keepdims=True) + NORM_EPS)
    return (y * w.astype(_F32)).astype(x.dtype)


def _masked_softmax(s, mask):
    s = jnp.where(mask, s.astype(_F32), -jnp.inf)
    m = jnp.max(s, axis=-1, keepdims=True)
    m = jnp.where(jnp.isfinite(m), m, 0.0)
    e = jnp.where(mask, jnp.exp(s - m), 0.0)
    return e / jnp.maximum(jnp.sum(e, axis=-1, keepdims=True), 1e-30)


def _compress(rows, pe, w1, w2, k_norm):
    B, L = rows.shape[:2]
    a_dh = rows.shape[-1]
    hid_n = w1.shape[-1]
    n_seg = L // CMP_STRIDE
    seg = rows[:, :n_seg * CMP_STRIDE].reshape(B, n_seg, CMP_STRIDE, 2, A_KV, a_dh)
    w1r = w1.reshape(2, 2, CMP_STRIDE, a_dh, hid_n)
    per = jnp.einsum('bnjcgd,cxjdh->bnxcgh', seg, w1r)
    bias = jnp.einsum('cxjd,cxjdh->ch', pe.reshape(2, 2, CMP_STRIDE, a_dh), w1r)
    hid = per[:, :-1, 0] + per[:, 1:, 1] + bias[:, None, :]
    out = jnp.einsum('bncgh,chd->bncgd', jax.nn.silu(hid), w2)
    return _rms_norm(out[:, :, 0], k_norm), out[:, :, 1]


def _select_blocks(pg, tq, n_slc):
    Tq, _, n_cmp = pg.shape
    r = SLC_BLK // CMP_STRIDE
    pad_back = r * (n_slc + 1) - 1 - n_cmp
    pp = jnp.pad(pg, ((0, 0), (0, 0), (1, pad_back))).reshape(Tq, A_KV, n_slc + 1, r)
    ps = jnp.sum(pp[..., :-1, :], axis=-1) + pp[..., 1:, 0]
    blk = jnp.arange(n_slc, dtype=jnp.int32)[None, :]
    valid = blk * SLC_BLK <= tq[:, None]
    cur = (tq // SLC_BLK)[:, None]
    forced = (blk == 0) | (blk == cur) | (blk == cur - 1)
    score = jnp.where(valid[:, None], jnp.where(forced[:, None], jnp.inf, ps), -jnp.inf)
    si = score[..., None, :]
    sj = score[..., :, None]
    idx = jnp.arange(n_slc, dtype=jnp.int32)
    ahead = (si > sj) | ((si == sj) & (idx[None, :] < idx[:, None]))
    rank = jnp.sum(ahead, axis=-1)
    return rank < min(SLC_TOPN, n_slc)


def _nsa_prompt_one(q, g, kv4, kvw, kc, vc):
    T, _, dh = q.shape
    scale = dh ** -0.5
    tq = jnp.arange(T, dtype=jnp.int32)
    qg = q.reshape(T, A_KV, A_GROUP, dh)
    n_cmp = kc.shape[0]
    c_end = jnp.arange(n_cmp, dtype=jnp.int32) * CMP_STRIDE + (CMP_LEN - 1)
    s = jnp.einsum('tgad,ngd->tgan', qg, kc) * scale
    p = _masked_softmax(s, (c_end[None, :] <= tq[:, None])[:, None, None, :])
    o_cmp = jnp.einsum('tgan,ngd->tgad', p, vc)
    n_slc = T // SLC_BLK
    sel = _select_blocks(jnp.sum(p, axis=2), tq, n_slc)
    kpos = jnp.arange(T, dtype=jnp.int32)
    kmask = sel[:, :, kpos // SLC_BLK] & (kpos[None, None, :] <= tq[:, None, None])
    s = jnp.einsum('tgad,sgd->tgas', qg, kv4[:, 2]) * scale
    p = _masked_softmax(s, kmask[:, :, None, :])
    o_sel = jnp.einsum('tgas,sgd->tgad', p, kv4[:, 3])
    diff = tq[:, None] - kpos[None, :]
    wmask = (diff >= 0) & (diff < WINDOW)
    s = jnp.einsum('tgad,sgd->tgas', qg, kvw[:, 0]) * scale
    p = _masked_softmax(s, wmask[:, None, None, :])
    o_win = jnp.einsum('tgas,sgd->tgad', p, kvw[:, 1])
    gg = g.reshape(T, 3, A_KV, A_GROUP)[..., None]
    o = gg[:, 0] * o_cmp + gg[:, 1] * o_sel + gg[:, 2] * o_win
    return o.reshape(T, -1)


def _nsa_combine_jnp(q, tq, o_cmp, idx, k_sel, v_sel, kw, vw, pw, g):
    Tq, _, dh = q.shape
    qg = q.reshape(Tq, A_KV, A_GROUP, dh)
    scale = dh ** -0.5
    pos = idx[..., None] * SLC_BLK + jnp.arange(SLC_BLK, dtype=jnp.int32)
    s = jnp.einsum('tgad,tgmd->tgam', qg, k_sel.reshape(Tq, A_KV, -1, dh)) * scale
    p = _masked_softmax(s, (pos <= tq[:, None, None, None]).reshape(Tq, A_KV, 1, -1))
    o_sel = jnp.einsum('tgam,tgmd->tgad', p, v_sel.reshape(Tq, A_KV, -1, dh))
    diff = tq[:, None] - pw[None, :]
    wmask = (pw[None, :] >= 0) & (diff >= 0) & (diff < WINDOW)
    s = jnp.einsum('tgad,sgd->tgas', qg, kw) * scale
    p = _masked_softmax(s, wmask[:, None, None, :])
    o_win = jnp.einsum('tgas,sgd->tgad', p, vw)
    gg = g.astype(_F32).reshape(Tq, 3, A_KV, A_GROUP)[..., None]
    o = gg[:, 0] * o_cmp + gg[:, 1] * o_sel + gg[:, 2] * o_win
    return o.reshape(Tq, -1)


def _nsa_compressed(q, tq, kc, vc, c_end, n_slc):
    Tq, _, dh = q.shape
    qg = q.reshape(Tq, A_KV, A_GROUP, dh)
    s = jnp.einsum('tgad,ngd->tgan', qg, kc) * (dh ** -0.5)
    p = _masked_softmax(s, (c_end[None, :] <= tq[:, None])[:, None, None, :])
    o = jnp.einsum('tgan,ngd->tgad', p, vc)
    sel = _select_blocks(jnp.sum(p, axis=2), tq, n_slc)
    score = jnp.where(sel, 1.0, 0.0)
    _, idx = lax.top_k(score, min(SLC_TOPN, n_slc))
    return o, idx


def _nsa_sample(q, g, kv4, kvw, cache_kv, win_buf, page_table, layer, cmp_pe, cmp_w1, cmp_w2, k_norm_cmp):
    Bd, Ts, _, dh = q.shape
    n_pages = page_table.shape[1]
    past_len = n_pages * PAGE_SIZE
    past = cache_kv[layer, page_table, :, 0:2].reshape(Bd, past_len, 2, A_KV, dh)
    kc, vc = _compress(jnp.concatenate([past, kv4[:, :, 0:2]], axis=1), cmp_pe, cmp_w1, cmp_w2, k_norm_cmp)
    n_cmp = kc.shape[1]
    c_end = jnp.arange(n_cmp, dtype=jnp.int32) * CMP_STRIDE + (CMP_LEN - 1)
    n_past_blk = past_len // SLC_BLK
    n_new_blk = -(-Ts // SLC_BLK)
    n_slc = n_past_blk + n_new_blk
    blk_per_page = PAGE_SIZE // SLC_BLK
    new_blk = jnp.pad(kv4[:, :, 2:4], ((0, 0), (0, n_new_blk * SLC_BLK - Ts), (0, 0), (0, 0), (0, 0)))
    new_blk = new_blk.reshape(Bd, n_new_blk, SLC_BLK, 2, A_KV, dh)
    wb = win_buf.shape[1]
    tq = past_len + jnp.arange(Ts, dtype=jnp.int32)
    pw = past_len - wb + jnp.arange(wb + Ts, dtype=jnp.int32)
    g_ax = jnp.arange(A_KV)[None, :, None, None]
    r_blk = jnp.arange(SLC_BLK, dtype=jnp.int32)

    def one(qq, gg, kc_b, vc_b, new_b, kvw_b, win_b, pt_row):
        o_cmp, idx = _nsa_compressed(qq, tq, kc_b, vc_b, c_end, n_slc)
        pidx = jnp.minimum(idx, n_past_blk - 1)
        page = pt_row[pidx // blk_per_page][..., None]
        rows = ((pidx % blk_per_page) * SLC_BLK)[..., None] + r_blk
        nidx = jnp.clip(idx - n_past_blk, 0, n_new_blk - 1)[..., None]
        is_new = (idx >= n_past_blk)[..., None, None]
        k_sel = jnp.where(is_new, new_b[nidx, r_blk, 0, g_ax], cache_kv[layer, page, rows, 2, g_ax])
        v_sel = jnp.where(is_new, new_b[nidx, r_blk, 1, g_ax], cache_kv[layer, page, rows, 3, g_ax])
        kw = jnp.concatenate([win_b[:, 0], kvw_b[:, 0]], axis=0)
        vw = jnp.concatenate([win_b[:, 1], kvw_b[:, 1]], axis=0)
        return _nsa_combine(qq, tq, o_cmp, idx, k_sel, v_sel, kw, vw, pw, gg)

    return jax.vmap(one)(q, g, kc, vc, new_blk, kvw, win_buf, page_table)


def _mixer_inputs(x, norm_w, pw, m_width, a_width, a_dh):
    B, T, D = x.shape
    w_main, w_small, seg, hw, hm, groups, norm_tiles = pw
    M = B * T
    tm = next((t for t in (1024, 512) if M % t == 0), M)
    a, q, az, kv4, kvw, sm = _proj_in(x.reshape(M, D), norm_w, w_main, w_small, seg, hw, hm,
                                      groups, norm_tiles, tm)
    r3 = lambda v: v.reshape(B, T, v.shape[-1])
    return r3(a), r3(sm), r3(q), r3(kv4), r3(kvw), r3(az)


def _mixer_output(x, m_out, a_out, p, ow):
    B, T, D = x.shape
    M = B * T
    tm = 256 if M % 256 == 0 else M
    y = _out_proj(x.reshape(M, D), m_out.reshape(M, -1), a_out.reshape(M, -1), p.reshape(M, -1), *ow, tm)
    return y.reshape(B, T, D)


def kernel(x_prompt, x_sample, cache_nsa_kv, cache_win_kv, state_mlstm_C, state_mlstm_n,
           state_mlstm_m, state_mlstm_conv, page_table, p_prompt, p_sample, norm_w, w_in,
           m_conv_w, m_conv_b, m_wq, m_wk, m_wv, m_b_i, m_b_f, m_norm_w, a_q_norm, a_k_norm,
           cmp_pe, cmp_w1, cmp_w2, w_out, ple_proj, ple_norm, ple_gate):
    xp, xs = x_prompt, x_sample
    B, T, D = xp.shape
    depth = w_in.shape[0]
    m_width = m_conv_w.shape[-1]
    a_width = D - m_width
    a_dh = a_q_norm.shape[-1]
    dh = m_width // M_HEADS
    per_layer = []
    for i in range(depth):
        mw = (m_conv_w[i], m_conv_b[i], m_wq[i], m_wk[i], m_wv[i], m_b_i[i], m_b_f[i], m_norm_w[i])
        cw = (cmp_pe[i], cmp_w1[i], cmp_w2[i], a_k_norm[i, 0])
        w_out_i = jnp.concatenate(
            [w_out[i][:m_width], _heads_to_agd(w_out[i][m_width:].T, a_dh).T], axis=0)
        ow = (w_out_i, ple_proj[i], ple_norm[i], ple_gate[i])
        pw = _prep_proj_weights(w_in[i], a_q_norm[i], a_k_norm[i], m_width, a_width, a_dh)
        cwp = _prep_cmp_weights(*cw)
        a3, sm, q, kv4, kvw, az = _mixer_inputs(xp, norm_w[i], pw, m_width, a_width, a_dh)
        L = _MLSTM_L if T % _MLSTM_L == 0 else (M_CHUNK if T % M_CHUNK == 0 else T)
        m_out, conv_p, C_p, n_p, m_p = _mlstm(
            a3, sm, jnp.zeros((B, M_CONV - 1, m_width), _F32),
            jnp.zeros((B, M_HEADS, dh, dh), _F32), jnp.zeros((B, M_HEADS, dh), _F32),
            jnp.full((B, M_HEADS), -jnp.inf, _F32), *mw, L)
        kc, vc = _compress_prompt(kv4, cwp)
        a_out = _nsa_prompt(q, kv4, kvw, kc, vc, sm, az)
        xp_new = _mixer_output(xp, m_out, a_out, p_prompt[i], ow)
        kv_p = kv4.reshape(B, T, 4, A_KV, a_dh)
        win_p = kvw[:, -min(WINDOW, T):].reshape(B, -1, 2, A_KV, a_dh)
        a3, sm, q, kv4, kvw, az = _mixer_inputs(xs, norm_w[i], pw, m_width, a_width, a_dh)
        Bd, Ts = xs.shape[:2]
        Ls = _MLSTM_L if Ts % _MLSTM_L == 0 else (M_CHUNK if Ts % M_CHUNK == 0 else Ts)
        m_out, conv_s, C_s, n_s, m_s = _mlstm(
            a3, sm, state_mlstm_conv[i], state_mlstm_C[i], state_mlstm_n[i], state_mlstm_m[i], *mw, Ls)
        past_len = page_table.shape[1] * PAGE_SIZE
        cache_l = cache_nsa_kv[i]
        n_pool = cache_l.shape[0]
        wb = cache_win_kv.shape[2]
        cache_t = jnp.transpose(cache_l, (0, 2, 3, 4, 1))
        kc, vc = _compress_sample(cache_t.reshape(n_pool, 8, 2 * a_dh, PAGE_SIZE), page_table, cwp)
        ocmp, owin, idx = _nsa_sample_front(
            q, kc, vc, cache_win_kv[i].reshape(Bd, wb, 2 * A_KV * a_dh), kvw, past_len)
        topn = min(SLC_TOPN, past_len // SLC_BLK + 1)
        q_rows = q.reshape(Bd, Ts, A_GROUP, A_KV, a_dh).transpose(0, 3, 1, 2, 4)
        q_rows = jnp.pad(q_rows, ((0, 0), (0, 0), (0, 0), (0, 8 - A_GROUP), (0, 0)))
        kv_new = kv4.reshape(Bd, Ts, 4 * A_KV, a_dh).transpose(0, 2, 1, 3)
        osel = _nsa_sample_gather(
            idx[:, :, :topn].reshape(Bd, -1), page_table, q_rows.reshape(Bd, A_KV * Ts, 8, a_dh), kv_new,
            cache_t.reshape(n_pool, 4 * A_KV, a_dh, PAGE_SIZE), past_len)
        osel = osel.reshape(Bd, A_KV, Ts, 8, a_dh)[:, :, :, :A_GROUP].transpose(0, 2, 3, 1, 4)
        Ms = Bd * Ts
        a_out = _nsa_combine(ocmp.reshape(Ms, -1), osel.reshape(Ms, -1), owin.reshape(Ms, -1),
                             sm.reshape(Ms, -1), az.reshape(Ms, -1)).reshape(Bd, Ts, -1)
        xs_new = _mixer_output(xs, m_out, a_out, p_sample[i], ow)
        kv_s = kv4.reshape(Bd, Ts, 4, A_KV, a_dh)
        win_s = jnp.concatenate(
            [cache_win_kv[i], kvw.reshape(Bd, Ts, 2, A_KV, a_dh)], axis=1)[:, -wb:]
        per_layer.append((kv_p, kv_s, win_p, win_s, C_p, n_p, m_p, conv_p, C_s, n_s, m_s, conv_s))
        xp, xs = xp_new, xs_new
    (kv_p, kv_s, win_p, win_s, C_p, n_p, m_p, conv_p,
     C_s, n_s, m_s, conv_s) = [jnp.stack(a, axis=0) for a in zip(*per_layer)]
    return (xp, xs, kv_p, kv_s, win_p, win_s, C_p, n_p, m_p, conv_p, C_s, n_s, m_s, conv_s)
```

```python
import functools
import math

import jax
import jax.numpy as jnp
from jax import lax
from jax.experimental import pallas as pl
from jax.experimental.pallas import tpu as pltpu

NORM_EPS = 1e-6
M_HEADS = 4
M_CONV = 4
M_CHUNK = 64
A_HEADS = 16
A_KV = 4
A_GROUP = A_HEADS // A_KV
CMP_STRIDE = 16
CMP_LEN = 2 * CMP_STRIDE
SLC_BLK = 64
SLC_TOPN = 16
WINDOW = 512
PAGE_SIZE = 128

_BF16 = jnp.bfloat16
_F32 = jnp.float32
_VMEM_LIMIT = 56 * 1024 * 1024


def _dot(a, b):
    return jnp.dot(a.astype(_BF16), b.astype(_BF16), preferred_element_type=_F32)


_PT = 512


def _proj_in_kernel(groups, norm_tiles, x_ref, nw_ref, w_ref, wsm_ref, seg_ref, hw_ref, hm_ref,
                    a_ref, q_ref, az_ref, kv4_ref, kvw_ref, sm_ref, xn_ref):
    j = pl.program_id(1)

    @pl.when(j == 0)
    def _():
        x = x_ref[...]
        ms = jnp.mean(x * x, axis=-1, keepdims=True)
        xn = x * lax.rsqrt(ms + NORM_EPS) * nw_ref[...]
        xn_ref[...] = xn.astype(_BF16)
        sm = jnp.dot(xn_ref[...], wsm_ref[...], preferred_element_type=_F32)
        lane = lax.broadcasted_iota(jnp.int32, sm.shape, 1)
        is_gate = (lane >= 2 * M_HEADS) & (lane < 2 * M_HEADS + 3 * A_HEADS)
        sm_ref[...] = jnp.where(is_gate, jax.nn.sigmoid(sm), sm)

    y = jnp.dot(xn_ref[...], w_ref[...], preferred_element_type=_F32)
    outs = (a_ref, q_ref, az_ref, kv4_ref, kvw_ref)
    for (start, n), o_ref in zip(groups, outs):
        plain = [t for t in range(start, start + n) if t not in norm_tiles]
        normed = [t for t in range(start, start + n) if t in norm_tiles]
        if plain:
            cond = functools.reduce(jnp.logical_or, [j == t for t in plain])

            @pl.when(cond)
            def _(o_ref=o_ref):
                o_ref[...] = y

        if normed:
            cond = functools.reduce(jnp.logical_or, [j == t for t in normed])

            @pl.when(cond)
            def _(o_ref=o_ref):
                ms = jnp.dot((y * y).astype(_BF16), seg_ref[...], preferred_element_type=_F32)
                yn = y * lax.rsqrt(ms + NORM_EPS) * hw_ref[...]
                o_ref[...] = jnp.where(hm_ref[...] > 0, yn, y)


def _proj_in(x2d, norm_w, w_main, w_small, seg, hw, hm, groups, norm_tiles, tm):
    M, D = x2d.shape
    n_tiles = w_main.shape[1] // _PT
    widths = [n * _PT for _, n in groups]

    def out_map(start, n):
        return lambda i, j: (i, jnp.clip(j - start, 0, n - 1))

    out_specs = [pl.BlockSpec((tm, _PT), out_map(s, n)) for s, n in groups]
    out_specs.append(pl.BlockSpec((tm, 128), lambda i, j: (i, 0)))
    out_shape = [jax.ShapeDtypeStruct((M, w), _F32) for w in widths]
    out_shape.append(jax.ShapeDtypeStruct((M, 128), _F32))
    return pl.pallas_call(
        functools.partial(_proj_in_kernel, groups, norm_tiles),
        grid=(M // tm, n_tiles),
        in_specs=[
            pl.BlockSpec((tm, D), lambda i, j: (i, 0)),
            pl.BlockSpec((1, D), lambda i, j: (0, 0)),
            pl.BlockSpec((D, _PT), lambda i, j: (0, j)),
            pl.BlockSpec((D, 128), lambda i, j: (0, 0)),
            pl.BlockSpec((_PT, _PT), lambda i, j: (0, 0)),
            pl.BlockSpec((None, 1, _PT), lambda i, j: (j, 0, 0)),
            pl.BlockSpec((None, 1, _PT), lambda i, j: (j, 0, 0)),
        ],
        out_specs=out_specs,
        out_shape=out_shape,
        scratch_shapes=[pltpu.VMEM((tm, D), _BF16)],
        compiler_params=pltpu.CompilerParams(
            dimension_semantics=("arbitrary", "arbitrary"), vmem_limit_bytes=_VMEM_LIMIT),
        name="proj_in",
    )(x2d, norm_w.reshape(1, D), w_main, w_small, seg, hw, hm)


def _heads_to_agd(w, a_dh):
    lead = w.shape[:-1]
    return jnp.swapaxes(w.reshape(lead + (A_KV, A_GROUP, a_dh)), -3, -2).reshape(w.shape)


def _heads_to_gad(w, a_dh):
    lead = w.shape[:-1]
    return jnp.swapaxes(w.reshape(lead + (A_GROUP, A_KV, a_dh)), -3, -2).reshape(w.shape)


def _prep_proj_weights(w_in, a_q_norm, a_k_norm, m_width, a_width, a_dh):
    kvw = A_KV * a_dh
    sizes = (m_width, m_width, m_width, M_HEADS, M_HEADS, a_width, 6 * kvw, 3 * A_HEADS, a_width)
    offs = [0]
    for s in sizes:
        offs.append(offs[-1] + s)
    u0, o0, z0, i0, f0, q0, kv0, g0, az0, end = offs
    assert end == w_in.shape[1]
    w_main = jnp.concatenate([
        w_in[:, u0:i0], _heads_to_agd(w_in[:, q0:kv0], a_dh), _heads_to_agd(w_in[:, az0:end], a_dh),
        w_in[:, kv0:kv0 + 4 * kvw], w_in[:, kv0 + 4 * kvw:g0]], axis=1).astype(_BF16)
    n_small = 2 * M_HEADS + 3 * A_HEADS
    w_small = jnp.concatenate([
        w_in[:, i0:q0], w_in[:, g0:az0],
        jnp.zeros((w_in.shape[0], 128 - n_small), w_in.dtype)], axis=1).astype(_BF16)
    widths = (3 * m_width, a_width, a_width, 4 * kvw, 2 * kvw)
    groups, start = [], 0
    for w in widths:
        assert w % _PT == 0
        groups.append((start, w // _PT))
        start += w // _PT
    n_tiles = start
    hw = jnp.zeros((n_tiles * _PT,), _F32)
    hm = jnp.zeros((n_tiles * _PT,), _F32)
    qs = groups[1][0] * _PT
    hw = hw.at[qs:qs + a_width].set(jnp.tile(a_q_norm, a_width // a_dh))
    hm = hm.at[qs:qs + a_width].set(1.0)
    k4 = groups[3][0] * _PT
    hw = hw.at[k4 + 2 * kvw:k4 + 3 * kvw].set(jnp.tile(a_k_norm[1], A_KV))
    hm = hm.at[k4 + 2 * kvw:k4 + 3 * kvw].set(1.0)
    kw = groups[4][0] * _PT
    hw = hw.at[kw:kw + kvw].set(jnp.tile(a_k_norm[2], A_KV))
    hm = hm.at[kw:kw + kvw].set(1.0)
    assert kvw * 2 == _PT and _PT % a_dh == 0
    norm_tiles = tuple(range(groups[1][0], groups[1][0] + groups[1][1])) + (
        (k4 + 2 * kvw) // _PT, kw // _PT)
    r = jnp.arange(_PT) // a_dh
    seg = ((r[:, None] == r[None, :]).astype(_F32) / a_dh).astype(_BF16)
    return (w_main, w_small, seg, hw.reshape(n_tiles, 1, _PT), hm.reshape(n_tiles, 1, _PT),
            tuple(groups), norm_tiles)


def _out_proj_kernel(mw, x_ref, m_ref, a_ref, p_ref, wo_ref, pp_ref, pn_ref, pg_ref, o_ref):
    h = x_ref[...] + _dot(m_ref[...], wo_ref[0:mw, :]) + _dot(a_ref[...], wo_ref[mw:, :])
    ms = jnp.mean(h * h, axis=-1, keepdims=True)
    hn = h * lax.rsqrt(ms + NORM_EPS) * pn_ref[...]
    gate = jax.nn.sigmoid(_dot(hn, pg_ref[...]))
    o_ref[...] = h + gate * _dot(p_ref[...], pp_ref[...])


def _out_proj(x2d, m_out, a_out, p2d, w_out, ple_proj, ple_norm, ple_gate, tm):
    M, D = x2d.shape
    mw, aw, dp = m_out.shape[1], a_out.shape[1], p2d.shape[1]

    def const(shape):
        return pl.BlockSpec(shape, lambda i: (0, 0), pipeline_mode=pl.Buffered(1))

    return pl.pallas_call(
        functools.partial(_out_proj_kernel, mw),
        grid=(M // tm,),
        in_specs=[
            pl.BlockSpec((tm, D), lambda i: (i, 0)),
            pl.BlockSpec((tm, mw), lambda i: (i, 0)),
            pl.BlockSpec((tm, aw), lambda i: (i, 0)),
            pl.BlockSpec((tm, dp), lambda i: (i, 0)),
            const((mw + aw, D)),
            const((dp, D)),
            const((1, D)),
            const((D, D)),
        ],
        out_specs=pl.BlockSpec((tm, D), lambda i: (i, 0)),
        out_shape=jax.ShapeDtypeStruct((M, D), _F32),
        compiler_params=pltpu.CompilerParams(
            dimension_semantics=("arbitrary",), vmem_limit_bytes=_VMEM_LIMIT),
        name="out_proj",
    )(x2d, m_out, a_out, p2d, w_out.astype(_BF16), ple_proj.astype(_BF16),
      ple_norm.reshape(1, D), ple_gate.astype(_BF16))


_MLSTM_L = 256
_CONV_PAD = 8


def _mlstm_kernel(L, dh, n_steps,
                  u_ref, o_ref, z_ref, sm_ref, cbuf_ref, c0_ref, n0_ref, m0_ref,
                  wq_ref, wk_ref, wv_ref, cw_ref, cb_ref, bi_ref, bf_ref, nw_ref,
                  out_ref, cout_ref, nout_ref, mout_ref, convout_ref,
                  xbuf, c_s, n_s, m_s):
    h = pl.program_id(1)
    s = pl.program_id(2)
    tail = M_CONV - 1

    @pl.when(s == 0)
    def _():
        c_s[...] = c0_ref[...]
        n_s[...] = n0_ref[...]
        m_s[...] = m0_ref[...]
        xbuf[_CONV_PAD - tail:_CONV_PAD, :] = cbuf_ref[...]

    u = u_ref[...]
    xbuf[_CONV_PAD:_CONV_PAD + L, :] = u
    c = cb_ref[...]
    for j in range(M_CONV):
        c = c + xbuf[_CONV_PAD - tail + j:_CONV_PAD - tail + j + L, :] * cw_ref[j:j + 1, :]
    xbuf[_CONV_PAD - tail:_CONV_PAD, :] = u[L - tail:L, :]
    ch = c * jax.nn.sigmoid(c)
    q = _dot(ch, wq_ref[...])
    k = _dot(ch, wk_ref[...]) * (dh ** -0.5)
    v = _dot(u, wv_ref[...])

    sm = sm_ref[...]
    lane = lax.broadcasted_iota(jnp.int32, sm.shape, 1)
    i_pre = jnp.sum(jnp.where(lane == h, sm, 0.0), axis=1, keepdims=True)
    f_pre = jnp.sum(jnp.where(lane == h + M_HEADS, sm, 0.0), axis=1, keepdims=True)
    li_col = i_pre + bi_ref[:, 0:1]
    f_in = f_pre + bf_ref[:, 0:1]
    lf_col = jnp.minimum(f_in, 0.0) - jnp.log(1.0 + jnp.exp(-jnp.abs(f_in)))

    ii = lax.broadcasted_iota(jnp.int32, (L, L), 0)
    jj = lax.broadcasted_iota(jnp.int32, (L, L), 1)
    eye = ii == jj
    causal = jj <= ii
    lf_row = jnp.sum(jnp.where(eye, lf_col, 0.0), axis=0, keepdims=True)
    li_row = jnp.sum(jnp.where(eye, li_col, 0.0), axis=0, keepdims=True)
    b_col = jnp.sum(jnp.where(causal, lf_row, 0.0), axis=1, keepdims=True)
    b_row = jnp.sum(jnp.where(ii <= jj, lf_col, 0.0), axis=0, keepdims=True)

    m_prev = m_s[0:1, 0:1]
    c_prev = c_s[...]
    n_prev = n_s[...]
    logd = jnp.where(causal, b_col - b_row + li_row, -jnp.inf)
    inter = b_col + m_prev
    m_t = jnp.maximum(inter, jnp.max(logd, axis=1, keepdims=True))
    qb = q.astype(_BF16)
    sc = lax.dot_general(qb, k.astype(_BF16), (((1,), (1,)), ((), ())),
                         preferred_element_type=_F32) * jnp.exp(logd - m_t)
    a_col = jnp.exp(inter - m_t)
    num = _dot(sc, v) + a_col * _dot(qb, c_prev)
    den = jnp.sum(sc, axis=1, keepdims=True) + a_col * jnp.sum(q * n_prev, axis=1, keepdims=True)
    hh = num / jnp.maximum(jnp.abs(den), jnp.exp(-m_t))

    b_end = b_col[L - 1:L, :]
    logw = b_end - b_col + li_col
    m_new = jnp.maximum(b_end + m_prev, jnp.max(logw, axis=0, keepdims=True))
    w_col = jnp.exp(logw - m_new)
    decay = jnp.exp(b_end + m_prev - m_new)
    c_s[...] = decay * c_prev + lax.dot_general(
        k.astype(_BF16), (w_col * v).astype(_BF16), (((0,), (0,)), ((), ())),
        preferred_element_type=_F32)
    n_s[...] = decay * n_prev + jnp.sum(w_col * k, axis=0, keepdims=True)
    m_s[...] = jnp.broadcast_to(m_new, m_s.shape)

    hg = hh * jax.nn.sigmoid(o_ref[...])
    hn = hg * lax.rsqrt(jnp.mean(hg * hg, axis=-1, keepdims=True) + NORM_EPS) * nw_ref[...]
    zz = z_ref[...]
    out_ref[...] = hn * (zz * jax.nn.sigmoid(zz))

    @pl.when(s == n_steps - 1)
    def _():
        cout_ref[...] = c_s[...]
        nout_ref[...] = n_s[...]
        mout_ref[...] = m_s[...]
        convout_ref[...] = u[L - tail:L, :]


def _mlstm(a3, sm, conv_buf, C0, n0, m0, conv_w, conv_b, wq, wk, wv, b_i, b_f, norm_w, L):
    B, T, mw3 = a3.shape
    mw = mw3 // 3
    dh = mw // M_HEADS
    tail = M_CONV - 1
    assert T % L == 0 and L >= tail and (L % 8 == 0)
    n_steps = T // L
    H = M_HEADS

    def lane_b(vec):
        return jnp.broadcast_to(vec.astype(_F32)[:, None, None], (H, 1, 128))

    m0b = jnp.broadcast_to(m0.astype(_F32)[:, :, None, None], (B, H, 1, 128))
    blk = lambda off: pl.BlockSpec((None, L, dh), lambda b, h, s: (b, s, off * H + h))
    per_h = lambda r, c: pl.BlockSpec((None, r, c), lambda b, h, s: (h, 0, 0))
    st = lambda r, c: pl.BlockSpec((None, None, r, c), lambda b, h, s: (b, h, 0, 0))
    out, C, n, m, conv = pl.pallas_call(
        functools.partial(_mlstm_kernel, L, dh, n_steps),
        grid=(B, H, n_steps),
        in_specs=[
            blk(0), blk(1), blk(2),
            pl.BlockSpec((None, L, 128), lambda b, h, s: (b, s, 0)),
            pl.BlockSpec((None, tail, dh), lambda b, h, s: (b, 0, h)),
            st(dh, dh), st(1, dh), st(1, 128),
            per_h(dh, dh), per_h(dh, dh), per_h(dh, dh),
            pl.BlockSpec((M_CONV, dh), lambda b, h, s: (0, h)),
            pl.BlockSpec((1, dh), lambda b, h, s: (0, h)),
            per_h(1, 128), per_h(1, 128), per_h(1, dh),
        ],
        out_specs=[
            pl.BlockSpec((None, L, dh), lambda b, h, s: (b, s, h)),
            st(dh, dh), st(1, dh), st(1, 128),
            pl.BlockSpec((None, tail, dh), lambda b, h, s: (b, 0, h)),
        ],
        out_shape=[
            jax.ShapeDtypeStruct((B, T, mw), _F32),
            jax.ShapeDtypeStruct((B, H, dh, dh), _F32),
            jax.ShapeDtypeStruct((B, H, 1, dh), _F32),
            jax.ShapeDtypeStruct((B, H, 1, 128), _F32),
            jax.ShapeDtypeStruct((B, tail, mw), _F32),
        ],
        scratch_shapes=[
            pltpu.VMEM((_CONV_PAD + L, dh), _F32),
            pltpu.VMEM((dh, dh), _F32),
            pltpu.VMEM((1, dh), _F32),
            pltpu.VMEM((1, 128), _F32),
        ],
        compiler_params=pltpu.CompilerParams(
            dimension_semantics=("arbitrary", "arbitrary", "arbitrary"), vmem_limit_bytes=_VMEM_LIMIT),
        name="mlstm",
    )(a3, a3, a3, sm, conv_buf, C0, n0.reshape(B, H, 1, dh), m0b,
      wq.astype(_BF16), wk.astype(_BF16), wv.astype(_BF16), conv_w, conv_b.reshape(1, mw),
      lane_b(b_i), lane_b(b_f), norm_w.reshape(H, 1, dh))
    return out, conv, C, n.reshape(B, H, dh), m[:, :, 0, 0]


def _prep_cmp_weights(pe, w1, w2, k_norm):
    a_dh, hid = w1.shape[-2], w1.shape[-1]
    w1r = w1.reshape(2, 2, CMP_STRIDE, a_dh, hid)
    eye2 = jnp.eye(2, dtype=w1.dtype)
    w1p = jnp.einsum('cxjdh,ab->cjadxbh', w1r, eye2).reshape(2, CMP_STRIDE * 2 * a_dh, 2 * 2 * hid)
    bias = jnp.einsum('cxjd,cxjdh->ch', pe.reshape(2, 2, CMP_STRIDE, a_dh), w1r)
    bias2 = jnp.tile(bias, (1, 2)).reshape(2, 1, 2 * hid)
    w2p = jnp.einsum('chd,ab->cahbd', w2, eye2).reshape(2, 2 * hid, 2 * a_dh)
    r = jnp.arange(2 * a_dh) // a_dh
    seg = ((r[:, None] == r[None, :]).astype(_F32) / a_dh).astype(_BF16)
    kn = jnp.tile(k_norm, 2).reshape(1, 2 * a_dh)
    return w1p.astype(_BF16), bias2.astype(_F32), w2p.astype(_BF16), seg, kn.astype(_F32)


def _cmp_first_layer(rows_ref, n_seg, w1_ref, c):
    x = jnp.concatenate(
        [rows_ref[pl.ds(j, n_seg, stride=CMP_STRIDE), :].astype(_BF16)
         for j in range(CMP_STRIDE)], axis=1)
    return jnp.dot(x, w1_ref[c], preferred_element_type=_F32)


def _cmp_second_layer(p, n_seg, c, b_ref, w2_ref, seg_ref, kn_ref):
    hid = p[:, 0:128] + pltpu.roll(p[:, 128:256], n_seg - 1, axis=0) + b_ref[c]
    act = hid * jax.nn.sigmoid(hid)
    out = jnp.dot(act.astype(_BF16), w2_ref[c], preferred_element_type=_F32)
    if c == 0:
        ms = jnp.dot((out * out).astype(_BF16), seg_ref[...], preferred_element_type=_F32)
        out = out * lax.rsqrt(ms + NORM_EPS) * kn_ref[...]
    return out


def _compress_prompt_kernel(n_seg, r00, r01, r10, r11, w1_ref, b_ref, w2_ref, seg_ref, kn_ref,
                            kc_ref, vc_ref):
    for c, o_ref, refs in ((0, kc_ref, (r00, r01)), (1, vc_ref, (r10, r11))):
        for gp in range(2):
            p = _cmp_first_layer(refs[gp], n_seg, w1_ref, c)
            o_ref[:, gp * 128:(gp + 1) * 128] = _cmp_second_layer(p, n_seg, c, b_ref, w2_ref, seg_ref, kn_ref)


def _compress_prompt(kv4, cwp):
    B, T, W = kv4.shape
    assert W == 1024 and T % (CMP_STRIDE * 8) == 0
    n_seg = T // CMP_STRIDE
    w1p, bias2, w2p, seg, kn = cwp
    full = lambda a: pl.BlockSpec(a.shape, lambda b: (0,) * a.ndim)
    return pl.pallas_call(
        functools.partial(_compress_prompt_kernel, n_seg),
        grid=(B,),
        in_specs=[pl.BlockSpec((None, T, 128), functools.partial(lambda k, b: (b, 0, k), k))
                  for k in range(4)] + [full(w1p), full(bias2), full(w2p), full(seg), full(kn)],
        out_specs=[pl.BlockSpec((None, n_seg, 256), lambda b: (b, 0, 0))] * 2,
        out_shape=[jax.ShapeDtypeStruct((B, n_seg, 256), _F32)] * 2,
        compiler_params=pltpu.CompilerParams(
            dimension_semantics=("arbitrary",), vmem_limit_bytes=_VMEM_LIMIT),
        name="compress_prompt",
    )(kv4, kv4, kv4, kv4, w1p, bias2, w2p, seg, kn)


_QB = 128
_SEL_TK = 512
_GW = 32


def _softmax_parts(s):
    m = jnp.max(s, axis=-1, keepdims=True)
    m = jnp.where(m == -jnp.inf, 0.0, m)
    e = jnp.exp(s - m)
    return e, 1.0 / jnp.maximum(jnp.sum(e, axis=-1, keepdims=True), 1e-30)


def _softmax_rows(s, mask):
    e, inv = _softmax_parts(jnp.where(mask, s, -jnp.inf))
    return e * inv


def _col_softmax_parts(s):
    m = jnp.max(s, axis=0, keepdims=True)
    m = jnp.where(m == -jnp.inf, 0.0, m)
    e = jnp.exp(s - m)
    return e, 1.0 / jnp.maximum(jnp.sum(e, axis=0, keepdims=True), 1e-30)


def _tile_cols(x, n):
    return jnp.concatenate([x] * n, axis=1)


def _nsa_prompt_t_kernel(T, n_cmp, n_slc,
                         q_ref, ks_ref, kw_ref, kc_ref, vc_ref, sm_ref, az_ref, out_ref,
                         ksb, vst, kwb, vwt, vct, *scratch):
    QB, TK, G, A = _QB, _SEL_TK, A_KV, A_GROUP
    R = A * QB
    NB = T // 128
    qplt_ref, m_ref, l_ref, acc_ref, cmp_ref, win_ref = (scratch[i * G:(i + 1) * G] for i in range(6))

    @pl.when(pl.program_id(1) == 0)
    def _():
        for blk in range(NB):
            rows = slice(blk * 128, (blk + 1) * 128)
            ksb[rows, :] = ks_ref[rows, 0:256].astype(_BF16)
            vst[blk] = ks_ref[rows, 256:512].T.astype(_BF16)
            kwb[rows, :] = kw_ref[rows, 0:256].astype(_BF16)
            vwt[blk] = kw_ref[rows, 256:512].T.astype(_BF16)
        vct[...] = vc_ref[...].T.astype(_BF16)

    t0 = pl.program_id(1) * QB
    sub256 = lax.broadcasted_iota(jnp.int32, (256, QB), 0)
    qt = q_ref[...].T
    for g in range(G):
        gm = (sub256 // 64) == g
        qplt_ref[g][...] = jnp.concatenate(
            [jnp.where(gm, qt[a * 256:(a + 1) * 256, :] * 0.125, 0.0) for a in range(A)],
            axis=1).astype(_BF16)
    tq = t0 + lax.broadcasted_iota(jnp.int32, (1, QB), 1)

    n_seg = kc_ref.shape[0]
    kc = kc_ref[...].astype(_BF16)
    ci = lax.broadcasted_iota(jnp.int32, (n_seg, QB), 0)
    cbias = _tile_cols(jnp.where((ci < n_cmp) & (ci * CMP_STRIDE + (CMP_LEN - 1) <= tq), 0.0, -jnp.inf), A)
    ratio = SLC_BLK // CMP_STRIDE
    mj = lax.broadcasted_iota(jnp.int32, (_GW, n_seg), 0)
    mi = lax.broadcasted_iota(jnp.int32, (_GW, n_seg), 1)
    msel_t = ((mi >= ratio * mj - 1) & (mi <= ratio * mj + ratio - 1)).astype(_BF16)
    blk = lax.broadcasted_iota(jnp.int32, (_GW, QB), 0)
    cur = tq // SLC_BLK
    valid = (blk < n_slc) & (blk * SLC_BLK <= tq)
    forced = (blk == 0) | (blk == cur) | (blk == cur - 1)
    sels = []
    for g in range(G):
        s = jnp.dot(kc, qplt_ref[g][...], preferred_element_type=_F32) + cbias
        e, inv = _col_softmax_parts(s)
        cmp_ref[g][...] = jnp.dot(vct[...], e.astype(_BF16), preferred_element_type=_F32) * inv
        p = e * inv
        pg = p[:, 0:QB]
        for a in range(1, A):
            pg = pg + p[:, a * QB:(a + 1) * QB]
        pg_hi = pg.astype(_BF16)
        pg_lo = (pg - pg_hi.astype(_F32)).astype(_BF16)
        ps = jnp.dot(msel_t, pg_hi, preferred_element_type=_F32) \
            + jnp.dot(msel_t, pg_lo, preferred_element_type=_F32)
        score = jnp.where(valid, jnp.where(forced, jnp.inf, ps), -jnp.inf)
        rank = jnp.zeros((_GW, QB), _F32)
        for i in range(_GW):
            row = score[i:i + 1, :]
            ahead = (row > score) | ((row == score) & (blk > i))
            rank = rank + jnp.where(ahead, 1.0, 0.0)
        sels.append(jnp.where(rank < min(SLC_TOPN, n_slc), 1.0, 0.0).astype(_BF16))

    for g in range(G):
        m_ref[g][...] = jnp.full((1, R), -jnp.inf, _F32)
        l_ref[g][...] = jnp.zeros((1, R), _F32)
        acc_ref[g][...] = jnp.zeros((256, R), _F32)
    kr = lax.broadcasted_iota(jnp.int32, (TK, _GW), 0)
    kj = lax.broadcasted_iota(jnp.int32, (TK, _GW), 1)
    krow = lax.broadcasted_iota(jnp.int32, (TK, QB), 0)

    def sel_tile(kt, carry):
        k0 = pl.multiple_of(kt * TK, TK)
        kt_b = ksb[pl.ds(k0, TK), :]
        b0 = kt * (TK // 128)
        vt_b = jnp.concatenate([vst[b0 + i] for i in range(TK // 128)], axis=1)
        causal = (k0 + krow) <= tq
        expand_t = (((k0 + kr) // SLC_BLK) == kj).astype(_BF16)
        for g in range(G):
            hit = jnp.dot(expand_t, sels[g], preferred_element_type=_F32)
            bias = _tile_cols(jnp.where(causal & (hit > 0.5), 0.0, -jnp.inf), A)
            s = jnp.dot(kt_b, qplt_ref[g][...], preferred_element_type=_F32) + bias
            m_old = m_ref[g][...]
            m_new = jnp.maximum(m_old, jnp.max(s, axis=0, keepdims=True))
            m_safe = jnp.where(m_new == -jnp.inf, 0.0, m_new)
            p = jnp.exp(s - m_safe)
            alpha = jnp.exp(m_old - m_safe)
            l_ref[g][...] = alpha * l_ref[g][...] + jnp.sum(p, axis=0, keepdims=True)
            acc_ref[g][...] = alpha * acc_ref[g][...] + jnp.dot(
                vt_b, p.astype(_BF16), preferred_element_type=_F32)
            m_ref[g][...] = m_new
        return carry

    lax.fori_loop(0, (t0 + QB + TK - 1) // TK, sel_tile, 0)

    WK = WINDOW + QB
    ws = pl.multiple_of(jnp.clip(t0 - WINDOW, 0, T - WK), QB)
    kw_b = kwb[pl.ds(ws, WK), :]
    wb0 = ws // 128
    vw_b = jnp.concatenate([vwt[wb0 + i] for i in range(WK // 128)], axis=1)
    diff = tq - (ws + lax.broadcasted_iota(jnp.int32, (WK, QB), 0))
    wbias = _tile_cols(jnp.where((diff >= 0) & (diff < WINDOW), 0.0, -jnp.inf), A)
    for g in range(G):
        s = jnp.dot(kw_b, qplt_ref[g][...], preferred_element_type=_F32) + wbias
        e, inv = _col_softmax_parts(s)
        win_ref[g][...] = jnp.dot(vw_b, e.astype(_BF16), preferred_element_type=_F32) * inv
        acc_ref[g][...] = acc_ref[g][...] * (1.0 / jnp.maximum(l_ref[g][...], 1e-30))

    smt = sm_ref[...].T
    g0 = 2 * M_HEADS
    for a in range(A):
        lanes = slice(a * QB, (a + 1) * QB)
        mix = jnp.zeros((256, QB), _F32)
        for g in range(G):
            head = g0 + g * A + a
            t = (smt[head:head + 1, :] * cmp_ref[g][:, lanes]
                 + smt[head + A_HEADS:head + A_HEADS + 1, :] * acc_ref[g][:, lanes]
                 + smt[head + 2 * A_HEADS:head + 2 * A_HEADS + 1, :] * win_ref[g][:, lanes])
            mix = jnp.where((sub256 // 64) == g, t, mix)
        az = az_ref[:, a * 256:(a + 1) * 256]
        out_ref[:, a * 256:(a + 1) * 256] = mix.T * (az * jax.nn.sigmoid(az))


def _nsa_prompt_t(q, kv4, kvw, kc, vc, sm, az):
    B, T, _ = q.shape
    n_seg = kc.shape[1]
    n_cmp = n_seg - 1
    n_slc = T // SLC_BLK
    assert T % _SEL_TK == 0 and T >= WINDOW + _QB and n_slc <= _GW and n_seg % 128 == 0
    R = A_GROUP * _QB
    blk = lambda w: pl.BlockSpec((None, _QB, w), lambda b, i: (b, i, 0))
    return pl.pallas_call(
        functools.partial(_nsa_prompt_t_kernel, T, n_cmp, n_slc),
        grid=(B, T // _QB),
        in_specs=[
            blk(1024),
            pl.BlockSpec((None, T, 512), lambda b, i: (b, 0, 1)),
            pl.BlockSpec((None, T, 512), lambda b, i: (b, 0, 0)),
            pl.BlockSpec((None, n_seg, 256), lambda b, i: (b, 0, 0)),
            pl.BlockSpec((None, n_seg, 256), lambda b, i: (b, 0, 0)),
            blk(128), blk(1024),
        ],
        out_specs=blk(1024),
        out_shape=jax.ShapeDtypeStruct((B, T, 1024), _F32),
        scratch_shapes=(
            [pltpu.VMEM((T, 256), _BF16), pltpu.VMEM((T // 128, 256, 128), _BF16),
             pltpu.VMEM((T, 256), _BF16), pltpu.VMEM((T // 128, 256, 128), _BF16),
             pltpu.VMEM((256, n_seg), _BF16)]
            + [pltpu.VMEM((256, R), _BF16)] * A_KV
            + [pltpu.VMEM((1, R), _F32)] * (2 * A_KV)
            + [pltpu.VMEM((256, R), _F32)] * (3 * A_KV)
        ),
        compiler_params=pltpu.CompilerParams(
            dimension_semantics=("arbitrary", "arbitrary"), vmem_limit_bytes=_VMEM_LIMIT),
        name="nsa_prompt",
    )(q, kv4, kvw, kc, vc, sm, az)


_CMP_PAGES = 32


def _compress_sample_kernel(P, n_chunks, n_seg, pt_ref, cache_ref, w1_ref, b_ref, w2_ref, seg_ref,
                            kn_ref, kc_ref, vc_ref, stage, lanes, per, sem):
    b = pl.program_id(0)
    ch = pl.program_id(1)
    step = b * n_chunks + ch
    n_steps = pl.num_programs(0) * n_chunks
    slot = step % 2
    rows = P * PAGE_SIZE

    def page_copy(bb, cc, sl, p):
        page = pt_ref[bb, cc * P + p]
        return pltpu.make_async_copy(cache_ref.at[page, pl.ds(0, 4)], stage.at[sl, p], sem.at[sl])

    @pl.when(step == 0)
    def _():
        for p in range(P):
            page_copy(b, ch, slot, p).start()

    @pl.when(step + 1 < n_steps)
    def _():
        nxt = step + 1
        for p in range(P):
            page_copy(nxt // n_chunks, nxt % n_chunks, 1 - slot, p).start()

    pltpu.make_async_copy(cache_ref.at[pl.ds(0, P), pl.ds(0, 4)], stage.at[slot], sem.at[slot]).wait()

    segs = rows // CMP_STRIDE

    for k in range(4):
        for p in range(P):
            lanes[k, p * PAGE_SIZE:(p + 1) * PAGE_SIZE, :] = stage[slot, p, k].T
        per[pl.ds(pl.multiple_of(ch * segs, segs), segs), k * 256:(k + 1) * 256] = _cmp_first_layer(
            lanes.at[k], segs, w1_ref, k // 2)

    @pl.when(ch == n_chunks - 1)
    def _():
        for k in range(4):
            c, gp = k // 2, k % 2
            o_ref = kc_ref if c == 0 else vc_ref
            o_ref[:, gp * 128:(gp + 1) * 128] = _cmp_second_layer(
                per[:, k * 256:(k + 1) * 256], n_seg, c, b_ref, w2_ref, seg_ref, kn_ref)


def _compress_sample(cache_t, page_table, cwp):
    B, n_pages = page_table.shape
    P = _CMP_PAGES if n_pages % _CMP_PAGES == 0 else n_pages
    n_chunks = n_pages // P
    n_seg = n_pages * PAGE_SIZE // CMP_STRIDE
    w1p, bias2, w2p, seg, kn = cwp
    full = lambda a: pl.BlockSpec(a.shape, lambda b, c, pt: (0,) * a.ndim)
    grid_spec = pltpu.PrefetchScalarGridSpec(
        num_scalar_prefetch=1,
        grid=(B, n_chunks),
        in_specs=[pl.BlockSpec(memory_space=pl.ANY),
                  full(w1p), full(bias2), full(w2p), full(seg), full(kn)],
        out_specs=[pl.BlockSpec((None, n_seg, 256), lambda b, c, pt: (b, 0, 0))] * 2,
        scratch_shapes=[
            pltpu.VMEM((2, P, 4, 128, PAGE_SIZE), _F32),
            pltpu.VMEM((4, P * PAGE_SIZE, 128), _F32),
            pltpu.VMEM((n_seg, 1024), _F32),
            pltpu.SemaphoreType.DMA((2,)),
        ])
    return pl.pallas_call(
        functools.partial(_compress_sample_kernel, P, n_chunks, n_seg),
        grid_spec=grid_spec,
        out_shape=[jax.ShapeDtypeStruct((B, n_seg, 256), _F32)] * 2,
        compiler_params=pltpu.CompilerParams(
            dimension_semantics=("arbitrary", "arbitrary"), vmem_limit_bytes=_VMEM_LIMIT),
        name="compress_sample",
    )(page_table, cache_t, w1p, bias2, w2p, seg, kn)


def _merge_heads(o_ref, o_g, g, Ts):
    gm = (lax.broadcasted_iota(jnp.int32, (Ts, 256), 1) // 64) == g
    for a in range(A_GROUP):
        part = jnp.where(gm, o_g[a * Ts:(a + 1) * Ts], 0.0)
        cols = slice(a * 256, (a + 1) * 256)
        if g == 0:
            o_ref[:, cols] = part
        else:
            o_ref[:, cols] += part


def _nsa_sample_front_kernel(Ts, past_len, n_cmp, n_slc, NL, topn,
                             q_ref, kc_ref, vc_ref, wold_ref, wnew_ref,
                             ocmp_ref, owin_ref, idx_ref, qpl_ref):
    G, A = A_KV, A_GROUP
    R = A * Ts
    lane256 = lax.broadcasted_iota(jnp.int32, (Ts, 256), 1)
    q = q_ref[...]
    for g in range(G):
        gm = (lane256 // 64) == g
        qpl_ref[g] = jnp.concatenate(
            [jnp.where(gm, q[:, a * 256:(a + 1) * 256] * 0.125, 0.0) for a in range(A)],
            axis=0).astype(_BF16)
    tq_r = past_len + lax.broadcasted_iota(jnp.int32, (R, 1), 0) % Ts
    tq = past_len + lax.broadcasted_iota(jnp.int32, (Ts, 1), 0)

    n_seg = kc_ref.shape[0]
    kc = kc_ref[...].astype(_BF16)
    vc = vc_ref[...].astype(_BF16)
    ci = lax.broadcasted_iota(jnp.int32, (R, n_seg), 1)
    cmask = (ci < n_cmp) & (ci * CMP_STRIDE + (CMP_LEN - 1) <= tq_r)
    ratio = SLC_BLK // CMP_STRIDE
    mi = lax.broadcasted_iota(jnp.int32, (n_seg, NL), 0)
    mj = lax.broadcasted_iota(jnp.int32, (n_seg, NL), 1)
    msel = ((mi >= ratio * mj - 1) & (mi <= ratio * mj + ratio - 1)).astype(_BF16)
    lane = lax.broadcasted_iota(jnp.int32, (Ts, NL), 1)
    cur = tq // SLC_BLK
    valid = (lane < n_slc) & (lane * SLC_BLK <= tq)
    forced = (lane == 0) | (lane == cur) | (lane == cur - 1)
    scores = []
    for g in range(G):
        s = lax.dot_general(qpl_ref[g], kc, (((1,), (1,)), ((), ())), preferred_element_type=_F32)
        p = _softmax_rows(s, cmask)
        _merge_heads(ocmp_ref, jnp.dot(p.astype(_BF16), vc, preferred_element_type=_F32), g, Ts)
        pg = p[0:Ts]
        for a in range(1, A):
            pg = pg + p[a * Ts:(a + 1) * Ts]
        pg_hi = pg.astype(_BF16)
        pg_lo = (pg - pg_hi.astype(_F32)).astype(_BF16)
        ps = jnp.dot(pg_hi, msel, preferred_element_type=_F32) \
            + jnp.dot(pg_lo, msel, preferred_element_type=_F32)
        scores.append(jnp.where(valid, jnp.where(forced, jnp.inf, ps), -jnp.inf))
    score = jnp.concatenate(scores, axis=0)

    lane_f = lax.broadcasted_iota(jnp.int32, (G * Ts, NL), 1).astype(_F32)
    out_lane = lax.broadcasted_iota(jnp.int32, (G * Ts, 128), 1)
    avail = lane_f < n_slc
    picks = jnp.zeros((G * Ts, 128), _F32)
    for k in range(topn):
        mx = jnp.max(jnp.where(avail, score, -jnp.inf), axis=1, keepdims=True)
        pick = jnp.min(jnp.where(avail & (score == mx), lane_f, float(NL)), axis=1, keepdims=True)
        avail = avail & (lane_f != pick)
        picks = jnp.where(out_lane == k, pick, picks)
    idx_ref[...] = picks.astype(jnp.int32)

    wb = wold_ref.shape[0]
    k_old = wold_ref[:, 0:256].astype(_BF16)
    v_old = wold_ref[:, 256:512].astype(_BF16)
    k_new = wnew_ref[:, 0:256]
    v_new = wnew_ref[:, 256:512]
    pw_old = past_len - wb + lax.broadcasted_iota(jnp.int32, (R, wb), 1)
    d_old = tq_r - pw_old
    m_old = (pw_old >= 0) & (d_old >= 0) & (d_old < WINDOW)
    d_new = tq_r - (past_len + lax.broadcasted_iota(jnp.int32, (R, Ts), 1))
    m_new = (d_new >= 0) & (d_new < WINDOW)
    for g in range(G):
        qg = qpl_ref[g]
        s1 = lax.dot_general(qg, k_old, (((1,), (1,)), ((), ())), preferred_element_type=_F32)
        s2 = lax.dot_general(qg.astype(_F32), k_new, (((1,), (1,)), ((), ())),
                             preferred_element_type=_F32)
        s1 = jnp.where(m_old, s1, -jnp.inf)
        s2 = jnp.where(m_new, s2, -jnp.inf)
        mx = jnp.maximum(jnp.max(s1, axis=1, keepdims=True), jnp.max(s2, axis=1, keepdims=True))
        mx = jnp.where(mx == -jnp.inf, 0.0, mx)
        e1 = jnp.where(m_old, jnp.exp(s1 - mx), 0.0)
        e2 = jnp.where(m_new, jnp.exp(s2 - mx), 0.0)
        den = jnp.maximum(jnp.sum(e1, axis=1, keepdims=True) + jnp.sum(e2, axis=1, keepdims=True), 1e-30)
        o = (jnp.dot(e1.astype(_BF16), v_old, preferred_element_type=_F32)
             + jnp.dot(e2, v_new, preferred_element_type=_F32)) / den
        _merge_heads(owin_ref, o, g, Ts)


def _nsa_sample_front(q, kc, vc, win_old, win_new, past_len):
    B, Ts, _ = q.shape
    n_seg = kc.shape[1]
    n_cmp = (past_len + Ts) // CMP_STRIDE - 1
    assert n_cmp == n_seg - 1 and Ts % 8 == 0 and Ts <= SLC_BLK
    n_slc = past_len // SLC_BLK + 1
    NL = -(-n_slc // 128) * 128
    topn = min(SLC_TOPN, n_slc)
    wb = win_old.shape[1]
    R = A_GROUP * Ts
    b3 = lambda r, w: pl.BlockSpec((None, r, w), lambda b: (b, 0, 0))
    return pl.pallas_call(
        functools.partial(_nsa_sample_front_kernel, Ts, past_len, n_cmp, n_slc, NL, topn),
        grid=(B,),
        in_specs=[b3(Ts, 1024), b3(n_seg, 256), b3(n_seg, 256), b3(wb, 512), b3(Ts, 512)],
        out_specs=[b3(Ts, 1024), b3(Ts, 1024), b3(A_KV * Ts, 128)],
        out_shape=[jax.ShapeDtypeStruct((B, Ts, 1024), _F32), jax.ShapeDtypeStruct((B, Ts, 1024), _F32),
                   jax.ShapeDtypeStruct((B, A_KV * Ts, 128), jnp.int32)],
        scratch_shapes=[pltpu.VMEM((A_KV, R, 256), _BF16)],
        compiler_params=pltpu.CompilerParams(
            dimension_semantics=("arbitrary",), vmem_limit_bytes=_VMEM_LIMIT),
        name="nsa_sample_front",
    )(q, kc, vc, win_old, win_new)


_GATHER_SLOTS = 3


def _nsa_sample_gather_kernel(Ts, n_past_blk, past_len, topn,
                              idx_ref, pt_ref, q_ref, kvn_ref, cache_ref, o_ref, kvbuf, sem):
    b = pl.program_id(0)
    n_it = A_KV * Ts
    bpp = PAGE_SIZE // SLC_BLK
    nk = topn * PAGE_SIZE

    def issue(it, sl):
        g = it // Ts
        for k in range(topn):
            blk = jnp.minimum(idx_ref[b, it * topn + k], n_past_blk - 1)
            page = pt_ref[b, blk // bpp]
            pltpu.make_async_copy(cache_ref.at[page, pl.ds(2, 2), g], kvbuf.at[sl, k], sem.at[sl]).start()

    for i in range(_GATHER_SLOTS - 1):
        issue(i, i)
    lane = lax.broadcasted_iota(jnp.int32, (1, nk), 1)
    new_r = lax.broadcasted_iota(jnp.int32, (1, Ts), 1)

    def body(it, carry):
        sl = it % _GATHER_SLOTS

        @pl.when(it + _GATHER_SLOTS - 1 < n_it)
        def _():
            issue(it + _GATHER_SLOTS - 1, (it + _GATHER_SLOTS - 1) % _GATHER_SLOTS)

        pltpu.make_async_copy(cache_ref.at[pl.ds(0, topn), pl.ds(0, 2), 0], kvbuf.at[sl], sem.at[sl]).wait()

        t = it % Ts
        g = it // Ts
        tq = past_len + t
        pos = jnp.zeros((1, nk), jnp.int32)
        past = jnp.zeros((1, nk), jnp.int32)
        new_sel = jnp.zeros((1, Ts), jnp.int32)
        for k in range(topn):
            blk = idx_ref[b, it * topn + k]
            in_slot = (lane // PAGE_SIZE) == k
            row = lane % PAGE_SIZE
            hit = in_slot & ((row // SLC_BLK) == (blk % bpp))
            pos = jnp.where(in_slot, (blk // bpp) * PAGE_SIZE + row, pos)
            past = jnp.where(hit, jnp.where(blk < n_past_blk, 1, 0), past)
            new_sel = jnp.maximum(new_sel, jnp.where(blk == n_past_blk, 1, 0))
        m1 = (past > 0) & (pos <= tq)
        m2 = (new_sel > 0) & (n_past_blk * SLC_BLK + new_r <= tq)
        q8 = q_ref[it] * 0.125
        kt = jnp.concatenate([kvbuf[sl, k, 0] for k in range(topn)], axis=1).astype(_BF16)
        vt = jnp.concatenate([kvbuf[sl, k, 1] for k in range(topn)], axis=1).astype(_BF16)
        s1 = jnp.dot(q8.astype(_BF16), kt, preferred_element_type=_F32)
        s2 = lax.dot_general(q8, kvn_ref[2 * A_KV + g], (((1,), (1,)), ((), ())),
                             preferred_element_type=_F32)
        s1 = jnp.where(m1, s1, -jnp.inf)
        s2 = jnp.where(m2, s2, -jnp.inf)
        mx = jnp.maximum(jnp.max(s1, axis=1, keepdims=True), jnp.max(s2, axis=1, keepdims=True))
        mx = jnp.where(mx == -jnp.inf, 0.0, mx)
        e1 = jnp.where(m1, jnp.exp(s1 - mx), 0.0)
        e2 = jnp.where(m2, jnp.exp(s2 - mx), 0.0)
        den = jnp.maximum(jnp.sum(e1, axis=1, keepdims=True) + jnp.sum(e2, axis=1, keepdims=True), 1e-30)
        o_ref[it] = (lax.dot_general(e1.astype(_BF16), vt, (((1,), (1,)), ((), ())),
                                     preferred_element_type=_F32)
                     + jnp.dot(e2, kvn_ref[3 * A_KV + g], preferred_element_type=_F32)) / den
        return carry

    lax.fori_loop(0, n_it, body, 0)


def _nsa_sample_gather(idx, page_table, q_rows, kv_new, cache4, past_len):
    B, n_it, _, dh = q_rows.shape
    Ts = n_it // A_KV
    n_past_blk = past_len // SLC_BLK
    topn = idx.shape[1] // n_it
    grid_spec = pltpu.PrefetchScalarGridSpec(
        num_scalar_prefetch=2,
        grid=(B,),
        in_specs=[pl.BlockSpec((None, n_it, 8, dh), lambda b, i, p: (b, 0, 0, 0)),
                  pl.BlockSpec((None, 4 * A_KV, Ts, dh), lambda b, i, p: (b, 0, 0, 0)),
                  pl.BlockSpec(memory_space=pl.ANY)],
        out_specs=pl.BlockSpec((None, n_it, 8, dh), lambda b, i, p: (b, 0, 0, 0)),
        scratch_shapes=[
            pltpu.VMEM((_GATHER_SLOTS, topn, 2, dh, PAGE_SIZE), _F32),
            pltpu.SemaphoreType.DMA((_GATHER_SLOTS,)),
        ])
    return pl.pallas_call(
        functools.partial(_nsa_sample_gather_kernel, Ts, n_past_blk, past_len, topn),
        grid_spec=grid_spec,
        out_shape=jax.ShapeDtypeStruct((B, n_it, 8, dh), _F32),
        compiler_params=pltpu.CompilerParams(
            dimension_semantics=("arbitrary",), vmem_limit_bytes=_VMEM_LIMIT),
        name="nsa_sample_gather",
    )(idx, page_table, q_rows, kv_new, cache4)


def _nsa_combine_kernel(ocmp_ref, osel_ref, owin_ref, sm_ref, az_ref, o_ref):
    sm = sm_ref[...]
    shape = ocmp_ref.shape
    head_lane = lax.broadcasted_iota(jnp.int32, shape, 1) // 64
    g0 = 2 * M_HEADS
    acc = jnp.zeros(shape, _F32)
    for br, ref in enumerate((ocmp_ref, osel_ref, owin_ref)):
        gate = jnp.zeros(shape, _F32)
        for a in range(A_GROUP):
            for g in range(A_KV):
                c = g0 + br * A_HEADS + g * A_GROUP + a
                gate = jnp.where(head_lane == a * A_KV + g, sm[:, c:c + 1], gate)
        acc = acc + gate * ref[...]
    az = az_ref[...]
    o_ref[...] = acc * (az * jax.nn.sigmoid(az))


def _nsa_combine(ocmp, osel, owin, sm, az):
    M, W = ocmp.shape
    full = lambda w: pl.BlockSpec((M, w), lambda i: (0, 0))
    return pl.pallas_call(
        _nsa_combine_kernel,
        grid=(1,),
        in_specs=[full(W), full(W), full(W), full(128), full(W)],
        out_specs=full(W),
        out_shape=jax.ShapeDtypeStruct((M, W), _F32),
        name="nsa_combine",
    )(ocmp, osel, owin, sm, az)


def _mixer_inputs(x, norm_w, pw, m_width, a_width, a_dh):
    B, T, D = x.shape
    w_main, w_small, seg, hw, hm, groups, norm_tiles = pw
    M = B * T
    tm = next((t for t in (1024, 512) if M % t == 0), M)
    a, q, az, kv4, kvw, sm = _proj_in(x.reshape(M, D), norm_w, w_main, w_small, seg, hw, hm,
                                      groups, norm_tiles, tm)
    r3 = lambda v: v.reshape(B, T, v.shape[-1])
    return r3(a), r3(sm), r3(q), r3(kv4), r3(kvw), r3(az)


def _mixer_output(x, m_out, a_out, p, ow):
    B, T, D = x.shape
    M = B * T
    tm = 256 if M % 256 == 0 else M
    y = _out_proj(x.reshape(M, D), m_out.reshape(M, -1), a_out.reshape(M, -1), p.reshape(M, -1), *ow, tm)
    return y.reshape(B, T, D)


def kernel(x_prompt, x_sample, cache_nsa_kv, cache_win_kv, state_mlstm_C, state_mlstm_n,
           state_mlstm_m, state_mlstm_conv, page_table, p_prompt, p_sample, norm_w, w_in,
           m_conv_w, m_conv_b, m_wq, m_wk, m_wv, m_b_i, m_b_f, m_norm_w, a_q_norm, a_k_norm,
           cmp_pe, cmp_w1, cmp_w2, w_out, ple_proj, ple_norm, ple_gate):
    xp, xs = x_prompt, x_sample
    B, T, D = xp.shape
    depth = w_in.shape[0]
    m_width = m_conv_w.shape[-1]
    a_width = D - m_width
    a_dh = a_q_norm.shape[-1]
    dh = m_width // M_HEADS
    per_layer = []
    for i in range(depth):
        mw = (m_conv_w[i], m_conv_b[i], m_wq[i], m_wk[i], m_wv[i], m_b_i[i], m_b_f[i], m_norm_w[i])
        cw = (cmp_pe[i], cmp_w1[i], cmp_w2[i], a_k_norm[i, 0])
        w_out_i = jnp.concatenate(
            [w_out[i][:m_width], _heads_to_agd(w_out[i][m_width:].T, a_dh).T], axis=0)
        ow = (w_out_i, ple_proj[i], ple_norm[i], ple_gate[i])
        pw = _prep_proj_weights(w_in[i], a_q_norm[i], a_k_norm[i], m_width, a_width, a_dh)
        cwp = _prep_cmp_weights(*cw)
        a3, sm, q, kv4, kvw, az = _mixer_inputs(xp, norm_w[i], pw, m_width, a_width, a_dh)
        L = _MLSTM_L if T % _MLSTM_L == 0 else (M_CHUNK if T % M_CHUNK == 0 else T)
        m_out, conv_p, C_p, n_p, m_p = _mlstm(
            a3, sm, jnp.zeros((B, M_CONV - 1, m_width), _F32),
            jnp.zeros((B, M_HEADS, dh, dh), _F32), jnp.zeros((B, M_HEADS, dh), _F32),
            jnp.full((B, M_HEADS), -jnp.inf, _F32), *mw, L)
        kc, vc = _compress_prompt(kv4, cwp)
        a_out = _nsa_prompt_t(q, kv4, kvw, kc, vc, sm, az)
        xp_new = _mixer_output(xp, m_out, a_out, p_prompt[i], ow)
        kv_p = kv4.reshape(B, T, 4, A_KV, a_dh)
        win_p = kvw[:, -min(WINDOW, T):].reshape(B, -1, 2, A_KV, a_dh)
        a3, sm, q, kv4, kvw, az = _mixer_inputs(xs, norm_w[i], pw, m_width, a_width, a_dh)
        Bd, Ts = xs.shape[:2]
        Ls = _MLSTM_L if Ts % _MLSTM_L == 0 else (M_CHUNK if Ts % M_CHUNK == 0 else Ts)
        m_out, conv_s, C_s, n_s, m_s = _mlstm(
            a3, sm, state_mlstm_conv[i], state_mlstm_C[i], state_mlstm_n[i], state_mlstm_m[i], *mw, Ls)
        past_len = page_table.shape[1] * PAGE_SIZE
        cache_l = cache_nsa_kv[i]
        n_pool = cache_l.shape[0]
        wb = cache_win_kv.shape[2]
        cache_t = jnp.transpose(cache_l, (0, 2, 3, 4, 1))
        kc, vc = _compress_sample(cache_t.reshape(n_pool, 8, 2 * a_dh, PAGE_SIZE), page_table, cwp)
        ocmp, owin, idx = _nsa_sample_front(
            q, kc, vc, cache_win_kv[i].reshape(Bd, wb, 2 * A_KV * a_dh), kvw, past_len)
        topn = min(SLC_TOPN, past_len // SLC_BLK + 1)
        q_rows = q.reshape(Bd, Ts, A_GROUP, A_KV, a_dh).transpose(0, 3, 1, 2, 4)
        q_rows = jnp.pad(q_rows, ((0, 0), (0, 0), (0, 0), (0, 8 - A_GROUP), (0, 0)))
        kv_new = kv4.reshape(Bd, Ts, 4 * A_KV, a_dh).transpose(0, 2, 1, 3)
        osel = _nsa_sample_gather(
            idx[:, :, :topn].reshape(Bd, -1), page_table, q_rows.reshape(Bd, A_KV * Ts, 8, a_dh), kv_new,
            cache_t, past_len)
        osel = osel.reshape(Bd, A_KV, Ts, 8, a_dh)[:, :, :, :A_GROUP].transpose(0, 2, 3, 1, 4)
        Ms = Bd * Ts
        a_out = _nsa_combine(ocmp.reshape(Ms, -1), osel.reshape(Ms, -1), owin.reshape(Ms, -1),
                             sm.reshape(Ms, -1), az.reshape(Ms, -1)).reshape(Bd, Ts, -1)
        xs_new = _mixer_output(xs, m_out, a_out, p_sample[i], ow)
        kv_s = kv4.reshape(Bd, Ts, 4, A_KV, a_dh)
        win_s = jnp.concatenate(
            [cache_win_kv[i], kvw.reshape(Bd, Ts, 2, A_KV, a_dh)], axis=1)[:, -wb:]
        per_layer.append((kv_p, kv_s, win_p, win_s, C_p, n_p, m_p, conv_p, C_s, n_s, m_s, conv_s))
        xp, xs = xp_new, xs_new
    (kv_p, kv_s, win_p, win_s, C_p, n_p, m_p, conv_p,
     C_s, n_s, m_s, conv_s) = [jnp.stack(a, axis=0) for a in zip(*per_layer)]
    return (xp, xs, kv_p, kv_s, win_p, win_s, C_p, n_p, m_p, conv_p, C_s, n_s, m_s, conv_s)
```

```python
import functools
import math

import jax
import jax.numpy as jnp
from jax import lax
from jax.experimental import pallas as pl
from jax.experimental.pallas import tpu as pltpu

NORM_EPS = 1e-6
M_HEADS = 4
M_CONV = 4
M_CHUNK = 64
A_HEADS = 16
A_KV = 4
A_GROUP = A_HEADS // A_KV
CMP_STRIDE = 16
CMP_LEN = 2 * CMP_STRIDE
SLC_BLK = 64
SLC_TOPN = 16
WINDOW = 512
PAGE_SIZE = 128

_BF16 = jnp.bfloat16
_F32 = jnp.float32
_VMEM_LIMIT = 56 * 1024 * 1024


def _dot(a, b):
    return jnp.dot(a.astype(_BF16), b.astype(_BF16), preferred_element_type=_F32)


_PT = 512


def _proj_in_kernel(groups, norm_tiles, x_ref, nw_ref, w_ref, wsm_ref, seg_ref, hw_ref, hm_ref,
                    a_ref, q_ref, az_ref, kv4_ref, kvw_ref, sm_ref, xn_ref):
    j = pl.program_id(1)

    @pl.when(j == 0)
    def _():
        x = x_ref[...]
        ms = jnp.mean(x * x, axis=-1, keepdims=True)
        xn = x * lax.rsqrt(ms + NORM_EPS) * nw_ref[...]
        xn_ref[...] = xn.astype(_BF16)
        sm = jnp.dot(xn_ref[...], wsm_ref[...], preferred_element_type=_F32)
        lane = lax.broadcasted_iota(jnp.int32, sm.shape, 1)
        is_gate = (lane >= 2 * M_HEADS) & (lane < 2 * M_HEADS + 3 * A_HEADS)
        sm_ref[...] = jnp.where(is_gate, jax.nn.sigmoid(sm), sm)

    y = jnp.dot(xn_ref[...], w_ref[...], preferred_element_type=_F32)
    outs = (a_ref, q_ref, az_ref, kv4_ref, kvw_ref)
    for (start, n), o_ref in zip(groups, outs):
        plain = [t for t in range(start, start + n) if t not in norm_tiles]
        normed = [t for t in range(start, start + n) if t in norm_tiles]
        if plain:
            cond = functools.reduce(jnp.logical_or, [j == t for t in plain])

            @pl.when(cond)
            def _(o_ref=o_ref):
                o_ref[...] = y

        if normed:
            cond = functools.reduce(jnp.logical_or, [j == t for t in normed])

            @pl.when(cond)
            def _(o_ref=o_ref):
                ms = jnp.dot((y * y).astype(_BF16), seg_ref[...], preferred_element_type=_F32)
                yn = y * lax.rsqrt(ms + NORM_EPS) * hw_ref[...]
                o_ref[...] = jnp.where(hm_ref[...] > 0, yn, y)


def _proj_in(x2d, norm_w, w_main, w_small, seg, hw, hm, groups, norm_tiles, tm):
    M, D = x2d.shape
    n_tiles = w_main.shape[1] // _PT
    widths = [n * _PT for _, n in groups]

    def out_map(start, n):
        return lambda i, j: (i, jnp.clip(j - start, 0, n - 1))

    out_specs = [pl.BlockSpec((tm, _PT), out_map(s, n)) for s, n in groups]
    out_specs.append(pl.BlockSpec((tm, 128), lambda i, j: (i, 0)))
    out_shape = [jax.ShapeDtypeStruct((M, w), _F32) for w in widths]
    out_shape.append(jax.ShapeDtypeStruct((M, 128), _F32))
    return pl.pallas_call(
        functools.partial(_proj_in_kernel, groups, norm_tiles),
        grid=(M // tm, n_tiles),
        in_specs=[
            pl.BlockSpec((tm, D), lambda i, j: (i, 0)),
            pl.BlockSpec((1, D), lambda i, j: (0, 0)),
            pl.BlockSpec((D, _PT), lambda i, j: (0, j)),
            pl.BlockSpec((D, 128), lambda i, j: (0, 0)),
            pl.BlockSpec((_PT, _PT), lambda i, j: (0, 0)),
            pl.BlockSpec((None, 1, _PT), lambda i, j: (j, 0, 0)),
            pl.BlockSpec((None, 1, _PT), lambda i, j: (j, 0, 0)),
        ],
        out_specs=out_specs,
        out_shape=out_shape,
        scratch_shapes=[pltpu.VMEM((tm, D), _BF16)],
        compiler_params=pltpu.CompilerParams(
            dimension_semantics=("arbitrary", "arbitrary"), vmem_limit_bytes=_VMEM_LIMIT),
        name="proj_in",
    )(x2d, norm_w.reshape(1, D), w_main, w_small, seg, hw, hm)


def _heads_to_agd(w, a_dh):
    lead = w.shape[:-1]
    return jnp.swapaxes(w.reshape(lead + (A_KV, A_GROUP, a_dh)), -3, -2).reshape(w.shape)


def _heads_to_gad(w, a_dh):
    lead = w.shape[:-1]
    return jnp.swapaxes(w.reshape(lead + (A_GROUP, A_KV, a_dh)), -3, -2).reshape(w.shape)


def _prep_proj_weights(w_in, a_q_norm, a_k_norm, m_width, a_width, a_dh):
    kvw = A_KV * a_dh
    sizes = (m_width, m_width, m_width, M_HEADS, M_HEADS, a_width, 6 * kvw, 3 * A_HEADS, a_width)
    offs = [0]
    for s in sizes:
        offs.append(offs[-1] + s)
    u0, o0, z0, i0, f0, q0, kv0, g0, az0, end = offs
    assert end == w_in.shape[1]
    w_main = jnp.concatenate([
        w_in[:, u0:i0], _heads_to_agd(w_in[:, q0:kv0], a_dh), _heads_to_agd(w_in[:, az0:end], a_dh),
        w_in[:, kv0:kv0 + 4 * kvw], w_in[:, kv0 + 4 * kvw:g0]], axis=1).astype(_BF16)
    n_small = 2 * M_HEADS + 3 * A_HEADS
    w_small = jnp.concatenate([
        w_in[:, i0:q0], w_in[:, g0:az0],
        jnp.zeros((w_in.shape[0], 128 - n_small), w_in.dtype)], axis=1).astype(_BF16)
    widths = (3 * m_width, a_width, a_width, 4 * kvw, 2 * kvw)
    groups, start = [], 0
    for w in widths:
        assert w % _PT == 0
        groups.append((start, w // _PT))
        start += w // _PT
    n_tiles = start
    hw = jnp.zeros((n_tiles * _PT,), _F32)
    hm = jnp.zeros((n_tiles * _PT,), _F32)
    qs = groups[1][0] * _PT
    hw = hw.at[qs:qs + a_width].set(jnp.tile(a_q_norm, a_width // a_dh))
    hm = hm.at[qs:qs + a_width].set(1.0)
    k4 = groups[3][0] * _PT
    hw = hw.at[k4 + 2 * kvw:k4 + 3 * kvw].set(jnp.tile(a_k_norm[1], A_KV))
    hm = hm.at[k4 + 2 * kvw:k4 + 3 * kvw].set(1.0)
    kw = groups[4][0] * _PT
    hw = hw.at[kw:kw + kvw].set(jnp.tile(a_k_norm[2], A_KV))
    hm = hm.at[kw:kw + kvw].set(1.0)
    assert kvw * 2 == _PT and _PT % a_dh == 0
    norm_tiles = tuple(range(groups[1][0], groups[1][0] + groups[1][1])) + (
        (k4 + 2 * kvw) // _PT, kw // _PT)
    r = jnp.arange(_PT) // a_dh
    seg = ((r[:, None] == r[None, :]).astype(_F32) / a_dh).astype(_BF16)
    return (w_main, w_small, seg, hw.reshape(n_tiles, 1, _PT), hm.reshape(n_tiles, 1, _PT),
            tuple(groups), norm_tiles)


def _out_proj_kernel(mw, x_ref, m_ref, a_ref, p_ref, wo_ref, pp_ref, pn_ref, pg_ref, o_ref):
    h = x_ref[...] + _dot(m_ref[...], wo_ref[0:mw, :]) + _dot(a_ref[...], wo_ref[mw:, :])
    ms = jnp.mean(h * h, axis=-1, keepdims=True)
    hn = h * lax.rsqrt(ms + NORM_EPS) * pn_ref[...]
    gate = jax.nn.sigmoid(_dot(hn, pg_ref[...]))
    o_ref[...] = h + gate * _dot(p_ref[...], pp_ref[...])


def _out_proj(x2d, m_out, a_out, p2d, w_out, ple_proj, ple_norm, ple_gate, tm):
    M, D = x2d.shape
    mw, aw, dp = m_out.shape[1], a_out.shape[1], p2d.shape[1]

    def const(shape):
        return pl.BlockSpec(shape, lambda i: (0, 0), pipeline_mode=pl.Buffered(1))

    return pl.pallas_call(
        functools.partial(_out_proj_kernel, mw),
        grid=(M // tm,),
        in_specs=[
            pl.BlockSpec((tm, D), lambda i: (i, 0)),
            pl.BlockSpec((tm, mw), lambda i: (i, 0)),
            pl.BlockSpec((tm, aw), lambda i: (i, 0)),
            pl.BlockSpec((tm, dp), lambda i: (i, 0)),
            const((mw + aw, D)),
            const((dp, D)),
            const((1, D)),
            const((D, D)),
        ],
        out_specs=pl.BlockSpec((tm, D), lambda i: (i, 0)),
        out_shape=jax.ShapeDtypeStruct((M, D), _F32),
        compiler_params=pltpu.CompilerParams(
            dimension_semantics=("arbitrary",), vmem_limit_bytes=_VMEM_LIMIT),
        name="out_proj",
    )(x2d, m_out, a_out, p2d, w_out.astype(_BF16), ple_proj.astype(_BF16),
      ple_norm.reshape(1, D), ple_gate.astype(_BF16))


_MLSTM_L = 256
_CONV_PAD = 8


def _mlstm_kernel(L, dh, n_steps,
                  u_ref, o_ref, z_ref, sm_ref, cbuf_ref, c0_ref, n0_ref, m0_ref,
                  wq_ref, wk_ref, wv_ref, cw_ref, cb_ref, bi_ref, bf_ref, nw_ref,
                  out_ref, cout_ref, nout_ref, mout_ref, convout_ref,
                  xbuf, c_s, n_s, m_s):
    h = pl.program_id(1)
    s = pl.program_id(2)
    tail = M_CONV - 1

    @pl.when(s == 0)
    def _():
        c_s[...] = c0_ref[...]
        n_s[...] = n0_ref[...]
        m_s[...] = m0_ref[...]
        xbuf[_CONV_PAD - tail:_CONV_PAD, :] = cbuf_ref[...]

    u = u_ref[...]
    xbuf[_CONV_PAD:_CONV_PAD + L, :] = u
    c = cb_ref[...]
    for j in range(M_CONV):
        c = c + xbuf[_CONV_PAD - tail + j:_CONV_PAD - tail + j + L, :] * cw_ref[j:j + 1, :]
    xbuf[_CONV_PAD - tail:_CONV_PAD, :] = u[L - tail:L, :]
    ch = c * jax.nn.sigmoid(c)
    q = _dot(ch, wq_ref[...])
    k = _dot(ch, wk_ref[...]) * (dh ** -0.5)
    v = _dot(u, wv_ref[...])

    sm = sm_ref[...]
    lane = lax.broadcasted_iota(jnp.int32, sm.shape, 1)
    i_pre = jnp.sum(jnp.where(lane == h, sm, 0.0), axis=1, keepdims=True)
    f_pre = jnp.sum(jnp.where(lane == h + M_HEADS, sm, 0.0), axis=1, keepdims=True)
    li_col = i_pre + bi_ref[:, 0:1]
    f_in = f_pre + bf_ref[:, 0:1]
    lf_col = jnp.minimum(f_in, 0.0) - jnp.log(1.0 + jnp.exp(-jnp.abs(f_in)))

    ii = lax.broadcasted_iota(jnp.int32, (L, L), 0)
    jj = lax.broadcasted_iota(jnp.int32, (L, L), 1)
    eye = ii == jj
    causal = jj <= ii
    lf_row = jnp.sum(jnp.where(eye, lf_col, 0.0), axis=0, keepdims=True)
    li_row = jnp.sum(jnp.where(eye, li_col, 0.0), axis=0, keepdims=True)
    b_col = jnp.sum(jnp.where(causal, lf_row, 0.0), axis=1, keepdims=True)
    b_row = jnp.sum(jnp.where(ii <= jj, lf_col, 0.0), axis=0, keepdims=True)

    m_prev = m_s[0:1, 0:1]
    c_prev = c_s[...]
    n_prev = n_s[...]
    logd = jnp.where(causal, b_col - b_row + li_row, -jnp.inf)
    inter = b_col + m_prev
    m_t = jnp.maximum(inter, jnp.max(logd, axis=1, keepdims=True))
    qb = q.astype(_BF16)
    sc = lax.dot_general(qb, k.astype(_BF16), (((1,), (1,)), ((), ())),
                         preferred_element_type=_F32) * jnp.exp(logd - m_t)
    a_col = jnp.exp(inter - m_t)
    num = _dot(sc, v) + a_col * _dot(qb, c_prev)
    den = jnp.sum(sc, axis=1, keepdims=True) + a_col * jnp.sum(q * n_prev, axis=1, keepdims=True)
    hh = num / jnp.maximum(jnp.abs(den), jnp.exp(-m_t))

    b_end = b_col[L - 1:L, :]
    logw = b_end - b_col + li_col
    m_new = jnp.maximum(b_end + m_prev, jnp.max(logw, axis=0, keepdims=True))
    w_col = jnp.exp(logw - m_new)
    decay = jnp.exp(b_end + m_prev - m_new)
    c_s[...] = decay * c_prev + lax.dot_general(
        k.astype(_BF16), (w_col * v).astype(_BF16), (((0,), (0,)), ((), ())),
        preferred_element_type=_F32)
    n_s[...] = decay * n_prev + jnp.sum(w_col * k, axis=0, keepdims=True)
    m_s[...] = jnp.broadcast_to(m_new, m_s.shape)

    hg = hh * jax.nn.sigmoid(o_ref[...])
    hn = hg * lax.rsqrt(jnp.mean(hg * hg, axis=-1, keepdims=True) + NORM_EPS) * nw_ref[...]
    zz = z_ref[...]
    out_ref[...] = hn * (zz * jax.nn.sigmoid(zz))

    @pl.when(s == n_steps - 1)
    def _():
        cout_ref[...] = c_s[...]
        nout_ref[...] = n_s[...]
        mout_ref[...] = m_s[...]
        convout_ref[...] = u[L - tail:L, :]


def _mlstm(a3, sm, conv_buf, C0, n0, m0, conv_w, conv_b, wq, wk, wv, b_i, b_f, norm_w, L):
    B, T, mw3 = a3.shape
    mw = mw3 // 3
    dh = mw // M_HEADS
    tail = M_CONV - 1
    assert T % L == 0 and L >= tail and (L % 8 == 0)
    n_steps = T // L
    H = M_HEADS

    def lane_b(vec):
        return jnp.broadcast_to(vec.astype(_F32)[:, None, None], (H, 1, 128))

    m0b = jnp.broadcast_to(m0.astype(_F32)[:, :, None, None], (B, H, 1, 128))
    blk = lambda off: pl.BlockSpec((None, L, dh), lambda b, h, s: (b, s, off * H + h))
    per_h = lambda r, c: pl.BlockSpec((None, r, c), lambda b, h, s: (h, 0, 0))
    st = lambda r, c: pl.BlockSpec((None, None, r, c), lambda b, h, s: (b, h, 0, 0))
    out, C, n, m, conv = pl.pallas_call(
        functools.partial(_mlstm_kernel, L, dh, n_steps),
        grid=(B, H, n_steps),
        in_specs=[
            blk(0), blk(1), blk(2),
            pl.BlockSpec((None, L, 128), lambda b, h, s: (b, s, 0)),
            pl.BlockSpec((None, tail, dh), lambda b, h, s: (b, 0, h)),
            st(dh, dh), st(1, dh), st(1, 128),
            per_h(dh, dh), per_h(dh, dh), per_h(dh, dh),
            pl.BlockSpec((M_CONV, dh), lambda b, h, s: (0, h)),
            pl.BlockSpec((1, dh), lambda b, h, s: (0, h)),
            per_h(1, 128), per_h(1, 128), per_h(1, dh),
        ],
        out_specs=[
            pl.BlockSpec((None, L, dh), lambda b, h, s: (b, s, h)),
            st(dh, dh), st(1, dh), st(1, 128),
            pl.BlockSpec((None, tail, dh), lambda b, h, s: (b, 0, h)),
        ],
        out_shape=[
            jax.ShapeDtypeStruct((B, T, mw), _F32),
            jax.ShapeDtypeStruct((B, H, dh, dh), _F32),
            jax.ShapeDtypeStruct((B, H, 1, dh), _F32),
            jax.ShapeDtypeStruct((B, H, 1, 128), _F32),
            jax.ShapeDtypeStruct((B, tail, mw), _F32),
        ],
        scratch_shapes=[
            pltpu.VMEM((_CONV_PAD + L, dh), _F32),
            pltpu.VMEM((dh, dh), _F32),
            pltpu.VMEM((1, dh), _F32),
            pltpu.VMEM((1, 128), _F32),
        ],
        compiler_params=pltpu.CompilerParams(
            dimension_semantics=("arbitrary", "arbitrary", "arbitrary"), vmem_limit_bytes=_VMEM_LIMIT),
        name="mlstm",
    )(a3, a3, a3, sm, conv_buf, C0, n0.reshape(B, H, 1, dh), m0b,
      wq.astype(_BF16), wk.astype(_BF16), wv.astype(_BF16), conv_w, conv_b.reshape(1, mw),
      lane_b(b_i), lane_b(b_f), norm_w.reshape(H, 1, dh))
    return out, conv, C, n.reshape(B, H, dh), m[:, :, 0, 0]


def _prep_cmp_weights(pe, w1, w2, k_norm):
    a_dh, hid = w1.shape[-2], w1.shape[-1]
    w1r = w1.reshape(2, 2, CMP_STRIDE, a_dh, hid)
    eye2 = jnp.eye(2, dtype=w1.dtype)
    w1p = jnp.einsum('cxjdh,ab->cjadxbh', w1r, eye2).reshape(2, CMP_STRIDE * 2 * a_dh, 2 * 2 * hid)
    bias = jnp.einsum('cxjd,cxjdh->ch', pe.reshape(2, 2, CMP_STRIDE, a_dh), w1r)
    bias2 = jnp.tile(bias, (1, 2)).reshape(2, 1, 2 * hid)
    w2p = jnp.einsum('chd,ab->cahbd', w2, eye2).reshape(2, 2 * hid, 2 * a_dh)
    r = jnp.arange(2 * a_dh) // a_dh
    seg = ((r[:, None] == r[None, :]).astype(_F32) / a_dh).astype(_BF16)
    kn = jnp.tile(k_norm, 2).reshape(1, 2 * a_dh)
    return w1p.astype(_BF16), bias2.astype(_F32), w2p.astype(_BF16), seg, kn.astype(_F32)


def _cmp_first_layer(rows_ref, n_seg, w1_ref, c):
    x = jnp.concatenate(
        [rows_ref[pl.ds(j, n_seg, stride=CMP_STRIDE), :].astype(_BF16)
         for j in range(CMP_STRIDE)], axis=1)
    return jnp.dot(x, w1_ref[c], preferred_element_type=_F32)


def _cmp_second_layer(p, n_seg, c, b_ref, w2_ref, seg_ref, kn_ref):
    hid = p[:, 0:128] + pltpu.roll(p[:, 128:256], n_seg - 1, axis=0) + b_ref[c]
    act = hid * jax.nn.sigmoid(hid)
    out = jnp.dot(act.astype(_BF16), w2_ref[c], preferred_element_type=_F32)
    if c == 0:
        ms = jnp.dot((out * out).astype(_BF16), seg_ref[...], preferred_element_type=_F32)
        out = out * lax.rsqrt(ms + NORM_EPS) * kn_ref[...]
    return out


def _compress_prompt_kernel(n_seg, r00, r01, r10, r11, w1_ref, b_ref, w2_ref, seg_ref, kn_ref,
                            kc_ref, vc_ref):
    for c, o_ref, refs in ((0, kc_ref, (r00, r01)), (1, vc_ref, (r10, r11))):
        for gp in range(2):
            p = _cmp_first_layer(refs[gp], n_seg, w1_ref, c)
            o_ref[:, gp * 128:(gp + 1) * 128] = _cmp_second_layer(p, n_seg, c, b_ref, w2_ref, seg_ref, kn_ref)


def _compress_prompt(kv4, cwp):
    B, T, W = kv4.shape
    assert W == 1024 and T % (CMP_STRIDE * 8) == 0
    n_seg = T // CMP_STRIDE
    w1p, bias2, w2p, seg, kn = cwp
    full = lambda a: pl.BlockSpec(a.shape, lambda b: (0,) * a.ndim)
    return pl.pallas_call(
        functools.partial(_compress_prompt_kernel, n_seg),
        grid=(B,),
        in_specs=[pl.BlockSpec((None, T, 128), functools.partial(lambda k, b: (b, 0, k), k))
                  for k in range(4)] + [full(w1p), full(bias2), full(w2p), full(seg), full(kn)],
        out_specs=[pl.BlockSpec((None, n_seg, 256), lambda b: (b, 0, 0))] * 2,
        out_shape=[jax.ShapeDtypeStruct((B, n_seg, 256), _F32)] * 2,
        compiler_params=pltpu.CompilerParams(
            dimension_semantics=("arbitrary",), vmem_limit_bytes=_VMEM_LIMIT),
        name="compress_prompt",
    )(kv4, kv4, kv4, kv4, w1p, bias2, w2p, seg, kn)


_QB = 256
_SEL_TK = 512
_GW = 32
_LOG2E = 1.4426950408889634


def _softmax_parts(s):
    m = jnp.max(s, axis=-1, keepdims=True)
    m = jnp.where(m == -jnp.inf, 0.0, m)
    e = jnp.exp(s - m)
    return e, 1.0 / jnp.maximum(jnp.sum(e, axis=-1, keepdims=True), 1e-30)


def _softmax_rows(s, mask):
    e, inv = _softmax_parts(jnp.where(mask, s, -jnp.inf))
    return e * inv


def _col_softmax_parts(s):
    m = jnp.max(s, axis=0, keepdims=True)
    m = jnp.where(m == -jnp.inf, 0.0, m)
    e = jnp.exp2(s - m)
    return e, 1.0 / jnp.maximum(jnp.sum(e, axis=0, keepdims=True), 1e-30)


def _tile_cols(x, n):
    return jnp.concatenate([x] * n, axis=1)


def _nsa_prompt_t_kernel(T, n_cmp, n_slc,
                         q_ref, ks_ref, kw_ref, kc_ref, vc_ref, sm_ref, az_ref, out_ref,
                         ksb, vst, kwb, vwt, vct, *scratch):
    QB, TK, G, A = _QB, _SEL_TK, A_KV, A_GROUP
    R = A * QB
    NB = T // 128
    qplt_ref, sel_ref, m_ref, l_ref, acc_ref, cmp_ref, win_ref = (
        scratch[i * G:(i + 1) * G] for i in range(7))
    DH = 256 // G

    @pl.when(pl.program_id(1) == 0)
    def _():
        for blk in range(NB):
            rows = slice(blk * 128, (blk + 1) * 128)
            ksb[rows, :] = ks_ref[rows, 0:256].astype(_BF16)
            vst[blk] = ks_ref[rows, 256:512].T.astype(_BF16)
            kwb[rows, :] = kw_ref[rows, 0:256].astype(_BF16)
            vwt[blk] = kw_ref[rows, 256:512].T.astype(_BF16)
        vct[...] = vc_ref[...].T.astype(_BF16)

    t0 = pl.program_id(1) * QB
    sub256 = lax.broadcasted_iota(jnp.int32, (256, QB), 0)
    qt = q_ref[...].T
    for g in range(G):
        gm = (sub256 // 64) == g
        qplt_ref[g][...] = jnp.concatenate(
            [jnp.where(gm, qt[a * 256:(a + 1) * 256, :] * (0.125 * _LOG2E), 0.0) for a in range(A)],
            axis=1).astype(_BF16)
    tq = t0 + lax.broadcasted_iota(jnp.int32, (1, QB), 1)

    n_seg = kc_ref.shape[0]
    kc = kc_ref[...].astype(_BF16)
    ci = lax.broadcasted_iota(jnp.int32, (n_seg, QB), 0)
    cbias = _tile_cols(jnp.where((ci < n_cmp) & (ci * CMP_STRIDE + (CMP_LEN - 1) <= tq), 0.0, -jnp.inf), A)
    ratio = SLC_BLK // CMP_STRIDE
    mj = lax.broadcasted_iota(jnp.int32, (_GW, n_seg), 0)
    mi = lax.broadcasted_iota(jnp.int32, (_GW, n_seg), 1)
    msel_t = ((mi >= ratio * mj - 1) & (mi <= ratio * mj + ratio - 1)).astype(_BF16)
    blk = lax.broadcasted_iota(jnp.int32, (_GW, QB), 0)
    cur = tq // SLC_BLK
    valid = (blk < n_slc) & (blk * SLC_BLK <= tq)
    forced = (blk == 0) | (blk == cur) | (blk == cur - 1)
    for g in range(G):
        hrows = slice(g * DH, (g + 1) * DH)
        s = jnp.dot(kc, qplt_ref[g][...], preferred_element_type=_F32) + cbias
        e, inv = _col_softmax_parts(s)
        cmp_ref[g][...] = jnp.dot(vct[hrows, :], e.astype(_BF16), preferred_element_type=_F32) * inv
        p = e * inv
        pg = p[:, 0:QB]
        for a in range(1, A):
            pg = pg + p[:, a * QB:(a + 1) * QB]
        pg_hi = pg.astype(_BF16)
        pg_lo = (pg - pg_hi.astype(_F32)).astype(_BF16)
        ps = jnp.dot(msel_t, pg_hi, preferred_element_type=_F32) \
            + jnp.dot(msel_t, pg_lo, preferred_element_type=_F32)
        score = jnp.where(valid, jnp.where(forced, jnp.inf, ps), -jnp.inf)
        rank = jnp.zeros((_GW, QB), _F32)
        for i in range(_GW):
            row = score[i:i + 1, :]
            ahead = (row > score) | ((row == score) & (blk > i))
            rank = rank + jnp.where(ahead, 1.0, 0.0)
        sel_ref[g][...] = jnp.where(rank < min(SLC_TOPN, n_slc), 1.0, 0.0)

    for g in range(G):
        m_ref[g][...] = jnp.full((1, R), -jnp.inf, _F32)
        l_ref[g][...] = jnp.zeros((1, R), _F32)
        acc_ref[g][...] = jnp.zeros((DH, R), _F32)
    krow = lax.broadcasted_iota(jnp.int32, (TK, QB), 0)

    def sel_tile(kt, carry):
        k0 = pl.multiple_of(kt * TK, TK)
        kt_b = ksb[pl.ds(k0, TK), :]
        b0 = kt * (TK // 128)
        j0 = kt * (TK // SLC_BLK)
        vt_b = jnp.concatenate([vst[b0 + i] for i in range(TK // 128)], axis=1)
        causal = (k0 + krow) <= tq
        for g in range(G):
            member = jnp.concatenate(
                [jnp.broadcast_to(sel_ref[g][pl.ds(j0 + j, 1), :], (SLC_BLK, QB))
                 for j in range(TK // SLC_BLK)], axis=0)
            bias = _tile_cols(jnp.where(causal & (member > 0.5), 0.0, -jnp.inf), A)
            s = jnp.dot(kt_b, qplt_ref[g][...], preferred_element_type=_F32) + bias
            m_old = m_ref[g][...]
            m_new = jnp.maximum(m_old, jnp.max(s, axis=0, keepdims=True))
            m_safe = jnp.where(m_new == -jnp.inf, 0.0, m_new)
            p = jnp.exp2(s - m_safe)
            alpha = jnp.exp2(m_old - m_safe)
            l_ref[g][...] = alpha * l_ref[g][...] + jnp.sum(p, axis=0, keepdims=True)
            acc_ref[g][...] = alpha * acc_ref[g][...] + jnp.dot(
                vt_b[g * DH:(g + 1) * DH, :], p.astype(_BF16), preferred_element_type=_F32)
            m_ref[g][...] = m_new
        return carry

    lax.fori_loop(0, (t0 + QB + TK - 1) // TK, sel_tile, 0)

    WK = WINDOW + QB
    ws = pl.multiple_of(jnp.clip(t0 - WINDOW, 0, T - WK), QB)
    kw_b = kwb[pl.ds(ws, WK), :]
    wb0 = ws // 128
    vw_b = jnp.concatenate([vwt[wb0 + i] for i in range(WK // 128)], axis=1)
    diff = tq - (ws + lax.broadcasted_iota(jnp.int32, (WK, QB), 0))
    wbias = _tile_cols(jnp.where((diff >= 0) & (diff < WINDOW), 0.0, -jnp.inf), A)
    for g in range(G):
        s = jnp.dot(kw_b, qplt_ref[g][...], preferred_element_type=_F32) + wbias
        e, inv = _col_softmax_parts(s)
        win_ref[g][...] = jnp.dot(vw_b[g * DH:(g + 1) * DH, :], e.astype(_BF16),
                                  preferred_element_type=_F32) * inv
        acc_ref[g][...] = acc_ref[g][...] * (1.0 / jnp.maximum(l_ref[g][...], 1e-30))

    smt = sm_ref[...].T
    g0 = 2 * M_HEADS
    for a in range(A):
        lanes = slice(a * QB, (a + 1) * QB)
        parts = []
        for g in range(G):
            head = g0 + g * A + a
            parts.append(smt[head:head + 1, :] * cmp_ref[g][:, lanes]
                         + smt[head + A_HEADS:head + A_HEADS + 1, :] * acc_ref[g][:, lanes]
                         + smt[head + 2 * A_HEADS:head + 2 * A_HEADS + 1, :] * win_ref[g][:, lanes])
        mix = jnp.concatenate(parts, axis=0)
        az = az_ref[:, a * 256:(a + 1) * 256]
        out_ref[:, a * 256:(a + 1) * 256] = mix.T * (az * jax.nn.sigmoid(az))


def _nsa_prompt_t(q, kv4, kvw, kc, vc, sm, az):
    B, T, _ = q.shape
    n_seg = kc.shape[1]
    n_cmp = n_seg - 1
    n_slc = T // SLC_BLK
    assert T % _SEL_TK == 0 and T >= WINDOW + _QB and n_slc <= _GW and n_seg % 128 == 0
    R = A_GROUP * _QB
    blk = lambda w: pl.BlockSpec((None, _QB, w), lambda b, i: (b, i, 0))
    return pl.pallas_call(
        functools.partial(_nsa_prompt_t_kernel, T, n_cmp, n_slc),
        grid=(B, T // _QB),
        in_specs=[
            blk(1024),
            pl.BlockSpec((None, T, 512), lambda b, i: (b, 0, 1)),
            pl.BlockSpec((None, T, 512), lambda b, i: (b, 0, 0)),
            pl.BlockSpec((None, n_seg, 256), lambda b, i: (b, 0, 0)),
            pl.BlockSpec((None, n_seg, 256), lambda b, i: (b, 0, 0)),
            blk(128), blk(1024),
        ],
        out_specs=blk(1024),
        out_shape=jax.ShapeDtypeStruct((B, T, 1024), _F32),
        scratch_shapes=(
            [pltpu.VMEM((T, 256), _BF16), pltpu.VMEM((T // 128, 256, 128), _BF16),
             pltpu.VMEM((T, 256), _BF16), pltpu.VMEM((T // 128, 256, 128), _BF16),
             pltpu.VMEM((256, n_seg), _BF16)]
            + [pltpu.VMEM((256, R), _BF16)] * A_KV
            + [pltpu.VMEM((_GW, _QB), _F32)] * A_KV
            + [pltpu.VMEM((1, R), _F32)] * (2 * A_KV)
            + [pltpu.VMEM((256 // A_KV, R), _F32)] * (3 * A_KV)
        ),
        compiler_params=pltpu.CompilerParams(
            dimension_semantics=("arbitrary", "arbitrary"), vmem_limit_bytes=_VMEM_LIMIT),
        name="nsa_prompt",
    )(q, kv4, kvw, kc, vc, sm, az)


_CMP_PAGES = 32


def _compress_sample_kernel(P, n_chunks, n_seg, pt_ref, cache_ref, w1_ref, b_ref, w2_ref, seg_ref,
                            kn_ref, kc_ref, vc_ref, stage, lanes, per, sem):
    b = pl.program_id(0)
    ch = pl.program_id(1)
    step = b * n_chunks + ch
    n_steps = pl.num_programs(0) * n_chunks
    slot = step % 2
    rows = P * PAGE_SIZE

    def page_copy(bb, cc, sl, p):
        page = pt_ref[bb, cc * P + p]
        return pltpu.make_async_copy(cache_ref.at[page, pl.ds(0, 4)], stage.at[sl, p], sem.at[sl])

    @pl.when(step == 0)
    def _():
        for p in range(P):
            page_copy(b, ch, slot, p).start()

    @pl.when(step + 1 < n_steps)
    def _():
        nxt = step + 1
        for p in range(P):
            page_copy(nxt // n_chunks, nxt % n_chunks, 1 - slot, p).start()

    pltpu.make_async_copy(cache_ref.at[pl.ds(0, P), pl.ds(0, 4)], stage.at[slot], sem.at[slot]).wait()

    segs = rows // CMP_STRIDE

    for k in range(4):
        for p in range(P):
            lanes[k, p * PAGE_SIZE:(p + 1) * PAGE_SIZE, :] = stage[slot, p, k].T
        per[pl.ds(pl.multiple_of(ch * segs, segs), segs), k * 256:(k + 1) * 256] = _cmp_first_layer(
            lanes.at[k], segs, w1_ref, k // 2)

    @pl.when(ch == n_chunks - 1)
    def _():
        for k in range(4):
            c, gp = k // 2, k % 2
            o_ref = kc_ref if c == 0 else vc_ref
            o_ref[:, gp * 128:(gp + 1) * 128] = _cmp_second_layer(
                per[:, k * 256:(k + 1) * 256], n_seg, c, b_ref, w2_ref, seg_ref, kn_ref)


def _compress_sample(cache_t, page_table, cwp):
    B, n_pages = page_table.shape
    P = _CMP_PAGES if n_pages % _CMP_PAGES == 0 else n_pages
    n_chunks = n_pages // P
    n_seg = n_pages * PAGE_SIZE // CMP_STRIDE
    w1p, bias2, w2p, seg, kn = cwp
    full = lambda a: pl.BlockSpec(a.shape, lambda b, c, pt: (0,) * a.ndim)
    grid_spec = pltpu.PrefetchScalarGridSpec(
        num_scalar_prefetch=1,
        grid=(B, n_chunks),
        in_specs=[pl.BlockSpec(memory_space=pl.ANY),
                  full(w1p), full(bias2), full(w2p), full(seg), full(kn)],
        out_specs=[pl.BlockSpec((None, n_seg, 256), lambda b, c, pt: (b, 0, 0))] * 2,
        scratch_shapes=[
            pltpu.VMEM((2, P, 4, 128, PAGE_SIZE), _F32),
            pltpu.VMEM((4, P * PAGE_SIZE, 128), _F32),
            pltpu.VMEM((n_seg, 1024), _F32),
            pltpu.SemaphoreType.DMA((2,)),
        ])
    return pl.pallas_call(
        functools.partial(_compress_sample_kernel, P, n_chunks, n_seg),
        grid_spec=grid_spec,
        out_shape=[jax.ShapeDtypeStruct((B, n_seg, 256), _F32)] * 2,
        compiler_params=pltpu.CompilerParams(
            dimension_semantics=("arbitrary", "arbitrary"), vmem_limit_bytes=_VMEM_LIMIT),
        name="compress_sample",
    )(page_table, cache_t, w1p, bias2, w2p, seg, kn)


def _merge_heads(o_ref, o_g, g, Ts):
    gm = (lax.broadcasted_iota(jnp.int32, (Ts, 256), 1) // 64) == g
    for a in range(A_GROUP):
        part = jnp.where(gm, o_g[a * Ts:(a + 1) * Ts], 0.0)
        cols = slice(a * 256, (a + 1) * 256)
        if g == 0:
            o_ref[:, cols] = part
        else:
            o_ref[:, cols] += part


def _nsa_sample_front_kernel(Ts, past_len, n_cmp, n_slc, NL, topn,
                             q_ref, kc_ref, vc_ref, wold_ref, wnew_ref,
                             ocmp_ref, owin_ref, idx_ref, qpl_ref):
    G, A = A_KV, A_GROUP
    R = A * Ts
    lane256 = lax.broadcasted_iota(jnp.int32, (Ts, 256), 1)
    q = q_ref[...]
    for g in range(G):
        gm = (lane256 // 64) == g
        qpl_ref[g] = jnp.concatenate(
            [jnp.where(gm, q[:, a * 256:(a + 1) * 256] * 0.125, 0.0) for a in range(A)],
            axis=0).astype(_BF16)
    tq_r = past_len + lax.broadcasted_iota(jnp.int32, (R, 1), 0) % Ts
    tq = past_len + lax.broadcasted_iota(jnp.int32, (Ts, 1), 0)

    n_seg = kc_ref.shape[0]
    kc = kc_ref[...].astype(_BF16)
    vc = vc_ref[...].astype(_BF16)
    ci = lax.broadcasted_iota(jnp.int32, (R, n_seg), 1)
    cmask = (ci < n_cmp) & (ci * CMP_STRIDE + (CMP_LEN - 1) <= tq_r)
    ratio = SLC_BLK // CMP_STRIDE
    mi = lax.broadcasted_iota(jnp.int32, (n_seg, NL), 0)
    mj = lax.broadcasted_iota(jnp.int32, (n_seg, NL), 1)
    msel = ((mi >= ratio * mj - 1) & (mi <= ratio * mj + ratio - 1)).astype(_BF16)
    lane = lax.broadcasted_iota(jnp.int32, (Ts, NL), 1)
    cur = tq // SLC_BLK
    valid = (lane < n_slc) & (lane * SLC_BLK <= tq)
    forced = (lane == 0) | (lane == cur) | (lane == cur - 1)
    scores = []
    for g in range(G):
        s = lax.dot_general(qpl_ref[g], kc, (((1,), (1,)), ((), ())), preferred_element_type=_F32)
        p = _softmax_rows(s, cmask)
        _merge_heads(ocmp_ref, jnp.dot(p.astype(_BF16), vc, preferred_element_type=_F32), g, Ts)
        pg = p[0:Ts]
        for a in range(1, A):
            pg = pg + p[a * Ts:(a + 1) * Ts]
        pg_hi = pg.astype(_BF16)
        pg_lo = (pg - pg_hi.astype(_F32)).astype(_BF16)
        ps = jnp.dot(pg_hi, msel, preferred_element_type=_F32) \
            + jnp.dot(pg_lo, msel, preferred_element_type=_F32)
        scores.append(jnp.where(valid, jnp.where(forced, jnp.inf, ps), -jnp.inf))
    score = jnp.concatenate(scores, axis=0)

    lane_f = lax.broadcasted_iota(jnp.int32, (G * Ts, NL), 1).astype(_F32)
    out_lane = lax.broadcasted_iota(jnp.int32, (G * Ts, 128), 1)
    avail = lane_f < n_slc
    picks = jnp.zeros((G * Ts, 128), _F32)
    for k in range(topn):
        mx = jnp.max(jnp.where(avail, score, -jnp.inf), axis=1, keepdims=True)
        pick = jnp.min(jnp.where(avail & (score == mx), lane_f, float(NL)), axis=1, keepdims=True)
        avail = avail & (lane_f != pick)
        picks = jnp.where(out_lane == k, pick, picks)
    idx_ref[...] = picks.astype(jnp.int32)

    wb = wold_ref.shape[0]
    k_old = wold_ref[:, 0:256].astype(_BF16)
    v_old = wold_ref[:, 256:512].astype(_BF16)
    k_new = wnew_ref[:, 0:256]
    v_new = wnew_ref[:, 256:512]
    pw_old = past_len - wb + lax.broadcasted_iota(jnp.int32, (R, wb), 1)
    d_old = tq_r - pw_old
    m_old = (pw_old >= 0) & (d_old >= 0) & (d_old < WINDOW)
    d_new = tq_r - (past_len + lax.broadcasted_iota(jnp.int32, (R, Ts), 1))
    m_new = (d_new >= 0) & (d_new < WINDOW)
    for g in range(G):
        qg = qpl_ref[g]
        s1 = lax.dot_general(qg, k_old, (((1,), (1,)), ((), ())), preferred_element_type=_F32)
        s2 = lax.dot_general(qg.astype(_F32), k_new, (((1,), (1,)), ((), ())),
                             preferred_element_type=_F32)
        s1 = jnp.where(m_old, s1, -jnp.inf)
        s2 = jnp.where(m_new, s2, -jnp.inf)
        mx = jnp.maximum(jnp.max(s1, axis=1, keepdims=True), jnp.max(s2, axis=1, keepdims=True))
        mx = jnp.where(mx == -jnp.inf, 0.0, mx)
        e1 = jnp.where(m_old, jnp.exp(s1 - mx), 0.0)
        e2 = jnp.where(m_new, jnp.exp(s2 - mx), 0.0)
        den = jnp.maximum(jnp.sum(e1, axis=1, keepdims=True) + jnp.sum(e2, axis=1, keepdims=True), 1e-30)
        o = (jnp.dot(e1.astype(_BF16), v_old, preferred_element_type=_F32)
             + jnp.dot(e2, v_new, preferred_element_type=_F32)) / den
        _merge_heads(owin_ref, o, g, Ts)


def _nsa_sample_front(q, kc, vc, win_old, win_new, past_len):
    B, Ts, _ = q.shape
    n_seg = kc.shape[1]
    n_cmp = (past_len + Ts) // CMP_STRIDE - 1
    assert n_cmp == n_seg - 1 and Ts % 8 == 0 and Ts <= SLC_BLK
    n_slc = past_len // SLC_BLK + 1
    NL = -(-n_slc // 128) * 128
    topn = min(SLC_TOPN, n_slc)
    wb = win_old.shape[1]
    R = A_GROUP * Ts
    b3 = lambda r, w: pl.BlockSpec((None, r, w), lambda b: (b, 0, 0))
    return pl.pallas_call(
        functools.partial(_nsa_sample_front_kernel, Ts, past_len, n_cmp, n_slc, NL, topn),
        grid=(B,),
        in_specs=[b3(Ts, 1024), b3(n_seg, 256), b3(n_seg, 256), b3(wb, 512), b3(Ts, 512)],
        out_specs=[b3(Ts, 1024), b3(Ts, 1024), b3(A_KV * Ts, 128)],
        out_shape=[jax.ShapeDtypeStruct((B, Ts, 1024), _F32), jax.ShapeDtypeStruct((B, Ts, 1024), _F32),
                   jax.ShapeDtypeStruct((B, A_KV * Ts, 128), jnp.int32)],
        scratch_shapes=[pltpu.VMEM((A_KV, R, 256), _BF16)],
        compiler_params=pltpu.CompilerParams(
            dimension_semantics=("arbitrary",), vmem_limit_bytes=_VMEM_LIMIT),
        name="nsa_sample_front",
    )(q, kc, vc, win_old, win_new)


_GATHER_SLOTS = 3


def _nsa_sample_gather_kernel(Ts, n_past_blk, past_len, topn,
                              idx_ref, pt_ref, q_ref, kvn_ref, cache_ref, o_ref, kvbuf, sem):
    b = pl.program_id(0)
    n_it = A_KV * Ts
    bpp = PAGE_SIZE // SLC_BLK
    nk = topn * PAGE_SIZE

    def issue(it, sl):
        g = it // Ts
        for k in range(topn):
            blk = jnp.minimum(idx_ref[b, it * topn + k], n_past_blk - 1)
            page = pt_ref[b, blk // bpp]
            pltpu.make_async_copy(cache_ref.at[page, pl.ds(2, 2), g], kvbuf.at[sl, k], sem.at[sl]).start()

    for i in range(_GATHER_SLOTS - 1):
        issue(i, i)
    lane = lax.broadcasted_iota(jnp.int32, (1, nk), 1)
    new_r = lax.broadcasted_iota(jnp.int32, (1, Ts), 1)

    def body(it, carry):
        sl = it % _GATHER_SLOTS

        @pl.when(it + _GATHER_SLOTS - 1 < n_it)
        def _():
            issue(it + _GATHER_SLOTS - 1, (it + _GATHER_SLOTS - 1) % _GATHER_SLOTS)

        pltpu.make_async_copy(cache_ref.at[pl.ds(0, topn), pl.ds(0, 2), 0], kvbuf.at[sl], sem.at[sl]).wait()

        t = it % Ts
        g = it // Ts
        tq = past_len + t
        pos = jnp.zeros((1, nk), jnp.int32)
        past = jnp.zeros((1, nk), jnp.int32)
        new_sel = jnp.zeros((1, Ts), jnp.int32)
        for k in range(topn):
            blk = idx_ref[b, it * topn + k]
            in_slot = (lane // PAGE_SIZE) == k
            row = lane % PAGE_SIZE
            hit = in_slot & ((row // SLC_BLK) == (blk % bpp))
            pos = jnp.where(in_slot, (blk // bpp) * PAGE_SIZE + row, pos)
            past = jnp.where(hit, jnp.where(blk < n_past_blk, 1, 0), past)
            new_sel = jnp.maximum(new_sel, jnp.where(blk == n_past_blk, 1, 0))
        m1 = (past > 0) & (pos <= tq)
        m2 = (new_sel > 0) & (n_past_blk * SLC_BLK + new_r <= tq)
        q8 = q_ref[it] * 0.125
        kt = jnp.concatenate([kvbuf[sl, k, 0] for k in range(topn)], axis=1).astype(_BF16)
        vt = jnp.concatenate([kvbuf[sl, k, 1] for k in range(topn)], axis=1).astype(_BF16)
        s1 = jnp.dot(q8.astype(_BF16), kt, preferred_element_type=_F32)
        s2 = lax.dot_general(q8, kvn_ref[2 * A_KV + g], (((1,), (1,)), ((), ())),
                             preferred_element_type=_F32)
        s1 = jnp.where(m1, s1, -jnp.inf)
        s2 = jnp.where(m2, s2, -jnp.inf)
        mx = jnp.maximum(jnp.max(s1, axis=1, keepdims=True), jnp.max(s2, axis=1, keepdims=True))
        mx = jnp.where(mx == -jnp.inf, 0.0, mx)
        e1 = jnp.where(m1, jnp.exp(s1 - mx), 0.0)
        e2 = jnp.where(m2, jnp.exp(s2 - mx), 0.0)
        den = jnp.maximum(jnp.sum(e1, axis=1, keepdims=True) + jnp.sum(e2, axis=1, keepdims=True), 1e-30)
        o_ref[it] = (lax.dot_general(e1.astype(_BF16), vt, (((1,), (1,)), ((), ())),
                                     preferred_element_type=_F32)
                     + jnp.dot(e2, kvn_ref[3 * A_KV + g], preferred_element_type=_F32)) / den
        return carry

    lax.fori_loop(0, n_it, body, 0)


def _nsa_sample_gather(idx, page_table, q_rows, kv_new, cache4, past_len):
    B, n_it, _, dh = q_rows.shape
    Ts = n_it // A_KV
    n_past_blk = past_len // SLC_BLK
    topn = idx.shape[1] // n_it
    grid_spec = pltpu.PrefetchScalarGridSpec(
        num_scalar_prefetch=2,
        grid=(B,),
        in_specs=[pl.BlockSpec((None, n_it, 8, dh), lambda b, i, p: (b, 0, 0, 0)),
                  pl.BlockSpec((None, 4 * A_KV, Ts, dh), lambda b, i, p: (b, 0, 0, 0)),
                  pl.BlockSpec(memory_space=pl.ANY)],
        out_specs=pl.BlockSpec((None, n_it, 8, dh), lambda b, i, p: (b, 0, 0, 0)),
        scratch_shapes=[
            pltpu.VMEM((_GATHER_SLOTS, topn, 2, dh, PAGE_SIZE), _F32),
            pltpu.SemaphoreType.DMA((_GATHER_SLOTS,)),
        ])
    return pl.pallas_call(
        functools.partial(_nsa_sample_gather_kernel, Ts, n_past_blk, past_len, topn),
        grid_spec=grid_spec,
        out_shape=jax.ShapeDtypeStruct((B, n_it, 8, dh), _F32),
        compiler_params=pltpu.CompilerParams(
            dimension_semantics=("arbitrary",), vmem_limit_bytes=_VMEM_LIMIT),
        name="nsa_sample_gather",
    )(idx, page_table, q_rows, kv_new, cache4)


def _nsa_combine_kernel(ocmp_ref, osel_ref, owin_ref, sm_ref, az_ref, o_ref):
    sm = sm_ref[...]
    shape = ocmp_ref.shape
    head_lane = lax.broadcasted_iota(jnp.int32, shape, 1) // 64
    g0 = 2 * M_HEADS
    acc = jnp.zeros(shape, _F32)
    for br, ref in enumerate((ocmp_ref, osel_ref, owin_ref)):
        gate = jnp.zeros(shape, _F32)
        for a in range(A_GROUP):
            for g in range(A_KV):
                c = g0 + br * A_HEADS + g * A_GROUP + a
                gate = jnp.where(head_lane == a * A_KV + g, sm[:, c:c + 1], gate)
        acc = acc + gate * ref[...]
    az = az_ref[...]
    o_ref[...] = acc * (az * jax.nn.sigmoid(az))


def _nsa_combine(ocmp, osel, owin, sm, az):
    M, W = ocmp.shape
    full = lambda w: pl.BlockSpec((M, w), lambda i: (0, 0))
    return pl.pallas_call(
        _nsa_combine_kernel,
        grid=(1,),
        in_specs=[full(W), full(W), full(W), full(128), full(W)],
        out_specs=full(W),
        out_shape=jax.ShapeDtypeStruct((M, W), _F32),
        name="nsa_combine",
    )(ocmp, osel, owin, sm, az)


def _mixer_inputs(x, norm_w, pw, m_width, a_width, a_dh):
    B, T, D = x.shape
    w_main, w_small, seg, hw, hm, groups, norm_tiles = pw
    M = B * T
    tm = next((t for t in (1024, 512) if M % t == 0), M)
    a, q, az, kv4, kvw, sm = _proj_in(x.reshape(M, D), norm_w, w_main, w_small, seg, hw, hm,
                                      groups, norm_tiles, tm)
    r3 = lambda v: v.reshape(B, T, v.shape[-1])
    return r3(a), r3(sm), r3(q), r3(kv4), r3(kvw), r3(az)


def _mixer_output(x, m_out, a_out, p, ow):
    B, T, D = x.shape
    M = B * T
    tm = 256 if M % 256 == 0 else M
    y = _out_proj(x.reshape(M, D), m_out.reshape(M, -1), a_out.reshape(M, -1), p.reshape(M, -1), *ow, tm)
    return y.reshape(B, T, D)


def kernel(x_prompt, x_sample, cache_nsa_kv, cache_win_kv, state_mlstm_C, state_mlstm_n,
           state_mlstm_m, state_mlstm_conv, page_table, p_prompt, p_sample, norm_w, w_in,
           m_conv_w, m_conv_b, m_wq, m_wk, m_wv, m_b_i, m_b_f, m_norm_w, a_q_norm, a_k_norm,
           cmp_pe, cmp_w1, cmp_w2, w_out, ple_proj, ple_norm, ple_gate):
    xp, xs = x_prompt, x_sample
    B, T, D = xp.shape
    depth = w_in.shape[0]
    m_width = m_conv_w.shape[-1]
    a_width = D - m_width
    a_dh = a_q_norm.shape[-1]
    dh = m_width // M_HEADS
    per_layer = []
    for i in range(depth):
        mw = (m_conv_w[i], m_conv_b[i], m_wq[i], m_wk[i], m_wv[i], m_b_i[i], m_b_f[i], m_norm_w[i])
        cw = (cmp_pe[i], cmp_w1[i], cmp_w2[i], a_k_norm[i, 0])
        w_out_i = jnp.concatenate(
            [w_out[i][:m_width], _heads_to_agd(w_out[i][m_width:].T, a_dh).T], axis=0)
        ow = (w_out_i, ple_proj[i], ple_norm[i], ple_gate[i])
        pw = _prep_proj_weights(w_in[i], a_q_norm[i], a_k_norm[i], m_width, a_width, a_dh)
        cwp = _prep_cmp_weights(*cw)
        a3, sm, q, kv4, kvw, az = _mixer_inputs(xp, norm_w[i], pw, m_width, a_width, a_dh)
        L = _MLSTM_L if T % _MLSTM_L == 0 else (M_CHUNK if T % M_CHUNK == 0 else T)
        m_out, conv_p, C_p, n_p, m_p = _mlstm(
            a3, sm, jnp.zeros((B, M_CONV - 1, m_width), _F32),
            jnp.zeros((B, M_HEADS, dh, dh), _F32), jnp.zeros((B, M_HEADS, dh), _F32),
            jnp.full((B, M_HEADS), -jnp.inf, _F32), *mw, L)
        kc, vc = _compress_prompt(kv4, cwp)
        a_out = _nsa_prompt_t(q, kv4, kvw, kc, vc, sm, az)
        xp_new = _mixer_output(xp, m_out, a_out, p_prompt[i], ow)
        kv_p = kv4.reshape(B, T, 4, A_KV, a_dh)
        win_p = kvw[:, -min(WINDOW, T):].reshape(B, -1, 2, A_KV, a_dh)
        a3, sm, q, kv4, kvw, az = _mixer_inputs(xs, norm_w[i], pw, m_width, a_width, a_dh)
        Bd, Ts = xs.shape[:2]
        Ls = _MLSTM_L if Ts % _MLSTM_L == 0 else (M_CHUNK if Ts % M_CHUNK == 0 else Ts)
        m_out, conv_s, C_s, n_s, m_s = _mlstm(
            a3, sm, state_mlstm_conv[i], state_mlstm_C[i], state_mlstm_n[i], state_mlstm_m[i], *mw, Ls)
        past_len = page_table.shape[1] * PAGE_SIZE
        cache_l = cache_nsa_kv[i]
        n_pool = cache_l.shape[0]
        wb = cache_win_kv.shape[2]
        cache_t = jnp.transpose(cache_l, (0, 2, 3, 4, 1))
        kc, vc = _compress_sample(cache_t.reshape(n_pool, 8, 2 * a_dh, PAGE_SIZE), page_table, cwp)
        ocmp, owin, idx = _nsa_sample_front(
            q, kc, vc, cache_win_kv[i].reshape(Bd, wb, 2 * A_KV * a_dh), kvw, past_len)
        topn = min(SLC_TOPN, past_len // SLC_BLK + 1)
        q_rows = q.reshape(Bd, Ts, A_GROUP, A_KV, a_dh).transpose(0, 3, 1, 2, 4)
        q_rows = jnp.pad(q_rows, ((0, 0), (0, 0), (0, 0), (0, 8 - A_GROUP), (0, 0)))
        kv_new = kv4.reshape(Bd, Ts, 4 * A_KV, a_dh).transpose(0, 2, 1, 3)
        osel = _nsa_sample_gather(
            idx[:, :, :topn].reshape(Bd, -1), page_table, q_rows.reshape(Bd, A_KV * Ts, 8, a_dh), kv_new,
            cache_t, past_len)
        osel = osel.reshape(Bd, A_KV, Ts, 8, a_dh)[:, :, :, :A_GROUP].transpose(0, 2, 3, 1, 4)
        Ms = Bd * Ts
        a_out = _nsa_combine(ocmp.reshape(Ms, -1), osel.reshape(Ms, -1), owin.reshape(Ms, -1),
                             sm.reshape(Ms, -1), az.reshape(Ms, -1)).reshape(Bd, Ts, -1)
        xs_new = _mixer_output(xs, m_out, a_out, p_sample[i], ow)
        kv_s = kv4.reshape(Bd, Ts, 4, A_KV, a_dh)
        win_s = jnp.concatenate(
            [cache_win_kv[i], kvw.reshape(Bd, Ts, 2, A_KV, a_dh)], axis=1)[:, -wb:]
        per_layer.append((kv_p, kv_s, win_p, win_s, C_p, n_p, m_p, conv_p, C_s, n_s, m_s, conv_s))
        xp, xs = xp_new, xs_new
    (kv_p, kv_s, win_p, win_s, C_p, n_p, m_p, conv_p,
     C_s, n_s, m_s, conv_s) = [jnp.stack(a, axis=0) for a in zip(*per_layer)]
    return (xp, xs, kv_p, kv_s, win_p, win_s, C_p, n_p, m_p, conv_p, C_s, n_s, m_s, conv_s)
```

```python
import functools
import math

import jax
import jax.numpy as jnp
from jax import lax
from jax.experimental import pallas as pl
from jax.experimental.pallas import tpu as pltpu

NORM_EPS = 1e-6
M_HEADS = 4
M_CONV = 4
M_CHUNK = 64
A_HEADS = 16
A_KV = 4
A_GROUP = A_HEADS // A_KV
CMP_STRIDE = 16
CMP_LEN = 2 * CMP_STRIDE
SLC_BLK = 64
SLC_TOPN = 16
WINDOW = 512
PAGE_SIZE = 128

_BF16 = jnp.bfloat16
_F32 = jnp.float32
_VMEM_LIMIT = 56 * 1024 * 1024


def _dot(a, b):
    return jnp.dot(a.astype(_BF16), b.astype(_BF16), preferred_element_type=_F32)


_PT = 512


def _proj_in_kernel(groups, norm_tiles, x_ref, nw_ref, w_ref, wsm_ref, seg_ref, hw_ref, hm_ref,
                    a_ref, q_ref, az_ref, kv4_ref, kvw_ref, sm_ref, *rest):
    xn_ref = rest[-1]
    t_refs = (None, None, None, rest[0] if len(rest) == 2 else None, None)
    j = pl.program_id(1)

    @pl.when(j == 0)
    def _():
        x = x_ref[...]
        ms = jnp.mean(x * x, axis=-1, keepdims=True)
        xn = x * lax.rsqrt(ms + NORM_EPS) * nw_ref[...]
        xn_ref[...] = xn.astype(_BF16)
        sm = jnp.dot(xn_ref[...], wsm_ref[...], preferred_element_type=_F32)
        lane = lax.broadcasted_iota(jnp.int32, sm.shape, 1)
        is_gate = (lane >= 2 * M_HEADS) & (lane < 2 * M_HEADS + 3 * A_HEADS)
        sm_ref[...] = jnp.where(is_gate, jax.nn.sigmoid(sm), sm)

    y = jnp.dot(xn_ref[...], w_ref[...], preferred_element_type=_F32)
    outs = (a_ref, q_ref, az_ref, kv4_ref, kvw_ref)

    def emit(o_ref, t_ref, val):
        o_ref[...] = val
        if t_ref is not None:
            t_ref[...] = val.T

    for (start, n), o_ref, t_ref in zip(groups, outs, t_refs):
        plain = [t for t in range(start, start + n) if t not in norm_tiles]
        normed = [t for t in range(start, start + n) if t in norm_tiles]
        if plain:
            cond = functools.reduce(jnp.logical_or, [j == t for t in plain])

            @pl.when(cond)
            def _(o_ref=o_ref, t_ref=t_ref):
                emit(o_ref, t_ref, y)

        if normed:
            cond = functools.reduce(jnp.logical_or, [j == t for t in normed])

            @pl.when(cond)
            def _(o_ref=o_ref, t_ref=t_ref):
                ms = jnp.dot((y * y).astype(_BF16), seg_ref[...], preferred_element_type=_F32)
                yn = y * lax.rsqrt(ms + NORM_EPS) * hw_ref[...]
                emit(o_ref, t_ref, jnp.where(hm_ref[...] > 0, yn, y))


def _proj_in(x2d, norm_w, w_main, w_small, seg, hw, hm, groups, norm_tiles, tm, seq_len):
    M, D = x2d.shape
    n_tiles = w_main.shape[1] // _PT
    widths = [n * _PT for _, n in groups]

    def out_map(start, n):
        return lambda i, j: (i, jnp.clip(j - start, 0, n - 1))

    out_specs = [pl.BlockSpec((tm, _PT), out_map(s, n)) for s, n in groups]
    out_specs.append(pl.BlockSpec((tm, 128), lambda i, j: (i, 0)))
    out_shape = [jax.ShapeDtypeStruct((M, w), _F32) for w in widths]
    out_shape.append(jax.ShapeDtypeStruct((M, 128), _F32))
    if seq_len % tm == 0:
        tpb = seq_len // tm
        s4, n4 = groups[3]
        out_specs.append(pl.BlockSpec(
            (None, _PT, tm), lambda i, j: (i // tpb, jnp.clip(j - s4, 0, n4 - 1), i % tpb)))
        out_shape.append(jax.ShapeDtypeStruct((M // seq_len, widths[3], seq_len), _F32))
    return pl.pallas_call(
        functools.partial(_proj_in_kernel, groups, norm_tiles),
        grid=(M // tm, n_tiles),
        in_specs=[
            pl.BlockSpec((tm, D), lambda i, j: (i, 0)),
            pl.BlockSpec((1, D), lambda i, j: (0, 0)),
            pl.BlockSpec((D, _PT), lambda i, j: (0, j)),
            pl.BlockSpec((D, 128), lambda i, j: (0, 0)),
            pl.BlockSpec((_PT, _PT), lambda i, j: (0, 0)),
            pl.BlockSpec((None, 1, _PT), lambda i, j: (j, 0, 0)),
            pl.BlockSpec((None, 1, _PT), lambda i, j: (j, 0, 0)),
        ],
        out_specs=out_specs,
        out_shape=out_shape,
        scratch_shapes=[pltpu.VMEM((tm, D), _BF16)],
        compiler_params=pltpu.CompilerParams(
            dimension_semantics=("arbitrary", "arbitrary"), vmem_limit_bytes=_VMEM_LIMIT),
        name="proj_in",
    )(x2d, norm_w.reshape(1, D), w_main, w_small, seg, hw, hm)


def _heads_to_agd(w, a_dh):
    lead = w.shape[:-1]
    return jnp.swapaxes(w.reshape(lead + (A_KV, A_GROUP, a_dh)), -3, -2).reshape(w.shape)


def _heads_to_gad(w, a_dh):
    lead = w.shape[:-1]
    return jnp.swapaxes(w.reshape(lead + (A_GROUP, A_KV, a_dh)), -3, -2).reshape(w.shape)


def _prep_proj_weights(w_in, a_q_norm, a_k_norm, m_width, a_width, a_dh):
    kvw = A_KV * a_dh
    sizes = (m_width, m_width, m_width, M_HEADS, M_HEADS, a_width, 6 * kvw, 3 * A_HEADS, a_width)
    offs = [0]
    for s in sizes:
        offs.append(offs[-1] + s)
    u0, o0, z0, i0, f0, q0, kv0, g0, az0, end = offs
    assert end == w_in.shape[1]
    w_main = jnp.concatenate([
        w_in[:, u0:i0], _heads_to_agd(w_in[:, q0:kv0], a_dh), _heads_to_agd(w_in[:, az0:end], a_dh),
        w_in[:, kv0:kv0 + 4 * kvw], w_in[:, kv0 + 4 * kvw:g0]], axis=1).astype(_BF16)
    n_small = 2 * M_HEADS + 3 * A_HEADS
    w_small = jnp.concatenate([
        w_in[:, i0:q0], w_in[:, g0:az0],
        jnp.zeros((w_in.shape[0], 128 - n_small), w_in.dtype)], axis=1).astype(_BF16)
    widths = (3 * m_width, a_width, a_width, 4 * kvw, 2 * kvw)
    groups, start = [], 0
    for w in widths:
        assert w % _PT == 0
        groups.append((start, w // _PT))
        start += w // _PT
    n_tiles = start
    hw = jnp.zeros((n_tiles * _PT,), _F32)
    hm = jnp.zeros((n_tiles * _PT,), _F32)
    qs = groups[1][0] * _PT
    hw = hw.at[qs:qs + a_width].set(jnp.tile(a_q_norm, a_width // a_dh))
    hm = hm.at[qs:qs + a_width].set(1.0)
    k4 = groups[3][0] * _PT
    hw = hw.at[k4 + 2 * kvw:k4 + 3 * kvw].set(jnp.tile(a_k_norm[1], A_KV))
    hm = hm.at[k4 + 2 * kvw:k4 + 3 * kvw].set(1.0)
    kw = groups[4][0] * _PT
    hw = hw.at[kw:kw + kvw].set(jnp.tile(a_k_norm[2], A_KV))
    hm = hm.at[kw:kw + kvw].set(1.0)
    assert kvw * 2 == _PT and _PT % a_dh == 0
    norm_tiles = tuple(range(groups[1][0], groups[1][0] + groups[1][1])) + (
        (k4 + 2 * kvw) // _PT, kw // _PT)
    r = jnp.arange(_PT) // a_dh
    seg = ((r[:, None] == r[None, :]).astype(_F32) / a_dh).astype(_BF16)
    return (w_main, w_small, seg, hw.reshape(n_tiles, 1, _PT), hm.reshape(n_tiles, 1, _PT),
            tuple(groups), norm_tiles)


def _out_proj_kernel(mw, x_ref, m_ref, a_ref, p_ref, wo_ref, pp_ref, pn_ref, pg_ref, o_ref):
    h = x_ref[...] + _dot(m_ref[...], wo_ref[0:mw, :]) + _dot(a_ref[...], wo_ref[mw:, :])
    ms = jnp.mean(h * h, axis=-1, keepdims=True)
    hn = h * lax.rsqrt(ms + NORM_EPS) * pn_ref[...]
    gate = jax.nn.sigmoid(_dot(hn, pg_ref[...]))
    o_ref[...] = h + gate * _dot(p_ref[...], pp_ref[...])


def _out_proj(x2d, m_out, a_out, p2d, w_out, ple_proj, ple_norm, ple_gate, tm):
    M, D = x2d.shape
    mw, aw, dp = m_out.shape[1], a_out.shape[1], p2d.shape[1]

    def const(shape):
        return pl.BlockSpec(shape, lambda i: (0, 0), pipeline_mode=pl.Buffered(1))

    return pl.pallas_call(
        functools.partial(_out_proj_kernel, mw),
        grid=(M // tm,),
        in_specs=[
            pl.BlockSpec((tm, D), lambda i: (i, 0)),
            pl.BlockSpec((tm, mw), lambda i: (i, 0)),
            pl.BlockSpec((tm, aw), lambda i: (i, 0)),
            pl.BlockSpec((tm, dp), lambda i: (i, 0)),
            const((mw + aw, D)),
            const((dp, D)),
            const((1, D)),
            const((D, D)),
        ],
        out_specs=pl.BlockSpec((tm, D), lambda i: (i, 0)),
        out_shape=jax.ShapeDtypeStruct((M, D), _F32),
        compiler_params=pltpu.CompilerParams(
            dimension_semantics=("arbitrary",), vmem_limit_bytes=_VMEM_LIMIT),
        name="out_proj",
    )(x2d, m_out, a_out, p2d, w_out.astype(_BF16), ple_proj.astype(_BF16),
      ple_norm.reshape(1, D), ple_gate.astype(_BF16))


_MLSTM_L = 256
_CONV_PAD = 8


def _mlstm_kernel(L, dh, n_steps,
                  u_ref, o_ref, z_ref, sm_ref, cbuf_ref, c0_ref, n0_ref, m0_ref,
                  wq_ref, wk_ref, wv_ref, cw_ref, cb_ref, bi_ref, bf_ref, nw_ref,
                  out_ref, cout_ref, nout_ref, mout_ref, convout_ref,
                  xbuf, c_s, n_s, m_s):
    s = pl.program_id(1)
    tail = M_CONV - 1

    @pl.when(s == 0)
    def _():
        c_s[...] = c0_ref[...]
        n_s[...] = n0_ref[...]
        m_s[...] = m0_ref[...]
        xbuf[_CONV_PAD - tail:_CONV_PAD, :] = cbuf_ref[...]

    u_all = u_ref[...]
    xbuf[_CONV_PAD:_CONV_PAD + L, :] = u_all
    c = cb_ref[...]
    for j in range(M_CONV):
        c = c + xbuf[_CONV_PAD - tail + j:_CONV_PAD - tail + j + L, :] * cw_ref[j:j + 1, :]
    xbuf[_CONV_PAD - tail:_CONV_PAD, :] = u_all[L - tail:L, :]
    ch_all = c * jax.nn.sigmoid(c)

    sm = sm_ref[...]
    ii = lax.broadcasted_iota(jnp.int32, (L, L), 0)
    jj = lax.broadcasted_iota(jnp.int32, (L, L), 1)
    eye = ii == jj
    causal = jj <= ii

    for h in range(M_HEADS):
        cols = slice(h * dh, (h + 1) * dh)
        u = u_all[:, cols]
        ch = ch_all[:, cols]
        q = _dot(ch, wq_ref[h])
        k = _dot(ch, wk_ref[h]) * (dh ** -0.5)
        v = _dot(u, wv_ref[h])

        li_col = sm[:, h:h + 1] + bi_ref[h][:, 0:1]
        f_in = sm[:, M_HEADS + h:M_HEADS + h + 1] + bf_ref[h][:, 0:1]
        lf_col = jnp.minimum(f_in, 0.0) - jnp.log(1.0 + jnp.exp(-jnp.abs(f_in)))

        lf_row = jnp.sum(jnp.where(eye, lf_col, 0.0), axis=0, keepdims=True)
        b_col = jnp.sum(jnp.where(causal, lf_row, 0.0), axis=1, keepdims=True)

        m_prev = m_s[h][0:1, 0:1]
        c_prev = c_s[h]
        n_prev = n_s[h]
        g_col = li_col - b_col
        g_row = jnp.sum(jnp.where(eye, g_col, 0.0), axis=0, keepdims=True)
        gm = jnp.where(causal, g_row, -jnp.inf)
        big_m = jnp.maximum(m_prev, jnp.max(gm, axis=1, keepdims=True))
        m_t = b_col + big_m
        qb = q.astype(_BF16)
        sc = lax.dot_general(qb, k.astype(_BF16), (((1,), (1,)), ((), ())),
                             preferred_element_type=_F32) * jnp.exp(gm - big_m)
        a_col = jnp.exp(m_prev - big_m)
        num = _dot(sc, v) + a_col * _dot(qb, c_prev)
        den = jnp.sum(sc, axis=1, keepdims=True) + a_col * jnp.sum(q * n_prev, axis=1, keepdims=True)
        hh = num / jnp.maximum(jnp.abs(den), jnp.exp(-m_t))

        b_end = b_col[L - 1:L, :]
        logw = b_end - b_col + li_col
        m_new = jnp.maximum(b_end + m_prev, jnp.max(logw, axis=0, keepdims=True))
        w_col = jnp.exp(logw - m_new)
        decay = jnp.exp(b_end + m_prev - m_new)
        c_s[h] = decay * c_prev + lax.dot_general(
            k.astype(_BF16), (w_col * v).astype(_BF16), (((0,), (0,)), ((), ())),
            preferred_element_type=_F32)
        n_s[h] = decay * n_prev + jnp.sum(w_col * k, axis=0, keepdims=True)
        m_s[h] = jnp.broadcast_to(m_new, (1, 128))

        hg = hh * jax.nn.sigmoid(o_ref[:, cols])
        hn = hg * lax.rsqrt(jnp.mean(hg * hg, axis=-1, keepdims=True) + NORM_EPS) * nw_ref[h]
        zz = z_ref[:, cols]
        out_ref[:, cols] = hn * (zz * jax.nn.sigmoid(zz))

    @pl.when(s == n_steps - 1)
    def _():
        cout_ref[...] = c_s[...]
        nout_ref[...] = n_s[...]
        mout_ref[...] = m_s[...]
        convout_ref[...] = u_all[L - tail:L, :]


def _mlstm(a3, sm, conv_buf, C0, n0, m0, conv_w, conv_b, wq, wk, wv, b_i, b_f, norm_w, L):
    B, T, mw3 = a3.shape
    mw = mw3 // 3
    dh = mw // M_HEADS
    tail = M_CONV - 1
    assert T % L == 0 and L >= tail and (L % 8 == 0)
    n_steps = T // L
    H = M_HEADS

    def lane_b(vec):
        return jnp.broadcast_to(vec.astype(_F32)[:, None, None], (H, 1, 128))

    m0b = jnp.broadcast_to(m0.astype(_F32)[:, :, None, None], (B, H, 1, 128))
    blk = lambda off: pl.BlockSpec((None, L, mw), lambda b, s: (b, s, off))
    full = lambda *shape: pl.BlockSpec(shape, lambda b, s: (0,) * len(shape))
    st = lambda r, c: pl.BlockSpec((None, H, r, c), lambda b, s: (b, 0, 0, 0))
    out, C, n, m, conv = pl.pallas_call(
        functools.partial(_mlstm_kernel, L, dh, n_steps),
        grid=(B, n_steps),
        in_specs=[
            blk(0), blk(1), blk(2),
            pl.BlockSpec((None, L, 128), lambda b, s: (b, s, 0)),
            pl.BlockSpec((None, tail, mw), lambda b, s: (b, 0, 0)),
            st(dh, dh), st(1, dh), st(1, 128),
            full(H, dh, dh), full(H, dh, dh), full(H, dh, dh),
            full(M_CONV, mw), full(1, mw),
            full(H, 1, 128), full(H, 1, 128), full(H, 1, dh),
        ],
        out_specs=[
            pl.BlockSpec((None, L, mw), lambda b, s: (b, s, 0)),
            st(dh, dh), st(1, dh), st(1, 128),
            pl.BlockSpec((None, tail, mw), lambda b, s: (b, 0, 0)),
        ],
        out_shape=[
            jax.ShapeDtypeStruct((B, T, mw), _F32),
            jax.ShapeDtypeStruct((B, H, dh, dh), _F32),
            jax.ShapeDtypeStruct((B, H, 1, dh), _F32),
            jax.ShapeDtypeStruct((B, H, 1, 128), _F32),
            jax.ShapeDtypeStruct((B, tail, mw), _F32),
        ],
        scratch_shapes=[
            pltpu.VMEM((_CONV_PAD + L, mw), _F32),
            pltpu.VMEM((H, dh, dh), _F32),
            pltpu.VMEM((H, 1, dh), _F32),
            pltpu.VMEM((H, 1, 128), _F32),
        ],
        compiler_params=pltpu.CompilerParams(
            dimension_semantics=("arbitrary", "arbitrary"), vmem_limit_bytes=_VMEM_LIMIT),
        name="mlstm",
    )(a3, a3, a3, sm, conv_buf, C0, n0.reshape(B, H, 1, dh), m0b,
      wq.astype(_BF16), wk.astype(_BF16), wv.astype(_BF16), conv_w, conv_b.reshape(1, mw),
      lane_b(b_i), lane_b(b_f), norm_w.reshape(H, 1, dh))
    return out, conv, C, n.reshape(B, H, dh), m[:, :, 0, 0]


def _prep_cmp_weights(pe, w1, w2, k_norm):
    a_dh, hid = w1.shape[-2], w1.shape[-1]
    w1r = w1.reshape(2, 2, CMP_STRIDE, a_dh, hid)
    eye2 = jnp.eye(2, dtype=w1.dtype)
    w1p = jnp.einsum('cxjdh,ab->cjadxbh', w1r, eye2).reshape(2, CMP_STRIDE * 2 * a_dh, 2 * 2 * hid)
    bias = jnp.einsum('cxjd,cxjdh->ch', pe.reshape(2, 2, CMP_STRIDE, a_dh), w1r)
    bias2 = jnp.tile(bias, (1, 2)).reshape(2, 1, 2 * hid)
    w2p = jnp.einsum('chd,ab->cahbd', w2, eye2).reshape(2, 2 * hid, 2 * a_dh)
    r = jnp.arange(2 * a_dh) // a_dh
    seg = ((r[:, None] == r[None, :]).astype(_F32) / a_dh).astype(_BF16)
    kn = jnp.tile(k_norm, 2).reshape(1, 2 * a_dh)
    return w1p.astype(_BF16), bias2.astype(_F32), w2p.astype(_BF16), seg, kn.astype(_F32)


def _cmp_first_layer(rows_ref, n_seg, w1_ref, c):
    x = jnp.concatenate(
        [rows_ref[pl.ds(j, n_seg, stride=CMP_STRIDE), :].astype(_BF16)
         for j in range(CMP_STRIDE)], axis=1)
    return jnp.dot(x, w1_ref[c], preferred_element_type=_F32)


def _cmp_second_layer(p, n_seg, c, b_ref, w2_ref, seg_ref, kn_ref):
    hid = p[:, 0:128] + pltpu.roll(p[:, 128:256], n_seg - 1, axis=0) + b_ref[c]
    act = hid * jax.nn.sigmoid(hid)
    out = jnp.dot(act.astype(_BF16), w2_ref[c], preferred_element_type=_F32)
    if c == 0:
        ms = jnp.dot((out * out).astype(_BF16), seg_ref[...], preferred_element_type=_F32)
        out = out * lax.rsqrt(ms + NORM_EPS) * kn_ref[...]
    return out


def _compress_prompt_kernel(n_seg, r00, r01, r10, r11, w1_ref, b_ref, w2_ref, seg_ref, kn_ref,
                            kc_ref, vc_ref):
    for c, o_ref, refs in ((0, kc_ref, (r00, r01)), (1, vc_ref, (r10, r11))):
        for gp in range(2):
            p = _cmp_first_layer(refs[gp], n_seg, w1_ref, c)
            o_ref[:, gp * 128:(gp + 1) * 128] = _cmp_second_layer(p, n_seg, c, b_ref, w2_ref, seg_ref, kn_ref)


def _compress_prompt(kv4, cwp):
    B, T, W = kv4.shape
    assert W == 1024 and T % (CMP_STRIDE * 8) == 0
    n_seg = T // CMP_STRIDE
    w1p, bias2, w2p, seg, kn = cwp
    full = lambda a: pl.BlockSpec(a.shape, lambda b: (0,) * a.ndim)
    return pl.pallas_call(
        functools.partial(_compress_prompt_kernel, n_seg),
        grid=(B,),
        in_specs=[pl.BlockSpec((None, T, 128), functools.partial(lambda k, b: (b, 0, k), k))
                  for k in range(4)] + [full(w1p), full(bias2), full(w2p), full(seg), full(kn)],
        out_specs=[pl.BlockSpec((None, n_seg, 256), lambda b: (b, 0, 0))] * 2,
        out_shape=[jax.ShapeDtypeStruct((B, n_seg, 256), _F32)] * 2,
        compiler_params=pltpu.CompilerParams(
            dimension_semantics=("arbitrary",), vmem_limit_bytes=_VMEM_LIMIT),
        name="compress_prompt",
    )(kv4, kv4, kv4, kv4, w1p, bias2, w2p, seg, kn)


_QB = 256
_SEL_TK = 512
_GW = 32
_LOG2E = 1.4426950408889634


def _softmax_parts(s):
    m = jnp.max(s, axis=-1, keepdims=True)
    m = jnp.where(m == -jnp.inf, 0.0, m)
    e = jnp.exp(s - m)
    return e, 1.0 / jnp.maximum(jnp.sum(e, axis=-1, keepdims=True), 1e-30)


def _softmax_rows(s, mask):
    e, inv = _softmax_parts(jnp.where(mask, s, -jnp.inf))
    return e * inv


def _col_softmax_parts(s):
    m = jnp.max(s, axis=0, keepdims=True)
    m = jnp.where(m == -jnp.inf, 0.0, m)
    e = jnp.exp2(s - m)
    return e, 1.0 / jnp.maximum(jnp.sum(e, axis=0, keepdims=True), 1e-30)


def _tile_cols(x, n):
    return jnp.concatenate([x] * n, axis=1)


def _nsa_prompt_t_kernel(T, n_cmp, n_slc,
                         q_ref, ks_ref, kw_ref, kc_ref, vc_ref, sm_ref, az_ref, out_ref,
                         ksb, vst, kwb, vwt, vct, *scratch):
    QB, TK, G, A = _QB, _SEL_TK, A_KV, A_GROUP
    R = A * QB
    NB = T // 128
    qplt_ref, sel_ref, m_ref, l_ref, acc_ref, cmp_ref, win_ref = (
        scratch[i * G:(i + 1) * G] for i in range(7))
    DH = 256 // G

    @pl.when(pl.program_id(1) == 0)
    def _():
        for blk in range(NB):
            rows = slice(blk * 128, (blk + 1) * 128)
            ksb[rows, :] = ks_ref[rows, 0:256].astype(_BF16)
            vst[blk] = ks_ref[rows, 256:512].T.astype(_BF16)
            kwb[rows, :] = kw_ref[rows, 0:256].astype(_BF16)
            vwt[blk] = kw_ref[rows, 256:512].T.astype(_BF16)
        vct[...] = vc_ref[...].T.astype(_BF16)

    t0 = pl.program_id(1) * QB
    sub256 = lax.broadcasted_iota(jnp.int32, (256, QB), 0)
    qt = q_ref[...].T
    for g in range(G):
        gm = (sub256 // 64) == g
        qplt_ref[g][...] = jnp.concatenate(
            [jnp.where(gm, qt[a * 256:(a + 1) * 256, :] * (0.125 * _LOG2E), 0.0) for a in range(A)],
            axis=1).astype(_BF16)
    tq = t0 + lax.broadcasted_iota(jnp.int32, (1, QB), 1)

    n_seg = kc_ref.shape[0]
    kc = kc_ref[...].astype(_BF16)
    ci = lax.broadcasted_iota(jnp.int32, (n_seg, QB), 0)
    cbias = _tile_cols(jnp.where((ci < n_cmp) & (ci * CMP_STRIDE + (CMP_LEN - 1) <= tq), 0.0, -jnp.inf), A)
    ratio = SLC_BLK // CMP_STRIDE
    mj = lax.broadcasted_iota(jnp.int32, (_GW, n_seg), 0)
    mi = lax.broadcasted_iota(jnp.int32, (_GW, n_seg), 1)
    msel_t = ((mi >= ratio * mj - 1) & (mi <= ratio * mj + ratio - 1)).astype(_BF16)
    blk = lax.broadcasted_iota(jnp.int32, (_GW, QB), 0)
    cur = tq // SLC_BLK
    valid = (blk < n_slc) & (blk * SLC_BLK <= tq)
    forced = (blk == 0) | (blk == cur) | (blk == cur - 1)
    for g in range(G):
        hrows = slice(g * DH, (g + 1) * DH)
        s = jnp.dot(kc, qplt_ref[g][...], preferred_element_type=_F32) + cbias
        e, inv = _col_softmax_parts(s)
        cmp_ref[g][...] = jnp.dot(vct[hrows, :], e.astype(_BF16), preferred_element_type=_F32) * inv
        p = e * inv
        pg = p[:, 0:QB]
        for a in range(1, A):
            pg = pg + p[:, a * QB:(a + 1) * QB]
        pg_hi = pg.astype(_BF16)
        pg_lo = (pg - pg_hi.astype(_F32)).astype(_BF16)
        ps = jnp.dot(msel_t, pg_hi, preferred_element_type=_F32) \
            + jnp.dot(msel_t, pg_lo, preferred_element_type=_F32)
        score = jnp.where(valid, jnp.where(forced, jnp.inf, ps), -jnp.inf)
        rank = jnp.zeros((_GW, QB), _F32)
        for i in range(_GW):
            row = score[i:i + 1, :]
            ahead = (row > score) | ((row == score) & (blk > i))
            rank = rank + jnp.where(ahead, 1.0, 0.0)
        sel_ref[g][...] = jnp.where(rank < min(SLC_TOPN, n_slc), 1.0, 0.0)

    for g in range(G):
        m_ref[g][...] = jnp.full((1, R), -jnp.inf, _F32)
        l_ref[g][...] = jnp.zeros((1, R), _F32)
        acc_ref[g][...] = jnp.zeros((DH, R), _F32)
    krow = lax.broadcasted_iota(jnp.int32, (TK, QB), 0)

    def sel_tile(kt, carry):
        k0 = pl.multiple_of(kt * TK, TK)
        kt_b = ksb[pl.ds(k0, TK), :]
        b0 = kt * (TK // 128)
        j0 = kt * (TK // SLC_BLK)
        vt_b = jnp.concatenate([vst[b0 + i] for i in range(TK // 128)], axis=1)
        causal = (k0 + krow) <= tq
        for g in range(G):
            member = jnp.concatenate(
                [jnp.broadcast_to(sel_ref[g][pl.ds(j0 + j, 1), :], (SLC_BLK, QB))
                 for j in range(TK // SLC_BLK)], axis=0)
            bias = _tile_cols(jnp.where(causal & (member > 0.5), 0.0, -jnp.inf), A)
            s = jnp.dot(kt_b, qplt_ref[g][...], preferred_element_type=_F32) + bias
            m_old = m_ref[g][...]
            m_new = jnp.maximum(m_old, jnp.max(s, axis=0, keepdims=True))
            m_safe = jnp.where(m_new == -jnp.inf, 0.0, m_new)
            p = jnp.exp2(s - m_safe)
            alpha = jnp.exp2(m_old - m_safe)
            l_ref[g][...] = alpha * l_ref[g][...] + jnp.sum(p, axis=0, keepdims=True)
            acc_ref[g][...] = alpha * acc_ref[g][...] + jnp.dot(
                vt_b[g * DH:(g + 1) * DH, :], p.astype(_BF16), preferred_element_type=_F32)
            m_ref[g][...] = m_new
        return carry

    lax.fori_loop(0, (t0 + QB + TK - 1) // TK, sel_tile, 0)

    WK = WINDOW + QB
    ws = pl.multiple_of(jnp.clip(t0 - WINDOW, 0, T - WK), QB)
    kw_b = kwb[pl.ds(ws, WK), :]
    wb0 = ws // 128
    vw_b = jnp.concatenate([vwt[wb0 + i] for i in range(WK // 128)], axis=1)
    diff = tq - (ws + lax.broadcasted_iota(jnp.int32, (WK, QB), 0))
    wbias = _tile_cols(jnp.where((diff >= 0) & (diff < WINDOW), 0.0, -jnp.inf), A)
    for g in range(G):
        s = jnp.dot(kw_b, qplt_ref[g][...], preferred_element_type=_F32) + wbias
        e, inv = _col_softmax_parts(s)
        win_ref[g][...] = jnp.dot(vw_b[g * DH:(g + 1) * DH, :], e.astype(_BF16),
                                  preferred_element_type=_F32) * inv
        acc_ref[g][...] = acc_ref[g][...] * (1.0 / jnp.maximum(l_ref[g][...], 1e-30))

    smt = sm_ref[...].T
    g0 = 2 * M_HEADS
    for a in range(A):
        lanes = slice(a * QB, (a + 1) * QB)
        parts = []
        for g in range(G):
            head = g0 + g * A + a
            parts.append(smt[head:head + 1, :] * cmp_ref[g][:, lanes]
                         + smt[head + A_HEADS:head + A_HEADS + 1, :] * acc_ref[g][:, lanes]
                         + smt[head + 2 * A_HEADS:head + 2 * A_HEADS + 1, :] * win_ref[g][:, lanes])
        mix = jnp.concatenate(parts, axis=0)
        az = az_ref[:, a * 256:(a + 1) * 256]
        out_ref[:, a * 256:(a + 1) * 256] = mix.T * (az * jax.nn.sigmoid(az))


def _nsa_prompt_t(q, kv4, kvw, kc, vc, sm, az):
    B, T, _ = q.shape
    n_seg = kc.shape[1]
    n_cmp = n_seg - 1
    n_slc = T // SLC_BLK
    assert T % _SEL_TK == 0 and T >= WINDOW + _QB and n_slc <= _GW and n_seg % 128 == 0
    R = A_GROUP * _QB
    blk = lambda w: pl.BlockSpec((None, _QB, w), lambda b, i: (b, i, 0))
    return pl.pallas_call(
        functools.partial(_nsa_prompt_t_kernel, T, n_cmp, n_slc),
        grid=(B, T // _QB),
        in_specs=[
            blk(1024),
            pl.BlockSpec((None, T, 512), lambda b, i: (b, 0, 1)),
            pl.BlockSpec((None, T, 512), lambda b, i: (b, 0, 0)),
            pl.BlockSpec((None, n_seg, 256), lambda b, i: (b, 0, 0)),
            pl.BlockSpec((None, n_seg, 256), lambda b, i: (b, 0, 0)),
            blk(128), blk(1024),
        ],
        out_specs=blk(1024),
        out_shape=jax.ShapeDtypeStruct((B, T, 1024), _F32),
        scratch_shapes=(
            [pltpu.VMEM((T, 256), _BF16), pltpu.VMEM((T // 128, 256, 128), _BF16),
             pltpu.VMEM((T, 256), _BF16), pltpu.VMEM((T // 128, 256, 128), _BF16),
             pltpu.VMEM((256, n_seg), _BF16)]
            + [pltpu.VMEM((256, R), _BF16)] * A_KV
            + [pltpu.VMEM((_GW, _QB), _F32)] * A_KV
            + [pltpu.VMEM((1, R), _F32)] * (2 * A_KV)
            + [pltpu.VMEM((256 // A_KV, R), _F32)] * (3 * A_KV)
        ),
        compiler_params=pltpu.CompilerParams(
            dimension_semantics=("arbitrary", "arbitrary"), vmem_limit_bytes=_VMEM_LIMIT),
        name="nsa_prompt",
    )(q, kv4, kvw, kc, vc, sm, az)


_CMP_PAGES = 32


def _compress_sample_kernel(P, n_chunks, n_seg, pt_ref, cache_ref, w1_ref, b_ref, w2_ref, seg_ref,
                            kn_ref, kc_ref, vc_ref, stage, lanes, per, sem):
    b = pl.program_id(0)
    ch = pl.program_id(1)
    step = b * n_chunks + ch
    n_steps = pl.num_programs(0) * n_chunks
    slot = step % 2
    rows = P * PAGE_SIZE

    def page_copy(bb, cc, sl, p):
        page = pt_ref[bb, cc * P + p]
        return pltpu.make_async_copy(cache_ref.at[page, pl.ds(0, 4)], stage.at[sl, p], sem.at[sl])

    @pl.when(step == 0)
    def _():
        for p in range(P):
            page_copy(b, ch, slot, p).start()

    @pl.when(step + 1 < n_steps)
    def _():
        nxt = step + 1
        for p in range(P):
            page_copy(nxt // n_chunks, nxt % n_chunks, 1 - slot, p).start()

    pltpu.make_async_copy(cache_ref.at[pl.ds(0, P), pl.ds(0, 4)], stage.at[slot], sem.at[slot]).wait()

    segs = rows // CMP_STRIDE

    for k in range(4):
        for p in range(P):
            lanes[k, p * PAGE_SIZE:(p + 1) * PAGE_SIZE, :] = stage[slot, p, k].T
        per[pl.ds(pl.multiple_of(ch * segs, segs), segs), k * 256:(k + 1) * 256] = _cmp_first_layer(
            lanes.at[k], segs, w1_ref, k // 2)

    @pl.when(ch == n_chunks - 1)
    def _():
        for k in range(4):
            c, gp = k // 2, k % 2
            o_ref = kc_ref if c == 0 else vc_ref
            o_ref[:, gp * 128:(gp + 1) * 128] = _cmp_second_layer(
                per[:, k * 256:(k + 1) * 256], n_seg, c, b_ref, w2_ref, seg_ref, kn_ref)


def _compress_sample(cache_t, page_table, cwp):
    B, n_pages = page_table.shape
    P = _CMP_PAGES if n_pages % _CMP_PAGES == 0 else n_pages
    n_chunks = n_pages // P
    n_seg = n_pages * PAGE_SIZE // CMP_STRIDE
    w1p, bias2, w2p, seg, kn = cwp
    full = lambda a: pl.BlockSpec(a.shape, lambda b, c, pt: (0,) * a.ndim)
    grid_spec = pltpu.PrefetchScalarGridSpec(
        num_scalar_prefetch=1,
        grid=(B, n_chunks),
        in_specs=[pl.BlockSpec(memory_space=pl.ANY),
                  full(w1p), full(bias2), full(w2p), full(seg), full(kn)],
        out_specs=[pl.BlockSpec((None, n_seg, 256), lambda b, c, pt: (b, 0, 0))] * 2,
        scratch_shapes=[
            pltpu.VMEM((2, P, 4, 128, PAGE_SIZE), _F32),
            pltpu.VMEM((4, P * PAGE_SIZE, 128), _F32),
            pltpu.VMEM((n_seg, 1024), _F32),
            pltpu.SemaphoreType.DMA((2,)),
        ])
    return pl.pallas_call(
        functools.partial(_compress_sample_kernel, P, n_chunks, n_seg),
        grid_spec=grid_spec,
        out_shape=[jax.ShapeDtypeStruct((B, n_seg, 256), _F32)] * 2,
        compiler_params=pltpu.CompilerParams(
            dimension_semantics=("arbitrary", "arbitrary"), vmem_limit_bytes=_VMEM_LIMIT),
        name="compress_sample",
    )(page_table, cache_t, w1p, bias2, w2p, seg, kn)


def _merge_heads(o_ref, o_g, g, Ts):
    gm = (lax.broadcasted_iota(jnp.int32, (Ts, 256), 1) // 64) == g
    for a in range(A_GROUP):
        part = jnp.where(gm, o_g[a * Ts:(a + 1) * Ts], 0.0)
        cols = slice(a * 256, (a + 1) * 256)
        if g == 0:
            o_ref[:, cols] = part
        else:
            o_ref[:, cols] += part


def _nsa_sample_front_kernel(Ts, past_len, n_cmp, n_slc, NL, topn,
                             q_ref, kc_ref, vc_ref, wold_ref, wnew_ref,
                             ocmp_ref, owin_ref, idx_ref, qpl_ref):
    G, A = A_KV, A_GROUP
    R = A * Ts
    lane256 = lax.broadcasted_iota(jnp.int32, (Ts, 256), 1)
    q = q_ref[...]
    for g in range(G):
        gm = (lane256 // 64) == g
        qpl_ref[g] = jnp.concatenate(
            [jnp.where(gm, q[:, a * 256:(a + 1) * 256] * 0.125, 0.0) for a in range(A)],
            axis=0).astype(_BF16)
    tq_r = past_len + lax.broadcasted_iota(jnp.int32, (R, 1), 0) % Ts
    tq = past_len + lax.broadcasted_iota(jnp.int32, (Ts, 1), 0)

    n_seg = kc_ref.shape[0]
    kc = kc_ref[...].astype(_BF16)
    vc = vc_ref[...].astype(_BF16)
    ci = lax.broadcasted_iota(jnp.int32, (R, n_seg), 1)
    cmask = (ci < n_cmp) & (ci * CMP_STRIDE + (CMP_LEN - 1) <= tq_r)
    ratio = SLC_BLK // CMP_STRIDE
    mi = lax.broadcasted_iota(jnp.int32, (n_seg, NL), 0)
    mj = lax.broadcasted_iota(jnp.int32, (n_seg, NL), 1)
    msel = ((mi >= ratio * mj - 1) & (mi <= ratio * mj + ratio - 1)).astype(_BF16)
    lane = lax.broadcasted_iota(jnp.int32, (Ts, NL), 1)
    cur = tq // SLC_BLK
    valid = (lane < n_slc) & (lane * SLC_BLK <= tq)
    forced = (lane == 0) | (lane == cur) | (lane == cur - 1)
    scores = []
    for g in range(G):
        s = lax.dot_general(qpl_ref[g], kc, (((1,), (1,)), ((), ())), preferred_element_type=_F32)
        p = _softmax_rows(s, cmask)
        _merge_heads(ocmp_ref, jnp.dot(p.astype(_BF16), vc, preferred_element_type=_F32), g, Ts)
        pg = p[0:Ts]
        for a in range(1, A):
            pg = pg + p[a * Ts:(a + 1) * Ts]
        pg_hi = pg.astype(_BF16)
        pg_lo = (pg - pg_hi.astype(_F32)).astype(_BF16)
        ps = jnp.dot(pg_hi, msel, preferred_element_type=_F32) \
            + jnp.dot(pg_lo, msel, preferred_element_type=_F32)
        scores.append(jnp.where(valid, jnp.where(forced, jnp.inf, ps), -jnp.inf))
    score = jnp.concatenate(scores, axis=0)

    lane_f = lax.broadcasted_iota(jnp.int32, (G * Ts, NL), 1).astype(_F32)
    out_lane = lax.broadcasted_iota(jnp.int32, (G * Ts, 128), 1)
    avail = lane_f < n_slc
    picks = jnp.zeros((G * Ts, 128), _F32)
    for k in range(topn):
        mx = jnp.max(jnp.where(avail, score, -jnp.inf), axis=1, keepdims=True)
        pick = jnp.min(jnp.where(avail & (score == mx), lane_f, float(NL)), axis=1, keepdims=True)
        avail = avail & (lane_f != pick)
        picks = jnp.where(out_lane == k, pick, picks)
    idx_ref[...] = picks.astype(jnp.int32)

    wb = wold_ref.shape[0]
    k_old = wold_ref[:, 0:256].astype(_BF16)
    v_old = wold_ref[:, 256:512].astype(_BF16)
    k_new = wnew_ref[:, 0:256]
    v_new = wnew_ref[:, 256:512]
    pw_old = past_len - wb + lax.broadcasted_iota(jnp.int32, (R, wb), 1)
    d_old = tq_r - pw_old
    m_old = (pw_old >= 0) & (d_old >= 0) & (d_old < WINDOW)
    d_new = tq_r - (past_len + lax.broadcasted_iota(jnp.int32, (R, Ts), 1))
    m_new = (d_new >= 0) & (d_new < WINDOW)
    for g in range(G):
        qg = qpl_ref[g]
        s1 = lax.dot_general(qg, k_old, (((1,), (1,)), ((), ())), preferred_element_type=_F32)
        s2 = lax.dot_general(qg.astype(_F32), k_new, (((1,), (1,)), ((), ())),
                             preferred_element_type=_F32)
        s1 = jnp.where(m_old, s1, -jnp.inf)
        s2 = jnp.where(m_new, s2, -jnp.inf)
        mx = jnp.maximum(jnp.max(s1, axis=1, keepdims=True), jnp.max(s2, axis=1, keepdims=True))
        mx = jnp.where(mx == -jnp.inf, 0.0, mx)
        e1 = jnp.where(m_old, jnp.exp(s1 - mx), 0.0)
        e2 = jnp.where(m_new, jnp.exp(s2 - mx), 0.0)
        den = jnp.maximum(jnp.sum(e1, axis=1, keepdims=True) + jnp.sum(e2, axis=1, keepdims=True), 1e-30)
        o = (jnp.dot(e1.astype(_BF16), v_old, preferred_element_type=_F32)
             + jnp.dot(e2, v_new, preferred_element_type=_F32)) / den
        _merge_heads(owin_ref, o, g, Ts)


def _nsa_sample_front(q, kc, vc, win_old, win_new, past_len):
    B, Ts, _ = q.shape
    n_seg = kc.shape[1]
    n_cmp = (past_len + Ts) // CMP_STRIDE - 1
    assert n_cmp == n_seg - 1 and Ts % 8 == 0 and Ts <= SLC_BLK
    n_slc = past_len // SLC_BLK + 1
    NL = -(-n_slc // 128) * 128
    topn = min(SLC_TOPN, n_slc)
    wb = win_old.shape[1]
    R = A_GROUP * Ts
    b3 = lambda r, w: pl.BlockSpec((None, r, w), lambda b: (b, 0, 0))
    return pl.pallas_call(
        functools.partial(_nsa_sample_front_kernel, Ts, past_len, n_cmp, n_slc, NL, topn),
        grid=(B,),
        in_specs=[b3(Ts, 1024), b3(n_seg, 256), b3(n_seg, 256), b3(wb, 512), b3(Ts, 512)],
        out_specs=[b3(Ts, 1024), b3(Ts, 1024), b3(A_KV * Ts, 128)],
        out_shape=[jax.ShapeDtypeStruct((B, Ts, 1024), _F32), jax.ShapeDtypeStruct((B, Ts, 1024), _F32),
                   jax.ShapeDtypeStruct((B, A_KV * Ts, 128), jnp.int32)],
        scratch_shapes=[pltpu.VMEM((A_KV, R, 256), _BF16)],
        compiler_params=pltpu.CompilerParams(
            dimension_semantics=("arbitrary",), vmem_limit_bytes=_VMEM_LIMIT),
        name="nsa_sample_front",
    )(q, kc, vc, win_old, win_new)


_GATHER_SLOTS = 3


def _nsa_sample_gather_kernel(Ts, n_past_blk, past_len, topn,
                              idx_ref, pt_ref, q_ref, kvn_ref, cache_ref, o_ref, kvbuf, sem):
    b = pl.program_id(0)
    n_it = A_KV * Ts
    bpp = PAGE_SIZE // SLC_BLK
    nk = topn * PAGE_SIZE

    def issue(it, sl):
        g = it // Ts
        for k in range(topn):
            blk = jnp.minimum(idx_ref[b, it * topn + k], n_past_blk - 1)
            page = pt_ref[b, blk // bpp]
            pltpu.make_async_copy(cache_ref.at[page, pl.ds(2, 2), g], kvbuf.at[sl, k], sem.at[sl]).start()

    for i in range(_GATHER_SLOTS - 1):
        issue(i, i)
    lane = lax.broadcasted_iota(jnp.int32, (1, nk), 1)
    new_r = lax.broadcasted_iota(jnp.int32, (1, Ts), 1)

    def body(it, carry):
        sl = it % _GATHER_SLOTS

        @pl.when(it + _GATHER_SLOTS - 1 < n_it)
        def _():
            issue(it + _GATHER_SLOTS - 1, (it + _GATHER_SLOTS - 1) % _GATHER_SLOTS)

        pltpu.make_async_copy(cache_ref.at[pl.ds(0, topn), pl.ds(0, 2), 0], kvbuf.at[sl], sem.at[sl]).wait()

        t = it % Ts
        g = it // Ts
        tq = past_len + t
        pos = jnp.zeros((1, nk), jnp.int32)
        past = jnp.zeros((1, nk), jnp.int32)
        new_sel = jnp.zeros((1, Ts), jnp.int32)
        for k in range(topn):
            blk = idx_ref[b, it * topn + k]
            in_slot = (lane // PAGE_SIZE) == k
            row = lane % PAGE_SIZE
            hit = in_slot & ((row // SLC_BLK) == (blk % bpp))
            pos = jnp.where(in_slot, (blk // bpp) * PAGE_SIZE + row, pos)
            past = jnp.where(hit, jnp.where(blk < n_past_blk, 1, 0), past)
            new_sel = jnp.maximum(new_sel, jnp.where(blk == n_past_blk, 1, 0))
        m1 = (past > 0) & (pos <= tq)
        m2 = (new_sel > 0) & (n_past_blk * SLC_BLK + new_r <= tq)
        q8 = q_ref[it] * 0.125
        kt = jnp.concatenate([kvbuf[sl, k, 0] for k in range(topn)], axis=1).astype(_BF16)
        vt = jnp.concatenate([kvbuf[sl, k, 1] for k in range(topn)], axis=1).astype(_BF16)
        s1 = jnp.dot(q8.astype(_BF16), kt, preferred_element_type=_F32)
        s2 = lax.dot_general(q8, kvn_ref[2 * A_KV + g], (((1,), (1,)), ((), ())),
                             preferred_element_type=_F32)
        s1 = jnp.where(m1, s1, -jnp.inf)
        s2 = jnp.where(m2, s2, -jnp.inf)
        mx = jnp.maximum(jnp.max(s1, axis=1, keepdims=True), jnp.max(s2, axis=1, keepdims=True))
        mx = jnp.where(mx == -jnp.inf, 0.0, mx)
        e1 = jnp.where(m1, jnp.exp(s1 - mx), 0.0)
        e2 = jnp.where(m2, jnp.exp(s2 - mx), 0.0)
        den = jnp.maximum(jnp.sum(e1, axis=1, keepdims=True) + jnp.sum(e2, axis=1, keepdims=True), 1e-30)
        o_ref[it] = (lax.dot_general(e1.astype(_BF16), vt, (((1,), (1,)), ((), ())),
                                     preferred_element_type=_F32)
                     + jnp.dot(e2, kvn_ref[3 * A_KV + g], preferred_element_type=_F32)) / den
        return carry

    lax.fori_loop(0, n_it, body, 0)


def _nsa_sample_gather(idx, page_table, q_rows, kv_new, cache4, past_len):
    B, n_it, _, dh = q_rows.shape
    Ts = n_it // A_KV
    n_past_blk = past_len // SLC_BLK
    topn = idx.shape[1] // n_it
    grid_spec = pltpu.PrefetchScalarGridSpec(
        num_scalar_prefetch=2,
        grid=(B,),
        in_specs=[pl.BlockSpec((None, n_it, 8, dh), lambda b, i, p: (b, 0, 0, 0)),
                  pl.BlockSpec((None, 4 * A_KV, Ts, dh), lambda b, i, p: (b, 0, 0, 0)),
                  pl.BlockSpec(memory_space=pl.ANY)],
        out_specs=pl.BlockSpec((None, n_it, 8, dh), lambda b, i, p: (b, 0, 0, 0)),
        scratch_shapes=[
            pltpu.VMEM((_GATHER_SLOTS, topn, 2, dh, PAGE_SIZE), _F32),
            pltpu.SemaphoreType.DMA((_GATHER_SLOTS,)),
        ])
    return pl.pallas_call(
        functools.partial(_nsa_sample_gather_kernel, Ts, n_past_blk, past_len, topn),
        grid_spec=grid_spec,
        out_shape=jax.ShapeDtypeStruct((B, n_it, 8, dh), _F32),
        compiler_params=pltpu.CompilerParams(
            dimension_semantics=("arbitrary",), vmem_limit_bytes=_VMEM_LIMIT),
        name="nsa_sample_gather",
    )(idx, page_table, q_rows, kv_new, cache4)


def _nsa_combine_kernel(ocmp_ref, osel_ref, owin_ref, sm_ref, az_ref, o_ref):
    sm = sm_ref[...]
    shape = ocmp_ref.shape
    head_lane = lax.broadcasted_iota(jnp.int32, shape, 1) // 64
    g0 = 2 * M_HEADS
    acc = jnp.zeros(shape, _F32)
    for br, ref in enumerate((ocmp_ref, osel_ref, owin_ref)):
        gate = jnp.zeros(shape, _F32)
        for a in range(A_GROUP):
            for g in range(A_KV):
                c = g0 + br * A_HEADS + g * A_GROUP + a
                gate = jnp.where(head_lane == a * A_KV + g, sm[:, c:c + 1], gate)
        acc = acc + gate * ref[...]
    az = az_ref[...]
    o_ref[...] = acc * (az * jax.nn.sigmoid(az))


def _nsa_combine(ocmp, osel, owin, sm, az):
    M, W = ocmp.shape
    full = lambda w: pl.BlockSpec((M, w), lambda i: (0, 0))
    return pl.pallas_call(
        _nsa_combine_kernel,
        grid=(1,),
        in_specs=[full(W), full(W), full(W), full(128), full(W)],
        out_specs=full(W),
        out_shape=jax.ShapeDtypeStruct((M, W), _F32),
        name="nsa_combine",
    )(ocmp, osel, owin, sm, az)


def _mixer_inputs(x, norm_w, pw, m_width, a_width, a_dh):
    B, T, D = x.shape
    w_main, w_small, seg, hw, hm, groups, norm_tiles = pw
    M = B * T
    tm = next((t for t in (1024, 512) if M % t == 0), M)
    a, q, az, kv4, kvw, sm, *kv4t = _proj_in(x.reshape(M, D), norm_w, w_main, w_small, seg, hw, hm,
                                             groups, norm_tiles, tm, T)
    r3 = lambda v: v.reshape(B, T, v.shape[-1])
    kv_rows = r3(kv4).reshape(B, T, 4, A_KV, a_dh)
    if kv4t:
        kv_rows = jnp.transpose(kv4t[0].reshape(B, 4, A_KV, a_dh, T), (0, 4, 1, 2, 3))
    return r3(a), r3(sm), r3(q), r3(kv4), r3(kvw), r3(az), kv_rows


def _mixer_output(x, m_out, a_out, p, ow):
    B, T, D = x.shape
    M = B * T
    tm = 256 if M % 256 == 0 else M
    y = _out_proj(x.reshape(M, D), m_out.reshape(M, -1), a_out.reshape(M, -1), p.reshape(M, -1), *ow, tm)
    return y.reshape(B, T, D)


def kernel(x_prompt, x_sample, cache_nsa_kv, cache_win_kv, state_mlstm_C, state_mlstm_n,
           state_mlstm_m, state_mlstm_conv, page_table, p_prompt, p_sample, norm_w, w_in,
           m_conv_w, m_conv_b, m_wq, m_wk, m_wv, m_b_i, m_b_f, m_norm_w, a_q_norm, a_k_norm,
           cmp_pe, cmp_w1, cmp_w2, w_out, ple_proj, ple_norm, ple_gate):
    xp, xs = x_prompt, x_sample
    B, T, D = xp.shape
    depth = w_in.shape[0]
    m_width = m_conv_w.shape[-1]
    a_width = D - m_width
    a_dh = a_q_norm.shape[-1]
    dh = m_width // M_HEADS
    per_layer = []
    for i in range(depth):
        mw = (m_conv_w[i], m_conv_b[i], m_wq[i], m_wk[i], m_wv[i], m_b_i[i], m_b_f[i], m_norm_w[i])
        cw = (cmp_pe[i], cmp_w1[i], cmp_w2[i], a_k_norm[i, 0])
        w_out_i = jnp.concatenate(
            [w_out[i][:m_width], _heads_to_agd(w_out[i][m_width:].T, a_dh).T], axis=0)
        ow = (w_out_i, ple_proj[i], ple_norm[i], ple_gate[i])
        pw = _prep_proj_weights(w_in[i], a_q_norm[i], a_k_norm[i], m_width, a_width, a_dh)
        cwp = _prep_cmp_weights(*cw)
        a3, sm, q, kv4, kvw, az, kv_p = _mixer_inputs(xp, norm_w[i], pw, m_width, a_width, a_dh)
        L = _MLSTM_L if T % _MLSTM_L == 0 else (M_CHUNK if T % M_CHUNK == 0 else T)
        m_out, conv_p, C_p, n_p, m_p = _mlstm(
            a3, sm, jnp.zeros((B, M_CONV - 1, m_width), _F32),
            jnp.zeros((B, M_HEADS, dh, dh), _F32), jnp.zeros((B, M_HEADS, dh), _F32),
            jnp.full((B, M_HEADS), -jnp.inf, _F32), *mw, L)
        kc, vc = _compress_prompt(kv4, cwp)
        a_out = _nsa_prompt_t(q, kv4, kvw, kc, vc, sm, az)
        xp_new = _mixer_output(xp, m_out, a_out, p_prompt[i], ow)
        win_p = kvw[:, -min(WINDOW, T):].reshape(B, -1, 2, A_KV, a_dh)
        a3, sm, q, kv4, kvw, az, kv_s = _mixer_inputs(xs, norm_w[i], pw, m_width, a_width, a_dh)
        Bd, Ts = xs.shape[:2]
        Ls = _MLSTM_L if Ts % _MLSTM_L == 0 else (M_CHUNK if Ts % M_CHUNK == 0 else Ts)
        m_out, conv_s, C_s, n_s, m_s = _mlstm(
            a3, sm, state_mlstm_conv[i], state_mlstm_C[i], state_mlstm_n[i], state_mlstm_m[i], *mw, Ls)
        past_len = page_table.shape[1] * PAGE_SIZE
        cache_l = cache_nsa_kv[i]
        n_pool = cache_l.shape[0]
        wb = cache_win_kv.shape[2]
        cache_t = jnp.transpose(cache_l, (0, 2, 3, 4, 1))
        kc, vc = _compress_sample(cache_t.reshape(n_pool, 8, 2 * a_dh, PAGE_SIZE), page_table, cwp)
        ocmp, owin, idx = _nsa_sample_front(
            q, kc, vc, cache_win_kv[i].reshape(Bd, wb, 2 * A_KV * a_dh), kvw, past_len)
        topn = min(SLC_TOPN, past_len // SLC_BLK + 1)
        q_rows = q.reshape(Bd, Ts, A_GROUP, A_KV, a_dh).transpose(0, 3, 1, 2, 4)
        q_rows = jnp.pad(q_rows, ((0, 0), (0, 0), (0, 0), (0, 8 - A_GROUP), (0, 0)))
        kv_new = kv4.reshape(Bd, Ts, 4 * A_KV, a_dh).transpose(0, 2, 1, 3)
        osel = _nsa_sample_gather(
            idx[:, :, :topn].reshape(Bd, -1), page_table, q_rows.reshape(Bd, A_KV * Ts, 8, a_dh), kv_new,
            cache_t, past_len)
        osel = osel.reshape(Bd, A_KV, Ts, 8, a_dh)[:, :, :, :A_GROUP].transpose(0, 2, 3, 1, 4)
        Ms = Bd * Ts
        a_out = _nsa_combine(ocmp.reshape(Ms, -1), osel.reshape(Ms, -1), owin.reshape(Ms, -1),
                             sm.reshape(Ms, -1), az.reshape(Ms, -1)).reshape(Bd, Ts, -1)
        xs_new = _mixer_output(xs, m_out, a_out, p_sample[i], ow)
        win_s = jnp.concatenate(
            [cache_win_kv[i], kvw.reshape(Bd, Ts, 2, A_KV, a_dh)], axis=1)[:, -wb:]
        per_layer.append((kv_p, kv_s, win_p, win_s, C_p, n_p, m_p, conv_p, C_s, n_s, m_s, conv_s))
        xp, xs = xp_new, xs_new
    (kv_p, kv_s, win_p, win_s, C_p, n_p, m_p, conv_p,
     C_s, n_s, m_s, conv_s) = [jnp.stack(a, axis=0) for a in zip(*per_layer)]
    return (xp, xs, kv_p, kv_s, win_p, win_s, C_p, n_p, m_p, conv_p, C_s, n_s, m_s, conv_s)
```

```python
import functools

import jax
import jax.numpy as jnp
from jax import lax
from jax.experimental import pallas as pl
from jax.experimental.pallas import tpu as pltpu

NORM_EPS = 1e-6
M_HEADS = 4
M_CONV = 4
M_CHUNK = 64
A_HEADS = 16
A_KV = 4
A_GROUP = A_HEADS // A_KV
CMP_STRIDE = 16
CMP_LEN = 2 * CMP_STRIDE
SLC_BLK = 64
SLC_TOPN = 16
WINDOW = 512
PAGE_SIZE = 128

_BF16 = jnp.bfloat16
_F32 = jnp.float32
_VMEM_LIMIT = 56 * 1024 * 1024


def _dot(a, b):
    return jnp.dot(a.astype(_BF16), b.astype(_BF16), preferred_element_type=_F32)


_PT = 512


def _proj_in_kernel(groups, norm_tiles, x_ref, nw_ref, w_ref, wsm_ref, seg_ref, hw_ref, hm_ref,
                    a_ref, q_ref, az_ref, kv4_ref, kvw_ref, sm_ref, *rest):
    xn_ref = rest[-1]
    t_refs = (None, None, None, rest[0] if len(rest) == 2 else None, None)
    j = pl.program_id(1)

    @pl.when(j == 0)
    def _():
        x = x_ref[...]
        ms = jnp.mean(x * x, axis=-1, keepdims=True)
        xn = x * lax.rsqrt(ms + NORM_EPS) * nw_ref[...]
        xn_ref[...] = xn.astype(_BF16)
        sm = jnp.dot(xn_ref[...], wsm_ref[...], preferred_element_type=_F32)
        lane = lax.broadcasted_iota(jnp.int32, sm.shape, 1)
        is_gate = (lane >= 2 * M_HEADS) & (lane < 2 * M_HEADS + 3 * A_HEADS)
        sm_ref[...] = jnp.where(is_gate, jax.nn.sigmoid(sm), sm)

    y = jnp.dot(xn_ref[...], w_ref[...], preferred_element_type=_F32)
    outs = (a_ref, q_ref, az_ref, kv4_ref, kvw_ref)

    def emit(o_ref, t_ref, val):
        o_ref[...] = val
        if t_ref is not None:
            t_ref[...] = val.T

    for (start, n), o_ref, t_ref in zip(groups, outs, t_refs):
        plain = [t for t in range(start, start + n) if t not in norm_tiles]
        normed = [t for t in range(start, start + n) if t in norm_tiles]
        if plain:
            cond = functools.reduce(jnp.logical_or, [j == t for t in plain])

            @pl.when(cond)
            def _(o_ref=o_ref, t_ref=t_ref):
                emit(o_ref, t_ref, y)

        if normed:
            cond = functools.reduce(jnp.logical_or, [j == t for t in normed])

            @pl.when(cond)
            def _(o_ref=o_ref, t_ref=t_ref):
                ms = jnp.dot((y * y).astype(_BF16), seg_ref[...], preferred_element_type=_F32)
                yn = y * lax.rsqrt(ms + NORM_EPS) * hw_ref[...]
                emit(o_ref, t_ref, jnp.where(hm_ref[...] > 0, yn, y))


def _proj_in(x2d, norm_w, w_main, w_small, seg, hw, hm, groups, norm_tiles, tm, seq_len):
    M, D = x2d.shape
    n_tiles = w_main.shape[1] // _PT
    widths = [n * _PT for _, n in groups]

    def out_map(start, n):
        return lambda i, j: (i, jnp.clip(j - start, 0, n - 1))

    out_specs = [pl.BlockSpec((tm, _PT), out_map(s, n)) for s, n in groups]
    out_specs.append(pl.BlockSpec((tm, 128), lambda i, j: (i, 0)))
    out_shape = [jax.ShapeDtypeStruct((M, w), _F32) for w in widths]
    out_shape.append(jax.ShapeDtypeStruct((M, 128), _F32))
    if seq_len % tm == 0:
        tpb = seq_len // tm
        s4, n4 = groups[3]
        out_specs.append(pl.BlockSpec(
            (None, _PT, tm), lambda i, j: (i // tpb, jnp.clip(j - s4, 0, n4 - 1), i % tpb)))
        out_shape.append(jax.ShapeDtypeStruct((M // seq_len, widths[3], seq_len), _F32))
    return pl.pallas_call(
        functools.partial(_proj_in_kernel, groups, norm_tiles),
        grid=(M // tm, n_tiles),
        in_specs=[
            pl.BlockSpec((tm, D), lambda i, j: (i, 0)),
            pl.BlockSpec((1, D), lambda i, j: (0, 0)),
            pl.BlockSpec((D, _PT), lambda i, j: (0, j)),
            pl.BlockSpec((D, 128), lambda i, j: (0, 0)),
            pl.BlockSpec((_PT, _PT), lambda i, j: (0, 0)),
            pl.BlockSpec((None, 1, _PT), lambda i, j: (j, 0, 0)),
            pl.BlockSpec((None, 1, _PT), lambda i, j: (j, 0, 0)),
        ],
        out_specs=out_specs,
        out_shape=out_shape,
        scratch_shapes=[pltpu.VMEM((tm, D), _BF16)],
        compiler_params=pltpu.CompilerParams(
            dimension_semantics=("arbitrary", "arbitrary"), vmem_limit_bytes=_VMEM_LIMIT),
        name="proj_in",
    )(x2d, norm_w.reshape(1, D), w_main, w_small, seg, hw, hm)


def _heads_to_agd(w, a_dh):
    lead = w.shape[:-1]
    return jnp.swapaxes(w.reshape(lead + (A_KV, A_GROUP, a_dh)), -3, -2).reshape(w.shape)


def _prep_proj_weights(w_in, a_q_norm, a_k_norm, m_width, a_width, a_dh):
    kvw = A_KV * a_dh
    sizes = (m_width, m_width, m_width, M_HEADS, M_HEADS, a_width, 6 * kvw, 3 * A_HEADS, a_width)
    offs = [0]
    for s in sizes:
        offs.append(offs[-1] + s)
    u0, o0, z0, i0, f0, q0, kv0, g0, az0, end = offs
    assert end == w_in.shape[1]
    w_main = jnp.concatenate([
        w_in[:, u0:i0], _heads_to_agd(w_in[:, q0:kv0], a_dh), _heads_to_agd(w_in[:, az0:end], a_dh),
        w_in[:, kv0:kv0 + 4 * kvw], w_in[:, kv0 + 4 * kvw:g0]], axis=1).astype(_BF16)
    n_small = 2 * M_HEADS + 3 * A_HEADS
    w_small = jnp.concatenate([
        w_in[:, i0:q0], w_in[:, g0:az0],
        jnp.zeros((w_in.shape[0], 128 - n_small), w_in.dtype)], axis=1).astype(_BF16)
    widths = (3 * m_width, a_width, a_width, 4 * kvw, 2 * kvw)
    groups, start = [], 0
    for w in widths:
        assert w % _PT == 0
        groups.append((start, w // _PT))
        start += w // _PT
    n_tiles = start
    hw = jnp.zeros((n_tiles * _PT,), _F32)
    hm = jnp.zeros((n_tiles * _PT,), _F32)
    qs = groups[1][0] * _PT
    hw = hw.at[qs:qs + a_width].set(jnp.tile(a_q_norm, a_width // a_dh))
    hm = hm.at[qs:qs + a_width].set(1.0)
    k4 = groups[3][0] * _PT
    hw = hw.at[k4 + 2 * kvw:k4 + 3 * kvw].set(jnp.tile(a_k_norm[1], A_KV))
    hm = hm.at[k4 + 2 * kvw:k4 + 3 * kvw].set(1.0)
    kw = groups[4][0] * _PT
    hw = hw.at[kw:kw + kvw].set(jnp.tile(a_k_norm[2], A_KV))
    hm = hm.at[kw:kw + kvw].set(1.0)
    assert kvw * 2 == _PT and _PT % a_dh == 0
    norm_tiles = tuple(range(groups[1][0], groups[1][0] + groups[1][1])) + (
        (k4 + 2 * kvw) // _PT, kw // _PT)
    r = jnp.arange(_PT) // a_dh
    seg = ((r[:, None] == r[None, :]).astype(_F32) / a_dh).astype(_BF16)
    return (w_main, w_small, seg, hw.reshape(n_tiles, 1, _PT), hm.reshape(n_tiles, 1, _PT),
            tuple(groups), norm_tiles)


def _out_proj_kernel(mw, x_ref, m_ref, a_ref, p_ref, wo_ref, pp_ref, pn_ref, pg_ref, o_ref):
    h = x_ref[...] + _dot(m_ref[...], wo_ref[0:mw, :]) + _dot(a_ref[...], wo_ref[mw:, :])
    ms = jnp.mean(h * h, axis=-1, keepdims=True)
    hn = h * lax.rsqrt(ms + NORM_EPS) * pn_ref[...]
    gate = jax.nn.sigmoid(_dot(hn, pg_ref[...]))
    o_ref[...] = h + gate * _dot(p_ref[...], pp_ref[...])


def _out_proj(x2d, m_out, a_out, p2d, w_out, ple_proj, ple_norm, ple_gate, tm):
    M, D = x2d.shape
    mw, aw, dp = m_out.shape[1], a_out.shape[1], p2d.shape[1]

    def const(shape):
        return pl.BlockSpec(shape, lambda i: (0, 0), pipeline_mode=pl.Buffered(1))

    return pl.pallas_call(
        functools.partial(_out_proj_kernel, mw),
        grid=(M // tm,),
        in_specs=[
            pl.BlockSpec((tm, D), lambda i: (i, 0)),
            pl.BlockSpec((tm, mw), lambda i: (i, 0)),
            pl.BlockSpec((tm, aw), lambda i: (i, 0)),
            pl.BlockSpec((tm, dp), lambda i: (i, 0)),
            const((mw + aw, D)),
            const((dp, D)),
            const((1, D)),
            const((D, D)),
        ],
        out_specs=pl.BlockSpec((tm, D), lambda i: (i, 0)),
        out_shape=jax.ShapeDtypeStruct((M, D), _F32),
        compiler_params=pltpu.CompilerParams(
            dimension_semantics=("arbitrary",), vmem_limit_bytes=_VMEM_LIMIT),
        name="out_proj",
    )(x2d, m_out, a_out, p2d, w_out.astype(_BF16), ple_proj.astype(_BF16),
      ple_norm.reshape(1, D), ple_gate.astype(_BF16))


_MLSTM_L = 256
_CONV_PAD = 8


def _mlstm_kernel(L, dh, n_steps,
                  u_ref, o_ref, z_ref, sm_ref, cbuf_ref, c0_ref, n0_ref, m0_ref,
                  wq_ref, wk_ref, wv_ref, cw_ref, cb_ref, bi_ref, bf_ref, nw_ref,
                  out_ref, cout_ref, nout_ref, mout_ref, convout_ref,
                  xbuf, c_s, n_s, m_s):
    s = pl.program_id(1)
    tail = M_CONV - 1

    @pl.when(s == 0)
    def _():
        c_s[...] = c0_ref[...]
        n_s[...] = n0_ref[...]
        m_s[...] = m0_ref[...]
        xbuf[_CONV_PAD - tail:_CONV_PAD, :] = cbuf_ref[...]

    u_all = u_ref[...]
    xbuf[_CONV_PAD:_CONV_PAD + L, :] = u_all
    c = cb_ref[...]
    for j in range(M_CONV):
        c = c + xbuf[_CONV_PAD - tail + j:_CONV_PAD - tail + j + L, :] * cw_ref[j:j + 1, :]
    xbuf[_CONV_PAD - tail:_CONV_PAD, :] = u_all[L - tail:L, :]
    ch_all = c * jax.nn.sigmoid(c)

    sm = sm_ref[...]
    ii = lax.broadcasted_iota(jnp.int32, (L, L), 0)
    jj = lax.broadcasted_iota(jnp.int32, (L, L), 1)
    eye = ii == jj
    causal = jj <= ii

    for h in range(M_HEADS):
        cols = slice(h * dh, (h + 1) * dh)
        u = u_all[:, cols]
        ch = ch_all[:, cols]
        q = _dot(ch, wq_ref[h])
        k = _dot(ch, wk_ref[h]) * (dh ** -0.5)
        v = _dot(u, wv_ref[h])

        li_col = sm[:, h:h + 1] + bi_ref[h][:, 0:1]
        f_in = sm[:, M_HEADS + h:M_HEADS + h + 1] + bf_ref[h][:, 0:1]
        lf_col = jnp.minimum(f_in, 0.0) - jnp.log(1.0 + jnp.exp(-jnp.abs(f_in)))

        lf_row = jnp.sum(jnp.where(eye, lf_col, 0.0), axis=0, keepdims=True)
        b_col = jnp.sum(jnp.where(causal, lf_row, 0.0), axis=1, keepdims=True)

        m_prev = m_s[h][0:1, 0:1]
        c_prev = c_s[h]
        n_prev = n_s[h]
        g_col = li_col - b_col
        g_row = jnp.sum(jnp.where(eye, g_col, 0.0), axis=0, keepdims=True)
        gm = jnp.where(causal, g_row, -jnp.inf)
        big_m = jnp.maximum(m_prev, jnp.max(gm, axis=1, keepdims=True))
        m_t = b_col + big_m
        qb = q.astype(_BF16)
        sc = lax.dot_general(qb, k.astype(_BF16), (((1,), (1,)), ((), ())),
                             preferred_element_type=_F32) * jnp.exp(gm - big_m)
        a_col = jnp.exp(m_prev - big_m)
        num = _dot(sc, v) + a_col * _dot(qb, c_prev)
        den = jnp.sum(sc, axis=1, keepdims=True) + a_col * jnp.sum(q * n_prev, axis=1, keepdims=True)
        hh = num / jnp.maximum(jnp.abs(den), jnp.exp(-m_t))

        b_end = b_col[L - 1:L, :]
        logw = b_end - b_col + li_col
        m_new = jnp.maximum(b_end + m_prev, jnp.max(logw, axis=0, keepdims=True))
        w_col = jnp.exp(logw - m_new)
        decay = jnp.exp(b_end + m_prev - m_new)
        c_s[h] = decay * c_prev + lax.dot_general(
            k.astype(_BF16), (w_col * v).astype(_BF16), (((0,), (0,)), ((), ())),
            preferred_element_type=_F32)
        n_s[h] = decay * n_prev + jnp.sum(w_col * k, axis=0, keepdims=True)
        m_s[h] = jnp.broadcast_to(m_new, (1, 128))

        hg = hh * jax.nn.sigmoid(o_ref[:, cols])
        hn = hg * lax.rsqrt(jnp.mean(hg * hg, axis=-1, keepdims=True) + NORM_EPS) * nw_ref[h]
        zz = z_ref[:, cols]
        out_ref[:, cols] = (hn * (zz * jax.nn.sigmoid(zz))).astype(out_ref.dtype)

    @pl.when(s == n_steps - 1)
    def _():
        cout_ref[...] = c_s[...]
        nout_ref[...] = n_s[...]
        mout_ref[...] = m_s[...]
        convout_ref[...] = u_all[L - tail:L, :]


def _mlstm(a3, sm, conv_buf, C0, n0, m0, conv_w, conv_b, wq, wk, wv, b_i, b_f, norm_w, L):
    B, T, mw3 = a3.shape
    mw = mw3 // 3
    dh = mw // M_HEADS
    tail = M_CONV - 1
    assert T % L == 0 and L >= tail and (L % 8 == 0)
    n_steps = T // L
    H = M_HEADS

    def lane_b(vec):
        return jnp.broadcast_to(vec.astype(_F32)[:, None, None], (H, 1, 128))

    m0b = jnp.broadcast_to(m0.astype(_F32)[:, :, None, None], (B, H, 1, 128))
    blk = lambda off: pl.BlockSpec((None, L, mw), lambda b, s: (b, s, off))
    full = lambda *shape: pl.BlockSpec(shape, lambda b, s: (0,) * len(shape))
    st = lambda r, c: pl.BlockSpec((None, H, r, c), lambda b, s: (b, 0, 0, 0))
    out, C, n, m, conv = pl.pallas_call(
        functools.partial(_mlstm_kernel, L, dh, n_steps),
        grid=(B, n_steps),
        in_specs=[
            blk(0), blk(1), blk(2),
            pl.BlockSpec((None, L, 128), lambda b, s: (b, s, 0)),
            pl.BlockSpec((None, tail, mw), lambda b, s: (b, 0, 0)),
            st(dh, dh), st(1, dh), st(1, 128),
            full(H, dh, dh), full(H, dh, dh), full(H, dh, dh),
            full(M_CONV, mw), full(1, mw),
            full(H, 1, 128), full(H, 1, 128), full(H, 1, dh),
        ],
        out_specs=[
            pl.BlockSpec((None, L, mw), lambda b, s: (b, s, 0)),
            st(dh, dh), st(1, dh), st(1, 128),
            pl.BlockSpec((None, tail, mw), lambda b, s: (b, 0, 0)),
        ],
        out_shape=[
            jax.ShapeDtypeStruct((B, T, mw), _BF16),
            jax.ShapeDtypeStruct((B, H, dh, dh), _F32),
            jax.ShapeDtypeStruct((B, H, 1, dh), _F32),
            jax.ShapeDtypeStruct((B, H, 1, 128), _F32),
            jax.ShapeDtypeStruct((B, tail, mw), _F32),
        ],
        scratch_shapes=[
            pltpu.VMEM((_CONV_PAD + L, mw), _F32),
            pltpu.VMEM((H, dh, dh), _F32),
            pltpu.VMEM((H, 1, dh), _F32),
            pltpu.VMEM((H, 1, 128), _F32),
        ],
        compiler_params=pltpu.CompilerParams(
            dimension_semantics=("arbitrary", "arbitrary"), vmem_limit_bytes=_VMEM_LIMIT),
        name="mlstm",
    )(a3, a3, a3, sm, conv_buf, C0, n0.reshape(B, H, 1, dh), m0b,
      wq.astype(_BF16), wk.astype(_BF16), wv.astype(_BF16), conv_w, conv_b.reshape(1, mw),
      lane_b(b_i), lane_b(b_f), norm_w.reshape(H, 1, dh))
    return out, conv, C, n.reshape(B, H, dh), m[:, :, 0, 0]


def _prep_cmp_weights(pe, w1, w2, k_norm):
    a_dh, hid = w1.shape[-2], w1.shape[-1]
    w1r = w1.reshape(2, 2, CMP_STRIDE, a_dh, hid)
    eye2 = jnp.eye(2, dtype=w1.dtype)
    w1p = jnp.einsum('cxjdh,ab->cjadxbh', w1r, eye2).reshape(2, CMP_STRIDE * 2 * a_dh, 2 * 2 * hid)
    bias = jnp.einsum('cxjd,cxjdh->ch', pe.reshape(2, 2, CMP_STRIDE, a_dh), w1r)
    bias2 = jnp.tile(bias, (1, 2)).reshape(2, 1, 2 * hid)
    w2p = jnp.einsum('chd,ab->cahbd', w2, eye2).reshape(2, 2 * hid, 2 * a_dh)
    r = jnp.arange(2 * a_dh) // a_dh
    seg = ((r[:, None] == r[None, :]).astype(_F32) / a_dh).astype(_BF16)
    kn = jnp.tile(k_norm, 2).reshape(1, 2 * a_dh)
    return w1p.astype(_BF16), bias2.astype(_F32), w2p.astype(_BF16), seg, kn.astype(_F32)


def _cmp_first_layer(rows_ref, n_seg, w1_ref, c):
    x = jnp.concatenate(
        [rows_ref[pl.ds(j, n_seg, stride=CMP_STRIDE), :].astype(_BF16)
         for j in range(CMP_STRIDE)], axis=1)
    return jnp.dot(x, w1_ref[c], preferred_element_type=_F32)


def _cmp_second_layer(p, n_seg, c, b_ref, w2_ref, seg_ref, kn_ref):
    hid = p[:, 0:128] + pltpu.roll(p[:, 128:256], n_seg - 1, axis=0) + b_ref[c]
    act = hid * jax.nn.sigmoid(hid)
    out = jnp.dot(act.astype(_BF16), w2_ref[c], preferred_element_type=_F32)
    if c == 0:
        ms = jnp.dot((out * out).astype(_BF16), seg_ref[...], preferred_element_type=_F32)
        out = out * lax.rsqrt(ms + NORM_EPS) * kn_ref[...]
    return out


def _compress_prompt_kernel(n_seg, r00, r01, r10, r11, w1_ref, b_ref, w2_ref, seg_ref, kn_ref,
                            kc_ref, vc_ref):
    for c, o_ref, refs in ((0, kc_ref, (r00, r01)), (1, vc_ref, (r10, r11))):
        for gp in range(2):
            p = _cmp_first_layer(refs[gp], n_seg, w1_ref, c)
            o_ref[:, gp * 128:(gp + 1) * 128] = _cmp_second_layer(p, n_seg, c, b_ref, w2_ref, seg_ref, kn_ref)


def _compress_prompt(kv4, cwp):
    B, T, W = kv4.shape
    assert W == 1024 and T % (CMP_STRIDE * 8) == 0
    n_seg = T // CMP_STRIDE
    w1p, bias2, w2p, seg, kn = cwp
    full = lambda a: pl.BlockSpec(a.shape, lambda b: (0,) * a.ndim)
    return pl.pallas_call(
        functools.partial(_compress_prompt_kernel, n_seg),
        grid=(B,),
        in_specs=[pl.BlockSpec((None, T, 128), functools.partial(lambda k, b: (b, 0, k), k))
                  for k in range(4)] + [full(w1p), full(bias2), full(w2p), full(seg), full(kn)],
        out_specs=[pl.BlockSpec((None, n_seg, 256), lambda b: (b, 0, 0))] * 2,
        out_shape=[jax.ShapeDtypeStruct((B, n_seg, 256), _F32)] * 2,
        compiler_params=pltpu.CompilerParams(
            dimension_semantics=("arbitrary",), vmem_limit_bytes=_VMEM_LIMIT),
        name="compress_prompt",
    )(kv4, kv4, kv4, kv4, w1p, bias2, w2p, seg, kn)


_QB = 256
_SEL_TK = 512
_GW = 32
_LOG2E = 1.4426950408889634


def _softmax_parts(s):
    m = jnp.max(s, axis=-1, keepdims=True)
    m = jnp.where(m == -jnp.inf, 0.0, m)
    e = jnp.exp(s - m)
    return e, 1.0 / jnp.maximum(jnp.sum(e, axis=-1, keepdims=True), 1e-30)


def _softmax_rows(s, mask):
    e, inv = _softmax_parts(jnp.where(mask, s, -jnp.inf))
    return e * inv


def _col_softmax_parts(s):
    m = jnp.max(s, axis=0, keepdims=True)
    m = jnp.where(m == -jnp.inf, 0.0, m)
    e = jnp.exp2(s - m)
    return e, 1.0 / jnp.maximum(jnp.sum(e, axis=0, keepdims=True), 1e-30)


def _tile_cols(x, n):
    return jnp.concatenate([x] * n, axis=1)


def _nsa_prompt_t_kernel(T, n_cmp, n_slc,
                         q_ref, ks_ref, kw_ref, kc_ref, vc_ref, sm_ref, az_ref, out_ref,
                         ksb, vst, kwb, vwt, vct, *scratch):
    QB, TK, G, A = _QB, _SEL_TK, A_KV, A_GROUP
    R = A * QB
    NB = T // 128
    qplt_ref, sel_ref, m_ref, l_ref, acc_ref, cmp_ref, win_ref = (
        scratch[i * G:(i + 1) * G] for i in range(7))
    DH = 256 // G

    @pl.when(pl.program_id(1) == 0)
    def _():
        for blk in range(NB):
            rows = slice(blk * 128, (blk + 1) * 128)
            ksb[rows, :] = ks_ref[rows, 0:256].astype(_BF16)
            vst[blk] = ks_ref[rows, 256:512].T.astype(_BF16)
            kwb[rows, :] = kw_ref[rows, 0:256].astype(_BF16)
            vwt[blk] = kw_ref[rows, 256:512].T.astype(_BF16)
        vct[...] = vc_ref[...].T.astype(_BF16)

    t0 = pl.program_id(1) * QB
    qt = q_ref[...].T
    for g in range(G):
        qplt_ref[g][...] = jnp.concatenate(
            [qt[a * 256 + g * DH:a * 256 + (g + 1) * DH, :] * (0.125 * _LOG2E) for a in range(A)],
            axis=1).astype(_BF16)
    tq = t0 + lax.broadcasted_iota(jnp.int32, (1, QB), 1)

    n_seg = kc_ref.shape[0]
    kc = kc_ref[...].astype(_BF16)
    ci = lax.broadcasted_iota(jnp.int32, (n_seg, QB), 0)
    cbias = _tile_cols(jnp.where((ci < n_cmp) & (ci * CMP_STRIDE + (CMP_LEN - 1) <= tq), 0.0, -jnp.inf), A)
    ratio = SLC_BLK // CMP_STRIDE
    topn = min(SLC_TOPN, n_slc)
    mj = lax.broadcasted_iota(jnp.int32, (_GW, n_seg), 0)
    mi = lax.broadcasted_iota(jnp.int32, (_GW, n_seg), 1)
    msel_t = ((mi >= ratio * mj - 1) & (mi <= ratio * mj + ratio - 1)).astype(_BF16)
    blk = lax.broadcasted_iota(jnp.int32, (_GW, QB), 0)
    cur = tq // SLC_BLK
    valid = (blk < n_slc) & (blk * SLC_BLK <= tq)
    forced = (blk == 0) | (blk == cur) | (blk == cur - 1)
    for g in range(G):
        hrows = slice(g * DH, (g + 1) * DH)
        s = jnp.dot(kc[:, hrows], qplt_ref[g][...], preferred_element_type=_F32) + cbias
        e, inv = _col_softmax_parts(s)
        cmp_ref[g][...] = jnp.dot(vct[hrows, :], e.astype(_BF16), preferred_element_type=_F32) * inv
        p = e * inv
        pg = p[:, 0:QB]
        for a in range(1, A):
            pg = pg + p[:, a * QB:(a + 1) * QB]
        pg_hi = pg.astype(_BF16)
        pg_lo = (pg - pg_hi.astype(_F32)).astype(_BF16)
        ps = jnp.dot(msel_t, pg_hi, preferred_element_type=_F32) \
            + jnp.dot(msel_t, pg_lo, preferred_element_type=_F32)
        score = jnp.where(valid, jnp.where(forced, jnp.inf, ps), -jnp.inf)
        rank = jnp.zeros((_GW, QB), _F32)
        for i in range(_GW):
            row = score[i:i + 1, :]
            ahead = (row > score) | ((row == score) & (blk > i))
            rank = rank + jnp.where(ahead, 1.0, 0.0)
        sel_ref[g][...] = jnp.where(rank < topn, 1.0, 0.0)

    for g in range(G):
        m_ref[g][...] = jnp.full((1, R), -jnp.inf, _F32)
        l_ref[g][...] = jnp.zeros((1, R), _F32)
        acc_ref[g][...] = jnp.zeros((DH, R), _F32)
    krow = lax.broadcasted_iota(jnp.int32, (TK, QB), 0)

    def sel_tile(kt, carry):
        k0 = pl.multiple_of(kt * TK, TK)
        kt_b = ksb[pl.ds(k0, TK), :]
        b0 = kt * (TK // 128)
        j0 = kt * (TK // SLC_BLK)
        vt_b = jnp.concatenate([vst[b0 + i] for i in range(TK // 128)], axis=1)
        causal = (k0 + krow) <= tq
        for g in range(G):
            member = jnp.concatenate(
                [jnp.broadcast_to(sel_ref[g][pl.ds(j0 + j, 1), :], (SLC_BLK, QB))
                 for j in range(TK // SLC_BLK)], axis=0)
            bias = _tile_cols(jnp.where(causal & (member > 0.5), 0.0, -jnp.inf), A)
            s = jnp.dot(kt_b[:, g * DH:(g + 1) * DH], qplt_ref[g][...],
                        preferred_element_type=_F32) + bias
            m_old = m_ref[g][...]
            m_new = jnp.maximum(m_old, jnp.max(s, axis=0, keepdims=True))
            m_safe = jnp.where(m_new == -jnp.inf, 0.0, m_new)
            p = jnp.exp2(s - m_safe)
            alpha = jnp.exp2(m_old - m_safe)
            l_ref[g][...] = alpha * l_ref[g][...] + jnp.sum(p, axis=0, keepdims=True)
            acc_ref[g][...] = alpha * acc_ref[g][...] + jnp.dot(
                vt_b[g * DH:(g + 1) * DH, :], p.astype(_BF16), preferred_element_type=_F32)
            m_ref[g][...] = m_new
        return carry

    lax.fori_loop(0, (t0 + QB + TK - 1) // TK, sel_tile, 0)

    WK = WINDOW + QB
    ws = pl.multiple_of(jnp.clip(t0 - WINDOW, 0, T - WK), QB)
    kw_b = kwb[pl.ds(ws, WK), :]
    wb0 = ws // 128
    vw_b = jnp.concatenate([vwt[wb0 + i] for i in range(WK // 128)], axis=1)
    diff = tq - (ws + lax.broadcasted_iota(jnp.int32, (WK, QB), 0))
    wbias = _tile_cols(jnp.where((diff >= 0) & (diff < WINDOW), 0.0, -jnp.inf), A)
    for g in range(G):
        s = jnp.dot(kw_b[:, g * DH:(g + 1) * DH], qplt_ref[g][...], preferred_element_type=_F32) + wbias
        e, inv = _col_softmax_parts(s)
        win_ref[g][...] = jnp.dot(vw_b[g * DH:(g + 1) * DH, :], e.astype(_BF16),
                                  preferred_element_type=_F32) * inv
        acc_ref[g][...] = acc_ref[g][...] * (1.0 / jnp.maximum(l_ref[g][...], 1e-30))

    smt = sm_ref[...].T
    g0 = 2 * M_HEADS
    for a in range(A):
        lanes = slice(a * QB, (a + 1) * QB)
        parts = []
        for g in range(G):
            head = g0 + g * A + a
            parts.append(smt[head:head + 1, :] * cmp_ref[g][:, lanes]
                         + smt[head + A_HEADS:head + A_HEADS + 1, :] * acc_ref[g][:, lanes]
                         + smt[head + 2 * A_HEADS:head + 2 * A_HEADS + 1, :] * win_ref[g][:, lanes])
        mix = jnp.concatenate(parts, axis=0)
        az = az_ref[:, a * 256:(a + 1) * 256]
        out_ref[:, a * 256:(a + 1) * 256] = (mix.T * (az * jax.nn.sigmoid(az))).astype(out_ref.dtype)


def _nsa_prompt_t(q, kv4, kvw, kc, vc, sm, az):
    B, T, _ = q.shape
    n_seg = kc.shape[1]
    n_cmp = n_seg - 1
    n_slc = T // SLC_BLK
    assert T % _SEL_TK == 0 and T >= WINDOW + _QB and n_slc <= _GW and n_seg % 128 == 0
    R = A_GROUP * _QB
    blk = lambda w: pl.BlockSpec((None, _QB, w), lambda b, i: (b, i, 0))
    return pl.pallas_call(
        functools.partial(_nsa_prompt_t_kernel, T, n_cmp, n_slc),
        grid=(B, T // _QB),
        in_specs=[
            blk(1024),
            pl.BlockSpec((None, T, 512), lambda b, i: (b, 0, 1)),
            pl.BlockSpec((None, T, 512), lambda b, i: (b, 0, 0)),
            pl.BlockSpec((None, n_seg, 256), lambda b, i: (b, 0, 0)),
            pl.BlockSpec((None, n_seg, 256), lambda b, i: (b, 0, 0)),
            blk(128), blk(1024),
        ],
        out_specs=blk(1024),
        out_shape=jax.ShapeDtypeStruct((B, T, 1024), _BF16),
        scratch_shapes=(
            [pltpu.VMEM((T, 256), _BF16), pltpu.VMEM((T // 128, 256, 128), _BF16),
             pltpu.VMEM((T, 256), _BF16), pltpu.VMEM((T // 128, 256, 128), _BF16),
             pltpu.VMEM((256, n_seg), _BF16)]
            + [pltpu.VMEM((256 // A_KV, R), _BF16)] * A_KV
            + [pltpu.VMEM((_GW, _QB), _F32)] * A_KV
            + [pltpu.VMEM((1, R), _F32)] * (2 * A_KV)
            + [pltpu.VMEM((256 // A_KV, R), _F32)] * (3 * A_KV)
        ),
        compiler_params=pltpu.CompilerParams(
            dimension_semantics=("arbitrary", "arbitrary"), vmem_limit_bytes=_VMEM_LIMIT),
        name="nsa_prompt",
    )(q, kv4, kvw, kc, vc, sm, az)


_CMP_PAGES = 32


def _compress_sample_kernel(P, n_chunks, n_seg, pt_ref, cache_ref, w1_ref, b_ref, w2_ref, seg_ref,
                            kn_ref, kc_ref, vc_ref, stage, lanes, per, sem):
    b = pl.program_id(0)
    ch = pl.program_id(1)
    step = b * n_chunks + ch
    n_steps = pl.num_programs(0) * n_chunks
    slot = step % 2
    rows = P * PAGE_SIZE

    def page_copy(bb, cc, sl, p):
        page = pt_ref[bb, cc * P + p]
        return pltpu.make_async_copy(cache_ref.at[page, pl.ds(0, 4)], stage.at[sl, p], sem.at[sl])

    @pl.when(step == 0)
    def _():
        for p in range(P):
            page_copy(b, ch, slot, p).start()

    @pl.when(step + 1 < n_steps)
    def _():
        nxt = step + 1
        for p in range(P):
            page_copy(nxt // n_chunks, nxt % n_chunks, 1 - slot, p).start()

    pltpu.make_async_copy(cache_ref.at[pl.ds(0, P), pl.ds(0, 4)], stage.at[slot], sem.at[slot]).wait()

    segs = rows // CMP_STRIDE

    for k in range(4):
        for p in range(P):
            lanes[k, p * PAGE_SIZE:(p + 1) * PAGE_SIZE, :] = stage[slot, p, k].T
        per[pl.ds(pl.multiple_of(ch * segs, segs), segs), k * 256:(k + 1) * 256] = _cmp_first_layer(
            lanes.at[k], segs, w1_ref, k // 2)

    @pl.when(ch == n_chunks - 1)
    def _():
        for k in range(4):
            c, gp = k // 2, k % 2
            o_ref = kc_ref if c == 0 else vc_ref
            o_ref[:, gp * 128:(gp + 1) * 128] = _cmp_second_layer(
                per[:, k * 256:(k + 1) * 256], n_seg, c, b_ref, w2_ref, seg_ref, kn_ref)


def _compress_sample(cache_t, page_table, cwp):
    B, n_pages = page_table.shape
    P = _CMP_PAGES if n_pages % _CMP_PAGES == 0 else n_pages
    n_chunks = n_pages // P
    n_seg = n_pages * PAGE_SIZE // CMP_STRIDE
    w1p, bias2, w2p, seg, kn = cwp
    full = lambda a: pl.BlockSpec(a.shape, lambda b, c, pt: (0,) * a.ndim)
    grid_spec = pltpu.PrefetchScalarGridSpec(
        num_scalar_prefetch=1,
        grid=(B, n_chunks),
        in_specs=[pl.BlockSpec(memory_space=pl.ANY),
                  full(w1p), full(bias2), full(w2p), full(seg), full(kn)],
        out_specs=[pl.BlockSpec((None, n_seg, 256), lambda b, c, pt: (b, 0, 0))] * 2,
        scratch_shapes=[
            pltpu.VMEM((2, P, 4, 128, PAGE_SIZE), _F32),
            pltpu.VMEM((4, P * PAGE_SIZE, 128), _F32),
            pltpu.VMEM((n_seg, 1024), _F32),
            pltpu.SemaphoreType.DMA((2,)),
        ])
    return pl.pallas_call(
        functools.partial(_compress_sample_kernel, P, n_chunks, n_seg),
        grid_spec=grid_spec,
        out_shape=[jax.ShapeDtypeStruct((B, n_seg, 256), _F32)] * 2,
        compiler_params=pltpu.CompilerParams(
            dimension_semantics=("arbitrary", "arbitrary"), vmem_limit_bytes=_VMEM_LIMIT),
        name="compress_sample",
    )(page_table, cache_t, w1p, bias2, w2p, seg, kn)


def _merge_heads(o_ref, o_g, g, Ts):
    gm = (lax.broadcasted_iota(jnp.int32, (Ts, 256), 1) // 64) == g
    for a in range(A_GROUP):
        part = jnp.where(gm, o_g[a * Ts:(a + 1) * Ts], 0.0)
        cols = slice(a * 256, (a + 1) * 256)
        if g == 0:
            o_ref[:, cols] = part
        else:
            o_ref[:, cols] += part


def _nsa_sample_front_kernel(Ts, past_len, n_cmp, n_slc, NL, topn,
                             q_ref, kc_ref, vc_ref, wold_ref, wnew_ref,
                             ocmp_ref, owin_ref, idx_ref, qpl_ref):
    G, A = A_KV, A_GROUP
    R = A * Ts
    lane256 = lax.broadcasted_iota(jnp.int32, (Ts, 256), 1)
    q = q_ref[...]
    for g in range(G):
        gm = (lane256 // 64) == g
        qpl_ref[g] = jnp.concatenate(
            [jnp.where(gm, q[:, a * 256:(a + 1) * 256] * 0.125, 0.0) for a in range(A)],
            axis=0).astype(_BF16)
    tq_r = past_len + lax.broadcasted_iota(jnp.int32, (R, 1), 0) % Ts
    tq = past_len + lax.broadcasted_iota(jnp.int32, (Ts, 1), 0)

    n_seg = kc_ref.shape[0]
    kc = kc_ref[...].astype(_BF16)
    vc = vc_ref[...].astype(_BF16)
    ci = lax.broadcasted_iota(jnp.int32, (R, n_seg), 1)
    cmask = (ci < n_cmp) & (ci * CMP_STRIDE + (CMP_LEN - 1) <= tq_r)
    ratio = SLC_BLK // CMP_STRIDE
    mi = lax.broadcasted_iota(jnp.int32, (n_seg, NL), 0)
    mj = lax.broadcasted_iota(jnp.int32, (n_seg, NL), 1)
    msel = ((mi >= ratio * mj - 1) & (mi <= ratio * mj + ratio - 1)).astype(_BF16)
    lane = lax.broadcasted_iota(jnp.int32, (Ts, NL), 1)
    cur = tq // SLC_BLK
    valid = (lane < n_slc) & (lane * SLC_BLK <= tq)
    forced = (lane == 0) | (lane == cur) | (lane == cur - 1)
    scores = []
    for g in range(G):
        s = lax.dot_general(qpl_ref[g], kc, (((1,), (1,)), ((), ())), preferred_element_type=_F32)
        p = _softmax_rows(s, cmask)
        _merge_heads(ocmp_ref, jnp.dot(p.astype(_BF16), vc, preferred_element_type=_F32), g, Ts)
        pg = p[0:Ts]
        for a in range(1, A):
            pg = pg + p[a * Ts:(a + 1) * Ts]
        pg_hi = pg.astype(_BF16)
        pg_lo = (pg - pg_hi.astype(_F32)).astype(_BF16)
        ps = jnp.dot(pg_hi, msel, preferred_element_type=_F32) \
            + jnp.dot(pg_lo, msel, preferred_element_type=_F32)
        scores.append(jnp.where(valid, jnp.where(forced, jnp.inf, ps), -jnp.inf))
    score = jnp.concatenate(scores, axis=0)

    lane_f = lax.broadcasted_iota(jnp.int32, (G * Ts, NL), 1).astype(_F32)
    out_lane = lax.broadcasted_iota(jnp.int32, (G * Ts, 128), 1)
    avail = lane_f < n_slc
    picks = jnp.zeros((G * Ts, 128), _F32)
    for k in range(topn):
        mx = jnp.max(jnp.where(avail, score, -jnp.inf), axis=1, keepdims=True)
        pick = jnp.min(jnp.where(avail & (score == mx), lane_f, float(NL)), axis=1, keepdims=True)
        avail = avail & (lane_f != pick)
        picks = jnp.where(out_lane == k, pick, picks)
    idx_ref[...] = picks.astype(jnp.int32)

    wb = wold_ref.shape[0]
    k_old = wold_ref[:, 0:256].astype(_BF16)
    v_old = wold_ref[:, 256:512].astype(_BF16)
    k_new = wnew_ref[:, 0:256]
    v_new = wnew_ref[:, 256:512]
    pw_old = past_len - wb + lax.broadcasted_iota(jnp.int32, (R, wb), 1)
    d_old = tq_r - pw_old
    m_old = (pw_old >= 0) & (d_old >= 0) & (d_old < WINDOW)
    d_new = tq_r - (past_len + lax.broadcasted_iota(jnp.int32, (R, Ts), 1))
    m_new = (d_new >= 0) & (d_new < WINDOW)
    for g in range(G):
        qg = qpl_ref[g]
        s1 = lax.dot_general(qg, k_old, (((1,), (1,)), ((), ())), preferred_element_type=_F32)
        s2 = lax.dot_general(qg.astype(_F32), k_new, (((1,), (1,)), ((), ())),
                             preferred_element_type=_F32)
        s1 = jnp.where(m_old, s1, -jnp.inf)
        s2 = jnp.where(m_new, s2, -jnp.inf)
        mx = jnp.maximum(jnp.max(s1, axis=1, keepdims=True), jnp.max(s2, axis=1, keepdims=True))
        mx = jnp.where(mx == -jnp.inf, 0.0, mx)
        e1 = jnp.where(m_old, jnp.exp(s1 - mx), 0.0)
        e2 = jnp.where(m_new, jnp.exp(s2 - mx), 0.0)
        den = jnp.maximum(jnp.sum(e1, axis=1, keepdims=True) + jnp.sum(e2, axis=1, keepdims=True), 1e-30)
        o = (jnp.dot(e1.astype(_BF16), v_old, preferred_element_type=_F32)
             + jnp.dot(e2, v_new, preferred_element_type=_F32)) / den
        _merge_heads(owin_ref, o, g, Ts)


def _nsa_sample_front(q, kc, vc, win_old, win_new, past_len):
    B, Ts, _ = q.shape
    n_seg = kc.shape[1]
    n_cmp = (past_len + Ts) // CMP_STRIDE - 1
    assert n_cmp == n_seg - 1 and Ts % 8 == 0 and Ts <= SLC_BLK
    n_slc = past_len // SLC_BLK + 1
    NL = -(-n_slc // 128) * 128
    topn = min(SLC_TOPN, n_slc)
    wb = win_old.shape[1]
    R = A_GROUP * Ts
    b3 = lambda r, w: pl.BlockSpec((None, r, w), lambda b: (b, 0, 0))
    return pl.pallas_call(
        functools.partial(_nsa_sample_front_kernel, Ts, past_len, n_cmp, n_slc, NL, topn),
        grid=(B,),
        in_specs=[b3(Ts, 1024), b3(n_seg, 256), b3(n_seg, 256), b3(wb, 512), b3(Ts, 512)],
        out_specs=[b3(Ts, 1024), b3(Ts, 1024), b3(A_KV * Ts, 128)],
        out_shape=[jax.ShapeDtypeStruct((B, Ts, 1024), _F32), jax.ShapeDtypeStruct((B, Ts, 1024), _F32),
                   jax.ShapeDtypeStruct((B, A_KV * Ts, 128), jnp.int32)],
        scratch_shapes=[pltpu.VMEM((A_KV, R, 256), _BF16)],
        compiler_params=pltpu.CompilerParams(
            dimension_semantics=("arbitrary",), vmem_limit_bytes=_VMEM_LIMIT),
        name="nsa_sample_front",
    )(q, kc, vc, win_old, win_new)


_GATHER_SLOTS = 3


def _nsa_sample_gather_kernel(Ts, n_past_blk, past_len, topn,
                              idx_ref, pt_ref, q_ref, kvn_ref, cache_ref, o_ref, kvbuf, sem):
    b = pl.program_id(0)
    n_it = A_KV * Ts
    bpp = PAGE_SIZE // SLC_BLK
    nk = topn * PAGE_SIZE

    def issue(it, sl):
        g = it // Ts
        for k in range(topn):
            blk = jnp.minimum(idx_ref[b, it * topn + k], n_past_blk - 1)
            page = pt_ref[b, blk // bpp]
            pltpu.make_async_copy(cache_ref.at[page, pl.ds(2, 2), g], kvbuf.at[sl, k], sem.at[sl]).start()

    for i in range(_GATHER_SLOTS - 1):
        issue(i, i)
    lane = lax.broadcasted_iota(jnp.int32, (1, nk), 1)
    new_r = lax.broadcasted_iota(jnp.int32, (1, Ts), 1)

    def body(it, carry):
        sl = it % _GATHER_SLOTS

        @pl.when(it + _GATHER_SLOTS - 1 < n_it)
        def _():
            issue(it + _GATHER_SLOTS - 1, (it + _GATHER_SLOTS - 1) % _GATHER_SLOTS)

        pltpu.make_async_copy(cache_ref.at[pl.ds(0, topn), pl.ds(0, 2), 0], kvbuf.at[sl], sem.at[sl]).wait()

        t = it % Ts
        g = it // Ts
        tq = past_len + t
        pos = jnp.zeros((1, nk), jnp.int32)
        past = jnp.zeros((1, nk), jnp.int32)
        new_sel = jnp.zeros((1, Ts), jnp.int32)
        for k in range(topn):
            blk = idx_ref[b, it * topn + k]
            in_slot = (lane // PAGE_SIZE) == k
            row = lane % PAGE_SIZE
            hit = in_slot & ((row // SLC_BLK) == (blk % bpp))
            pos = jnp.where(in_slot, (blk // bpp) * PAGE_SIZE + row, pos)
            past = jnp.where(hit, jnp.where(blk < n_past_blk, 1, 0), past)
            new_sel = jnp.maximum(new_sel, jnp.where(blk == n_past_blk, 1, 0))
        m1 = (past > 0) & (pos <= tq)
        m2 = (new_sel > 0) & (n_past_blk * SLC_BLK + new_r <= tq)
        q8 = q_ref[it] * 0.125
        kt = jnp.concatenate([kvbuf[sl, k, 0] for k in range(topn)], axis=1).astype(_BF16)
        vt = jnp.concatenate([kvbuf[sl, k, 1] for k in range(topn)], axis=1).astype(_BF16)
        s1 = jnp.dot(q8.astype(_BF16), kt, preferred_element_type=_F32)
        s2 = lax.dot_general(q8, kvn_ref[2 * A_KV + g], (((1,), (1,)), ((), ())),
                             preferred_element_type=_F32)
        s1 = jnp.where(m1, s1, -jnp.inf)
        s2 = jnp.where(m2, s2, -jnp.inf)
        mx = jnp.maximum(jnp.max(s1, axis=1, keepdims=True), jnp.max(s2, axis=1, keepdims=True))
        mx = jnp.where(mx == -jnp.inf, 0.0, mx)
        e1 = jnp.where(m1, jnp.exp(s1 - mx), 0.0)
        e2 = jnp.where(m2, jnp.exp(s2 - mx), 0.0)
        den = jnp.maximum(jnp.sum(e1, axis=1, keepdims=True) + jnp.sum(e2, axis=1, keepdims=True), 1e-30)
        o_ref[it] = (lax.dot_general(e1.astype(_BF16), vt, (((1,), (1,)), ((), ())),
                                     preferred_element_type=_F32)
                     + jnp.dot(e2, kvn_ref[3 * A_KV + g], preferred_element_type=_F32)) / den
        return carry

    lax.fori_loop(0, n_it, body, 0)


def _nsa_sample_gather(idx, page_table, q_rows, kv_new, cache4, past_len):
    B, n_it, _, dh = q_rows.shape
    Ts = n_it // A_KV
    n_past_blk = past_len // SLC_BLK
    topn = idx.shape[1] // n_it
    grid_spec = pltpu.PrefetchScalarGridSpec(
        num_scalar_prefetch=2,
        grid=(B,),
        in_specs=[pl.BlockSpec((None, n_it, 8, dh), lambda b, i, p: (b, 0, 0, 0)),
                  pl.BlockSpec((None, 4 * A_KV, Ts, dh), lambda b, i, p: (b, 0, 0, 0)),
                  pl.BlockSpec(memory_space=pl.ANY)],
        out_specs=pl.BlockSpec((None, n_it, 8, dh), lambda b, i, p: (b, 0, 0, 0)),
        scratch_shapes=[
            pltpu.VMEM((_GATHER_SLOTS, topn, 2, dh, PAGE_SIZE), _F32),
            pltpu.SemaphoreType.DMA((_GATHER_SLOTS,)),
        ])
    return pl.pallas_call(
        functools.partial(_nsa_sample_gather_kernel, Ts, n_past_blk, past_len, topn),
        grid_spec=grid_spec,
        out_shape=jax.ShapeDtypeStruct((B, n_it, 8, dh), _F32),
        compiler_params=pltpu.CompilerParams(
            dimension_semantics=("arbitrary",), vmem_limit_bytes=_VMEM_LIMIT),
        name="nsa_sample_gather",
    )(idx, page_table, q_rows, kv_new, cache4)


def _nsa_combine_kernel(ocmp_ref, osel_ref, owin_ref, sm_ref, az_ref, o_ref):
    sm = sm_ref[...]
    shape = ocmp_ref.shape
    head_lane = lax.broadcasted_iota(jnp.int32, shape, 1) // 64
    g0 = 2 * M_HEADS
    acc = jnp.zeros(shape, _F32)
    for br, ref in enumerate((ocmp_ref, osel_ref, owin_ref)):
        gate = jnp.zeros(shape, _F32)
        for a in range(A_GROUP):
            for g in range(A_KV):
                c = g0 + br * A_HEADS + g * A_GROUP + a
                gate = jnp.where(head_lane == a * A_KV + g, sm[:, c:c + 1], gate)
        acc = acc + gate * ref[...]
    az = az_ref[...]
    o_ref[...] = (acc * (az * jax.nn.sigmoid(az))).astype(o_ref.dtype)


def _nsa_combine(ocmp, osel, owin, sm, az):
    M, W = ocmp.shape
    full = lambda w: pl.BlockSpec((M, w), lambda i: (0, 0))
    return pl.pallas_call(
        _nsa_combine_kernel,
        grid=(1,),
        in_specs=[full(W), full(W), full(W), full(128), full(W)],
        out_specs=full(W),
        out_shape=jax.ShapeDtypeStruct((M, W), _BF16),
        name="nsa_combine",
    )(ocmp, osel, owin, sm, az)


def _mixer_inputs(x, norm_w, pw, m_width, a_width, a_dh):
    B, T, D = x.shape
    w_main, w_small, seg, hw, hm, groups, norm_tiles = pw
    M = B * T
    tm = next((t for t in (1024, 512) if M % t == 0), M)
    a, q, az, kv4, kvw, sm, *kv4t = _proj_in(x.reshape(M, D), norm_w, w_main, w_small, seg, hw, hm,
                                             groups, norm_tiles, tm, T)
    r3 = lambda v: v.reshape(B, T, v.shape[-1])
    kv_rows = r3(kv4).reshape(B, T, 4, A_KV, a_dh)
    if kv4t:
        kv_rows = jnp.transpose(kv4t[0].reshape(B, 4, A_KV, a_dh, T), (0, 4, 1, 2, 3))
    return r3(a), r3(sm), r3(q), r3(kv4), r3(kvw), r3(az), kv_rows


def _mixer_output(x, m_out, a_out, p, ow):
    B, T, D = x.shape
    M = B * T
    tm = 256 if M % 256 == 0 else M
    y = _out_proj(x.reshape(M, D), m_out.reshape(M, -1), a_out.reshape(M, -1), p.reshape(M, -1), *ow, tm)
    return y.reshape(B, T, D)


def kernel(x_prompt, x_sample, cache_nsa_kv, cache_win_kv, state_mlstm_C, state_mlstm_n,
           state_mlstm_m, state_mlstm_conv, page_table, p_prompt, p_sample, norm_w, w_in,
           m_conv_w, m_conv_b, m_wq, m_wk, m_wv, m_b_i, m_b_f, m_norm_w, a_q_norm, a_k_norm,
           cmp_pe, cmp_w1, cmp_w2, w_out, ple_proj, ple_norm, ple_gate):
    xp, xs = x_prompt, x_sample
    B, T, D = xp.shape
    depth = w_in.shape[0]
    m_width = m_conv_w.shape[-1]
    a_width = D - m_width
    a_dh = a_q_norm.shape[-1]
    dh = m_width // M_HEADS
    per_layer = []
    for i in range(depth):
        mw = (m_conv_w[i], m_conv_b[i], m_wq[i], m_wk[i], m_wv[i], m_b_i[i], m_b_f[i], m_norm_w[i])
        cw = (cmp_pe[i], cmp_w1[i], cmp_w2[i], a_k_norm[i, 0])
        w_out_i = jnp.concatenate(
            [w_out[i][:m_width], _heads_to_agd(w_out[i][m_width:].T, a_dh).T], axis=0)
        ow = (w_out_i, ple_proj[i], ple_norm[i], ple_gate[i])
        pw = _prep_proj_weights(w_in[i], a_q_norm[i], a_k_norm[i], m_width, a_width, a_dh)
        cwp = _prep_cmp_weights(*cw)
        a3, sm, q, kv4, kvw, az, kv_p = _mixer_inputs(xp, norm_w[i], pw, m_width, a_width, a_dh)
        L = _MLSTM_L if T % _MLSTM_L == 0 else (M_CHUNK if T % M_CHUNK == 0 else T)
        m_out, conv_p, C_p, n_p, m_p = _mlstm(
            a3, sm, jnp.zeros((B, M_CONV - 1, m_width), _F32),
            jnp.zeros((B, M_HEADS, dh, dh), _F32), jnp.zeros((B, M_HEADS, dh), _F32),
            jnp.full((B, M_HEADS), -jnp.inf, _F32), *mw, L)
        kc, vc = _compress_prompt(kv4, cwp)
        a_out = _nsa_prompt_t(q, kv4, kvw, kc, vc, sm, az)
        xp_new = _mixer_output(xp, m_out, a_out, p_prompt[i], ow)
        win_p = kvw[:, -min(WINDOW, T):].reshape(B, -1, 2, A_KV, a_dh)
        a3, sm, q, kv4, kvw, az, kv_s = _mixer_inputs(xs, norm_w[i], pw, m_width, a_width, a_dh)
        Bd, Ts = xs.shape[:2]
        Ls = _MLSTM_L if Ts % _MLSTM_L == 0 else (M_CHUNK if Ts % M_CHUNK == 0 else Ts)
        m_out, conv_s, C_s, n_s, m_s = _mlstm(
            a3, sm, state_mlstm_conv[i], state_mlstm_C[i], state_mlstm_n[i], state_mlstm_m[i], *mw, Ls)
        past_len = page_table.shape[1] * PAGE_SIZE
        cache_l = cache_nsa_kv[i]
        n_pool = cache_l.shape[0]
        wb = cache_win_kv.shape[2]
        cache_t = jnp.transpose(cache_l, (0, 2, 3, 4, 1))
        kc, vc = _compress_sample(cache_t.reshape(n_pool, 8, 2 * a_dh, PAGE_SIZE), page_table, cwp)
        ocmp, owin, idx = _nsa_sample_front(
            q, kc, vc, cache_win_kv[i].reshape(Bd, wb, 2 * A_KV * a_dh), kvw, past_len)
        topn = min(SLC_TOPN, past_len // SLC_BLK + 1)
        q_rows = q.reshape(Bd, Ts, A_GROUP, A_KV, a_dh).transpose(0, 3, 1, 2, 4)
        q_rows = jnp.pad(q_rows, ((0, 0), (0, 0), (0, 0), (0, 8 - A_GROUP), (0, 0)))
        kv_new = kv4.reshape(Bd, Ts, 4 * A_KV, a_dh).transpose(0, 2, 1, 3)
        osel = _nsa_sample_gather(
            idx[:, :, :topn].reshape(Bd, -1), page_table, q_rows.reshape(Bd, A_KV * Ts, 8, a_dh), kv_new,
            cache_t, past_len)
        osel = osel.reshape(Bd, A_KV, Ts, 8, a_dh)[:, :, :, :A_GROUP].transpose(0, 2, 3, 1, 4)
        Ms = Bd * Ts
        a_out = _nsa_combine(ocmp.reshape(Ms, -1), osel.reshape(Ms, -1), owin.reshape(Ms, -1),
                             sm.reshape(Ms, -1), az.reshape(Ms, -1)).reshape(Bd, Ts, -1)
        xs_new = _mixer_output(xs, m_out, a_out, p_sample[i], ow)
        win_s = jnp.concatenate(
            [cache_win_kv[i], kvw.reshape(Bd, Ts, 2, A_KV, a_dh)], axis=1)[:, -wb:]
        per_layer.append((kv_p, kv_s, win_p, win_s, C_p, n_p, m_p, conv_p, C_s, n_s, m_s, conv_s))
        xp, xs = xp_new, xs_new
    (kv_p, kv_s, win_p, win_s, C_p, n_p, m_p, conv_p,
     C_s, n_s, m_s, conv_s) = [jnp.stack(a, axis=0) for a in zip(*per_layer)]
    return (xp, xs, kv_p, kv_s, win_p, win_s, C_p, n_p, m_p, conv_p, C_s, n_s, m_s, conv_s)
```

```python
import functools

import jax
import jax.numpy as jnp
from jax import lax
from jax.experimental import pallas as pl
from jax.experimental.pallas import tpu as pltpu

NORM_EPS = 1e-6
M_HEADS = 4
M_CONV = 4
M_CHUNK = 64
A_HEADS = 16
A_KV = 4
A_GROUP = A_HEADS // A_KV
CMP_STRIDE = 16
CMP_LEN = 2 * CMP_STRIDE
SLC_BLK = 64
SLC_TOPN = 16
WINDOW = 512
PAGE_SIZE = 128

_BF16 = jnp.bfloat16
_F32 = jnp.float32
_VMEM_LIMIT = 56 * 1024 * 1024


def _dot(a, b):
    return jnp.dot(a.astype(_BF16), b.astype(_BF16), preferred_element_type=_F32)


_PT = 512


def _proj_in_kernel(groups, norm_tiles, x_ref, nw_ref, w_ref, wsm_ref, seg_ref, hw_ref, hm_ref,
                    a_ref, q_ref, az_ref, kv4_ref, kvw_ref, sm_ref, *rest):
    xn_ref = rest[-1]
    t_refs = (None, None, None, rest[0] if len(rest) == 2 else None, None)
    j = pl.program_id(1)

    @pl.when(j == 0)
    def _():
        x = x_ref[...]
        ms = jnp.mean(x * x, axis=-1, keepdims=True)
        xn = x * lax.rsqrt(ms + NORM_EPS) * nw_ref[...]
        xn_ref[...] = xn.astype(_BF16)
        sm = jnp.dot(xn_ref[...], wsm_ref[...], preferred_element_type=_F32)
        lane = lax.broadcasted_iota(jnp.int32, sm.shape, 1)
        is_gate = (lane >= 2 * M_HEADS) & (lane < 2 * M_HEADS + 3 * A_HEADS)
        sm_ref[...] = jnp.where(is_gate, jax.nn.sigmoid(sm), sm)

    y = jnp.dot(xn_ref[...], w_ref[...], preferred_element_type=_F32)
    outs = (a_ref, q_ref, az_ref, kv4_ref, kvw_ref)

    def emit(o_ref, t_ref, val):
        o_ref[...] = val
        if t_ref is not None:
            t_ref[...] = val.T

    for (start, n), o_ref, t_ref in zip(groups, outs, t_refs):
        plain = [t for t in range(start, start + n) if t not in norm_tiles]
        normed = [t for t in range(start, start + n) if t in norm_tiles]
        if plain:
            cond = functools.reduce(jnp.logical_or, [j == t for t in plain])

            @pl.when(cond)
            def _(o_ref=o_ref, t_ref=t_ref):
                emit(o_ref, t_ref, y)

        if normed:
            cond = functools.reduce(jnp.logical_or, [j == t for t in normed])

            @pl.when(cond)
            def _(o_ref=o_ref, t_ref=t_ref):
                ms = jnp.dot((y * y).astype(_BF16), seg_ref[...], preferred_element_type=_F32)
                yn = y * lax.rsqrt(ms + NORM_EPS) * hw_ref[...]
                emit(o_ref, t_ref, jnp.where(hm_ref[...] > 0, yn, y))


def _proj_in(x2d, norm_w, w_main, w_small, seg, hw, hm, groups, norm_tiles, tm, seq_len):
    M, D = x2d.shape
    n_tiles = w_main.shape[1] // _PT
    widths = [n * _PT for _, n in groups]

    def out_map(start, n):
        return lambda i, j: (i, jnp.clip(j - start, 0, n - 1))

    out_specs = [pl.BlockSpec((tm, _PT), out_map(s, n)) for s, n in groups]
    out_specs.append(pl.BlockSpec((tm, 128), lambda i, j: (i, 0)))
    out_shape = [jax.ShapeDtypeStruct((M, w), _F32) for w in widths]
    out_shape.append(jax.ShapeDtypeStruct((M, 128), _F32))
    if seq_len % tm == 0:
        tpb = seq_len // tm
        s4, n4 = groups[3]
        out_specs.append(pl.BlockSpec(
            (None, _PT, tm), lambda i, j: (i // tpb, jnp.clip(j - s4, 0, n4 - 1), i % tpb)))
        out_shape.append(jax.ShapeDtypeStruct((M // seq_len, widths[3], seq_len), _F32))
    return pl.pallas_call(
        functools.partial(_proj_in_kernel, groups, norm_tiles),
        grid=(M // tm, n_tiles),
        in_specs=[
            pl.BlockSpec((tm, D), lambda i, j: (i, 0)),
            pl.BlockSpec((1, D), lambda i, j: (0, 0)),
            pl.BlockSpec((D, _PT), lambda i, j: (0, j)),
            pl.BlockSpec((D, 128), lambda i, j: (0, 0)),
            pl.BlockSpec((_PT, _PT), lambda i, j: (0, 0)),
            pl.BlockSpec((None, 1, _PT), lambda i, j: (j, 0, 0)),
            pl.BlockSpec((None, 1, _PT), lambda i, j: (j, 0, 0)),
        ],
        out_specs=out_specs,
        out_shape=out_shape,
        scratch_shapes=[pltpu.VMEM((tm, D), _BF16)],
        compiler_params=pltpu.CompilerParams(
            dimension_semantics=("arbitrary", "arbitrary"), vmem_limit_bytes=_VMEM_LIMIT),
        name="proj_in",
    )(x2d, norm_w.reshape(1, D), w_main, w_small, seg, hw, hm)


def _heads_to_agd(w, a_dh):
    lead = w.shape[:-1]
    return jnp.swapaxes(w.reshape(lead + (A_KV, A_GROUP, a_dh)), -3, -2).reshape(w.shape)


def _prep_proj_weights(w_in, a_q_norm, a_k_norm, m_width, a_width, a_dh):
    kvw = A_KV * a_dh
    sizes = (m_width, m_width, m_width, M_HEADS, M_HEADS, a_width, 6 * kvw, 3 * A_HEADS, a_width)
    offs = [0]
    for s in sizes:
        offs.append(offs[-1] + s)
    u0, o0, z0, i0, f0, q0, kv0, g0, az0, end = offs
    assert end == w_in.shape[1]
    w_main = jnp.concatenate([
        w_in[:, u0:i0], _heads_to_agd(w_in[:, q0:kv0], a_dh), _heads_to_agd(w_in[:, az0:end], a_dh),
        w_in[:, kv0:kv0 + 4 * kvw], w_in[:, kv0 + 4 * kvw:g0]], axis=1).astype(_BF16)
    n_small = 2 * M_HEADS + 3 * A_HEADS
    w_small = jnp.concatenate([
        w_in[:, i0:q0], w_in[:, g0:az0],
        jnp.zeros((w_in.shape[0], 128 - n_small), w_in.dtype)], axis=1).astype(_BF16)
    widths = (3 * m_width, a_width, a_width, 4 * kvw, 2 * kvw)
    groups, start = [], 0
    for w in widths:
        assert w % _PT == 0
        groups.append((start, w // _PT))
        start += w // _PT
    n_tiles = start
    hw = jnp.zeros((n_tiles * _PT,), _F32)
    hm = jnp.zeros((n_tiles * _PT,), _F32)
    qs = groups[1][0] * _PT
    hw = hw.at[qs:qs + a_width].set(jnp.tile(a_q_norm, a_width // a_dh))
    hm = hm.at[qs:qs + a_width].set(1.0)
    k4 = groups[3][0] * _PT
    hw = hw.at[k4 + 2 * kvw:k4 + 3 * kvw].set(jnp.tile(a_k_norm[1], A_KV))
    hm = hm.at[k4 + 2 * kvw:k4 + 3 * kvw].set(1.0)
    kw = groups[4][0] * _PT
    hw = hw.at[kw:kw + kvw].set(jnp.tile(a_k_norm[2], A_KV))
    hm = hm.at[kw:kw + kvw].set(1.0)
    assert kvw * 2 == _PT and _PT % a_dh == 0
    norm_tiles = tuple(range(groups[1][0], groups[1][0] + groups[1][1])) + (
        (k4 + 2 * kvw) // _PT, kw // _PT)
    r = jnp.arange(_PT) // a_dh
    seg = ((r[:, None] == r[None, :]).astype(_F32) / a_dh).astype(_BF16)
    return (w_main, w_small, seg, hw.reshape(n_tiles, 1, _PT), hm.reshape(n_tiles, 1, _PT),
            tuple(groups), norm_tiles)


def _out_proj_kernel(mw, x_ref, m_ref, a_ref, p_ref, wo_ref, pp_ref, pn_ref, pg_ref, o_ref):
    h = x_ref[...] + _dot(m_ref[...], wo_ref[0:mw, :]) + _dot(a_ref[...], wo_ref[mw:, :])
    ms = jnp.mean(h * h, axis=-1, keepdims=True)
    hn = h * lax.rsqrt(ms + NORM_EPS) * pn_ref[...]
    gate = jax.nn.sigmoid(_dot(hn, pg_ref[...]))
    o_ref[...] = h + gate * _dot(p_ref[...], pp_ref[...])


def _out_proj(x2d, m_out, a_out, p2d, w_out, ple_proj, ple_norm, ple_gate, tm):
    M, D = x2d.shape
    mw, aw, dp = m_out.shape[1], a_out.shape[1], p2d.shape[1]

    def const(shape):
        return pl.BlockSpec(shape, lambda i: (0, 0), pipeline_mode=pl.Buffered(1))

    return pl.pallas_call(
        functools.partial(_out_proj_kernel, mw),
        grid=(M // tm,),
        in_specs=[
            pl.BlockSpec((tm, D), lambda i: (i, 0)),
            pl.BlockSpec((tm, mw), lambda i: (i, 0)),
            pl.BlockSpec((tm, aw), lambda i: (i, 0)),
            pl.BlockSpec((tm, dp), lambda i: (i, 0)),
            const((mw + aw, D)),
            const((dp, D)),
            const((1, D)),
            const((D, D)),
        ],
        out_specs=pl.BlockSpec((tm, D), lambda i: (i, 0)),
        out_shape=jax.ShapeDtypeStruct((M, D), _F32),
        compiler_params=pltpu.CompilerParams(
            dimension_semantics=("arbitrary",), vmem_limit_bytes=_VMEM_LIMIT),
        name="out_proj",
    )(x2d, m_out, a_out, p2d, w_out.astype(_BF16), ple_proj.astype(_BF16),
      ple_norm.reshape(1, D), ple_gate.astype(_BF16))


_MLSTM_L = 256
_CONV_PAD = 8


def _mlstm_kernel(L, dh, n_steps,
                  u_ref, o_ref, z_ref, sm_ref, cbuf_ref, c0_ref, n0_ref, m0_ref,
                  wq_ref, wk_ref, wv_ref, cw_ref, cb_ref, bi_ref, bf_ref, nw_ref,
                  out_ref, cout_ref, nout_ref, mout_ref, convout_ref,
                  xbuf, c_s, n_s, m_s):
    s = pl.program_id(1)
    tail = M_CONV - 1

    @pl.when(s == 0)
    def _():
        c_s[...] = c0_ref[...]
        n_s[...] = n0_ref[...]
        m_s[...] = m0_ref[...]
        xbuf[_CONV_PAD - tail:_CONV_PAD, :] = cbuf_ref[...]

    u_all = u_ref[...]
    xbuf[_CONV_PAD:_CONV_PAD + L, :] = u_all
    c = cb_ref[...]
    for j in range(M_CONV):
        c = c + xbuf[_CONV_PAD - tail + j:_CONV_PAD - tail + j + L, :] * cw_ref[j:j + 1, :]
    xbuf[_CONV_PAD - tail:_CONV_PAD, :] = u_all[L - tail:L, :]
    ch_all = c * jax.nn.sigmoid(c)

    sm = sm_ref[...]
    ii = lax.broadcasted_iota(jnp.int32, (L, L), 0)
    jj = lax.broadcasted_iota(jnp.int32, (L, L), 1)
    eye = ii == jj
    causal = jj <= ii

    for h in range(M_HEADS):
        cols = slice(h * dh, (h + 1) * dh)
        u = u_all[:, cols]
        ch = ch_all[:, cols]
        q = _dot(ch, wq_ref[h])
        k = _dot(ch, wk_ref[h]) * (dh ** -0.5)
        v = _dot(u, wv_ref[h])

        li_col = sm[:, h:h + 1] + bi_ref[h][:, 0:1]
        f_in = sm[:, M_HEADS + h:M_HEADS + h + 1] + bf_ref[h][:, 0:1]
        lf_col = jnp.minimum(f_in, 0.0) - jnp.log(1.0 + jnp.exp(-jnp.abs(f_in)))

        lf_row = jnp.sum(jnp.where(eye, lf_col, 0.0), axis=0, keepdims=True)
        b_col = jnp.sum(jnp.where(causal, lf_row, 0.0), axis=1, keepdims=True)

        m_prev = m_s[h][0:1, 0:1]
        c_prev = c_s[h]
        n_prev = n_s[h]
        g_col = li_col - b_col
        g_row = jnp.sum(jnp.where(eye, g_col, 0.0), axis=0, keepdims=True)
        gm = jnp.where(causal, g_row, -jnp.inf)
        big_m = jnp.maximum(m_prev, jnp.max(gm, axis=1, keepdims=True))
        m_t = b_col + big_m
        qb = q.astype(_BF16)
        sc = lax.dot_general(qb, k.astype(_BF16), (((1,), (1,)), ((), ())),
                             preferred_element_type=_F32) * jnp.exp(gm - big_m)
        a_col = jnp.exp(m_prev - big_m)
        num = _dot(sc, v) + a_col * _dot(qb, c_prev)
        den = jnp.sum(sc, axis=1, keepdims=True) + a_col * jnp.sum(q * n_prev, axis=1, keepdims=True)
        hh = num / jnp.maximum(jnp.abs(den), jnp.exp(-m_t))

        b_end = b_col[L - 1:L, :]
        logw = b_end - b_col + li_col
        m_new = jnp.maximum(b_end + m_prev, jnp.max(logw, axis=0, keepdims=True))
        w_col = jnp.exp(logw - m_new)
        decay = jnp.exp(b_end + m_prev - m_new)
        c_s[h] = decay * c_prev + lax.dot_general(
            k.astype(_BF16), (w_col * v).astype(_BF16), (((0,), (0,)), ((), ())),
            preferred_element_type=_F32)
        n_s[h] = decay * n_prev + jnp.sum(w_col * k, axis=0, keepdims=True)
        m_s[h] = jnp.broadcast_to(m_new, (1, 128))

        hg = hh * jax.nn.sigmoid(o_ref[:, cols])
        hn = hg * lax.rsqrt(jnp.mean(hg * hg, axis=-1, keepdims=True) + NORM_EPS) * nw_ref[h]
        zz = z_ref[:, cols]
        out_ref[:, cols] = (hn * (zz * jax.nn.sigmoid(zz))).astype(out_ref.dtype)

    @pl.when(s == n_steps - 1)
    def _():
        cout_ref[...] = c_s[...]
        nout_ref[...] = n_s[...]
        mout_ref[...] = m_s[...]
        convout_ref[...] = u_all[L - tail:L, :]


def _mlstm(a3, sm, conv_buf, C0, n0, m0, conv_w, conv_b, wq, wk, wv, b_i, b_f, norm_w, L):
    B, T, mw3 = a3.shape
    mw = mw3 // 3
    dh = mw // M_HEADS
    tail = M_CONV - 1
    assert T % L == 0 and L >= tail and (L % 8 == 0)
    n_steps = T // L
    H = M_HEADS

    def lane_b(vec):
        return jnp.broadcast_to(vec.astype(_F32)[:, None, None], (H, 1, 128))

    m0b = jnp.broadcast_to(m0.astype(_F32)[:, :, None, None], (B, H, 1, 128))
    blk = lambda off: pl.BlockSpec((None, L, mw), lambda b, s: (b, s, off))
    full = lambda *shape: pl.BlockSpec(shape, lambda b, s: (0,) * len(shape))
    st = lambda r, c: pl.BlockSpec((None, H, r, c), lambda b, s: (b, 0, 0, 0))
    out, C, n, m, conv = pl.pallas_call(
        functools.partial(_mlstm_kernel, L, dh, n_steps),
        grid=(B, n_steps),
        in_specs=[
            blk(0), blk(1), blk(2),
            pl.BlockSpec((None, L, 128), lambda b, s: (b, s, 0)),
            pl.BlockSpec((None, tail, mw), lambda b, s: (b, 0, 0)),
            st(dh, dh), st(1, dh), st(1, 128),
            full(H, dh, dh), full(H, dh, dh), full(H, dh, dh),
            full(M_CONV, mw), full(1, mw),
            full(H, 1, 128), full(H, 1, 128), full(H, 1, dh),
        ],
        out_specs=[
            pl.BlockSpec((None, L, mw), lambda b, s: (b, s, 0)),
            st(dh, dh), st(1, dh), st(1, 128),
            pl.BlockSpec((None, tail, mw), lambda b, s: (b, 0, 0)),
        ],
        out_shape=[
            jax.ShapeDtypeStruct((B, T, mw), _BF16),
            jax.ShapeDtypeStruct((B, H, dh, dh), _F32),
            jax.ShapeDtypeStruct((B, H, 1, dh), _F32),
            jax.ShapeDtypeStruct((B, H, 1, 128), _F32),
            jax.ShapeDtypeStruct((B, tail, mw), _F32),
        ],
        scratch_shapes=[
            pltpu.VMEM((_CONV_PAD + L, mw), _F32),
            pltpu.VMEM((H, dh, dh), _F32),
            pltpu.VMEM((H, 1, dh), _F32),
            pltpu.VMEM((H, 1, 128), _F32),
        ],
        compiler_params=pltpu.CompilerParams(
            dimension_semantics=("arbitrary", "arbitrary"), vmem_limit_bytes=_VMEM_LIMIT),
        name="mlstm",
    )(a3, a3, a3, sm, conv_buf, C0, n0.reshape(B, H, 1, dh), m0b,
      wq.astype(_BF16), wk.astype(_BF16), wv.astype(_BF16), conv_w, conv_b.reshape(1, mw),
      lane_b(b_i), lane_b(b_f), norm_w.reshape(H, 1, dh))
    return out, conv, C, n.reshape(B, H, dh), m[:, :, 0, 0]


def _prep_cmp_weights(pe, w1, w2, k_norm):
    a_dh, hid = w1.shape[-2], w1.shape[-1]
    w1r = w1.reshape(2, 2, CMP_STRIDE, a_dh, hid)
    eye2 = jnp.eye(2, dtype=w1.dtype)
    w1p = jnp.einsum('cxjdh,ab->cjadxbh', w1r, eye2).reshape(2, CMP_STRIDE * 2 * a_dh, 2 * 2 * hid)
    bias = jnp.einsum('cxjd,cxjdh->ch', pe.reshape(2, 2, CMP_STRIDE, a_dh), w1r)
    bias2 = jnp.tile(bias, (1, 2)).reshape(2, 1, 2 * hid)
    w2p = jnp.einsum('chd,ab->cahbd', w2, eye2).reshape(2, 2 * hid, 2 * a_dh)
    r = jnp.arange(2 * a_dh) // a_dh
    seg = ((r[:, None] == r[None, :]).astype(_F32) / a_dh).astype(_BF16)
    kn = jnp.tile(k_norm, 2).reshape(1, 2 * a_dh)
    return w1p.astype(_BF16), bias2.astype(_F32), w2p.astype(_BF16), seg, kn.astype(_F32)


def _cmp_first_layer(rows_ref, n_seg, w1_ref, c):
    x = jnp.concatenate(
        [rows_ref[pl.ds(j, n_seg, stride=CMP_STRIDE), :].astype(_BF16)
         for j in range(CMP_STRIDE)], axis=1)
    return jnp.dot(x, w1_ref[c], preferred_element_type=_F32)


def _cmp_second_layer(p, n_seg, c, b_ref, w2_ref, seg_ref, kn_ref):
    hid = p[:, 0:128] + pltpu.roll(p[:, 128:256], n_seg - 1, axis=0) + b_ref[c]
    act = hid * jax.nn.sigmoid(hid)
    out = jnp.dot(act.astype(_BF16), w2_ref[c], preferred_element_type=_F32)
    if c == 0:
        ms = jnp.dot((out * out).astype(_BF16), seg_ref[...], preferred_element_type=_F32)
        out = out * lax.rsqrt(ms + NORM_EPS) * kn_ref[...]
    return out


def _compress_prompt_kernel(n_seg, r00, r01, r10, r11, w1_ref, b_ref, w2_ref, seg_ref, kn_ref,
                            kc_ref, vc_ref):
    for c, o_ref, refs in ((0, kc_ref, (r00, r01)), (1, vc_ref, (r10, r11))):
        for gp in range(2):
            p = _cmp_first_layer(refs[gp], n_seg, w1_ref, c)
            o_ref[:, gp * 128:(gp + 1) * 128] = _cmp_second_layer(p, n_seg, c, b_ref, w2_ref, seg_ref, kn_ref)


def _compress_prompt(kv4, cwp):
    B, T, W = kv4.shape
    assert W == 1024 and T % (CMP_STRIDE * 8) == 0
    n_seg = T // CMP_STRIDE
    w1p, bias2, w2p, seg, kn = cwp
    full = lambda a: pl.BlockSpec(a.shape, lambda b: (0,) * a.ndim)
    return pl.pallas_call(
        functools.partial(_compress_prompt_kernel, n_seg),
        grid=(B,),
        in_specs=[pl.BlockSpec((None, T, 128), functools.partial(lambda k, b: (b, 0, k), k))
                  for k in range(4)] + [full(w1p), full(bias2), full(w2p), full(seg), full(kn)],
        out_specs=[pl.BlockSpec((None, n_seg, 256), lambda b: (b, 0, 0))] * 2,
        out_shape=[jax.ShapeDtypeStruct((B, n_seg, 256), _F32)] * 2,
        compiler_params=pltpu.CompilerParams(
            dimension_semantics=("arbitrary",), vmem_limit_bytes=_VMEM_LIMIT),
        name="compress_prompt",
    )(kv4, kv4, kv4, kv4, w1p, bias2, w2p, seg, kn)


_QB = 256
_SEL_TK = 512
_GW = 32
_LOG2E = 1.4426950408889634


def _softmax_parts(s):
    m = jnp.max(s, axis=-1, keepdims=True)
    m = jnp.where(m == -jnp.inf, 0.0, m)
    e = jnp.exp(s - m)
    return e, 1.0 / jnp.maximum(jnp.sum(e, axis=-1, keepdims=True), 1e-30)


def _softmax_rows(s, mask):
    e, inv = _softmax_parts(jnp.where(mask, s, -jnp.inf))
    return e * inv


def _col_softmax_parts(s):
    m = jnp.max(s, axis=0, keepdims=True)
    m = jnp.where(m == -jnp.inf, 0.0, m)
    e = jnp.exp2(s - m)
    return e, 1.0 / jnp.maximum(jnp.sum(e, axis=0, keepdims=True), 1e-30)


def _tile_cols(x, n):
    return jnp.concatenate([x] * n, axis=1)


def _nsa_prompt_t_kernel(T, n_cmp, n_slc,
                         q_ref, ks_ref, kw_ref, kc_ref, vc_ref, sm_ref, az_ref, out_ref,
                         ksb, vst, kwb, vwt, vct, *scratch):
    QB, TK, G, A = _QB, _SEL_TK, A_KV, A_GROUP
    R = A * QB
    NB = T // 128
    qplt_ref, sel_ref, m_ref, l_ref, acc_ref, cmp_ref, win_ref = (
        scratch[i * G:(i + 1) * G] for i in range(7))
    DH = 256 // G

    @pl.when(pl.program_id(1) == 0)
    def _():
        for blk in range(NB):
            rows = slice(blk * 128, (blk + 1) * 128)
            ksb[rows, :] = ks_ref[rows, 0:256].astype(_BF16)
            vst[blk] = ks_ref[rows, 256:512].T.astype(_BF16)
            kwb[rows, :] = kw_ref[rows, 0:256].astype(_BF16)
            vwt[blk] = kw_ref[rows, 256:512].T.astype(_BF16)
        vct[...] = vc_ref[...].T.astype(_BF16)

    t0 = pl.program_id(1) * QB
    qt = q_ref[...].T
    for g in range(G):
        qplt_ref[g][...] = jnp.concatenate(
            [qt[a * 256 + g * DH:a * 256 + (g + 1) * DH, :] * (0.125 * _LOG2E) for a in range(A)],
            axis=1).astype(_BF16)
    tq = t0 + lax.broadcasted_iota(jnp.int32, (1, QB), 1)

    n_seg = kc_ref.shape[0]
    kc = kc_ref[...].astype(_BF16)
    ci = lax.broadcasted_iota(jnp.int32, (n_seg, QB), 0)
    cbias = _tile_cols(jnp.where((ci < n_cmp) & (ci * CMP_STRIDE + (CMP_LEN - 1) <= tq), 0.0, -jnp.inf), A)
    ratio = SLC_BLK // CMP_STRIDE
    topn = min(SLC_TOPN, n_slc)
    mj = lax.broadcasted_iota(jnp.int32, (_GW, n_seg), 0)
    mi = lax.broadcasted_iota(jnp.int32, (_GW, n_seg), 1)
    msel_t = ((mi >= ratio * mj - 1) & (mi <= ratio * mj + ratio - 1)).astype(_BF16)
    blk = lax.broadcasted_iota(jnp.int32, (_GW, QB), 0)
    cur = tq // SLC_BLK
    valid = (blk < n_slc) & (blk * SLC_BLK <= tq)
    forced = (blk == 0) | (blk == cur) | (blk == cur - 1)
    for g in range(G):
        hrows = slice(g * DH, (g + 1) * DH)
        s = jnp.dot(kc[:, hrows], qplt_ref[g][...], preferred_element_type=_F32) + cbias
        e, inv = _col_softmax_parts(s)
        cmp_ref[g][...] = jnp.dot(vct[hrows, :], e.astype(_BF16), preferred_element_type=_F32) * inv
        p = e * inv
        pg = p[:, 0:QB]
        for a in range(1, A):
            pg = pg + p[:, a * QB:(a + 1) * QB]
        pg_hi = pg.astype(_BF16)
        pg_lo = (pg - pg_hi.astype(_F32)).astype(_BF16)
        ps = jnp.dot(msel_t, pg_hi, preferred_element_type=_F32) \
            + jnp.dot(msel_t, pg_lo, preferred_element_type=_F32)
        score = jnp.where(valid, jnp.where(forced, jnp.inf, ps), -jnp.inf)
        rank = jnp.zeros((_GW, QB), _F32)
        for i in range(_GW):
            row = score[i:i + 1, :]
            ahead = (row > score) | ((row == score) & (blk > i))
            rank = rank + jnp.where(ahead, 1.0, 0.0)
        sel_ref[g][...] = jnp.where(rank < topn, 1.0, 0.0)

    for g in range(G):
        m_ref[g][...] = jnp.full((1, R), -jnp.inf, _F32)
        l_ref[g][...] = jnp.zeros((1, R), _F32)
        acc_ref[g][...] = jnp.zeros((DH, R), _F32)
    krow = lax.broadcasted_iota(jnp.int32, (TK, QB), 0)

    def sel_tile(kt, carry):
        k0 = pl.multiple_of(kt * TK, TK)
        kt_b = ksb[pl.ds(k0, TK), :]
        b0 = kt * (TK // 128)
        j0 = kt * (TK // SLC_BLK)
        vt_b = jnp.concatenate([vst[b0 + i] for i in range(TK // 128)], axis=1)
        causal = (k0 + krow) <= tq
        for g in range(G):
            member = jnp.concatenate(
                [jnp.broadcast_to(sel_ref[g][pl.ds(j0 + j, 1), :], (SLC_BLK, QB))
                 for j in range(TK // SLC_BLK)], axis=0)
            bias = _tile_cols(jnp.where(causal & (member > 0.5), 0.0, -jnp.inf), A)
            s = jnp.dot(kt_b[:, g * DH:(g + 1) * DH], qplt_ref[g][...],
                        preferred_element_type=_F32) + bias
            m_old = m_ref[g][...]
            m_new = jnp.maximum(m_old, jnp.max(s, axis=0, keepdims=True))
            m_safe = jnp.where(m_new == -jnp.inf, 0.0, m_new)
            p = jnp.exp2(s - m_safe)
            alpha = jnp.exp2(m_old - m_safe)
            l_ref[g][...] = alpha * l_ref[g][...] + jnp.sum(p, axis=0, keepdims=True)
            acc_ref[g][...] = alpha * acc_ref[g][...] + jnp.dot(
                vt_b[g * DH:(g + 1) * DH, :], p.astype(_BF16), preferred_element_type=_F32)
            m_ref[g][...] = m_new
        return carry

    lax.fori_loop(0, (t0 + QB + TK - 1) // TK, sel_tile, 0)

    WK = WINDOW + QB
    ws = pl.multiple_of(jnp.clip(t0 - WINDOW, 0, T - WK), QB)
    kw_b = kwb[pl.ds(ws, WK), :]
    wb0 = ws // 128
    vw_b = jnp.concatenate([vwt[wb0 + i] for i in range(WK // 128)], axis=1)
    diff = tq - (ws + lax.broadcasted_iota(jnp.int32, (WK, QB), 0))
    wbias = _tile_cols(jnp.where((diff >= 0) & (diff < WINDOW), 0.0, -jnp.inf), A)
    for g in range(G):
        s = jnp.dot(kw_b[:, g * DH:(g + 1) * DH], qplt_ref[g][...], preferred_element_type=_F32) + wbias
        e, inv = _col_softmax_parts(s)
        win_ref[g][...] = jnp.dot(vw_b[g * DH:(g + 1) * DH, :], e.astype(_BF16),
                                  preferred_element_type=_F32) * inv
        acc_ref[g][...] = acc_ref[g][...] * (1.0 / jnp.maximum(l_ref[g][...], 1e-30))

    smt = sm_ref[...].T
    g0 = 2 * M_HEADS
    for a in range(A):
        lanes = slice(a * QB, (a + 1) * QB)
        parts = []
        for g in range(G):
            head = g0 + g * A + a
            parts.append(smt[head:head + 1, :] * cmp_ref[g][:, lanes]
                         + smt[head + A_HEADS:head + A_HEADS + 1, :] * acc_ref[g][:, lanes]
                         + smt[head + 2 * A_HEADS:head + 2 * A_HEADS + 1, :] * win_ref[g][:, lanes])
        mix = jnp.concatenate(parts, axis=0)
        az = az_ref[:, a * 256:(a + 1) * 256]
        out_ref[:, a * 256:(a + 1) * 256] = (mix.T * (az * jax.nn.sigmoid(az))).astype(out_ref.dtype)


def _nsa_prompt_t(q, kv4, kvw, kc, vc, sm, az):
    B, T, _ = q.shape
    n_seg = kc.shape[1]
    n_cmp = n_seg - 1
    n_slc = T // SLC_BLK
    assert T % _SEL_TK == 0 and T >= WINDOW + _QB and n_slc <= _GW and n_seg % 128 == 0
    R = A_GROUP * _QB
    blk = lambda w: pl.BlockSpec((None, _QB, w), lambda b, i: (b, i, 0))
    return pl.pallas_call(
        functools.partial(_nsa_prompt_t_kernel, T, n_cmp, n_slc),
        grid=(B, T // _QB),
        in_specs=[
            blk(1024),
            pl.BlockSpec((None, T, 512), lambda b, i: (b, 0, 1)),
            pl.BlockSpec((None, T, 512), lambda b, i: (b, 0, 0)),
            pl.BlockSpec((None, n_seg, 256), lambda b, i: (b, 0, 0)),
            pl.BlockSpec((None, n_seg, 256), lambda b, i: (b, 0, 0)),
            blk(128), blk(1024),
        ],
        out_specs=blk(1024),
        out_shape=jax.ShapeDtypeStruct((B, T, 1024), _BF16),
        scratch_shapes=(
            [pltpu.VMEM((T, 256), _BF16), pltpu.VMEM((T // 128, 256, 128), _BF16),
             pltpu.VMEM((T, 256), _BF16), pltpu.VMEM((T // 128, 256, 128), _BF16),
             pltpu.VMEM((256, n_seg), _BF16)]
            + [pltpu.VMEM((256 // A_KV, R), _BF16)] * A_KV
            + [pltpu.VMEM((_GW, _QB), _F32)] * A_KV
            + [pltpu.VMEM((1, R), _F32)] * (2 * A_KV)
            + [pltpu.VMEM((256 // A_KV, R), _F32)] * (3 * A_KV)
        ),
        compiler_params=pltpu.CompilerParams(
            dimension_semantics=("arbitrary", "arbitrary"), vmem_limit_bytes=_VMEM_LIMIT),
        name="nsa_prompt",
    )(q, kv4, kvw, kc, vc, sm, az)


_CMP_PAGES = 32


def _compress_sample_kernel(P, n_chunks, n_seg, pt_ref, cache_ref, w1_ref, b_ref, w2_ref, seg_ref,
                            kn_ref, kc_ref, vc_ref, stage, lanes, per, sem):
    b = pl.program_id(0)
    ch = pl.program_id(1)
    step = b * n_chunks + ch
    n_steps = pl.num_programs(0) * n_chunks
    slot = step % 2
    rows = P * PAGE_SIZE

    def page_copy(bb, cc, sl, p):
        page = pt_ref[bb, cc * P + p]
        return pltpu.make_async_copy(cache_ref.at[page, pl.ds(0, 4)], stage.at[sl, p], sem.at[sl])

    @pl.when(step == 0)
    def _():
        for p in range(P):
            page_copy(b, ch, slot, p).start()

    @pl.when(step + 1 < n_steps)
    def _():
        nxt = step + 1
        for p in range(P):
            page_copy(nxt // n_chunks, nxt % n_chunks, 1 - slot, p).start()

    pltpu.make_async_copy(cache_ref.at[pl.ds(0, P), pl.ds(0, 4)], stage.at[slot], sem.at[slot]).wait()

    segs = rows // CMP_STRIDE

    for k in range(4):
        for p in range(P):
            lanes[k, p * PAGE_SIZE:(p + 1) * PAGE_SIZE, :] = stage[slot, p, k].T
        per[pl.ds(pl.multiple_of(ch * segs, segs), segs), k * 256:(k + 1) * 256] = _cmp_first_layer(
            lanes.at[k], segs, w1_ref, k // 2)

    @pl.when(ch == n_chunks - 1)
    def _():
        for k in range(4):
            c, gp = k // 2, k % 2
            o_ref = kc_ref if c == 0 else vc_ref
            o_ref[:, gp * 128:(gp + 1) * 128] = _cmp_second_layer(
                per[:, k * 256:(k + 1) * 256], n_seg, c, b_ref, w2_ref, seg_ref, kn_ref)


def _compress_sample(cache_t, page_table, cwp):
    B, n_pages = page_table.shape
    P = _CMP_PAGES if n_pages % _CMP_PAGES == 0 else n_pages
    n_chunks = n_pages // P
    n_seg = n_pages * PAGE_SIZE // CMP_STRIDE
    w1p, bias2, w2p, seg, kn = cwp
    full = lambda a: pl.BlockSpec(a.shape, lambda b, c, pt: (0,) * a.ndim)
    grid_spec = pltpu.PrefetchScalarGridSpec(
        num_scalar_prefetch=1,
        grid=(B, n_chunks),
        in_specs=[pl.BlockSpec(memory_space=pl.ANY),
                  full(w1p), full(bias2), full(w2p), full(seg), full(kn)],
        out_specs=[pl.BlockSpec((None, n_seg, 256), lambda b, c, pt: (b, 0, 0))] * 2,
        scratch_shapes=[
            pltpu.VMEM((2, P, 4, 128, PAGE_SIZE), _F32),
            pltpu.VMEM((4, P * PAGE_SIZE, 128), _F32),
            pltpu.VMEM((n_seg, 1024), _F32),
            pltpu.SemaphoreType.DMA((2,)),
        ])
    return pl.pallas_call(
        functools.partial(_compress_sample_kernel, P, n_chunks, n_seg),
        grid_spec=grid_spec,
        out_shape=[jax.ShapeDtypeStruct((B, n_seg, 256), _F32)] * 2,
        compiler_params=pltpu.CompilerParams(
            dimension_semantics=("arbitrary", "arbitrary"), vmem_limit_bytes=_VMEM_LIMIT),
        name="compress_sample",
    )(page_table, cache_t, w1p, bias2, w2p, seg, kn)


def _merge_heads(o_ref, o_g, g, Ts):
    gm = (lax.broadcasted_iota(jnp.int32, (Ts, 256), 1) // 64) == g
    for a in range(A_GROUP):
        part = jnp.where(gm, o_g[a * Ts:(a + 1) * Ts], 0.0)
        cols = slice(a * 256, (a + 1) * 256)
        if g == 0:
            o_ref[:, cols] = part
        else:
            o_ref[:, cols] += part


def _nsa_sample_front_kernel(Ts, past_len, n_cmp, n_slc, NL, topn,
                             q_ref, kc_ref, vc_ref, wold_ref, wnew_ref,
                             ocmp_ref, owin_ref, idx_ref, qpl_ref):
    G, A = A_KV, A_GROUP
    R = A * Ts
    lane256 = lax.broadcasted_iota(jnp.int32, (Ts, 256), 1)
    q = q_ref[...]
    for g in range(G):
        gm = (lane256 // 64) == g
        qpl_ref[g] = jnp.concatenate(
            [jnp.where(gm, q[:, a * 256:(a + 1) * 256] * 0.125, 0.0) for a in range(A)],
            axis=0).astype(_BF16)
    tq_r = past_len + lax.broadcasted_iota(jnp.int32, (R, 1), 0) % Ts
    tq = past_len + lax.broadcasted_iota(jnp.int32, (Ts, 1), 0)

    n_seg = kc_ref.shape[0]
    kc = kc_ref[...].astype(_BF16)
    vc = vc_ref[...].astype(_BF16)
    ci = lax.broadcasted_iota(jnp.int32, (R, n_seg), 1)
    cmask = (ci < n_cmp) & (ci * CMP_STRIDE + (CMP_LEN - 1) <= tq_r)
    ratio = SLC_BLK // CMP_STRIDE
    mi = lax.broadcasted_iota(jnp.int32, (n_seg, NL), 0)
    mj = lax.broadcasted_iota(jnp.int32, (n_seg, NL), 1)
    msel = ((mi >= ratio * mj - 1) & (mi <= ratio * mj + ratio - 1)).astype(_BF16)
    lane = lax.broadcasted_iota(jnp.int32, (Ts, NL), 1)
    cur = tq // SLC_BLK
    valid = (lane < n_slc) & (lane * SLC_BLK <= tq)
    forced = (lane == 0) | (lane == cur) | (lane == cur - 1)
    pgs = []
    for g in range(G):
        s = lax.dot_general(qpl_ref[g], kc, (((1,), (1,)), ((), ())), preferred_element_type=_F32)
        p = _softmax_rows(s, cmask)
        _merge_heads(ocmp_ref, jnp.dot(p.astype(_BF16), vc, preferred_element_type=_F32), g, Ts)
        pg = p[0:Ts]
        for a in range(1, A):
            pg = pg + p[a * Ts:(a + 1) * Ts]
        pgs.append(pg)
    pg = jnp.concatenate(pgs, axis=0)
    pg_hi = pg.astype(_BF16)
    pg_lo = (pg - pg_hi.astype(_F32)).astype(_BF16)
    ps = jnp.dot(pg_hi, msel, preferred_element_type=_F32) \
        + jnp.dot(pg_lo, msel, preferred_element_type=_F32)
    valid = jnp.concatenate([valid] * G, axis=0)
    forced = jnp.concatenate([forced] * G, axis=0)
    score = jnp.where(valid, jnp.where(forced, jnp.inf, ps), -jnp.inf)

    lane_f = lax.broadcasted_iota(jnp.int32, (G * Ts, NL), 1).astype(_F32)
    out_lane = lax.broadcasted_iota(jnp.int32, (G * Ts, 128), 1)
    avail = lane_f < n_slc
    picks = jnp.zeros((G * Ts, 128), _F32)
    for k in range(topn):
        mx = jnp.max(jnp.where(avail, score, -jnp.inf), axis=1, keepdims=True)
        pick = jnp.min(jnp.where(avail & (score == mx), lane_f, float(NL)), axis=1, keepdims=True)
        avail = avail & (lane_f != pick)
        picks = jnp.where(out_lane == k, pick, picks)
    idx_ref[...] = picks.astype(jnp.int32)

    wb = wold_ref.shape[0]
    k_old = wold_ref[:, 0:256].astype(_BF16)
    v_old = wold_ref[:, 256:512].astype(_BF16)
    k_new = wnew_ref[:, 0:256]
    v_new = wnew_ref[:, 256:512]
    pw_old = past_len - wb + lax.broadcasted_iota(jnp.int32, (R, wb), 1)
    d_old = tq_r - pw_old
    m_old = (pw_old >= 0) & (d_old >= 0) & (d_old < WINDOW)
    d_new = tq_r - (past_len + lax.broadcasted_iota(jnp.int32, (R, Ts), 1))
    m_new = (d_new >= 0) & (d_new < WINDOW)
    for g in range(G):
        qg = qpl_ref[g]
        s1 = lax.dot_general(qg, k_old, (((1,), (1,)), ((), ())), preferred_element_type=_F32)
        s2 = lax.dot_general(qg.astype(_F32), k_new, (((1,), (1,)), ((), ())),
                             preferred_element_type=_F32)
        s1 = jnp.where(m_old, s1, -jnp.inf)
        s2 = jnp.where(m_new, s2, -jnp.inf)
        mx = jnp.maximum(jnp.max(s1, axis=1, keepdims=True), jnp.max(s2, axis=1, keepdims=True))
        mx = jnp.where(mx == -jnp.inf, 0.0, mx)
        e1 = jnp.where(m_old, jnp.exp(s1 - mx), 0.0)
        e2 = jnp.where(m_new, jnp.exp(s2 - mx), 0.0)
        den = jnp.maximum(jnp.sum(e1, axis=1, keepdims=True) + jnp.sum(e2, axis=1, keepdims=True), 1e-30)
        o = (jnp.dot(e1.astype(_BF16), v_old, preferred_element_type=_F32)
             + jnp.dot(e2, v_new, preferred_element_type=_F32)) / den
        _merge_heads(owin_ref, o, g, Ts)


def _nsa_sample_front(q, kc, vc, win_old, win_new, past_len):
    B, Ts, _ = q.shape
    n_seg = kc.shape[1]
    n_cmp = (past_len + Ts) // CMP_STRIDE - 1
    assert n_cmp == n_seg - 1 and Ts % 8 == 0 and Ts <= SLC_BLK
    n_slc = past_len // SLC_BLK + 1
    NL = -(-n_slc // 128) * 128
    topn = min(SLC_TOPN, n_slc)
    wb = win_old.shape[1]
    R = A_GROUP * Ts
    b3 = lambda r, w: pl.BlockSpec((None, r, w), lambda b: (b, 0, 0))
    return pl.pallas_call(
        functools.partial(_nsa_sample_front_kernel, Ts, past_len, n_cmp, n_slc, NL, topn),
        grid=(B,),
        in_specs=[b3(Ts, 1024), b3(n_seg, 256), b3(n_seg, 256), b3(wb, 512), b3(Ts, 512)],
        out_specs=[b3(Ts, 1024), b3(Ts, 1024), b3(A_KV * Ts, 128)],
        out_shape=[jax.ShapeDtypeStruct((B, Ts, 1024), _F32), jax.ShapeDtypeStruct((B, Ts, 1024), _F32),
                   jax.ShapeDtypeStruct((B, A_KV * Ts, 128), jnp.int32)],
        scratch_shapes=[pltpu.VMEM((A_KV, R, 256), _BF16)],
        compiler_params=pltpu.CompilerParams(
            dimension_semantics=("arbitrary",), vmem_limit_bytes=_VMEM_LIMIT),
        name="nsa_sample_front",
    )(q, kc, vc, win_old, win_new)


_GATHER_SLOTS = 4


def _nsa_sample_gather_kernel(Ts, n_past_blk, past_len, topn,
                              idx_ref, pt_ref, q_ref, kvn_ref, cache_ref, o_ref, kvbuf, sem):
    b = pl.program_id(0)
    n_it = A_KV * Ts
    bpp = PAGE_SIZE // SLC_BLK
    nk = topn * PAGE_SIZE

    def issue(it, sl):
        g = it // Ts
        for k in range(topn):
            blk = jnp.minimum(idx_ref[b, it * topn + k], n_past_blk - 1)
            page = pt_ref[b, blk // bpp]
            pltpu.make_async_copy(cache_ref.at[page, pl.ds(2, 2), g], kvbuf.at[sl, k], sem.at[sl]).start()

    for i in range(_GATHER_SLOTS - 1):
        issue(i, i)
    lane = lax.broadcasted_iota(jnp.int32, (1, nk), 1)
    new_r = lax.broadcasted_iota(jnp.int32, (1, Ts), 1)

    def body(it, carry):
        sl = it % _GATHER_SLOTS

        @pl.when(it + _GATHER_SLOTS - 1 < n_it)
        def _():
            issue(it + _GATHER_SLOTS - 1, (it + _GATHER_SLOTS - 1) % _GATHER_SLOTS)

        pltpu.make_async_copy(cache_ref.at[pl.ds(0, topn), pl.ds(0, 2), 0], kvbuf.at[sl], sem.at[sl]).wait()

        t = it % Ts
        g = it // Ts
        tq = past_len + t
        pos = jnp.zeros((1, nk), jnp.int32)
        past = jnp.zeros((1, nk), jnp.int32)
        new_sel = jnp.zeros((1, Ts), jnp.int32)
        for k in range(topn):
            blk = idx_ref[b, it * topn + k]
            in_slot = (lane // PAGE_SIZE) == k
            row = lane % PAGE_SIZE
            hit = in_slot & ((row // SLC_BLK) == (blk % bpp))
            pos = jnp.where(in_slot, (blk // bpp) * PAGE_SIZE + row, pos)
            past = jnp.where(hit, jnp.where(blk < n_past_blk, 1, 0), past)
            new_sel = jnp.maximum(new_sel, jnp.where(blk == n_past_blk, 1, 0))
        m1 = (past > 0) & (pos <= tq)
        m2 = (new_sel > 0) & (n_past_blk * SLC_BLK + new_r <= tq)
        q8 = q_ref[it] * 0.125
        kt = jnp.concatenate([kvbuf[sl, k, 0] for k in range(topn)], axis=1).astype(_BF16)
        vt = jnp.concatenate([kvbuf[sl, k, 1] for k in range(topn)], axis=1).astype(_BF16)
        s1 = jnp.dot(q8.astype(_BF16), kt, preferred_element_type=_F32)
        s2 = lax.dot_general(q8, kvn_ref[2 * A_KV + g], (((1,), (1,)), ((), ())),
                             preferred_element_type=_F32)
        s1 = jnp.where(m1, s1, -jnp.inf)
        s2 = jnp.where(m2, s2, -jnp.inf)
        mx = jnp.maximum(jnp.max(s1, axis=1, keepdims=True), jnp.max(s2, axis=1, keepdims=True))
        mx = jnp.where(mx == -jnp.inf, 0.0, mx)
        e1 = jnp.where(m1, jnp.exp(s1 - mx), 0.0)
        e2 = jnp.where(m2, jnp.exp(s2 - mx), 0.0)
        den = jnp.maximum(jnp.sum(e1, axis=1, keepdims=True) + jnp.sum(e2, axis=1, keepdims=True), 1e-30)
        o_ref[it] = (lax.dot_general(e1.astype(_BF16), vt, (((1,), (1,)), ((), ())),
                                     preferred_element_type=_F32)
                     + jnp.dot(e2, kvn_ref[3 * A_KV + g], preferred_element_type=_F32)) / den
        return carry

    lax.fori_loop(0, n_it, body, 0)


def _nsa_sample_gather(idx, page_table, q_rows, kv_new, cache4, past_len):
    B, n_it, _, dh = q_rows.shape
    Ts = n_it // A_KV
    n_past_blk = past_len // SLC_BLK
    topn = idx.shape[1] // n_it
    grid_spec = pltpu.PrefetchScalarGridSpec(
        num_scalar_prefetch=2,
        grid=(B,),
        in_specs=[pl.BlockSpec((None, n_it, 8, dh), lambda b, i, p: (b, 0, 0, 0)),
                  pl.BlockSpec((None, 4 * A_KV, Ts, dh), lambda b, i, p: (b, 0, 0, 0)),
                  pl.BlockSpec(memory_space=pl.ANY)],
        out_specs=pl.BlockSpec((None, n_it, 8, dh), lambda b, i, p: (b, 0, 0, 0)),
        scratch_shapes=[
            pltpu.VMEM((_GATHER_SLOTS, topn, 2, dh, PAGE_SIZE), _F32),
            pltpu.SemaphoreType.DMA((_GATHER_SLOTS,)),
        ])
    return pl.pallas_call(
        functools.partial(_nsa_sample_gather_kernel, Ts, n_past_blk, past_len, topn),
        grid_spec=grid_spec,
        out_shape=jax.ShapeDtypeStruct((B, n_it, 8, dh), _F32),
        compiler_params=pltpu.CompilerParams(
            dimension_semantics=("arbitrary",), vmem_limit_bytes=_VMEM_LIMIT),
        name="nsa_sample_gather",
    )(idx, page_table, q_rows, kv_new, cache4)


def _nsa_combine_kernel(ocmp_ref, osel_ref, owin_ref, sm_ref, az_ref, o_ref):
    sm = sm_ref[...]
    shape = ocmp_ref.shape
    head_lane = lax.broadcasted_iota(jnp.int32, shape, 1) // 64
    g0 = 2 * M_HEADS
    acc = jnp.zeros(shape, _F32)
    for br, ref in enumerate((ocmp_ref, osel_ref, owin_ref)):
        gate = jnp.zeros(shape, _F32)
        for a in range(A_GROUP):
            for g in range(A_KV):
                c = g0 + br * A_HEADS + g * A_GROUP + a
                gate = jnp.where(head_lane == a * A_KV + g, sm[:, c:c + 1], gate)
        acc = acc + gate * ref[...]
    az = az_ref[...]
    o_ref[...] = (acc * (az * jax.nn.sigmoid(az))).astype(o_ref.dtype)


def _nsa_combine(ocmp, osel, owin, sm, az):
    M, W = ocmp.shape
    full = lambda w: pl.BlockSpec((M, w), lambda i: (0, 0))
    return pl.pallas_call(
        _nsa_combine_kernel,
        grid=(1,),
        in_specs=[full(W), full(W), full(W), full(128), full(W)],
        out_specs=full(W),
        out_shape=jax.ShapeDtypeStruct((M, W), _BF16),
        name="nsa_combine",
    )(ocmp, osel, owin, sm, az)


def _mixer_inputs(x, norm_w, pw, m_width, a_width, a_dh):
    B, T, D = x.shape
    w_main, w_small, seg, hw, hm, groups, norm_tiles = pw
    M = B * T
    tm = next((t for t in (1024, 512) if M % t == 0), M)
    a, q, az, kv4, kvw, sm, *kv4t = _proj_in(x.reshape(M, D), norm_w, w_main, w_small, seg, hw, hm,
                                             groups, norm_tiles, tm, T)
    r3 = lambda v: v.reshape(B, T, v.shape[-1])
    kv_rows = r3(kv4).reshape(B, T, 4, A_KV, a_dh)
    if kv4t:
        kv_rows = jnp.transpose(kv4t[0].reshape(B, 4, A_KV, a_dh, T), (0, 4, 1, 2, 3))
    return r3(a), r3(sm), r3(q), r3(kv4), r3(kvw), r3(az), kv_rows


def _mixer_output(x, m_out, a_out, p, ow):
    B, T, D = x.shape
    M = B * T
    tm = 256 if M % 256 == 0 else M
    y = _out_proj(x.reshape(M, D), m_out.reshape(M, -1), a_out.reshape(M, -1), p.reshape(M, -1), *ow, tm)
    return y.reshape(B, T, D)


def kernel(x_prompt, x_sample, cache_nsa_kv, cache_win_kv, state_mlstm_C, state_mlstm_n,
           state_mlstm_m, state_mlstm_conv, page_table, p_prompt, p_sample, norm_w, w_in,
           m_conv_w, m_conv_b, m_wq, m_wk, m_wv, m_b_i, m_b_f, m_norm_w, a_q_norm, a_k_norm,
           cmp_pe, cmp_w1, cmp_w2, w_out, ple_proj, ple_norm, ple_gate):
    xp, xs = x_prompt, x_sample
    B, T, D = xp.shape
    depth = w_in.shape[0]
    m_width = m_conv_w.shape[-1]
    a_width = D - m_width
    a_dh = a_q_norm.shape[-1]
    dh = m_width // M_HEADS
    per_layer = []
    for i in range(depth):
        mw = (m_conv_w[i], m_conv_b[i], m_wq[i], m_wk[i], m_wv[i], m_b_i[i], m_b_f[i], m_norm_w[i])
        cw = (cmp_pe[i], cmp_w1[i], cmp_w2[i], a_k_norm[i, 0])
        w_out_i = jnp.concatenate(
            [w_out[i][:m_width], _heads_to_agd(w_out[i][m_width:].T, a_dh).T], axis=0)
        ow = (w_out_i, ple_proj[i], ple_norm[i], ple_gate[i])
        pw = _prep_proj_weights(w_in[i], a_q_norm[i], a_k_norm[i], m_width, a_width, a_dh)
        cwp = _prep_cmp_weights(*cw)
        a3, sm, q, kv4, kvw, az, kv_p = _mixer_inputs(xp, norm_w[i], pw, m_width, a_width, a_dh)
        L = _MLSTM_L if T % _MLSTM_L == 0 else (M_CHUNK if T % M_CHUNK == 0 else T)
        m_out, conv_p, C_p, n_p, m_p = _mlstm(
            a3, sm, jnp.zeros((B, M_CONV - 1, m_width), _F32),
            jnp.zeros((B, M_HEADS, dh, dh), _F32), jnp.zeros((B, M_HEADS, dh), _F32),
            jnp.full((B, M_HEADS), -jnp.inf, _F32), *mw, L)
        kc, vc = _compress_prompt(kv4, cwp)
        a_out = _nsa_prompt_t(q, kv4, kvw, kc, vc, sm, az)
        xp_new = _mixer_output(xp, m_out, a_out, p_prompt[i], ow)
        win_p = kvw[:, -min(WINDOW, T):].reshape(B, -1, 2, A_KV, a_dh)
        a3, sm, q, kv4, kvw, az, kv_s = _mixer_inputs(xs, norm_w[i], pw, m_width, a_width, a_dh)
        Bd, Ts = xs.shape[:2]
        Ls = _MLSTM_L if Ts % _MLSTM_L == 0 else (M_CHUNK if Ts % M_CHUNK == 0 else Ts)
        m_out, conv_s, C_s, n_s, m_s = _mlstm(
            a3, sm, state_mlstm_conv[i], state_mlstm_C[i], state_mlstm_n[i], state_mlstm_m[i], *mw, Ls)
        past_len = page_table.shape[1] * PAGE_SIZE
        cache_l = cache_nsa_kv[i]
        n_pool = cache_l.shape[0]
        wb = cache_win_kv.shape[2]
        cache_t = jnp.transpose(cache_l, (0, 2, 3, 4, 1))
        kc, vc = _compress_sample(cache_t.reshape(n_pool, 8, 2 * a_dh, PAGE_SIZE), page_table, cwp)
        ocmp, owin, idx = _nsa_sample_front(
            q, kc, vc, cache_win_kv[i].reshape(Bd, wb, 2 * A_KV * a_dh), kvw, past_len)
        topn = min(SLC_TOPN, past_len // SLC_BLK + 1)
        q_rows = q.reshape(Bd, Ts, A_GROUP, A_KV, a_dh).transpose(0, 3, 1, 2, 4)
        q_rows = jnp.pad(q_rows, ((0, 0), (0, 0), (0, 0), (0, 8 - A_GROUP), (0, 0)))
        kv_new = kv4.reshape(Bd, Ts, 4 * A_KV, a_dh).transpose(0, 2, 1, 3)
        osel = _nsa_sample_gather(
            idx[:, :, :topn].reshape(Bd, -1), page_table, q_rows.reshape(Bd, A_KV * Ts, 8, a_dh), kv_new,
            cache_t, past_len)
        osel = osel.reshape(Bd, A_KV, Ts, 8, a_dh)[:, :, :, :A_GROUP].transpose(0, 2, 3, 1, 4)
        Ms = Bd * Ts
        a_out = _nsa_combine(ocmp.reshape(Ms, -1), osel.reshape(Ms, -1), owin.reshape(Ms, -1),
                             sm.reshape(Ms, -1), az.reshape(Ms, -1)).reshape(Bd, Ts, -1)
        xs_new = _mixer_output(xs, m_out, a_out, p_sample[i], ow)
        win_s = jnp.concatenate(
            [cache_win_kv[i], kvw.reshape(Bd, Ts, 2, A_KV, a_dh)], axis=1)[:, -wb:]
        per_layer.append((kv_p, kv_s, win_p, win_s, C_p, n_p, m_p, conv_p, C_s, n_s, m_s, conv_s))
        xp, xs = xp_new, xs_new
    (kv_p, kv_s, win_p, win_s, C_p, n_p, m_p, conv_p,
     C_s, n_s, m_s, conv_s) = [jnp.stack(a, axis=0) for a in zip(*per_layer)]
    return (xp, xs, kv_p, kv_s, win_p, win_s, C_p, n_p, m_p, conv_p, C_s, n_s, m_s, conv_s)
```

```python
import functools

import jax
import jax.numpy as jnp
from jax import lax
from jax.experimental import pallas as pl
from jax.experimental.pallas import tpu as pltpu

NORM_EPS = 1e-6
M_HEADS = 4
M_CONV = 4
M_CHUNK = 64
A_HEADS = 16
A_KV = 4
A_GROUP = A_HEADS // A_KV
CMP_STRIDE = 16
CMP_LEN = 2 * CMP_STRIDE
SLC_BLK = 64
SLC_TOPN = 16
WINDOW = 512
PAGE_SIZE = 128

_BF16 = jnp.bfloat16
_F32 = jnp.float32
_VMEM_LIMIT = 56 * 1024 * 1024


def _dot(a, b):
    return jnp.dot(a.astype(_BF16), b.astype(_BF16), preferred_element_type=_F32)


_PT = 512


def _proj_in_kernel(groups, norm_tiles, x_ref, nw_ref, w_ref, wsm_ref, seg_ref, hw_ref, hm_ref,
                    a_ref, q_ref, az_ref, kv4_ref, kvw_ref, sm_ref, *rest):
    xn_ref = rest[-1]
    t_refs = (None, None, None, rest[0] if len(rest) == 2 else None, None)
    j = pl.program_id(1)

    @pl.when(j == 0)
    def _():
        x = x_ref[...]
        ms = jnp.mean(x * x, axis=-1, keepdims=True)
        xn = x * lax.rsqrt(ms + NORM_EPS) * nw_ref[...]
        xn_ref[...] = xn.astype(_BF16)
        sm = jnp.dot(xn_ref[...], wsm_ref[...], preferred_element_type=_F32)
        lane = lax.broadcasted_iota(jnp.int32, sm.shape, 1)
        is_gate = (lane >= 2 * M_HEADS) & (lane < 2 * M_HEADS + 3 * A_HEADS)
        sm_ref[...] = jnp.where(is_gate, jax.nn.sigmoid(sm), sm)

    y = jnp.dot(xn_ref[...], w_ref[...], preferred_element_type=_F32)
    outs = (a_ref, q_ref, az_ref, kv4_ref, kvw_ref)

    def emit(o_ref, t_ref, val):
        o_ref[...] = val
        if t_ref is not None:
            t_ref[...] = val.T

    for (start, n), o_ref, t_ref in zip(groups, outs, t_refs):
        plain = [t for t in range(start, start + n) if t not in norm_tiles]
        normed = [t for t in range(start, start + n) if t in norm_tiles]
        if plain:
            cond = functools.reduce(jnp.logical_or, [j == t for t in plain])

            @pl.when(cond)
            def _(o_ref=o_ref, t_ref=t_ref):
                emit(o_ref, t_ref, y)

        if normed:
            cond = functools.reduce(jnp.logical_or, [j == t for t in normed])

            @pl.when(cond)
            def _(o_ref=o_ref, t_ref=t_ref):
                ms = jnp.dot((y * y).astype(_BF16), seg_ref[...], preferred_element_type=_F32)
                yn = y * lax.rsqrt(ms + NORM_EPS) * hw_ref[...]
                emit(o_ref, t_ref, jnp.where(hm_ref[...] > 0, yn, y))


def _proj_in(x2d, norm_w, w_main, w_small, seg, hw, hm, groups, norm_tiles, tm, seq_len):
    M, D = x2d.shape
    n_tiles = w_main.shape[1] // _PT
    widths = [n * _PT for _, n in groups]

    def out_map(start, n):
        return lambda i, j: (i, jnp.clip(j - start, 0, n - 1))

    out_specs = [pl.BlockSpec((tm, _PT), out_map(s, n)) for s, n in groups]
    out_specs.append(pl.BlockSpec((tm, 128), lambda i, j: (i, 0)))
    out_shape = [jax.ShapeDtypeStruct((M, w), _F32) for w in widths]
    out_shape.append(jax.ShapeDtypeStruct((M, 128), _F32))
    if seq_len % tm == 0:
        tpb = seq_len // tm
        s4, n4 = groups[3]
        out_specs.append(pl.BlockSpec(
            (None, _PT, tm), lambda i, j: (i // tpb, jnp.clip(j - s4, 0, n4 - 1), i % tpb)))
        out_shape.append(jax.ShapeDtypeStruct((M // seq_len, widths[3], seq_len), _F32))
    return pl.pallas_call(
        functools.partial(_proj_in_kernel, groups, norm_tiles),
        grid=(M // tm, n_tiles),
        in_specs=[
            pl.BlockSpec((tm, D), lambda i, j: (i, 0)),
            pl.BlockSpec((1, D), lambda i, j: (0, 0)),
            pl.BlockSpec((D, _PT), lambda i, j: (0, j)),
            pl.BlockSpec((D, 128), lambda i, j: (0, 0)),
            pl.BlockSpec((_PT, _PT), lambda i, j: (0, 0)),
            pl.BlockSpec((None, 1, _PT), lambda i, j: (j, 0, 0)),
            pl.BlockSpec((None, 1, _PT), lambda i, j: (j, 0, 0)),
        ],
        out_specs=out_specs,
        out_shape=out_shape,
        scratch_shapes=[pltpu.VMEM((tm, D), _BF16)],
        compiler_params=pltpu.CompilerParams(
            dimension_semantics=("arbitrary", "arbitrary"), vmem_limit_bytes=_VMEM_LIMIT),
        name="proj_in",
    )(x2d, norm_w.reshape(1, D), w_main, w_small, seg, hw, hm)


def _heads_to_agd(w, a_dh):
    lead = w.shape[:-1]
    return jnp.swapaxes(w.reshape(lead + (A_KV, A_GROUP, a_dh)), -3, -2).reshape(w.shape)


def _prep_proj_weights(w_in, a_q_norm, a_k_norm, m_width, a_width, a_dh):
    kvw = A_KV * a_dh
    sizes = (m_width, m_width, m_width, M_HEADS, M_HEADS, a_width, 6 * kvw, 3 * A_HEADS, a_width)
    offs = [0]
    for s in sizes:
        offs.append(offs[-1] + s)
    u0, o0, z0, i0, f0, q0, kv0, g0, az0, end = offs
    assert end == w_in.shape[1]
    w_main = jnp.concatenate([
        w_in[:, u0:i0], _heads_to_agd(w_in[:, q0:kv0], a_dh), _heads_to_agd(w_in[:, az0:end], a_dh),
        w_in[:, kv0:kv0 + 4 * kvw], w_in[:, kv0 + 4 * kvw:g0]], axis=1).astype(_BF16)
    n_small = 2 * M_HEADS + 3 * A_HEADS
    w_small = jnp.concatenate([
        w_in[:, i0:q0], w_in[:, g0:az0],
        jnp.zeros((w_in.shape[0], 128 - n_small), w_in.dtype)], axis=1).astype(_BF16)
    widths = (3 * m_width, a_width, a_width, 4 * kvw, 2 * kvw)
    groups, start = [], 0
    for w in widths:
        assert w % _PT == 0
        groups.append((start, w // _PT))
        start += w // _PT
    n_tiles = start
    hw = jnp.zeros((n_tiles * _PT,), _F32)
    hm = jnp.zeros((n_tiles * _PT,), _F32)
    qs = groups[1][0] * _PT
    hw = hw.at[qs:qs + a_width].set(jnp.tile(a_q_norm, a_width // a_dh))
    hm = hm.at[qs:qs + a_width].set(1.0)
    k4 = groups[3][0] * _PT
    hw = hw.at[k4 + 2 * kvw:k4 + 3 * kvw].set(jnp.tile(a_k_norm[1], A_KV))
    hm = hm.at[k4 + 2 * kvw:k4 + 3 * kvw].set(1.0)
    kw = groups[4][0] * _PT
    hw = hw.at[kw:kw + kvw].set(jnp.tile(a_k_norm[2], A_KV))
    hm = hm.at[kw:kw + kvw].set(1.0)
    assert kvw * 2 == _PT and _PT % a_dh == 0
    norm_tiles = tuple(range(groups[1][0], groups[1][0] + groups[1][1])) + (
        (k4 + 2 * kvw) // _PT, kw // _PT)
    r = jnp.arange(_PT) // a_dh
    seg = ((r[:, None] == r[None, :]).astype(_F32) / a_dh).astype(_BF16)
    return (w_main, w_small, seg, hw.reshape(n_tiles, 1, _PT), hm.reshape(n_tiles, 1, _PT),
            tuple(groups), norm_tiles)


def _out_proj_kernel(mw, x_ref, m_ref, a_ref, p_ref, wo_ref, pp_ref, pn_ref, pg_ref, o_ref):
    h = x_ref[...] + _dot(m_ref[...], wo_ref[0:mw, :]) + _dot(a_ref[...], wo_ref[mw:, :])
    ms = jnp.mean(h * h, axis=-1, keepdims=True)
    hn = h * lax.rsqrt(ms + NORM_EPS) * pn_ref[...]
    gate = jax.nn.sigmoid(_dot(hn, pg_ref[...]))
    o_ref[...] = h + gate * _dot(p_ref[...], pp_ref[...])


def _out_proj(x2d, m_out, a_out, p2d, w_out, ple_proj, ple_norm, ple_gate, tm):
    M, D = x2d.shape
    mw, aw, dp = m_out.shape[1], a_out.shape[1], p2d.shape[1]

    def const(shape):
        return pl.BlockSpec(shape, lambda i: (0, 0), pipeline_mode=pl.Buffered(1))

    return pl.pallas_call(
        functools.partial(_out_proj_kernel, mw),
        grid=(M // tm,),
        in_specs=[
            pl.BlockSpec((tm, D), lambda i: (i, 0)),
            pl.BlockSpec((tm, mw), lambda i: (i, 0)),
            pl.BlockSpec((tm, aw), lambda i: (i, 0)),
            pl.BlockSpec((tm, dp), lambda i: (i, 0)),
            const((mw + aw, D)),
            const((dp, D)),
            const((1, D)),
            const((D, D)),
        ],
        out_specs=pl.BlockSpec((tm, D), lambda i: (i, 0)),
        out_shape=jax.ShapeDtypeStruct((M, D), _F32),
        compiler_params=pltpu.CompilerParams(
            dimension_semantics=("arbitrary",), vmem_limit_bytes=_VMEM_LIMIT),
        name="out_proj",
    )(x2d, m_out, a_out, p2d, w_out.astype(_BF16), ple_proj.astype(_BF16),
      ple_norm.reshape(1, D), ple_gate.astype(_BF16))


_MLSTM_L = 256
_CONV_PAD = 8


def _mlstm_kernel(L, dh, n_steps,
                  u_ref, o_ref, z_ref, sm_ref, cbuf_ref, c0_ref, n0_ref, m0_ref,
                  wq_ref, wk_ref, wv_ref, cw_ref, cb_ref, bi_ref, bf_ref, nw_ref,
                  out_ref, cout_ref, nout_ref, mout_ref, convout_ref,
                  xbuf, c_s, n_s, m_s):
    s = pl.program_id(1)
    tail = M_CONV - 1

    @pl.when(s == 0)
    def _():
        c_s[...] = c0_ref[...]
        n_s[...] = n0_ref[...]
        m_s[...] = m0_ref[...]
        xbuf[_CONV_PAD - tail:_CONV_PAD, :] = cbuf_ref[...]

    u_all = u_ref[...]
    xbuf[_CONV_PAD:_CONV_PAD + L, :] = u_all
    c = cb_ref[...]
    for j in range(M_CONV):
        c = c + xbuf[_CONV_PAD - tail + j:_CONV_PAD - tail + j + L, :] * cw_ref[j:j + 1, :]
    xbuf[_CONV_PAD - tail:_CONV_PAD, :] = u_all[L - tail:L, :]
    ch_all = c * jax.nn.sigmoid(c)

    sm = sm_ref[...]
    ii = lax.broadcasted_iota(jnp.int32, (L, L), 0)
    jj = lax.broadcasted_iota(jnp.int32, (L, L), 1)
    eye = ii == jj
    causal = jj <= ii

    for h in range(M_HEADS):
        cols = slice(h * dh, (h + 1) * dh)
        u = u_all[:, cols]
        ch = ch_all[:, cols]
        q = _dot(ch, wq_ref[h])
        k = _dot(ch, wk_ref[h]) * (dh ** -0.5)
        v = _dot(u, wv_ref[h])

        li_col = sm[:, h:h + 1] + bi_ref[h][:, 0:1]
        f_in = sm[:, M_HEADS + h:M_HEADS + h + 1] + bf_ref[h][:, 0:1]
        lf_col = jnp.minimum(f_in, 0.0) - jnp.log(1.0 + jnp.exp(-jnp.abs(f_in)))

        lf_row = jnp.sum(jnp.where(eye, lf_col, 0.0), axis=0, keepdims=True)
        b_col = jnp.sum(jnp.where(causal, lf_row, 0.0), axis=1, keepdims=True)

        m_prev = m_s[h][0:1, 0:1]
        c_prev = c_s[h]
        n_prev = n_s[h]
        g_col = li_col - b_col
        g_row = jnp.sum(jnp.where(eye, g_col, 0.0), axis=0, keepdims=True)
        gm = jnp.where(causal, g_row, -jnp.inf)
        big_m = jnp.maximum(m_prev, jnp.max(gm, axis=1, keepdims=True))
        m_t = b_col + big_m
        qb = q.astype(_BF16)
        sc = lax.dot_general(qb, k.astype(_BF16), (((1,), (1,)), ((), ())),
                             preferred_element_type=_F32) * jnp.exp(gm - big_m)
        a_col = jnp.exp(m_prev - big_m)
        num = _dot(sc, v) + a_col * _dot(qb, c_prev)
        den = jnp.sum(sc, axis=1, keepdims=True) + a_col * jnp.sum(q * n_prev, axis=1, keepdims=True)
        hh = num / jnp.maximum(jnp.abs(den), jnp.exp(-m_t))

        b_end = b_col[L - 1:L, :]
        logw = b_end - b_col + li_col
        m_new = jnp.maximum(b_end + m_prev, jnp.max(logw, axis=0, keepdims=True))
        w_col = jnp.exp(logw - m_new)
        decay = jnp.exp(b_end + m_prev - m_new)
        c_s[h] = decay * c_prev + lax.dot_general(
            k.astype(_BF16), (w_col * v).astype(_BF16), (((0,), (0,)), ((), ())),
            preferred_element_type=_F32)
        n_s[h] = decay * n_prev + jnp.sum(w_col * k, axis=0, keepdims=True)
        m_s[h] = jnp.broadcast_to(m_new, (1, 128))

        hg = hh * jax.nn.sigmoid(o_ref[:, cols])
        hn = hg * lax.rsqrt(jnp.mean(hg * hg, axis=-1, keepdims=True) + NORM_EPS) * nw_ref[h]
        zz = z_ref[:, cols]
        out_ref[:, cols] = (hn * (zz * jax.nn.sigmoid(zz))).astype(out_ref.dtype)

    @pl.when(s == n_steps - 1)
    def _():
        cout_ref[...] = c_s[...]
        nout_ref[...] = n_s[...]
        mout_ref[...] = m_s[...]
        convout_ref[...] = u_all[L - tail:L, :]


def _mlstm(a3, sm, conv_buf, C0, n0, m0, conv_w, conv_b, wq, wk, wv, b_i, b_f, norm_w, L):
    B, T, mw3 = a3.shape
    mw = mw3 // 3
    dh = mw // M_HEADS
    tail = M_CONV - 1
    assert T % L == 0 and L >= tail and (L % 8 == 0)
    n_steps = T // L
    H = M_HEADS

    def lane_b(vec):
        return jnp.broadcast_to(vec.astype(_F32)[:, None, None], (H, 1, 128))

    m0b = jnp.broadcast_to(m0.astype(_F32)[:, :, None, None], (B, H, 1, 128))
    blk = lambda off: pl.BlockSpec((None, L, mw), lambda b, s: (b, s, off))
    full = lambda *shape: pl.BlockSpec(shape, lambda b, s: (0,) * len(shape))
    st = lambda r, c: pl.BlockSpec((None, H, r, c), lambda b, s: (b, 0, 0, 0))
    out, C, n, m, conv = pl.pallas_call(
        functools.partial(_mlstm_kernel, L, dh, n_steps),
        grid=(B, n_steps),
        in_specs=[
            blk(0), blk(1), blk(2),
            pl.BlockSpec((None, L, 128), lambda b, s: (b, s, 0)),
            pl.BlockSpec((None, tail, mw), lambda b, s: (b, 0, 0)),
            st(dh, dh), st(1, dh), st(1, 128),
            full(H, dh, dh), full(H, dh, dh), full(H, dh, dh),
            full(M_CONV, mw), full(1, mw),
            full(H, 1, 128), full(H, 1, 128), full(H, 1, dh),
        ],
        out_specs=[
            pl.BlockSpec((None, L, mw), lambda b, s: (b, s, 0)),
            st(dh, dh), st(1, dh), st(1, 128),
            pl.BlockSpec((None, tail, mw), lambda b, s: (b, 0, 0)),
        ],
        out_shape=[
            jax.ShapeDtypeStruct((B, T, mw), _BF16),
            jax.ShapeDtypeStruct((B, H, dh, dh), _F32),
            jax.ShapeDtypeStruct((B, H, 1, dh), _F32),
            jax.ShapeDtypeStruct((B, H, 1, 128), _F32),
            jax.ShapeDtypeStruct((B, tail, mw), _F32),
        ],
        scratch_shapes=[
            pltpu.VMEM((_CONV_PAD + L, mw), _F32),
            pltpu.VMEM((H, dh, dh), _F32),
            pltpu.VMEM((H, 1, dh), _F32),
            pltpu.VMEM((H, 1, 128), _F32),
        ],
        compiler_params=pltpu.CompilerParams(
            dimension_semantics=("arbitrary", "arbitrary"), vmem_limit_bytes=_VMEM_LIMIT),
        name="mlstm",
    )(a3, a3, a3, sm, conv_buf, C0, n0.reshape(B, H, 1, dh), m0b,
      wq.astype(_BF16), wk.astype(_BF16), wv.astype(_BF16), conv_w, conv_b.reshape(1, mw),
      lane_b(b_i), lane_b(b_f), norm_w.reshape(H, 1, dh))
    return out, conv, C, n.reshape(B, H, dh), m[:, :, 0, 0]


def _prep_cmp_weights(pe, w1, w2, k_norm):
    a_dh, hid = w1.shape[-2], w1.shape[-1]
    w1r = w1.reshape(2, 2, CMP_STRIDE, a_dh, hid)
    eye2 = jnp.eye(2, dtype=w1.dtype)
    w1p = jnp.einsum('cxjdh,ab->cjadxbh', w1r, eye2).reshape(2, CMP_STRIDE * 2 * a_dh, 2 * 2 * hid)
    bias = jnp.einsum('cxjd,cxjdh->ch', pe.reshape(2, 2, CMP_STRIDE, a_dh), w1r)
    bias2 = jnp.tile(bias, (1, 2)).reshape(2, 1, 2 * hid)
    w2p = jnp.einsum('chd,ab->cahbd', w2, eye2).reshape(2, 2 * hid, 2 * a_dh)
    r = jnp.arange(2 * a_dh) // a_dh
    seg = ((r[:, None] == r[None, :]).astype(_F32) / a_dh).astype(_BF16)
    kn = jnp.tile(k_norm, 2).reshape(1, 2 * a_dh)
    return w1p.astype(_BF16), bias2.astype(_F32), w2p.astype(_BF16), seg, kn.astype(_F32)


def _cmp_first_layer(rows_ref, n_seg, w1_ref, c):
    x = jnp.concatenate(
        [rows_ref[pl.ds(j, n_seg, stride=CMP_STRIDE), :].astype(_BF16)
         for j in range(CMP_STRIDE)], axis=1)
    return jnp.dot(x, w1_ref[c], preferred_element_type=_F32)


def _cmp_second_layer(p, n_seg, c, b_ref, w2_ref, seg_ref, kn_ref):
    hid = p[:, 0:128] + pltpu.roll(p[:, 128:256], n_seg - 1, axis=0) + b_ref[c]
    act = hid * jax.nn.sigmoid(hid)
    out = jnp.dot(act.astype(_BF16), w2_ref[c], preferred_element_type=_F32)
    if c == 0:
        ms = jnp.dot((out * out).astype(_BF16), seg_ref[...], preferred_element_type=_F32)
        out = out * lax.rsqrt(ms + NORM_EPS) * kn_ref[...]
    return out


def _compress_prompt_kernel(n_seg, r00, r01, r10, r11, w1_ref, b_ref, w2_ref, seg_ref, kn_ref,
                            kc_ref, vc_ref):
    for c, o_ref, refs in ((0, kc_ref, (r00, r01)), (1, vc_ref, (r10, r11))):
        for gp in range(2):
            p = _cmp_first_layer(refs[gp], n_seg, w1_ref, c)
            o_ref[:, gp * 128:(gp + 1) * 128] = _cmp_second_layer(p, n_seg, c, b_ref, w2_ref, seg_ref, kn_ref)


def _compress_prompt(kv4, cwp):
    B, T, W = kv4.shape
    assert W == 1024 and T % (CMP_STRIDE * 8) == 0
    n_seg = T // CMP_STRIDE
    w1p, bias2, w2p, seg, kn = cwp
    full = lambda a: pl.BlockSpec(a.shape, lambda b: (0,) * a.ndim)
    return pl.pallas_call(
        functools.partial(_compress_prompt_kernel, n_seg),
        grid=(B,),
        in_specs=[pl.BlockSpec((None, T, 128), functools.partial(lambda k, b: (b, 0, k), k))
                  for k in range(4)] + [full(w1p), full(bias2), full(w2p), full(seg), full(kn)],
        out_specs=[pl.BlockSpec((None, n_seg, 256), lambda b: (b, 0, 0))] * 2,
        out_shape=[jax.ShapeDtypeStruct((B, n_seg, 256), _F32)] * 2,
        compiler_params=pltpu.CompilerParams(
            dimension_semantics=("arbitrary",), vmem_limit_bytes=_VMEM_LIMIT),
        name="compress_prompt",
    )(kv4, kv4, kv4, kv4, w1p, bias2, w2p, seg, kn)


_QB = 256
_SEL_TK = 512
_GW = 32
_LOG2E = 1.4426950408889634


def _softmax_parts(s):
    m = jnp.max(s, axis=-1, keepdims=True)
    m = jnp.where(m == -jnp.inf, 0.0, m)
    e = jnp.exp(s - m)
    return e, 1.0 / jnp.maximum(jnp.sum(e, axis=-1, keepdims=True), 1e-30)


def _softmax_rows(s, mask):
    e, inv = _softmax_parts(jnp.where(mask, s, -jnp.inf))
    return e * inv


def _col_softmax_parts(s):
    m = jnp.max(s, axis=0, keepdims=True)
    m = jnp.where(m == -jnp.inf, 0.0, m)
    e = jnp.exp2(s - m)
    return e, 1.0 / jnp.maximum(jnp.sum(e, axis=0, keepdims=True), 1e-30)


def _tile_cols(x, n):
    return jnp.concatenate([x] * n, axis=1)


def _nsa_prompt_t_kernel(T, n_cmp, n_slc,
                         q_ref, ks_ref, kw_ref, kc_ref, vc_ref, sm_ref, az_ref, out_ref,
                         ksb, vst, kwb, vwt, vct, *scratch):
    QB, TK, G, A = _QB, _SEL_TK, A_KV, A_GROUP
    R = A * QB
    NB = T // 128
    qplt_ref, sel_ref, m_ref, l_ref, acc_ref, cmp_ref, win_ref = (
        scratch[i * G:(i + 1) * G] for i in range(7))
    DH = 256 // G

    @pl.when(pl.program_id(1) == 0)
    def _():
        for blk in range(NB):
            rows = slice(blk * 128, (blk + 1) * 128)
            ksb[rows, :] = ks_ref[rows, 0:256].astype(_BF16)
            vst[blk] = ks_ref[rows, 256:512].T.astype(_BF16)
            kwb[rows, :] = kw_ref[rows, 0:256].astype(_BF16)
            vwt[blk] = kw_ref[rows, 256:512].T.astype(_BF16)
        vct[...] = vc_ref[...].T.astype(_BF16)

    t0 = pl.program_id(1) * QB
    qt = q_ref[...].T
    for g in range(G):
        qplt_ref[g][...] = jnp.concatenate(
            [qt[a * 256 + g * DH:a * 256 + (g + 1) * DH, :] * (0.125 * _LOG2E) for a in range(A)],
            axis=1).astype(_BF16)
    tq = t0 + lax.broadcasted_iota(jnp.int32, (1, QB), 1)

    n_seg = kc_ref.shape[0]
    kc = kc_ref[...].astype(_BF16)
    ci = lax.broadcasted_iota(jnp.int32, (n_seg, QB), 0)
    cbias = _tile_cols(jnp.where((ci < n_cmp) & (ci * CMP_STRIDE + (CMP_LEN - 1) <= tq), 0.0, -jnp.inf), A)
    ratio = SLC_BLK // CMP_STRIDE
    topn = min(SLC_TOPN, n_slc)
    mj = lax.broadcasted_iota(jnp.int32, (_GW, n_seg), 0)
    mi = lax.broadcasted_iota(jnp.int32, (_GW, n_seg), 1)
    msel_t = ((mi >= ratio * mj - 1) & (mi <= ratio * mj + ratio - 1)).astype(_BF16)
    blk = lax.broadcasted_iota(jnp.int32, (_GW, QB), 0)
    cur = tq // SLC_BLK
    valid = (blk < n_slc) & (blk * SLC_BLK <= tq)
    forced = (blk == 0) | (blk == cur) | (blk == cur - 1)
    for g in range(G):
        hrows = slice(g * DH, (g + 1) * DH)
        s = jnp.dot(kc[:, hrows], qplt_ref[g][...], preferred_element_type=_F32) + cbias
        e, inv = _col_softmax_parts(s)
        cmp_ref[g][...] = jnp.dot(vct[hrows, :], e.astype(_BF16), preferred_element_type=_F32) * inv
        p = e * inv
        pg = p[:, 0:QB]
        for a in range(1, A):
            pg = pg + p[:, a * QB:(a + 1) * QB]
        pg_hi = pg.astype(_BF16)
        pg_lo = (pg - pg_hi.astype(_F32)).astype(_BF16)
        ps = jnp.dot(msel_t, pg_hi, preferred_element_type=_F32) \
            + jnp.dot(msel_t, pg_lo, preferred_element_type=_F32)
        score = jnp.where(valid, jnp.where(forced, jnp.inf, ps), -jnp.inf)
        rank = jnp.zeros((_GW, QB), _F32)
        for i in range(_GW):
            row = score[i:i + 1, :]
            ahead = (row > score) | ((row == score) & (blk > i))
            rank = rank + jnp.where(ahead, 1.0, 0.0)
        sel_ref[g][...] = jnp.where(rank < topn, 1.0, 0.0)

    for g in range(G):
        m_ref[g][...] = jnp.full((1, R), -jnp.inf, _F32)
        l_ref[g][...] = jnp.zeros((1, R), _F32)
        acc_ref[g][...] = jnp.zeros((DH, R), _F32)
    krow = lax.broadcasted_iota(jnp.int32, (TK, QB), 0)

    def sel_tile(kt, carry):
        k0 = pl.multiple_of(kt * TK, TK)
        kt_b = ksb[pl.ds(k0, TK), :]
        b0 = kt * (TK // 128)
        j0 = kt * (TK // SLC_BLK)
        vt_b = jnp.concatenate([vst[b0 + i] for i in range(TK // 128)], axis=1)
        causal = (k0 + krow) <= tq
        for g in range(G):
            member = jnp.concatenate(
                [jnp.broadcast_to(sel_ref[g][pl.ds(j0 + j, 1), :], (SLC_BLK, QB))
                 for j in range(TK // SLC_BLK)], axis=0)
            bias = _tile_cols(jnp.where(causal & (member > 0.5), 0.0, -jnp.inf), A)
            s = jnp.dot(kt_b[:, g * DH:(g + 1) * DH], qplt_ref[g][...],
                        preferred_element_type=_F32) + bias
            m_old = m_ref[g][...]
            m_new = jnp.maximum(m_old, jnp.max(s, axis=0, keepdims=True))
            m_safe = jnp.where(m_new == -jnp.inf, 0.0, m_new)
            p = jnp.exp2(s - m_safe)
            alpha = jnp.exp2(m_old - m_safe)
            l_ref[g][...] = alpha * l_ref[g][...] + jnp.sum(p, axis=0, keepdims=True)
            acc_ref[g][...] = alpha * acc_ref[g][...] + jnp.dot(
                vt_b[g * DH:(g + 1) * DH, :], p.astype(_BF16), preferred_element_type=_F32)
            m_ref[g][...] = m_new
        return carry

    lax.fori_loop(0, (t0 + QB + TK - 1) // TK, sel_tile, 0)

    WK = WINDOW + QB
    ws = pl.multiple_of(jnp.clip(t0 - WINDOW, 0, T - WK), QB)
    kw_b = kwb[pl.ds(ws, WK), :]
    wb0 = ws // 128
    vw_b = jnp.concatenate([vwt[wb0 + i] for i in range(WK // 128)], axis=1)
    diff = tq - (ws + lax.broadcasted_iota(jnp.int32, (WK, QB), 0))
    wbias = _tile_cols(jnp.where((diff >= 0) & (diff < WINDOW), 0.0, -jnp.inf), A)
    for g in range(G):
        s = jnp.dot(kw_b[:, g * DH:(g + 1) * DH], qplt_ref[g][...], preferred_element_type=_F32) + wbias
        e, inv = _col_softmax_parts(s)
        win_ref[g][...] = jnp.dot(vw_b[g * DH:(g + 1) * DH, :], e.astype(_BF16),
                                  preferred_element_type=_F32) * inv
        acc_ref[g][...] = acc_ref[g][...] * (1.0 / jnp.maximum(l_ref[g][...], 1e-30))

    smt = sm_ref[...].T
    g0 = 2 * M_HEADS
    for a in range(A):
        lanes = slice(a * QB, (a + 1) * QB)
        parts = []
        for g in range(G):
            head = g0 + g * A + a
            parts.append(smt[head:head + 1, :] * cmp_ref[g][:, lanes]
                         + smt[head + A_HEADS:head + A_HEADS + 1, :] * acc_ref[g][:, lanes]
                         + smt[head + 2 * A_HEADS:head + 2 * A_HEADS + 1, :] * win_ref[g][:, lanes])
        mix = jnp.concatenate(parts, axis=0)
        az = az_ref[:, a * 256:(a + 1) * 256]
        out_ref[:, a * 256:(a + 1) * 256] = (mix.T * (az * jax.nn.sigmoid(az))).astype(out_ref.dtype)


def _nsa_prompt_t(q, kv4, kvw, kc, vc, sm, az):
    B, T, _ = q.shape
    n_seg = kc.shape[1]
    n_cmp = n_seg - 1
    n_slc = T // SLC_BLK
    assert T % _SEL_TK == 0 and T >= WINDOW + _QB and n_slc <= _GW and n_seg % 128 == 0
    R = A_GROUP * _QB
    blk = lambda w: pl.BlockSpec((None, _QB, w), lambda b, i: (b, i, 0))
    return pl.pallas_call(
        functools.partial(_nsa_prompt_t_kernel, T, n_cmp, n_slc),
        grid=(B, T // _QB),
        in_specs=[
            blk(1024),
            pl.BlockSpec((None, T, 512), lambda b, i: (b, 0, 1)),
            pl.BlockSpec((None, T, 512), lambda b, i: (b, 0, 0)),
            pl.BlockSpec((None, n_seg, 256), lambda b, i: (b, 0, 0)),
            pl.BlockSpec((None, n_seg, 256), lambda b, i: (b, 0, 0)),
            blk(128), blk(1024),
        ],
        out_specs=blk(1024),
        out_shape=jax.ShapeDtypeStruct((B, T, 1024), _BF16),
        scratch_shapes=(
            [pltpu.VMEM((T, 256), _BF16), pltpu.VMEM((T // 128, 256, 128), _BF16),
             pltpu.VMEM((T, 256), _BF16), pltpu.VMEM((T // 128, 256, 128), _BF16),
             pltpu.VMEM((256, n_seg), _BF16)]
            + [pltpu.VMEM((256 // A_KV, R), _BF16)] * A_KV
            + [pltpu.VMEM((_GW, _QB), _F32)] * A_KV
            + [pltpu.VMEM((1, R), _F32)] * (2 * A_KV)
            + [pltpu.VMEM((256 // A_KV, R), _F32)] * (3 * A_KV)
        ),
        compiler_params=pltpu.CompilerParams(
            dimension_semantics=("arbitrary", "arbitrary"), vmem_limit_bytes=_VMEM_LIMIT),
        name="nsa_prompt",
    )(q, kv4, kvw, kc, vc, sm, az)


_CMP_PAGES = 32


def _compress_sample_kernel(P, n_chunks, n_seg, pt_ref, cache_ref, w1_ref, b_ref, w2_ref, seg_ref,
                            kn_ref, kc_ref, vc_ref, stage, lanes, per, sem):
    b = pl.program_id(0)
    ch = pl.program_id(1)
    step = b * n_chunks + ch
    n_steps = pl.num_programs(0) * n_chunks
    slot = step % 2
    rows = P * PAGE_SIZE

    def page_copy(bb, cc, sl, p):
        page = pt_ref[bb, cc * P + p]
        return pltpu.make_async_copy(cache_ref.at[page, pl.ds(0, 4)], stage.at[sl, p], sem.at[sl])

    @pl.when(step == 0)
    def _():
        for p in range(P):
            page_copy(b, ch, slot, p).start()

    @pl.when(step + 1 < n_steps)
    def _():
        nxt = step + 1
        for p in range(P):
            page_copy(nxt // n_chunks, nxt % n_chunks, 1 - slot, p).start()

    pltpu.make_async_copy(cache_ref.at[pl.ds(0, P), pl.ds(0, 4)], stage.at[slot], sem.at[slot]).wait()

    segs = rows // CMP_STRIDE

    for k in range(4):
        for p in range(P):
            lanes[k, p * PAGE_SIZE:(p + 1) * PAGE_SIZE, :] = stage[slot, p, k].T
        per[pl.ds(pl.multiple_of(ch * segs, segs), segs), k * 256:(k + 1) * 256] = _cmp_first_layer(
            lanes.at[k], segs, w1_ref, k // 2)

    @pl.when(ch == n_chunks - 1)
    def _():
        for k in range(4):
            c, gp = k // 2, k % 2
            o_ref = kc_ref if c == 0 else vc_ref
            o_ref[:, gp * 128:(gp + 1) * 128] = _cmp_second_layer(
                per[:, k * 256:(k + 1) * 256], n_seg, c, b_ref, w2_ref, seg_ref, kn_ref)


def _compress_sample(cache_t, page_table, cwp):
    B, n_pages = page_table.shape
    P = _CMP_PAGES if n_pages % _CMP_PAGES == 0 else n_pages
    n_chunks = n_pages // P
    n_seg = n_pages * PAGE_SIZE // CMP_STRIDE
    w1p, bias2, w2p, seg, kn = cwp
    full = lambda a: pl.BlockSpec(a.shape, lambda b, c, pt: (0,) * a.ndim)
    grid_spec = pltpu.PrefetchScalarGridSpec(
        num_scalar_prefetch=1,
        grid=(B, n_chunks),
        in_specs=[pl.BlockSpec(memory_space=pl.ANY),
                  full(w1p), full(bias2), full(w2p), full(seg), full(kn)],
        out_specs=[pl.BlockSpec((None, n_seg, 256), lambda b, c, pt: (b, 0, 0))] * 2,
        scratch_shapes=[
            pltpu.VMEM((2, P, 4, 128, PAGE_SIZE), _F32),
            pltpu.VMEM((4, P * PAGE_SIZE, 128), _F32),
            pltpu.VMEM((n_seg, 1024), _F32),
            pltpu.SemaphoreType.DMA((2,)),
        ])
    return pl.pallas_call(
        functools.partial(_compress_sample_kernel, P, n_chunks, n_seg),
        grid_spec=grid_spec,
        out_shape=[jax.ShapeDtypeStruct((B, n_seg, 256), _F32)] * 2,
        compiler_params=pltpu.CompilerParams(
            dimension_semantics=("arbitrary", "arbitrary"), vmem_limit_bytes=_VMEM_LIMIT),
        name="compress_sample",
    )(page_table, cache_t, w1p, bias2, w2p, seg, kn)


def _merge_heads(o_ref, o_g, g, Ts):
    gm = (lax.broadcasted_iota(jnp.int32, (Ts, 256), 1) // 64) == g
    for a in range(A_GROUP):
        part = jnp.where(gm, o_g[a * Ts:(a + 1) * Ts], 0.0)
        cols = slice(a * 256, (a + 1) * 256)
        if g == 0:
            o_ref[:, cols] = part
        else:
            o_ref[:, cols] += part


def _nsa_sample_front_kernel(Ts, past_len, n_cmp, n_slc, NL, topn,
                             q_ref, kc_ref, vc_ref, wold_ref, wnew_ref,
                             ocmp_ref, owin_ref, idx_ref, qpl_ref):
    G, A = A_KV, A_GROUP
    R = A * Ts
    lane256 = lax.broadcasted_iota(jnp.int32, (Ts, 256), 1)
    q = q_ref[...]
    for g in range(G):
        gm = (lane256 // 64) == g
        qpl_ref[g] = jnp.concatenate(
            [jnp.where(gm, q[:, a * 256:(a + 1) * 256] * 0.125, 0.0) for a in range(A)],
            axis=0).astype(_BF16)
    tq_r = past_len + lax.broadcasted_iota(jnp.int32, (R, 1), 0) % Ts
    tq = past_len + lax.broadcasted_iota(jnp.int32, (Ts, 1), 0)

    n_seg = kc_ref.shape[0]
    kc = kc_ref[...].astype(_BF16)
    vc = vc_ref[...].astype(_BF16)
    ci = lax.broadcasted_iota(jnp.int32, (R, n_seg), 1)
    cmask = (ci < n_cmp) & (ci * CMP_STRIDE + (CMP_LEN - 1) <= tq_r)
    ratio = SLC_BLK // CMP_STRIDE
    mi = lax.broadcasted_iota(jnp.int32, (n_seg, NL), 0)
    mj = lax.broadcasted_iota(jnp.int32, (n_seg, NL), 1)
    msel = ((mi >= ratio * mj - 1) & (mi <= ratio * mj + ratio - 1)).astype(_BF16)
    lane = lax.broadcasted_iota(jnp.int32, (Ts, NL), 1)
    cur = tq // SLC_BLK
    valid = (lane < n_slc) & (lane * SLC_BLK <= tq)
    forced = (lane == 0) | (lane == cur) | (lane == cur - 1)
    pgs = []
    for g in range(G):
        s = lax.dot_general(qpl_ref[g], kc, (((1,), (1,)), ((), ())), preferred_element_type=_F32)
        p = _softmax_rows(s, cmask)
        _merge_heads(ocmp_ref, jnp.dot(p.astype(_BF16), vc, preferred_element_type=_F32), g, Ts)
        pg = p[0:Ts]
        for a in range(1, A):
            pg = pg + p[a * Ts:(a + 1) * Ts]
        pgs.append(pg)
    pg = jnp.concatenate(pgs, axis=0)
    pg_hi = pg.astype(_BF16)
    pg_lo = (pg - pg_hi.astype(_F32)).astype(_BF16)
    ps = jnp.dot(pg_hi, msel, preferred_element_type=_F32) \
        + jnp.dot(pg_lo, msel, preferred_element_type=_F32)
    valid = jnp.concatenate([valid] * G, axis=0)
    forced = jnp.concatenate([forced] * G, axis=0)
    score = jnp.where(valid, jnp.where(forced, jnp.inf, ps), -jnp.inf)

    lane_f = lax.broadcasted_iota(jnp.int32, (G * Ts, NL), 1).astype(_F32)
    out_lane = lax.broadcasted_iota(jnp.int32, (G * Ts, 128), 1)
    avail = lane_f < n_slc
    picks = jnp.zeros((G * Ts, 128), _F32)
    for k in range(topn):
        mx = jnp.max(jnp.where(avail, score, -jnp.inf), axis=1, keepdims=True)
        pick = jnp.min(jnp.where(avail & (score == mx), lane_f, float(NL)), axis=1, keepdims=True)
        avail = avail & (lane_f != pick)
        picks = jnp.where(out_lane == k, pick, picks)
    idx_ref[...] = picks.astype(jnp.int32)

    wb = wold_ref.shape[0]
    k_old = wold_ref[:, 0:256].astype(_BF16)
    v_old = wold_ref[:, 256:512].astype(_BF16)
    k_new = wnew_ref[:, 0:256]
    v_new = wnew_ref[:, 256:512]
    pw_old = past_len - wb + lax.broadcasted_iota(jnp.int32, (R, wb), 1)
    d_old = tq_r - pw_old
    m_old = (pw_old >= 0) & (d_old >= 0) & (d_old < WINDOW)
    d_new = tq_r - (past_len + lax.broadcasted_iota(jnp.int32, (R, Ts), 1))
    m_new = (d_new >= 0) & (d_new < WINDOW)
    for g in range(G):
        qg = qpl_ref[g]
        s1 = lax.dot_general(qg, k_old, (((1,), (1,)), ((), ())), preferred_element_type=_F32)
        s2 = lax.dot_general(qg.astype(_F32), k_new, (((1,), (1,)), ((), ())),
                             preferred_element_type=_F32)
        s1 = jnp.where(m_old, s1, -jnp.inf)
        s2 = jnp.where(m_new, s2, -jnp.inf)
        mx = jnp.maximum(jnp.max(s1, axis=1, keepdims=True), jnp.max(s2, axis=1, keepdims=True))
        mx = jnp.where(mx == -jnp.inf, 0.0, mx)
        e1 = jnp.where(m_old, jnp.exp(s1 - mx), 0.0)
        e2 = jnp.where(m_new, jnp.exp(s2 - mx), 0.0)
        den = jnp.maximum(jnp.sum(e1, axis=1, keepdims=True) + jnp.sum(e2, axis=1, keepdims=True), 1e-30)
        o = (jnp.dot(e1.astype(_BF16), v_old, preferred_element_type=_F32)
             + jnp.dot(e2, v_new, preferred_element_type=_F32)) / den
        _merge_heads(owin_ref, o, g, Ts)


def _nsa_sample_front(q, kc, vc, win_old, win_new, past_len):
    B, Ts, _ = q.shape
    n_seg = kc.shape[1]
    n_cmp = (past_len + Ts) // CMP_STRIDE - 1
    assert n_cmp == n_seg - 1 and Ts % 8 == 0 and Ts <= SLC_BLK
    n_slc = past_len // SLC_BLK + 1
    NL = -(-n_slc // 128) * 128
    topn = min(SLC_TOPN, n_slc)
    wb = win_old.shape[1]
    R = A_GROUP * Ts
    b3 = lambda r, w: pl.BlockSpec((None, r, w), lambda b: (b, 0, 0))
    return pl.pallas_call(
        functools.partial(_nsa_sample_front_kernel, Ts, past_len, n_cmp, n_slc, NL, topn),
        grid=(B,),
        in_specs=[b3(Ts, 1024), b3(n_seg, 256), b3(n_seg, 256), b3(wb, 512), b3(Ts, 512)],
        out_specs=[b3(Ts, 1024), b3(Ts, 1024), b3(A_KV * Ts, 128)],
        out_shape=[jax.ShapeDtypeStruct((B, Ts, 1024), _F32), jax.ShapeDtypeStruct((B, Ts, 1024), _F32),
                   jax.ShapeDtypeStruct((B, A_KV * Ts, 128), jnp.int32)],
        scratch_shapes=[pltpu.VMEM((A_KV, R, 256), _BF16)],
        compiler_params=pltpu.CompilerParams(
            dimension_semantics=("arbitrary",), vmem_limit_bytes=_VMEM_LIMIT),
        name="nsa_sample_front",
    )(q, kc, vc, win_old, win_new)


_GATHER_SLOTS = 4


def _div_nonneg(x, n):
    if n & (n - 1) == 0:
        return lax.shift_right_logical(x, jnp.full_like(x, n.bit_length() - 1))
    return x // n


def _mod_nonneg(x, n):
    if n & (n - 1) == 0:
        return jnp.bitwise_and(x, n - 1)
    return x % n


def _nsa_sample_gather_kernel(Ts, n_past_blk, past_len, topn,
                              idx_ref, pt_ref, q_ref, kvn_ref, cache_ref, o_ref, kvbuf, sem):
    b = pl.program_id(0)
    n_it = A_KV * Ts
    bpp = PAGE_SIZE // SLC_BLK
    nk = topn * PAGE_SIZE

    def issue(it, sl):
        g = _div_nonneg(it, Ts)
        for k in range(topn):
            blk = jnp.minimum(idx_ref[b, it * topn + k], n_past_blk - 1)
            page = pt_ref[b, _div_nonneg(blk, bpp)]
            pltpu.make_async_copy(cache_ref.at[page, pl.ds(2, 2), g], kvbuf.at[sl, k], sem.at[sl]).start()

    for i in range(_GATHER_SLOTS - 1):
        issue(i, i)
    lane = lax.broadcasted_iota(jnp.int32, (1, nk), 1)
    slot_of_lane = lane // PAGE_SIZE
    row = lane % PAGE_SIZE
    half = row // SLC_BLK
    new_r = lax.broadcasted_iota(jnp.int32, (1, Ts), 1)

    def body(it, carry):
        sl = _mod_nonneg(it, _GATHER_SLOTS)

        @pl.when(it + _GATHER_SLOTS - 1 < n_it)
        def _():
            issue(it + _GATHER_SLOTS - 1, _mod_nonneg(it + _GATHER_SLOTS - 1, _GATHER_SLOTS))

        pltpu.make_async_copy(cache_ref.at[pl.ds(0, topn), pl.ds(0, 2), 0], kvbuf.at[sl], sem.at[sl]).wait()

        t = _mod_nonneg(it, Ts)
        g = _div_nonneg(it, Ts)
        tq = past_len + t
        pos = jnp.zeros((1, nk), jnp.int32)
        past = jnp.zeros((1, nk), jnp.int32)
        new_sel = jnp.zeros((1, Ts), jnp.int32)
        for k in range(topn):
            blk = idx_ref[b, it * topn + k]
            in_slot = slot_of_lane == k
            hit = in_slot & (half == _mod_nonneg(blk, bpp))
            pos = jnp.where(in_slot, _div_nonneg(blk, bpp) * PAGE_SIZE + row, pos)
            past = jnp.where(hit, jnp.where(blk < n_past_blk, 1, 0), past)
            new_sel = jnp.maximum(new_sel, jnp.where(blk == n_past_blk, 1, 0))
        m1 = (past > 0) & (pos <= tq)
        m2 = (new_sel > 0) & (n_past_blk * SLC_BLK + new_r <= tq)
        q8 = q_ref[it] * 0.125
        kt = jnp.concatenate([kvbuf[sl, k, 0] for k in range(topn)], axis=1).astype(_BF16)
        vt = jnp.concatenate([kvbuf[sl, k, 1] for k in range(topn)], axis=1).astype(_BF16)
        s1 = jnp.dot(q8.astype(_BF16), kt, preferred_element_type=_F32)
        s2 = lax.dot_general(q8, kvn_ref[2 * A_KV + g], (((1,), (1,)), ((), ())),
                             preferred_element_type=_F32)
        s1 = jnp.where(m1, s1, -jnp.inf)
        s2 = jnp.where(m2, s2, -jnp.inf)
        mx = jnp.maximum(jnp.max(s1, axis=1, keepdims=True), jnp.max(s2, axis=1, keepdims=True))
        mx = jnp.where(mx == -jnp.inf, 0.0, mx)
        e1 = jnp.where(m1, jnp.exp(s1 - mx), 0.0)
        e2 = jnp.where(m2, jnp.exp(s2 - mx), 0.0)
        den = jnp.maximum(jnp.sum(e1, axis=1, keepdims=True) + jnp.sum(e2, axis=1, keepdims=True), 1e-30)
        o_ref[it] = (lax.dot_general(e1.astype(_BF16), vt, (((1,), (1,)), ((), ())),
                                     preferred_element_type=_F32)
                     + jnp.dot(e2, kvn_ref[3 * A_KV + g], preferred_element_type=_F32)) / den
        return carry

    lax.fori_loop(0, n_it, body, 0)


def _nsa_sample_gather(idx, page_table, q_rows, kv_new, cache4, past_len):
    B, n_it, _, dh = q_rows.shape
    Ts = n_it // A_KV
    n_past_blk = past_len // SLC_BLK
    topn = idx.shape[1] // n_it
    grid_spec = pltpu.PrefetchScalarGridSpec(
        num_scalar_prefetch=2,
        grid=(B,),
        in_specs=[pl.BlockSpec((None, n_it, 8, dh), lambda b, i, p: (b, 0, 0, 0)),
                  pl.BlockSpec((None, 4 * A_KV, Ts, dh), lambda b, i, p: (b, 0, 0, 0)),
                  pl.BlockSpec(memory_space=pl.ANY)],
        out_specs=pl.BlockSpec((None, n_it, 8, dh), lambda b, i, p: (b, 0, 0, 0)),
        scratch_shapes=[
            pltpu.VMEM((_GATHER_SLOTS, topn, 2, dh, PAGE_SIZE), _F32),
            pltpu.SemaphoreType.DMA((_GATHER_SLOTS,)),
        ])
    return pl.pallas_call(
        functools.partial(_nsa_sample_gather_kernel, Ts, n_past_blk, past_len, topn),
        grid_spec=grid_spec,
        out_shape=jax.ShapeDtypeStruct((B, n_it, 8, dh), _F32),
        compiler_params=pltpu.CompilerParams(
            dimension_semantics=("arbitrary",), vmem_limit_bytes=_VMEM_LIMIT),
        name="nsa_sample_gather",
    )(idx, page_table, q_rows, kv_new, cache4)


def _nsa_combine_kernel(ocmp_ref, osel_ref, owin_ref, sm_ref, az_ref, o_ref):
    sm = sm_ref[...]
    shape = ocmp_ref.shape
    head_lane = lax.broadcasted_iota(jnp.int32, shape, 1) // 64
    g0 = 2 * M_HEADS
    acc = jnp.zeros(shape, _F32)
    for br, ref in enumerate((ocmp_ref, osel_ref, owin_ref)):
        gate = jnp.zeros(shape, _F32)
        for a in range(A_GROUP):
            for g in range(A_KV):
                c = g0 + br * A_HEADS + g * A_GROUP + a
                gate = jnp.where(head_lane == a * A_KV + g, sm[:, c:c + 1], gate)
        acc = acc + gate * ref[...]
    az = az_ref[...]
    o_ref[...] = (acc * (az * jax.nn.sigmoid(az))).astype(o_ref.dtype)


def _nsa_combine(ocmp, osel, owin, sm, az):
    M, W = ocmp.shape
    full = lambda w: pl.BlockSpec((M, w), lambda i: (0, 0))
    return pl.pallas_call(
        _nsa_combine_kernel,
        grid=(1,),
        in_specs=[full(W), full(W), full(W), full(128), full(W)],
        out_specs=full(W),
        out_shape=jax.ShapeDtypeStruct((M, W), _BF16),
        name="nsa_combine",
    )(ocmp, osel, owin, sm, az)


def _mixer_inputs(x, norm_w, pw, m_width, a_width, a_dh):
    B, T, D = x.shape
    w_main, w_small, seg, hw, hm, groups, norm_tiles = pw
    M = B * T
    tm = next((t for t in (1024, 512) if M % t == 0), M)
    a, q, az, kv4, kvw, sm, *kv4t = _proj_in(x.reshape(M, D), norm_w, w_main, w_small, seg, hw, hm,
                                             groups, norm_tiles, tm, T)
    r3 = lambda v: v.reshape(B, T, v.shape[-1])
    kv_rows = r3(kv4).reshape(B, T, 4, A_KV, a_dh)
    if kv4t:
        kv_rows = jnp.transpose(kv4t[0].reshape(B, 4, A_KV, a_dh, T), (0, 4, 1, 2, 3))
    return r3(a), r3(sm), r3(q), r3(kv4), r3(kvw), r3(az), kv_rows


def _mixer_output(x, m_out, a_out, p, ow):
    B, T, D = x.shape
    M = B * T
    tm = 256 if M % 256 == 0 else M
    y = _out_proj(x.reshape(M, D), m_out.reshape(M, -1), a_out.reshape(M, -1), p.reshape(M, -1), *ow, tm)
    return y.reshape(B, T, D)


def kernel(x_prompt, x_sample, cache_nsa_kv, cache_win_kv, state_mlstm_C, state_mlstm_n,
           state_mlstm_m, state_mlstm_conv, page_table, p_prompt, p_sample, norm_w, w_in,
           m_conv_w, m_conv_b, m_wq, m_wk, m_wv, m_b_i, m_b_f, m_norm_w, a_q_norm, a_k_norm,
           cmp_pe, cmp_w1, cmp_w2, w_out, ple_proj, ple_norm, ple_gate):
    xp, xs = x_prompt, x_sample
    B, T, D = xp.shape
    depth = w_in.shape[0]
    m_width = m_conv_w.shape[-1]
    a_width = D - m_width
    a_dh = a_q_norm.shape[-1]
    dh = m_width // M_HEADS
    per_layer = []
    for i in range(depth):
        mw = (m_conv_w[i], m_conv_b[i], m_wq[i], m_wk[i], m_wv[i], m_b_i[i], m_b_f[i], m_norm_w[i])
        cw = (cmp_pe[i], cmp_w1[i], cmp_w2[i], a_k_norm[i, 0])
        w_out_i = jnp.concatenate(
            [w_out[i][:m_width], _heads_to_agd(w_out[i][m_width:].T, a_dh).T], axis=0)
        ow = (w_out_i, ple_proj[i], ple_norm[i], ple_gate[i])
        pw = _prep_proj_weights(w_in[i], a_q_norm[i], a_k_norm[i], m_width, a_width, a_dh)
        cwp = _prep_cmp_weights(*cw)
        a3, sm, q, kv4, kvw, az, kv_p = _mixer_inputs(xp, norm_w[i], pw, m_width, a_width, a_dh)
        L = _MLSTM_L if T % _MLSTM_L == 0 else (M_CHUNK if T % M_CHUNK == 0 else T)
        m_out, conv_p, C_p, n_p, m_p = _mlstm(
            a3, sm, jnp.zeros((B, M_CONV - 1, m_width), _F32),
            jnp.zeros((B, M_HEADS, dh, dh), _F32), jnp.zeros((B, M_HEADS, dh), _F32),
            jnp.full((B, M_HEADS), -jnp.inf, _F32), *mw, L)
        kc, vc = _compress_prompt(kv4, cwp)
        a_out = _nsa_prompt_t(q, kv4, kvw, kc, vc, sm, az)
        xp_new = _mixer_output(xp, m_out, a_out, p_prompt[i], ow)
        win_p = kvw[:, -min(WINDOW, T):].reshape(B, -1, 2, A_KV, a_dh)
        a3, sm, q, kv4, kvw, az, kv_s = _mixer_inputs(xs, norm_w[i], pw, m_width, a_width, a_dh)
        Bd, Ts = xs.shape[:2]
        Ls = _MLSTM_L if Ts % _MLSTM_L == 0 else (M_CHUNK if Ts % M_CHUNK == 0 else Ts)
        m_out, conv_s, C_s, n_s, m_s = _mlstm(
            a3, sm, state_mlstm_conv[i], state_mlstm_C[i], state_mlstm_n[i], state_mlstm_m[i], *mw, Ls)
        past_len = page_table.shape[1] * PAGE_SIZE
        cache_l = cache_nsa_kv[i]
        n_pool = cache_l.shape[0]
        wb = cache_win_kv.shape[2]
        cache_t = jnp.transpose(cache_l, (0, 2, 3, 4, 1))
        kc, vc = _compress_sample(cache_t.reshape(n_pool, 8, 2 * a_dh, PAGE_SIZE), page_table, cwp)
        ocmp, owin, idx = _nsa_sample_front(
            q, kc, vc, cache_win_kv[i].reshape(Bd, wb, 2 * A_KV * a_dh), kvw, past_len)
        topn = min(SLC_TOPN, past_len // SLC_BLK + 1)
        q_rows = q.reshape(Bd, Ts, A_GROUP, A_KV, a_dh).transpose(0, 3, 1, 2, 4)
        q_rows = jnp.pad(q_rows, ((0, 0), (0, 0), (0, 0), (0, 8 - A_GROUP), (0, 0)))
        kv_new = kv4.reshape(Bd, Ts, 4 * A_KV, a_dh).transpose(0, 2, 1, 3)
        osel = _nsa_sample_gather(
            idx[:, :, :topn].reshape(Bd, -1), page_table, q_rows.reshape(Bd, A_KV * Ts, 8, a_dh), kv_new,
            cache_t, past_len)
        osel = osel.reshape(Bd, A_KV, Ts, 8, a_dh)[:, :, :, :A_GROUP].transpose(0, 2, 3, 1, 4)
        Ms = Bd * Ts
        a_out = _nsa_combine(ocmp.reshape(Ms, -1), osel.reshape(Ms, -1), owin.reshape(Ms, -1),
                             sm.reshape(Ms, -1), az.reshape(Ms, -1)).reshape(Bd, Ts, -1)
        xs_new = _mixer_output(xs, m_out, a_out, p_sample[i], ow)
        win_s = jnp.concatenate(
            [cache_win_kv[i], kvw.reshape(Bd, Ts, 2, A_KV, a_dh)], axis=1)[:, -wb:]
        per_layer.append((kv_p, kv_s, win_p, win_s, C_p, n_p, m_p, conv_p, C_s, n_s, m_s, conv_s))
        xp, xs = xp_new, xs_new
    (kv_p, kv_s, win_p, win_s, C_p, n_p, m_p, conv_p,
     C_s, n_s, m_s, conv_s) = [jnp.stack(a, axis=0) for a in zip(*per_layer)]
    return (xp, xs, kv_p, kv_s, win_p, win_s, C_p, n_p, m_p, conv_p, C_s, n_s, m_s, conv_s)
```
